```python
import math, functools
import jax, jax.numpy as jnp
from jax import lax
import numpy as np

D_MODEL = 1024
BATCH = 4
SEQ = 4096
DEPTH = 1
DEC_BATCH = 32
DEC_SEQ = 4
PAST_LEN = 16384
PAGE_SIZE = 128

RW_HEAD = 64
RW_HEADS = (D_MODEL // 2) // RW_HEAD
RW_WIDTH = RW_HEADS * RW_HEAD
W_LORA = 64
A_LORA = 64
G_LORA = 128
RW_GN_EPS = 64e-5
RW_SHIFT = 3 * RW_WIDTH + W_LORA + A_LORA + G_LORA

NSA_HEAD = 64
NSA_HEADS = (D_MODEL // 2) // NSA_HEAD
NSA_KV = 2
NSA_HPG = NSA_HEADS // NSA_KV
NSA_WIDTH = NSA_HEADS * NSA_HEAD
NSA_KV_COLS = NSA_KV * NSA_HEAD
NSA_SCALE = NSA_HEAD ** -0.5
CMP_STRIDE = 16
CMP_LEN = 2 * CMP_STRIDE
CMP_HID = 128
SEL_BLOCK = 64
SEL_TOPK = 16
WINDOW = 512
Q_BLOCK = 128

D_FF = 2816
N_IN = RW_SHIFT + NSA_WIDTH + 6 * NSA_KV_COLS + 3 * NSA_HEADS + 2 * D_MODEL
ALPHA = (2 * DEPTH) ** 0.25
BETA = (8 * DEPTH) ** -0.25
LN_EPS = 1e-5
NEG_INF = -1e30
BIG = 1e9

kernel_name = 'hybrid_rwkv7_nsa_macaron_deepnorm_step'


def layer_norm(x, g, b):
    xf = x.astype(jnp.float32)
    mu = xf.mean(-1, keepdims=True)
    var = jnp.square(xf - mu).mean(-1, keepdims=True)
    return ((xf - mu) * lax.rsqrt(var + LN_EPS) * g + b).astype(x.dtype)


def swiglu(u, w_gate, w_up, w_down):
    return (jax.nn.silu(u @ w_gate) * (u @ w_up)) @ w_down


def masked_softmax(s, mask):
    s = jnp.where(mask, s, NEG_INF)
    e = jnp.where(mask, jnp.exp(s - s.max(axis=-1, keepdims=True)), 0.0)
    return e / jnp.maximum(e.sum(axis=-1, keepdims=True), 1e-30)


def rwkv7_time_mix(p, prev, s0, mu, w0, w2, a0, a2, g2, k_k, k_a, r_k, ln_w, ln_b):
    B, T, _ = p.shape
    f32 = jnp.float32
    p_prev = jnp.concatenate([prev[:, None].astype(p.dtype), p[:, :-1]], axis=1)
    xs = p + (p_prev - p) * mu
    r, k, v, wd, ad, gd = jnp.split(xs, [RW_WIDTH, 2 * RW_WIDTH, 3 * RW_WIDTH,
                                         3 * RW_WIDTH + W_LORA, 3 * RW_WIDTH + W_LORA + A_LORA], axis=-1)
    heads = lambda t: t.reshape(B, T, RW_HEADS, RW_HEAD).astype(f32)
    log_w = -jax.nn.softplus(-(w0 + jnp.tanh(wd) @ w2).astype(f32)) - 0.5
    decay = jnp.exp(-jnp.exp(log_w))
    a = jax.nn.sigmoid(a0 + ad @ a2)
    g = jax.nn.sigmoid(gd) @ g2
    kk = heads(k * k_k)
    kk = kk * lax.rsqrt(jnp.maximum(jnp.sum(kk * kk, -1, keepdims=True), 1e-24))
    k = k * (1 + (a - 1) * k_a)
    r_h, k_h, v_h, a_h, w_h = heads(r), heads(k), heads(v), heads(a), heads(decay)

    def step(S, inp):
        r_t, w_t, k_t, v_t, kk_t, a_t = inp
        sa = jnp.einsum('bhij,bhj->bhi', S, -kk_t)
        S = S * w_t[:, :, None, :] + sa[..., None] * (kk_t * a_t)[:, :, None, :] + v_t[..., None] * k_t[:, :, None, :]
        return S, jnp.einsum('bhij,bhj->bhi', S, r_t)

    seq_first = lambda t: jnp.swapaxes(t, 0, 1)
    s_final, y = lax.scan(step, s0.astype(f32), tuple(seq_first(t) for t in (r_h, w_h, k_h, v_h, kk, a_h)))
    y = seq_first(y)
    mean = y.mean(-1, keepdims=True)
    var = jnp.square(y - mean).mean(-1, keepdims=True)
    y = ((y - mean) * lax.rsqrt(var + RW_GN_EPS)).reshape(B, T, RW_WIDTH) * ln_w + ln_b
    bonus = (jnp.sum(r_h * k_h * r_k, -1, keepdims=True) * v_h).reshape(B, T, RW_WIDTH)
    out = ((y + bonus) * g).astype(p.dtype)
    return out, p[:, -1], s_final


def compress_blocks(x, pe, w1, w2):
    B, L = x.shape[:2]
    n_cmp = (L - CMP_LEN) // CMP_STRIDE + 1
    ch = x[:, :(n_cmp + 1) * CMP_STRIDE].reshape(B, n_cmp + 1, CMP_STRIDE, NSA_KV, NSA_HEAD)
    h_lo = jnp.einsum('bnjgd,jde->bnge', ch, w1[:CMP_STRIDE])
    h_hi = jnp.einsum('bnjgd,jde->bnge', ch, w1[CMP_STRIDE:])
    h = jax.nn.gelu(h_lo[:, :-1] + h_hi[:, 1:] + jnp.einsum('jd,jde->e', pe, w1))
    return jnp.einsum('bnge,ed->bngd', h, w2)


def to_blocks(x):
    B, L = x.shape[:2]
    n_sel = -(-L // SEL_BLOCK)
    x = jnp.pad(x, ((0, 0), (0, n_sel * SEL_BLOCK - L), (0, 0), (0, 0)))
    return x.reshape(B, n_sel, SEL_BLOCK, NSA_KV, NSA_HEAD).transpose(0, 3, 1, 2, 4)


def block_importance(p, n_sel):
    R = SEL_BLOCK // CMP_STRIDE
    n_cmp = p.shape[-1]
    pad = [(0, 0)] * (p.ndim - 1) + [(1, R * (n_sel + 1) - 1 - n_cmp)]
    pp = jnp.pad(p, pad).reshape(p.shape[:-1] + (n_sel + 1, R))
    return pp[..., :-1, :].sum(-1) + pp[..., 1:, 0]


def nsa_attend(q, gates, q_pos, kc, vc, ksb, vsb, kw, vw, w_pos):
    B, Qb = q.shape[:2]
    f32 = jnp.float32
    qg = q.reshape(B, Qb, NSA_KV, NSA_HPG, NSA_HEAD) * NSA_SCALE
    cmp_end = jnp.arange(kc.shape[1]) * CMP_STRIDE + (CMP_LEN - 1)
    s = jnp.einsum('bqghd,bngd->bghqn', qg, kc, preferred_element_type=f32)
    p_c = masked_softmax(s, cmp_end[None, :] <= q_pos[:, None])
    o_c = jnp.einsum('bghqn,bngd->bqghd', p_c.astype(vc.dtype), vc)
    n_sel = ksb.shape[2]
    score = block_importance(p_c.sum(axis=2), n_sel)
    cur = (q_pos // SEL_BLOCK)[:, None]
    j = jnp.arange(n_sel)[None, :]
    forced = (j == 0) | (j == cur) | (j == cur - 1)
    score = jnp.where(j <= cur, jnp.where(forced, BIG, score), -BIG)
    _, idx = lax.top_k(score, min(SEL_TOPK, n_sel))
    bi = jnp.arange(B)[:, None, None, None]
    gi = jnp.arange(NSA_KV)[None, :, None, None]
    n_keys = idx.shape[-1] * SEL_BLOCK
    kb = ksb[bi, gi, idx].reshape(B, NSA_KV, Qb, n_keys, NSA_HEAD)
    vb = vsb[bi, gi, idx].reshape(B, NSA_KV, Qb, n_keys, NSA_HEAD)
    k_pos = (idx[..., None] * SEL_BLOCK + jnp.arange(SEL_BLOCK)).reshape(B, NSA_KV, 1, Qb, n_keys)
    s = jnp.einsum('bqghd,bgqld->bghql', qg, kb, preferred_element_type=f32)
    p_s = masked_softmax(s, k_pos <= q_pos[:, None])
    o_s = jnp.einsum('bghql,bgqld->bqghd', p_s.astype(vb.dtype), vb)
    s = jnp.einsum('bqghd,blgd->bghql', qg, kw, preferred_element_type=f32)
    dist = q_pos[:, None] - w_pos[None, :]
    p_w = masked_softmax(s, (dist >= 0) & (dist < WINDOW) & (w_pos[None, :] >= 0))
    o_w = jnp.einsum('bghql,blgd->bqghd', p_w.astype(vw.dtype), vw)
    gt = gates.reshape(B, Qb, NSA_KV, NSA_HPG, 3)
    o = gt[..., 0:1] * o_c + gt[..., 1:2] * o_s + gt[..., 2:3] * o_w
    return o.reshape(B, Qb, NSA_WIDTH)


def nsa_prompt(q, gates, kvc, kvs, kvw, lp):
    B, T = q.shape[:2]
    kc = compress_blocks(kvc[:, :, 0], lp['phi_pe'][0], lp['phi_w1'][0], lp['phi_w2'][0])
    vc = compress_blocks(kvc[:, :, 1], lp['phi_pe'][1], lp['phi_w1'][1], lp['phi_w2'][1])
    ksb, vsb = to_blocks(kvs[:, :, 0]), to_blocks(kvs[:, :, 1])
    qb = min(Q_BLOCK, T)
    nb = T // qb
    kvw_pad = jnp.pad(kvw, ((0, 0), (WINDOW, 0), (0, 0), (0, 0), (0, 0)))

    def one_block(args):
        i, q_i, g_i = args
        t0 = i * qb
        kvw_i = lax.dynamic_slice_in_dim(kvw_pad, t0, WINDOW + qb, axis=1)
        return nsa_attend(q_i, g_i, t0 + jnp.arange(qb), kc, vc, ksb, vsb,
                          kvw_i[:, :, 0], kvw_i[:, :, 1], t0 - WINDOW + jnp.arange(WINDOW + qb))

    blocks = lambda t: jnp.swapaxes(t.reshape((B, nb, qb) + t.shape[2:]), 0, 1)
    out = lax.map(one_block, (jnp.arange(nb), blocks(q), blocks(gates)))
    y = jnp.swapaxes(out, 0, 1).reshape(B, T, NSA_WIDTH)
    return y, kvw[:, T - min(WINDOW, T):]


def nsa_sample(q, gates, kvc, kvs, kvw, lp, past_kvc, past_kvs, win_buf, past_len):
    B, T = q.shape[:2]
    kvc_all = jnp.concatenate([past_kvc.astype(kvc.dtype), kvc], axis=1)
    kvs_all = jnp.concatenate([past_kvs.astype(kvs.dtype), kvs], axis=1)
    kc = compress_blocks(kvc_all[:, :, 0], lp['phi_pe'][0], lp['phi_w1'][0], lp['phi_w2'][0])
    vc = compress_blocks(kvc_all[:, :, 1], lp['phi_pe'][1], lp['phi_w1'][1], lp['phi_w2'][1])
    ksb, vsb = to_blocks(kvs_all[:, :, 0]), to_blocks(kvs_all[:, :, 1])
    n_buf = win_buf.shape[1]
    win_all = jnp.concatenate([win_buf.astype(kvw.dtype), kvw], axis=1)
    w_pos = past_len - n_buf + jnp.arange(n_buf + T)
    y = nsa_attend(q, gates, past_len + jnp.arange(T), kc, vc, ksb, vsb,
                   win_all[:, :, 0], win_all[:, :, 1], w_pos)
    return y, win_all[:, T:]


def group_layer(x, c, lp, rw_prev, rw_wkv, nsa_fn):
    B, T, _ = x.shape
    mod = (jax.nn.silu(c) @ lp['w_ada'] + lp['b_ada']).reshape(B, 9, 1, D_MODEL)
    u = x * (1 + mod[:, 1]) + mod[:, 0]
    f = swiglu(u, lp['ffn_w_gate'][0], lp['ffn_w_up'][0], lp['ffn_w_down'][0])
    x = layer_norm(ALPHA * x + 0.5 * (1 + mod[:, 2]) * f, lp['ln_g'][0], lp['ln_b'][0])
    u = x * (1 + mod[:, 4]) + mod[:, 3]
    proj = u @ lp['w_in'] + lp['b_in']
    c1 = RW_SHIFT
    c2 = c1 + NSA_WIDTH
    c3 = c2 + 6 * NSA_KV_COLS
    c4 = c3 + 3 * NSA_HEADS
    p_rw, p_q, p_kv, p_gate, p_merge = jnp.split(proj, [c1, c2, c3, c4], axis=-1)
    y_a, new_prev, new_wkv = rwkv7_time_mix(p_rw, rw_prev, rw_wkv, lp['rw_mu'], lp['rw_w0'], lp['rw_w2'],
                                            lp['rw_a0'], lp['rw_a2'], lp['rw_g2'], lp['rw_k_k'],
                                            lp['rw_k_a'], lp['rw_r_k'], lp['rw_ln_w'], lp['rw_ln_b'])
    kv = p_kv.reshape(B, T, 3, 2, NSA_KV, NSA_HEAD)
    y_b, new_win = nsa_fn(p_q.reshape(B, T, NSA_HEADS, NSA_HEAD),
                          jax.nn.sigmoid(p_gate).reshape(B, T, NSA_HEADS, 3),
                          kv[:, :, 0], kv[:, :, 1], kv[:, :, 2], lp)
    g_a, g_b = jnp.split(jax.nn.sigmoid(p_merge), 2, axis=-1)
    m = (g_a * (y_a @ lp['w_out_a']) + g_b * (y_b @ lp['w_out_b'])) @ lp['w_o']
    x = layer_norm(ALPHA * x + (1 + mod[:, 5]) * m, lp['ln_g'][1], lp['ln_b'][1])
    u = x * (1 + mod[:, 7]) + mod[:, 6]
    f = swiglu(u, lp['ffn_w_gate'][1], lp['ffn_w_up'][1], lp['ffn_w_down'][1])
    x = layer_norm(ALPHA * x + 0.5 * (1 + mod[:, 8]) * f, lp['ln_g'][2], lp['ln_b'][2])
    return x, (kv[:, :, 0], kv[:, :, 1], new_win, new_wkv, new_prev)


def setup_inputs(seed: int = 0) -> dict:
    key = jax.random.key(seed)
    ks = jax.random.split(key, 40)
    nrm = lambda k, shape, scale: scale * jax.random.normal(k, shape, jnp.float32)
    L = DEPTH
    n_pages = PAST_LEN // PAGE_SIZE
    n_used = DEC_BATCH * n_pages
    n_phys = n_used + max(1, n_used // 4)
    win_buf = min(WINDOW, PAST_LEN)
    page_table = jax.random.permutation(ks[0], n_phys)[:n_used].reshape(DEC_BATCH, n_pages).astype(jnp.int32)
    return {
        'x_prompt': nrm(ks[1], (BATCH, SEQ, D_MODEL), 1.0),
        'x_sample': nrm(ks[2], (DEC_BATCH, DEC_SEQ, D_MODEL), 1.0),
        'c_prompt': nrm(ks[3], (BATCH, D_MODEL), 1.0),
        'c_sample': nrm(ks[4], (DEC_BATCH, D_MODEL), 1.0),
        'cache_kv_cmp': nrm(ks[5], (L, n_phys, PAGE_SIZE, 2, NSA_KV, NSA_HEAD), 1.0),
        'cache_kv_sel': nrm(ks[6], (L, n_phys, PAGE_SIZE, 2, NSA_KV, NSA_HEAD), 1.0),
        'state_kv_win': nrm(ks[7], (L, DEC_BATCH, win_buf, 2, NSA_KV, NSA_HEAD), 1.0),
        'state_wkv': nrm(ks[8], (L, DEC_BATCH, RW_HEADS, RW_HEAD, RW_HEAD), 0.3),
        'state_shift': nrm(ks[9], (L, DEC_BATCH, RW_SHIFT), 1.0),
        'page_table': page_table,
        'w_ada': nrm(ks[10], (L, D_MODEL, 9 * D_MODEL), 0.2 * D_MODEL ** -0.5),
        'b_ada': nrm(ks[11], (L, 9 * D_MODEL), 0.02),
        'ln_g': 1.0 + nrm(ks[12], (L, 3, D_MODEL), 0.05),
        'ln_b': nrm(ks[13], (L, 3, D_MODEL), 0.02),
        'ffn_w_gate': nrm(ks[14], (L, 2, D_MODEL, D_FF), D_MODEL ** -0.5),
        'ffn_w_up': nrm(ks[15], (L, 2, D_MODEL, D_FF), D_MODEL ** -0.5),
        'ffn_w_down': nrm(ks[16], (L, 2, D_FF, D_MODEL), BETA * D_FF ** -0.5),
        'w_in': nrm(ks[17], (L, D_MODEL, N_IN), D_MODEL ** -0.5),
        'b_in': nrm(ks[18], (L, N_IN), 0.02),
        'rw_mu': jax.random.uniform(ks[19], (L, RW_SHIFT), jnp.float32),
        'rw_w0': -1.0 + nrm(ks[20], (L, RW_WIDTH), 0.5),
        'rw_w2': nrm(ks[21], (L, W_LORA, RW_WIDTH), 0.5 * W_LORA ** -0.5),
        'rw_a0': nrm(ks[22], (L, RW_WIDTH), 0.5),
        'rw_a2': nrm(ks[23], (L, A_LORA, RW_WIDTH), 0.5 * A_LORA ** -0.5),
        'rw_g2': nrm(ks[24], (L, G_LORA, RW_WIDTH), G_LORA ** -0.5),
        'rw_k_k': 0.85 + nrm(ks[25], (L, RW_WIDTH), 0.05),
        'rw_k_a': 1.0 + nrm(ks[26], (L, RW_WIDTH), 0.05),
        'rw_r_k': nrm(ks[27], (L, RW_HEADS, RW_HEAD), 0.1),
        'rw_ln_w': 1.0 + nrm(ks[28], (L, RW_WIDTH), 0.05),
        'rw_ln_b': nrm(ks[29], (L, RW_WIDTH), 0.02),
        'nsa_phi_pe': nrm(ks[30], (L, 2, CMP_LEN, NSA_HEAD), 0.1),
        'nsa_phi_w1': nrm(ks[31], (L, 2, CMP_LEN, NSA_HEAD, CMP_HID), (CMP_LEN * NSA_HEAD) ** -0.5),
        'nsa_phi_w2': nrm(ks[32], (L, 2, CMP_HID, NSA_HEAD), CMP_HID ** -0.5),
        'w_out_a': nrm(ks[33], (L, RW_WIDTH, D_MODEL), BETA * RW_WIDTH ** -0.5),
        'w_out_b': nrm(ks[34], (L, NSA_WIDTH, D_MODEL), BETA * NSA_WIDTH ** -0.5),
        'w_o': nrm(ks[35], (L, D_MODEL, D_MODEL), BETA * D_MODEL ** -0.5),
    }


def reference(x_prompt, x_sample, c_prompt, c_sample, cache_kv_cmp, cache_kv_sel, state_kv_win,
              state_wkv, state_shift, page_table, w_ada, b_ada, ln_g, ln_b, ffn_w_gate, ffn_w_up,
              ffn_w_down, w_in, b_in, rw_mu, rw_w0, rw_w2, rw_a0, rw_a2, rw_g2, rw_k_k, rw_k_a,
              rw_r_k, rw_ln_w, rw_ln_b, nsa_phi_pe, nsa_phi_w1, nsa_phi_w2, w_out_a, w_out_b, w_o):
    bp = x_prompt.shape[0]
    bd = x_sample.shape[0]
    past_len = page_table.shape[1] * cache_kv_cmp.shape[2]
    y_prompt, y_sample = x_prompt, x_sample
    st_prompt, st_sample = [], []
    for l in range(DEPTH):
        lp = dict(w_ada=w_ada[l], b_ada=b_ada[l], ln_g=ln_g[l], ln_b=ln_b[l],
                  ffn_w_gate=ffn_w_gate[l], ffn_w_up=ffn_w_up[l], ffn_w_down=ffn_w_down[l],
                  w_in=w_in[l], b_in=b_in[l], rw_mu=rw_mu[l], rw_w0=rw_w0[l], rw_w2=rw_w2[l],
                  rw_a0=rw_a0[l], rw_a2=rw_a2[l], rw_g2=rw_g2[l], rw_k_k=rw_k_k[l], rw_k_a=rw_k_a[l],
                  rw_r_k=rw_r_k[l], rw_ln_w=rw_ln_w[l], rw_ln_b=rw_ln_b[l],
                  phi_pe=nsa_phi_pe[l], phi_w1=nsa_phi_w1[l], phi_w2=nsa_phi_w2[l],
                  w_out_a=w_out_a[l], w_out_b=w_out_b[l], w_o=w_o[l])
        y_prompt, st_p = group_layer(y_prompt, c_prompt, lp,
                                     jnp.zeros((bp, RW_SHIFT), x_prompt.dtype),
                                     jnp.zeros((bp, RW_HEADS, RW_HEAD, RW_HEAD), jnp.float32),
                                     nsa_prompt)
        past_kvc = cache_kv_cmp[l, page_table].reshape(bd, past_len, 2, NSA_KV, NSA_HEAD)
        past_kvs = cache_kv_sel[l, page_table].reshape(bd, past_len, 2, NSA_KV, NSA_HEAD)
        nsa_s = functools.partial(nsa_sample, past_kvc=past_kvc, past_kvs=past_kvs,
                                  win_buf=state_kv_win[l], past_len=past_len)
        y_sample, st_s = group_layer(y_sample, c_sample, lp, state_shift[l], state_wkv[l], nsa_s)
        st_prompt.append(st_p)
        st_sample.append(st_s)
    new_kv_cmp_prompt = jnp.stack([s[0] for s in st_prompt])
    new_kv_sel_prompt = jnp.stack([s[1] for s in st_prompt])
    new_kv_win_prompt = jnp.stack([s[2] for s in st_prompt])
    new_wkv_prompt = jnp.stack([s[3] for s in st_prompt])
    new_shift_prompt = jnp.stack([s[4] for s in st_prompt])
    new_kv_cmp_sample = jnp.stack([s[0] for s in st_sample])
    new_kv_sel_sample = jnp.stack([s[1] for s in st_sample])
    new_kv_win_sample = jnp.stack([s[2] for s in st_sample])
    new_wkv_sample = jnp.stack([s[3] for s in st_sample])
    new_shift_sample = jnp.stack([s[4] for s in st_sample])
    return (y_prompt, y_sample, new_kv_cmp_prompt, new_kv_sel_prompt, new_kv_win_prompt,
            new_wkv_prompt, new_shift_prompt, new_kv_cmp_sample, new_kv_sel_sample,
            new_kv_win_sample, new_wkv_sample, new_shift_sample)
```

```python
import functools
import math

import jax
import jax.numpy as jnp
from jax import lax
from jax.experimental import pallas as pl
from jax.experimental.pallas import tpu as pltpu

f32 = jnp.float32
bf16 = jnp.bfloat16

D_MODEL = 1024
RW_HEAD = 64
RW_HEADS = 8
RW_WIDTH = RW_HEADS * RW_HEAD
W_LORA = 64
A_LORA = 64
G_LORA = 128
LORA_IN = W_LORA + A_LORA + G_LORA
RW_GN_EPS = 64e-5
RW_SHIFT = 3 * RW_WIDTH + LORA_IN
NSA_HEAD = 64
NSA_HEADS = 8
NSA_KV = 2
NSA_HPG = NSA_HEADS // NSA_KV
NSA_WIDTH = NSA_HEADS * NSA_HEAD
NSA_KV_COLS = NSA_KV * NSA_HEAD
KV_COLS = 2 * NSA_KV_COLS
NSA_SCALE = NSA_HEAD ** -0.5
CMP_STRIDE = 16
CMP_LEN = 2 * CMP_STRIDE
CMP_HID = 128
CMP_ROW = CMP_STRIDE * KV_COLS
SEL_BLOCK = 64
SEL_TOPK = 16
WINDOW = 512
D_FF = 2816
N_GATE = 3 * NSA_HEADS
LN_EPS = 1e-5
NEG_INF = -1e30
BIG = 1e9

LANES = 128
GATE_PAD = LANES
FF_CHUNK = 256
RW_CHUNK = 64
Q_TILE = 128
K_TILE = 128
S_KTILE = 512
VMEM_LIMIT = 56 * 1024 * 1024

_NT = (((1,), (1,)), ((), ()))
_TN = (((0,), (0,)), ((), ()))
_HI = lax.Precision.HIGHEST


def _dot(a, b):
    return jnp.dot(a.astype(bf16), b.astype(bf16), preferred_element_type=f32)


def _dot_nt(a, b):
    return lax.dot_general(a.astype(bf16), b.astype(bf16), _NT, preferred_element_type=f32)


def _dot_f32(a, b, dims=None):
    if dims is None:
        return jnp.dot(a, b, preferred_element_type=f32, precision=_HI)
    return lax.dot_general(a, b, dims, preferred_element_type=f32, precision=_HI)


def _dot_sel(x, m01):
    hi = x.astype(bf16)
    r1 = x - hi.astype(f32)
    mid = r1.astype(bf16)
    lo = (r1 - mid.astype(f32)).astype(bf16)
    d = lambda t: jnp.dot(t, m01, preferred_element_type=f32)
    return d(hi) + d(mid) + d(lo)


def _layer_norm(x, g, b):
    mu = jnp.mean(x, axis=-1, keepdims=True)
    xc = x - mu
    var = jnp.mean(xc * xc, axis=-1, keepdims=True)
    return xc * lax.rsqrt(var + LN_EPS) * g + b


def _cparams(sem):
    return pltpu.CompilerParams(dimension_semantics=sem, vmem_limit_bytes=VMEM_LIMIT)


def _resident(shape):
    nd = len(shape)
    return pl.BlockSpec(shape, lambda *_: (0,) * nd, pipeline_mode=pl.Buffered(1))


def _ada_kernel(c_ref, w_ref, b_ref, o_ref):
    c = c_ref[...]
    o_ref[...] = _dot(c * jax.nn.sigmoid(c), w_ref[...]) + b_ref[...]


def _ada_mod(c, w_ada, b_ada):
    rows, d = c.shape
    n = w_ada.shape[1]
    tn = d
    return pl.pallas_call(
        _ada_kernel,
        out_shape=jax.ShapeDtypeStruct((rows, n), f32),
        grid=(n // tn,),
        in_specs=[pl.BlockSpec((rows, d), lambda j: (0, 0)),
                  pl.BlockSpec((d, tn), lambda j: (0, j)),
                  pl.BlockSpec((1, tn), lambda j: (0, j))],
        out_specs=pl.BlockSpec((rows, tn), lambda j: (0, j)),
        compiler_params=_cparams(("parallel",)),
        name="ada_mod",
    )(c, w_ada, b_ada.reshape(1, n))


def _ffn(u, wg_ref, wu_ref, wd_ref, acc_ref):
    ub = u.astype(bf16)
    acc_ref[...] = jnp.zeros(acc_ref.shape, f32)

    def body(c, carry):
        hg = jnp.dot(ub, wg_ref[c], preferred_element_type=f32)
        hu = jnp.dot(ub, wu_ref[c], preferred_element_type=f32)
        h = hg * jax.nn.sigmoid(hg) * hu
        acc_ref[...] += jnp.dot(h.astype(bf16), wd_ref[c], preferred_element_type=f32)
        return carry

    lax.fori_loop(0, wg_ref.shape[0], body, 0)
    return acc_ref[...]


def _ffn_in_kernel(alpha, seg, x_ref, mod_ref, wg_ref, wu_ref, wd_ref, lng_ref, lnb_ref, win_ref, bin_ref,
                   x1_ref, prw_ref, pq_ref, pkc_ref, pks_ref, pkw_ref, pm_ref, pg_ref, acc_ref):
    x = x_ref[...]
    u = x * (1.0 + mod_ref[1]) + mod_ref[0]
    f = _ffn(u, wg_ref, wu_ref, wd_ref, acc_ref)
    x1 = _layer_norm(alpha * x + 0.5 * (1.0 + mod_ref[2]) * f, lng_ref[0:1, :], lnb_ref[0:1, :])
    x1_ref[...] = x1
    u2 = (x1 * (1.0 + mod_ref[4]) + mod_ref[3]).astype(bf16)
    outs = (prw_ref, pq_ref, pkc_ref, pks_ref, pkw_ref, pm_ref, pg_ref)
    for o_ref, (lo, hi) in zip(outs, seg):
        o_ref[...] = jnp.dot(u2, win_ref[:, lo:hi], preferred_element_type=f32) + bin_ref[:, lo:hi]


def _proj_segments():
    widths = (RW_SHIFT, NSA_WIDTH, KV_COLS, KV_COLS, KV_COLS, 2 * D_MODEL, GATE_PAD)
    seg, lo = [], 0
    for w in widths:
        seg.append((lo, lo + w))
        lo += w
    return tuple(seg), lo


def _ffn_in(x, mod, wts, tm):
    g, t, d = x.shape
    r = mod.shape[2]
    seg, n_in = _proj_segments()
    row = lambda w: pl.BlockSpec((None, tm, w), lambda i, j: (i, j, 0))
    outs = [jax.ShapeDtypeStruct((g, t, hi - lo), f32) for lo, hi in seg]
    return pl.pallas_call(
        functools.partial(_ffn_in_kernel, wts["alpha"], seg),
        out_shape=[jax.ShapeDtypeStruct((g, t, d), f32)] + outs,
        grid=(g, t // tm),
        in_specs=[row(d),
                  pl.BlockSpec((None, 9, r, d), lambda i, j: (i, 0, 0, 0)),
                  _resident(wts["wg1"].shape), _resident(wts["wu1"].shape), _resident(wts["wd1"].shape),
                  _resident(wts["ln_g"].shape), _resident(wts["ln_b"].shape),
                  _resident(wts["w_in"].shape), _resident(wts["b_in"].shape)],
        out_specs=[row(d)] + [row(hi - lo) for lo, hi in seg],
        scratch_shapes=[pltpu.VMEM((tm, d), f32)],
        compiler_params=_cparams(("parallel", "parallel")),
        name="ffn_in",
    )(x, mod, wts["wg1"], wts["wu1"], wts["wd1"], wts["ln_g"], wts["ln_b"], wts["w_in"], wts["b_in"])


def _mix_ffn_kernel(alpha, x1_ref, ya_ref, yb_ref, pm_ref, mod_ref, woa_ref, wob_ref, wo_ref,
                    wg_ref, wu_ref, wd_ref, lng_ref, lnb_ref, o_ref, acc_ref):
    x1 = x1_ref[...]
    pm = pm_ref[...]
    d = x1.shape[-1]
    ga = jax.nn.sigmoid(pm[:, :d])
    gb = jax.nn.sigmoid(pm[:, d:])
    merged = ga * _dot(ya_ref[...], woa_ref[...]) + gb * _dot(yb_ref[...], wob_ref[...])
    m = _dot(merged, wo_ref[...])
    x2 = _layer_norm(alpha * x1 + (1.0 + mod_ref[5]) * m, lng_ref[1:2, :], lnb_ref[1:2, :])
    u = x2 * (1.0 + mod_ref[7]) + mod_ref[6]
    f = _ffn(u, wg_ref, wu_ref, wd_ref, acc_ref)
    o_ref[...] = _layer_norm(alpha * x2 + 0.5 * (1.0 + mod_ref[8]) * f, lng_ref[2:3, :], lnb_ref[2:3, :])


def _mix_ffn(x1, ya, yb, pm, mod, wts, tm):
    g, t, d = x1.shape
    r = mod.shape[2]
    row = lambda w: pl.BlockSpec((None, tm, w), lambda i, j: (i, j, 0))
    return pl.pallas_call(
        functools.partial(_mix_ffn_kernel, wts["alpha"]),
        out_shape=jax.ShapeDtypeStruct((g, t, d), f32),
        grid=(g, t // tm),
        in_specs=[row(d), row(ya.shape[-1]), row(yb.shape[-1]), row(pm.shape[-1]),
                  pl.BlockSpec((None, 9, r, d), lambda i, j: (i, 0, 0, 0)),
                  _resident(wts["w_out_a"].shape), _resident(wts["w_out_b"].shape), _resident(wts["w_o"].shape),
                  _resident(wts["wg2"].shape), _resident(wts["wu2"].shape), _resident(wts["wd2"].shape),
                  _resident(wts["ln_g"].shape), _resident(wts["ln_b"].shape)],
        out_specs=row(d),
        scratch_shapes=[pltpu.VMEM((tm, d), f32)],
        compiler_params=_cparams(("parallel", "parallel")),
        name="mix_ffn",
    )(x1, ya, yb, pm, mod, wts["w_out_a"], wts["w_out_b"], wts["w_o"], wts["wg2"], wts["wu2"], wts["wd2"],
      wts["ln_g"], wts["ln_b"])


def _rwkv_kernel(t_blk, p_ref, prev_ref, s0_ref, mu_ref, wl_ref, vec_ref, hsum_ref,
                 y_ref, sout_ref, state_ref, prevrow_ref, stage_ref, yn_ref):
    c = stage_ref.shape[0]
    w = RW_WIDTH
    ci = pl.program_id(1)

    @pl.when(ci == 0)
    def _():
        state_ref[...] = s0_ref[...]
        prevrow_ref[...] = prev_ref[...]

    if t_blk < c:
        stage_ref[...] = jnp.zeros(stage_ref.shape, f32)
        stage_ref[0:t_blk, :] = p_ref[...]
        p = stage_ref[...]
    else:
        p = p_ref[...]
    ridx = lax.broadcasted_iota(jnp.int32, (c, 1), 0)
    valid = ridx < t_blk
    p_prev = jnp.where(ridx == 0, prevrow_ref[...], pltpu.roll(p, 1, 0))
    prevrow_ref[...] = p[c - 1:c, :]
    xs = p + (p_prev - p) * mu_ref[...]

    tail = xs[:, 3 * w:]
    li = lax.broadcasted_iota(jnp.int32, tail.shape, 1)
    act = jnp.where(li < W_LORA, jnp.tanh(tail),
                    jnp.where(li < W_LORA + A_LORA, tail, jax.nn.sigmoid(tail)))
    lora = _dot(act, wl_ref[...])
    w0, a0, k_k, k_a, r_k, ln_w, ln_b = (vec_ref[i:i + 1, :] for i in range(7))
    z = -(w0 + lora[:, :w])
    softplus = jnp.maximum(z, 0.0) + jnp.log1p(jnp.exp(-jnp.abs(z)))
    lw = -jnp.exp(-softplus - 0.5)
    a = jax.nn.sigmoid(a0 + lora[:, w:2 * w])
    g = lora[:, 2 * w:]
    r, k, v = xs[:, :w], xs[:, w:2 * w], xs[:, 2 * w:3 * w]
    hsum = hsum_ref[...]
    kk = k * k_k
    kk = kk * lax.rsqrt(jnp.maximum(_dot_sel(kk * kk, hsum), 1e-24))
    k2 = k * (1.0 + (a - 1.0) * k_a)
    bonus = _dot_sel(r * k2 * r_k, hsum) * v
    if t_blk < c:
        lw = jnp.where(valid, lw, 0.0)
        kk = jnp.where(valid, kk, 0.0)
        k2 = jnp.where(valid, k2, 0.0)
        v = jnp.where(valid, v, 0.0)
    b = kk * a

    rr = lax.broadcasted_iota(jnp.int32, (c, c), 0)
    cc = lax.broadcasted_iota(jnp.int32, (c, c), 1)
    cum = _dot_f32(jnp.where(rr >= cc, 1.0, 0.0).astype(f32), lw)
    cum_last = cum[c - 1:c, :]
    e_neg = jnp.exp(-cum)
    e_rem = jnp.exp(cum_last - cum)
    at = -kk * jnp.exp(cum - lw)
    rt = r * jnp.exp(cum)
    bt, kt = b * e_neg, k2 * e_neg
    bh, kh = b * e_rem, k2 * e_rem
    p_last = jnp.exp(cum_last)
    strict = rr > cc
    incl = rr >= cc

    for h in range(RW_HEADS):
        sl = slice(h * RW_HEAD, (h + 1) * RW_HEAD)
        ar = jnp.concatenate([at[:, sl], rt[:, sl]], axis=0)
        bk = jnp.concatenate([bt[:, sl], kt[:, sl]], axis=0)
        gram = _dot_f32(ar, bk, _NT)
        s0 = state_ref[h]
        ars = _dot_f32(ar, s0, _NT)
        vh = v[:, sl]
        low = jnp.where(strict, gram[:c, :c], 0.0)
        u = ars[:c] + _dot_f32(jnp.where(strict, gram[:c, c:], 0.0), vh)
        n_fac = max(1, (c - 1).bit_length())
        for i in range(n_fac):
            u = u + _dot_f32(low, u)
            if i + 1 < n_fac:
                low = _dot_f32(low, low)
        uv = jnp.concatenate([u, vh], axis=0)
        rbk = jnp.concatenate([jnp.where(incl, gram[c:, :c], 0.0), jnp.where(incl, gram[c:, c:], 0.0)], axis=1)
        yh = ars[c:] + _dot_f32(rbk, uv)
        bkh = jnp.concatenate([bh[:, sl], kh[:, sl]], axis=0)
        state_ref[h] = s0 * p_last[:, sl] + _dot_f32(uv, bkh, _TN)
        mean = jnp.mean(yh, axis=-1, keepdims=True)
        yc = yh - mean
        var = jnp.mean(yc * yc, axis=-1, keepdims=True)
        yn_ref[:, sl] = yc * lax.rsqrt(var + RW_GN_EPS)

    y = (yn_ref[...] * ln_w + ln_b + bonus) * g
    y_ref[...] = y[0:t_blk, :] if t_blk < c else y

    @pl.when(ci == pl.num_programs(1) - 1)
    def _():
        sout_ref[...] = state_ref[...]


def _rwkv(p_rw, prev, s0, wts):
    g, t, ws = p_rw.shape
    c = RW_CHUNK
    t_blk = min(c, t)
    assert t % t_blk == 0
    return pl.pallas_call(
        functools.partial(_rwkv_kernel, t_blk),
        out_shape=[jax.ShapeDtypeStruct((g, t, RW_WIDTH), f32), jax.ShapeDtypeStruct(s0.shape, f32)],
        grid=(g, t // t_blk),
        in_specs=[pl.BlockSpec((None, t_blk, ws), lambda i, j: (i, j, 0)),
                  pl.BlockSpec((None, 1, ws), lambda i, j: (i, 0, 0)),
                  pl.BlockSpec((None,) + s0.shape[1:], lambda i, j: (i, 0, 0, 0)),
                  _resident(wts["rw_mu"].shape), _resident(wts["rw_lora"].shape),
                  _resident(wts["rw_vec"].shape), _resident(wts["rw_hsum"].shape)],
        out_specs=[pl.BlockSpec((None, t_blk, RW_WIDTH), lambda i, j: (i, j, 0)),
                   pl.BlockSpec((None,) + s0.shape[1:], lambda i, j: (i, 0, 0, 0))],
        scratch_shapes=[pltpu.VMEM(s0.shape[1:], f32), pltpu.VMEM((1, ws), f32),
                        pltpu.VMEM((c, ws), f32), pltpu.VMEM((c, RW_WIDTH), f32)],
        compiler_params=_cparams(("parallel", "arbitrary")),
        name="rwkv",
    )(p_rw, prev.reshape(g, 1, ws), s0, wts["rw_mu"], wts["rw_lora"], wts["rw_vec"], wts["rw_hsum"])


def _gelu_tanh(x):
    return x * (0.5 * (1.0 + jnp.tanh(math.sqrt(2.0 / math.pi) * (x + 0.044715 * (x * x * x)))))


def _compress_rows(x, wbig_ref, pe_ref, w1f_ref, w2_ref, o_ref):
    n = x.shape[0]
    hid = _dot(x, wbig_ref[...])
    for kv in range(2):
        pe_term = _dot(pe_ref[kv], w1f_ref[kv])[0:1, :]
        for grp in range(NSA_KV):
            base = (kv * NSA_KV + grp) * 2 * CMP_HID
            lo = hid[:, base:base + CMP_HID]
            hi = pltpu.roll(hid[:, base + CMP_HID:base + 2 * CMP_HID], n - 1, 0)
            o = _dot(_gelu_tanh(lo + hi + pe_term), w2_ref[kv])
            col = (kv * NSA_KV + grp) * NSA_HEAD
            o_ref[:, col:col + NSA_HEAD] = o


def _compress_dense_kernel(x_ref, wbig_ref, pe_ref, w1f_ref, w2_ref, o_ref):
    _compress_rows(x_ref[...], wbig_ref, pe_ref, w1f_ref, w2_ref, o_ref)


def _compress_dense(x, wts):
    b, n, _ = x.shape
    return pl.pallas_call(
        _compress_dense_kernel,
        out_shape=jax.ShapeDtypeStruct((b, n, KV_COLS), f32),
        grid=(b,),
        in_specs=[pl.BlockSpec((None, n, CMP_ROW), lambda i: (i, 0, 0)),
                  _resident(wts["phi_big"].shape), _resident(wts["phi_pe"].shape),
                  _resident(wts["phi_w1f"].shape), _resident(wts["phi_w2"].shape)],
        out_specs=pl.BlockSpec((None, n, KV_COLS), lambda i: (i, 0, 0)),
        compiler_params=_cparams(("parallel",)),
        name="compress_dense",
    )(x, wts["phi_big"], wts["phi_pe"], wts["phi_w1f"], wts["phi_w2"])


def _page_copy(cache_ref, buf_ref, sem, page, i, rows):
    return pltpu.make_async_copy(cache_ref.at[page], buf_ref.at[pl.ds(i * rows, rows)], sem)


def _gather_pages(pt_ref, b, cache_ref, buf_ref, sem, n_pages, rows):
    def start(i, carry):
        _page_copy(cache_ref, buf_ref, sem, pt_ref[b, i], i, rows).start()
        return carry

    def wait(i, carry):
        _page_copy(cache_ref, buf_ref, sem, pt_ref[b, i], i, rows).wait()
        return carry

    lax.fori_loop(0, n_pages, start, 0)
    lax.fori_loop(0, n_pages, wait, 0)


def _compress_paged_kernel(pt_ref, cache_ref, wbig_ref, pe_ref, w1f_ref, w2_ref, o_ref, xbuf_ref, sem):
    n_pages = pt_ref.shape[1]
    rows = cache_ref.shape[1]
    _gather_pages(pt_ref, pl.program_id(0), cache_ref, xbuf_ref, sem, n_pages, rows)
    _compress_rows(xbuf_ref[...], wbig_ref, pe_ref, w1f_ref, w2_ref, o_ref)


def _compress_paged(cache, page_table, wts):
    b, n_pages = page_table.shape
    rows = cache.shape[1]
    n = n_pages * rows
    full = lambda a: pl.BlockSpec(a.shape, lambda i, pt: (0,) * a.ndim, pipeline_mode=pl.Buffered(1))
    return pl.pallas_call(
        _compress_paged_kernel,
        out_shape=jax.ShapeDtypeStruct((b, n, KV_COLS), f32),
        grid_spec=pltpu.PrefetchScalarGridSpec(
            num_scalar_prefetch=1, grid=(b,),
            in_specs=[pl.BlockSpec(memory_space=pl.ANY), full(wts["phi_big"]), full(wts["phi_pe"]),
                      full(wts["phi_w1f"]), full(wts["phi_w2"])],
            out_specs=pl.BlockSpec((None, n, KV_COLS), lambda i, pt: (i, 0, 0)),
            scratch_shapes=[pltpu.VMEM((n, CMP_ROW), f32), pltpu.SemaphoreType.DMA(())]),
        compiler_params=_cparams(("arbitrary",)),
        name="compress_paged",
    )(page_table, cache, wts["phi_big"], wts["phi_pe"], wts["phi_w1f"], wts["phi_w2"])


def _softmax_rows(s, mask):
    s = jnp.where(mask, s, NEG_INF)
    e = jnp.where(mask, jnp.exp(s - jnp.max(s, axis=-1, keepdims=True)), 0.0)
    return e / jnp.maximum(jnp.sum(e, axis=-1, keepdims=True), 1e-30)


def _flash_step(q, k, v, mask, m, l, acc):
    s = jnp.where(mask, _dot_nt(q, k), NEG_INF)
    m_new = jnp.maximum(m, jnp.max(s, axis=-1, keepdims=True))
    alpha = jnp.exp(m - m_new)
    e = jnp.where(mask, jnp.exp(s - m_new), 0.0)
    return m_new, alpha * l + jnp.sum(e, axis=-1, keepdims=True), alpha * acc + _dot(e, v)


def _flash_init(rows):
    return (jnp.full((rows, 1), NEG_INF, f32), jnp.zeros((rows, 1), f32), jnp.zeros((rows, NSA_HEAD), f32))


def _flash_out(m, l, acc):
    return acc / jnp.maximum(l, 1e-30)


def _importance_matrix(n_cmp_rows, n_sel_cols):
    ratio = SEL_BLOCK // CMP_STRIDE
    i = lax.broadcasted_iota(jnp.int32, (n_cmp_rows, n_sel_cols), 0)
    j = lax.broadcasted_iota(jnp.int32, (n_cmp_rows, n_sel_cols), 1)
    return jnp.where((i >= ratio * j - 1) & (i <= ratio * j + ratio - 1), 1.0, 0.0).astype(bf16)


def _block_scores(imp, j, cur):
    forced = (j == 0) | (j == cur) | (j == cur - 1)
    return jnp.where(j <= cur, jnp.where(forced, BIG, imp), -BIG)


def _stack_heads(x, grp, rows):
    return jnp.concatenate(
        [x[:, (grp * NSA_HPG + h) * NSA_HEAD:(grp * NSA_HPG + h + 1) * NSA_HEAD] for h in range(NSA_HPG)], axis=0)


def _stack_gates(gates, grp, branch):
    return jnp.concatenate(
        [gates[:, 3 * (grp * NSA_HPG + h) + branch:3 * (grp * NSA_HPG + h) + branch + 1] for h in range(NSA_HPG)],
        axis=0)


def _nsa_prompt_kernel(q_ref, gate_ref, kvc_ref, kvs_ref, kvw_ref, o_ref):
    tq = q_ref.shape[0]
    n_cmp = kvc_ref.shape[0]
    n_sel = kvs_ref.shape[0] // SEL_BLOCK
    i = pl.program_id(1)
    t0 = i * tq
    rows = NSA_HPG * tq
    q = q_ref[...] * NSA_SCALE
    gates = jax.nn.sigmoid(gate_ref[...])
    q_pos = t0 + lax.broadcasted_iota(jnp.int32, (tq, 1), 0)
    q_pos_r = jnp.concatenate([q_pos] * NSA_HPG, axis=0)
    cmp_end = lax.broadcasted_iota(jnp.int32, (1, n_cmp), 1) * CMP_STRIDE + (CMP_LEN - 1)
    imp_mat = _importance_matrix(n_cmp, n_sel)
    jt = lax.broadcasted_iota(jnp.int32, (n_sel, tq), 0)
    cur_t = (t0 + lax.broadcasted_iota(jnp.int32, (n_sel, tq), 1)) // SEL_BLOCK
    kcol = lax.broadcasted_iota(jnp.int32, (1, K_TILE), 1)
    ej = lax.broadcasted_iota(jnp.int32, (n_sel, K_TILE), 0)
    ec = lax.broadcasted_iota(jnp.int32, (n_sel, K_TILE), 1) // SEL_BLOCK

    for grp in range(NSA_KV):
        kcol0, vcol0 = grp * NSA_HEAD, NSA_KV_COLS + grp * NSA_HEAD
        qs = _stack_heads(q, grp, tq).astype(bf16)
        p_c = _softmax_rows(_dot_nt(qs, kvc_ref[:, kcol0:kcol0 + NSA_HEAD]), cmp_end <= q_pos_r)
        o_c = _dot(p_c, kvc_ref[:, vcol0:vcol0 + NSA_HEAD])
        p_sum = p_c[0:tq]
        for h in range(1, NSA_HPG):
            p_sum = p_sum + p_c[h * tq:(h + 1) * tq]
        score = _block_scores(_dot_sel(p_sum, imp_mat).T, jt, cur_t)
        rank = jnp.zeros((n_sel, tq), f32)
        for j2 in range(n_sel):
            row = score[j2:j2 + 1, :]
            ge = jnp.where(row >= score, 1.0, 0.0)
            gt = jnp.where(row > score, 1.0, 0.0)
            rank = rank + jnp.where(jt > j2, ge, gt)
        sel = jnp.where(rank < SEL_TOPK, 1.0, 0.0).T.astype(bf16)
        sel = jnp.concatenate([sel] * NSA_HPG, axis=0)

        def sel_step(kt, carry):
            k0 = pl.multiple_of(kt * K_TILE, K_TILE)
            expand = jnp.where(ej == kt * (K_TILE // SEL_BLOCK) + ec, 1.0, 0.0).astype(bf16)
            chosen = (jnp.dot(sel, expand, preferred_element_type=f32) > 0.5) & ((k0 + kcol) <= q_pos_r)
            return _flash_step(qs, kvs_ref[pl.ds(k0, K_TILE), kcol0:kcol0 + NSA_HEAD],
                               kvs_ref[pl.ds(k0, K_TILE), vcol0:vcol0 + NSA_HEAD], chosen, *carry)

        o_s = _flash_out(*lax.fori_loop(0, i + 1, sel_step, _flash_init(rows)))

        def win_step(kt, carry):
            k0 = pl.multiple_of(kt * K_TILE, K_TILE)
            dist = q_pos_r - (k0 + kcol)
            return _flash_step(qs, kvw_ref[pl.ds(k0, K_TILE), kcol0:kcol0 + NSA_HEAD],
                               kvw_ref[pl.ds(k0, K_TILE), vcol0:vcol0 + NSA_HEAD],
                               (dist >= 0) & (dist < WINDOW), *carry)

        first = jnp.maximum(i - WINDOW // K_TILE, 0)
        o_w = _flash_out(*lax.fori_loop(first, i + 1, win_step, _flash_init(rows)))

        o = _stack_gates(gates, grp, 0) * o_c + _stack_gates(gates, grp, 1) * o_s + _stack_gates(gates, grp, 2) * o_w
        for h in range(NSA_HPG):
            col = (grp * NSA_HPG + h) * NSA_HEAD
            o_ref[:, col:col + NSA_HEAD] = o[h * tq:(h + 1) * tq]


def _nsa_prompt(p_q, p_gate, kvc_cmp, p_kvs, p_kvw):
    b, t, _ = p_q.shape
    n_cmp = kvc_cmp.shape[1]
    tq = Q_TILE
    assert t % tq == 0 and tq == K_TILE and t % SEL_BLOCK == 0
    seq = lambda n, w: pl.BlockSpec((None, n, w), lambda i, j: (i, 0, 0))
    return pl.pallas_call(
        _nsa_prompt_kernel,
        out_shape=jax.ShapeDtypeStruct((b, t, NSA_WIDTH), f32),
        grid=(b, t // tq),
        in_specs=[pl.BlockSpec((None, tq, NSA_WIDTH), lambda i, j: (i, j, 0)),
                  pl.BlockSpec((None, tq, GATE_PAD), lambda i, j: (i, j, 0)),
                  seq(n_cmp, KV_COLS), seq(t, KV_COLS), seq(t, KV_COLS)],
        out_specs=pl.BlockSpec((None, tq, NSA_WIDTH), lambda i, j: (i, j, 0)),
        compiler_params=_cparams(("parallel", "arbitrary")),
        name="nsa_prompt",
    )(p_q, p_gate, kvc_cmp, p_kvs, p_kvw)


def _nsa_sample_kernel(past_len, pt_ref, q_ref, gate_ref, kvc_ref, cache_ref, kvs_new_ref, win_ref, kvw_new_ref,
                       o_ref, selbuf_ref, winbuf_ref, qbuf_ref, gbuf_ref, obuf_ref, sem):
    t_new = q_ref.shape[0]
    tp = qbuf_ref.shape[0]
    n_pages = pt_ref.shape[1]
    page = cache_ref.shape[1]
    n_cmp = kvc_ref.shape[0]
    n_win = win_ref.shape[0]
    total = past_len + t_new
    n_sel = -(-total // SEL_BLOCK)
    n_sel_pad = -(-n_sel // LANES) * LANES
    n_kt = selbuf_ref.shape[0] // S_KTILE
    rows = NSA_HPG * tp

    selbuf_ref[pl.ds(past_len, selbuf_ref.shape[0] - past_len), :] = jnp.zeros(
        (selbuf_ref.shape[0] - past_len, KV_COLS), f32)
    selbuf_ref[pl.ds(past_len, t_new), :] = kvs_new_ref[...]
    _gather_pages(pt_ref, pl.program_id(0), cache_ref, selbuf_ref, sem, n_pages, page)
    winbuf_ref[...] = jnp.zeros(winbuf_ref.shape, f32)
    winbuf_ref[0:n_win, :] = win_ref[...]
    winbuf_ref[n_win:n_win + t_new, :] = kvw_new_ref[...]
    qbuf_ref[...] = jnp.zeros(qbuf_ref.shape, f32)
    qbuf_ref[0:t_new, :] = q_ref[...]
    gbuf_ref[...] = jnp.zeros(gbuf_ref.shape, f32)
    gbuf_ref[0:t_new, :] = gate_ref[...]

    q = qbuf_ref[...] * NSA_SCALE
    gates = jax.nn.sigmoid(gbuf_ref[...])
    q_pos = past_len + lax.broadcasted_iota(jnp.int32, (tp, 1), 0)
    q_pos_r = jnp.concatenate([q_pos] * NSA_HPG, axis=0)
    cmp_end = lax.broadcasted_iota(jnp.int32, (1, n_cmp), 1) * CMP_STRIDE + (CMP_LEN - 1)
    imp_mat = _importance_matrix(n_cmp, n_sel_pad)
    jj = lax.broadcasted_iota(jnp.int32, (tp, n_sel_pad), 1)
    kcol = lax.broadcasted_iota(jnp.int32, (1, S_KTILE), 1)
    ej = lax.broadcasted_iota(jnp.int32, (n_sel_pad, S_KTILE), 0)
    ec = lax.broadcasted_iota(jnp.int32, (n_sel_pad, S_KTILE), 1) // SEL_BLOCK

    for grp in range(NSA_KV):
        kcol0, vcol0 = grp * NSA_HEAD, NSA_KV_COLS + grp * NSA_HEAD
        qs = _stack_heads(q, grp, tp).astype(bf16)
        p_c = _softmax_rows(_dot_nt(qs, kvc_ref[:, kcol0:kcol0 + NSA_HEAD]), cmp_end <= q_pos_r)
        o_c = _dot(p_c, kvc_ref[:, vcol0:vcol0 + NSA_HEAD])
        p_sum = p_c[0:tp]
        for h in range(1, NSA_HPG):
            p_sum = p_sum + p_c[h * tp:(h + 1) * tp]
        score = _block_scores(_dot_sel(p_sum, imp_mat), jj, q_pos // SEL_BLOCK)
        rank = jnp.zeros((tp, n_sel_pad), f32)
        for j2 in range(n_sel):
            col = score[:, j2:j2 + 1]
            ge = jnp.where(col >= score, 1.0, 0.0)
            gt = jnp.where(col > score, 1.0, 0.0)
            rank = rank + jnp.where(jj > j2, ge, gt)
        sel = jnp.where((rank < SEL_TOPK) & (jj < n_sel), 1.0, 0.0).astype(bf16)
        sel = jnp.concatenate([sel] * NSA_HPG, axis=0)

        def sel_step(kt, carry):
            k0 = pl.multiple_of(kt * S_KTILE, S_KTILE)
            expand = jnp.where(ej == kt * (S_KTILE // SEL_BLOCK) + ec, 1.0, 0.0).astype(bf16)
            chosen = (jnp.dot(sel, expand, preferred_element_type=f32) > 0.5) & ((k0 + kcol) <= q_pos_r)
            return _flash_step(qs, selbuf_ref[pl.ds(k0, S_KTILE), kcol0:kcol0 + NSA_HEAD],
                               selbuf_ref[pl.ds(k0, S_KTILE), vcol0:vcol0 + NSA_HEAD], chosen, *carry)

        o_s = _flash_out(*lax.fori_loop(0, n_kt, sel_step, _flash_init(rows)))

        carry = _flash_init(rows)
        for kt in range(winbuf_ref.shape[0] // S_KTILE):
            w_pos = past_len - n_win + kt * S_KTILE + kcol
            dist = jnp.where(w_pos >= 0, q_pos_r - w_pos, -1)
            carry = _flash_step(qs, winbuf_ref[kt * S_KTILE:(kt + 1) * S_KTILE, kcol0:kcol0 + NSA_HEAD],
                                winbuf_ref[kt * S_KTILE:(kt + 1) * S_KTILE, vcol0:vcol0 + NSA_HEAD],
                                (dist >= 0) & (dist < WINDOW), *carry)
        o_w = _flash_out(*carry)

        o = _stack_gates(gates, grp, 0) * o_c + _stack_gates(gates, grp, 1) * o_s + _stack_gates(gates, grp, 2) * o_w
        for h in range(NSA_HPG):
            col = (grp * NSA_HPG + h) * NSA_HEAD
            obuf_ref[:, col:col + NSA_HEAD] = o[h * tp:(h + 1) * tp]
    o_ref[...] = obuf_ref[0:t_new, :]


def _nsa_sample(p_q, p_gate, kvc_cmp, cache_sel, page_table, p_kvs, win_state, p_kvw):
    b, t_new, _ = p_q.shape
    n_pages = page_table.shape[1]
    page = cache_sel.shape[1]
    past_len = n_pages * page
    n_cmp = kvc_cmp.shape[1]
    n_win = win_state.shape[1]
    tp = 8
    assert t_new <= tp and past_len % S_KTILE == 0
    sel_rows = past_len + S_KTILE
    win_rows = -(-(n_win + t_new) // S_KTILE) * S_KTILE
    per_seq = lambda n, w: pl.BlockSpec((None, n, w), lambda i, pt: (i, 0, 0))
    return pl.pallas_call(
        functools.partial(_nsa_sample_kernel, past_len),
        out_shape=jax.ShapeDtypeStruct((b, t_new, NSA_WIDTH), f32),
        grid_spec=pltpu.PrefetchScalarGridSpec(
            num_scalar_prefetch=1, grid=(b,),
            in_specs=[per_seq(t_new, NSA_WIDTH), per_seq(t_new, GATE_PAD), per_seq(n_cmp, KV_COLS),
                      pl.BlockSpec(memory_space=pl.ANY), per_seq(t_new, KV_COLS), per_seq(n_win, KV_COLS),
                      per_seq(t_new, KV_COLS)],
            out_specs=per_seq(t_new, NSA_WIDTH),
            scratch_shapes=[pltpu.VMEM((sel_rows, KV_COLS), f32), pltpu.VMEM((win_rows, KV_COLS), f32),
                            pltpu.VMEM((tp, NSA_WIDTH), f32), pltpu.VMEM((tp, GATE_PAD), f32),
                            pltpu.VMEM((tp, NSA_WIDTH), f32), pltpu.SemaphoreType.DMA(())]),
        compiler_params=_cparams(("arbitrary",)),
        name="nsa_sample",
    )(page_table, p_q, p_gate, kvc_cmp, cache_sel, p_kvs, win_state, p_kvw)


def _prep_weights(l, depth, ln_g, ln_b, ffn_w_gate, ffn_w_up, ffn_w_down, w_in, b_in, rw_mu, rw_w0, rw_w2, rw_a0,
                  rw_a2, rw_g2, rw_k_k, rw_k_a, rw_r_k, rw_ln_w, rw_ln_b, nsa_phi_pe, nsa_phi_w1, nsa_phi_w2,
                  w_out_a, w_out_b, w_o):
    d = D_MODEL
    nc = D_FF // FF_CHUNK
    up = lambda w: w.astype(bf16).reshape(d, nc, FF_CHUNK).transpose(1, 0, 2)
    down = lambda w: w.astype(bf16).reshape(nc, FF_CHUNK, d)
    c1 = RW_SHIFT
    c2 = c1 + NSA_WIDTH
    c3 = c2 + 3 * KV_COLS
    c4 = c3 + N_GATE
    cols = lambda a: jnp.concatenate(
        [a[..., :c3], a[..., c4:], a[..., c3:c4], jnp.zeros(a.shape[:-1] + (GATE_PAD - N_GATE,), a.dtype)], axis=-1)
    z = lambda r, c: jnp.zeros((r, c), f32)
    w = RW_WIDTH
    lora = jnp.concatenate([
        jnp.concatenate([rw_w2[l], z(W_LORA, w), z(W_LORA, w)], axis=1),
        jnp.concatenate([z(A_LORA, w), rw_a2[l], z(A_LORA, w)], axis=1),
        jnp.concatenate([z(G_LORA, w), z(G_LORA, w), rw_g2[l]], axis=1)], axis=0)
    hid = lax.broadcasted_iota(jnp.int32, (w, w), 0) // RW_HEAD == lax.broadcasted_iota(jnp.int32, (w, w), 1) // RW_HEAD
    vec = jnp.stack([rw_w0[l], rw_a0[l], rw_k_k[l], rw_k_a[l], rw_r_k[l].reshape(w), rw_ln_w[l], rw_ln_b[l],
                     jnp.zeros((w,), f32)])
    w1 = nsa_phi_w1[l].reshape(2, 2, CMP_STRIDE, NSA_HEAD, CMP_HID)
    eye = jnp.eye(NSA_KV, dtype=f32)
    big = jnp.einsum("kljde,kK,gG->jkgdKGle", w1, jnp.eye(2, dtype=f32), eye)
    pe = jnp.broadcast_to(nsa_phi_pe[l].reshape(2, 1, CMP_LEN * NSA_HEAD), (2, 8, CMP_LEN * NSA_HEAD))
    return dict(
        alpha=(2 * depth) ** 0.25,
        wg1=up(ffn_w_gate[l, 0]), wu1=up(ffn_w_up[l, 0]), wd1=down(ffn_w_down[l, 0]),
        wg2=up(ffn_w_gate[l, 1]), wu2=up(ffn_w_up[l, 1]), wd2=down(ffn_w_down[l, 1]),
        ln_g=ln_g[l], ln_b=ln_b[l],
        w_in=cols(w_in[l]).astype(bf16), b_in=cols(b_in[l]).reshape(1, -1),
        rw_mu=rw_mu[l].reshape(1, -1), rw_lora=lora.astype(bf16), rw_vec=vec, rw_hsum=hid.astype(bf16),
        phi_big=big.reshape(CMP_ROW, 2 * NSA_KV * 2 * CMP_HID).astype(bf16), phi_pe=pe,
        phi_w1f=nsa_phi_w1[l].reshape(2, CMP_LEN * NSA_HEAD, CMP_HID).astype(bf16),
        phi_w2=nsa_phi_w2[l].astype(bf16),
        w_out_a=w_out_a[l].astype(bf16), w_out_b=w_out_b[l].astype(bf16), w_o=w_o[l].astype(bf16))


def _kv_state(p):
    return p.reshape(p.shape[:-1] + (2, NSA_KV, NSA_HEAD))


def kernel(x_prompt, x_sample, c_prompt, c_sample, cache_kv_cmp, cache_kv_sel, state_kv_win, state_wkv, state_shift, page_table, w_ada, b_ada, ln_g, ln_b, ffn_w_gate, ffn_w_up, ffn_w_down, w_in, b_in, rw_mu, rw_w0, rw_w2, rw_a0, rw_a2, rw_g2, rw_k_k, rw_k_a, rw_r_k, rw_ln_w, rw_ln_b, nsa_phi_pe, nsa_phi_w1, nsa_phi_w2, w_out_a, w_out_b, w_o):
    bp, seq, d = x_prompt.shape
    bd, t_new, _ = x_sample.shape
    depth = w_ada.shape[0]
    n_phys, page = cache_kv_cmp.shape[1:3]
    n_pages = page_table.shape[1]
    n_win = state_kv_win.shape[2]
    rows_s = bd * t_new
    tm_p = 256
    assert seq % tm_p == 0 and seq % CMP_STRIDE == 0 and page % CMP_STRIDE == 0

    y_p, y_s = x_prompt, x_sample.reshape(1, rows_s, d)
    st_p, st_s = [], []
    for l in range(depth):
        wts = _prep_weights(l, depth, ln_g, ln_b, ffn_w_gate, ffn_w_up, ffn_w_down, w_in, b_in, rw_mu, rw_w0,
                            rw_w2, rw_a0, rw_a2, rw_g2, rw_k_k, rw_k_a, rw_r_k, rw_ln_w, rw_ln_b, nsa_phi_pe,
                            nsa_phi_w1, nsa_phi_w2, w_out_a, w_out_b, w_o)
        mod = _ada_mod(jnp.concatenate([c_prompt, c_sample], axis=0), w_ada[l], b_ada[l])
        mod_p = mod[:bp].reshape(bp, 9, 1, d)
        mod_s = jnp.repeat(mod[bp:].reshape(bd, 9, d), t_new, axis=0).reshape(rows_s, 9, d)
        mod_s = mod_s.transpose(1, 0, 2).reshape(1, 9, rows_s, d)

        x1, p_rw, p_q, p_kc, p_ks, p_kw, p_m, p_g = _ffn_in(y_p, mod_p, wts, tm_p)
        ya, wkv_p = _rwkv(p_rw, jnp.zeros((bp, RW_SHIFT), f32), jnp.zeros((bp, RW_HEADS, RW_HEAD, RW_HEAD), f32),
                          wts)
        kvc_cmp = _compress_dense(p_kc.reshape(bp, seq // CMP_STRIDE, CMP_ROW), wts)
        yb = _nsa_prompt(p_q, p_g, kvc_cmp, p_ks, p_kw)
        y_p = _mix_ffn(x1, ya, yb, p_m, mod_p, wts, tm_p)
        n_keep = min(WINDOW, seq)
        st_p.append((_kv_state(p_kc), _kv_state(p_ks), _kv_state(p_kw[:, seq - n_keep:]), wkv_p, p_rw[:, -1]))

        x1, p_rw, p_q, p_kc, p_ks, p_kw, p_m, p_g = _ffn_in(y_s, mod_s, wts, rows_s)
        per_seq = lambda a: a.reshape(bd, t_new, a.shape[-1])
        p_rw, p_q, p_kc, p_ks, p_kw, p_g = map(per_seq, (p_rw, p_q, p_kc, p_ks, p_kw, p_g))
        ya, wkv_s = _rwkv(p_rw, state_shift[l], state_wkv[l], wts)
        kvc_cmp = _compress_paged(cache_kv_cmp[l].reshape(n_phys, page // CMP_STRIDE, CMP_ROW), page_table, wts)
        win_state = state_kv_win[l].reshape(bd, n_win, KV_COLS)
        yb = _nsa_sample(p_q, p_g, kvc_cmp, cache_kv_sel[l].reshape(n_phys, page, KV_COLS), page_table, p_ks,
                         win_state, p_kw)
        y_s = _mix_ffn(x1, ya.reshape(1, rows_s, RW_WIDTH), yb.reshape(1, rows_s, NSA_WIDTH), p_m, mod_s, wts,
                       rows_s)
        new_win = jnp.concatenate([win_state, p_kw], axis=1)[:, t_new:]
        st_s.append((_kv_state(p_kc), _kv_state(p_ks), _kv_state(new_win), wkv_s, p_rw[:, -1]))

    stack = lambda sts, i: jnp.stack([s[i] for s in sts])
    return (y_p, y_s.reshape(bd, t_new, d)) + tuple(stack(st_p, i) for i in range(5)) + tuple(
        stack(st_s, i) for i in range(5))
```

```python
import functools
import math

import jax
import jax.numpy as jnp
from jax import lax
from jax.experimental import pallas as pl
from jax.experimental.pallas import tpu as pltpu

f32 = jnp.float32
bf16 = jnp.bfloat16

D_MODEL = 1024
RW_HEAD = 64
RW_HEADS = 8
RW_WIDTH = RW_HEADS * RW_HEAD
W_LORA = 64
A_LORA = 64
G_LORA = 128
LORA_IN = W_LORA + A_LORA + G_LORA
RW_GN_EPS = 64e-5
RW_SHIFT = 3 * RW_WIDTH + LORA_IN
NSA_HEAD = 64
NSA_HEADS = 8
NSA_KV = 2
NSA_HPG = NSA_HEADS // NSA_KV
NSA_WIDTH = NSA_HEADS * NSA_HEAD
NSA_KV_COLS = NSA_KV * NSA_HEAD
KV_COLS = 2 * NSA_KV_COLS
NSA_SCALE = NSA_HEAD ** -0.5
CMP_STRIDE = 16
CMP_LEN = 2 * CMP_STRIDE
CMP_HID = 128
CMP_ROW = CMP_STRIDE * KV_COLS
SEL_BLOCK = 64
SEL_TOPK = 16
WINDOW = 512
D_FF = 2816
N_GATE = 3 * NSA_HEADS
LN_EPS = 1e-5
NEG_INF = -1e30
BIG = 1e9

LANES = 128
GATE_PAD = LANES
FF_CHUNK = 256
RW_CHUNK = 64
RW_HPB = 4
Q_TILE = 128
K_TILE = 256
S_KTILE = 512
VMEM_LIMIT = 56 * 1024 * 1024

_NT = (((1,), (1,)), ((), ()))
_TN = (((0,), (0,)), ((), ()))
_HI = lax.Precision.HIGHEST


def _dot(a, b):
    return jnp.dot(a.astype(bf16), b.astype(bf16), preferred_element_type=f32)


def _dot_nt(a, b):
    return lax.dot_general(a.astype(bf16), b.astype(bf16), _NT, preferred_element_type=f32)


def _dot_f32(a, b, dims=None):
    if dims is None:
        return jnp.dot(a, b, preferred_element_type=f32, precision=_HI)
    return lax.dot_general(a, b, dims, preferred_element_type=f32, precision=_HI)


def _dot_sel(x, m01):
    hi = x.astype(bf16)
    r1 = x - hi.astype(f32)
    mid = r1.astype(bf16)
    lo = (r1 - mid.astype(f32)).astype(bf16)
    d = lambda t: jnp.dot(t, m01, preferred_element_type=f32)
    return d(hi) + d(mid) + d(lo)


def _layer_norm(x, g, b):
    mu = jnp.mean(x, axis=-1, keepdims=True)
    xc = x - mu
    var = jnp.mean(xc * xc, axis=-1, keepdims=True)
    return xc * lax.rsqrt(var + LN_EPS) * g + b


def _cparams(sem):
    return pltpu.CompilerParams(dimension_semantics=sem, vmem_limit_bytes=VMEM_LIMIT)


def _resident(shape):
    nd = len(shape)
    return pl.BlockSpec(shape, lambda *_: (0,) * nd, pipeline_mode=pl.Buffered(1))


def _ada_kernel(c_ref, w_ref, b_ref, o_ref):
    c = c_ref[...]
    o_ref[...] = _dot(c * jax.nn.sigmoid(c), w_ref[...]) + b_ref[...]


def _ada_mod(c, w_ada, b_ada):
    rows, d = c.shape
    n = w_ada.shape[1]
    tn = d
    return pl.pallas_call(
        _ada_kernel,
        out_shape=jax.ShapeDtypeStruct((rows, n), f32),
        grid=(n // tn,),
        in_specs=[pl.BlockSpec((rows, d), lambda j: (0, 0)),
                  pl.BlockSpec((d, tn), lambda j: (0, j)),
                  pl.BlockSpec((1, tn), lambda j: (0, j))],
        out_specs=pl.BlockSpec((rows, tn), lambda j: (0, j)),
        compiler_params=_cparams(("parallel",)),
        name="ada_mod",
    )(c, w_ada, b_ada.reshape(1, n))


def _ffn(u, wg_ref, wu_ref, wd_ref, acc_ref):
    ub = u.astype(bf16)
    acc_ref[...] = jnp.zeros(acc_ref.shape, f32)

    def body(c, carry):
        hg = jnp.dot(ub, wg_ref[c], preferred_element_type=f32)
        hu = jnp.dot(ub, wu_ref[c], preferred_element_type=f32)
        h = hg * jax.nn.sigmoid(hg) * hu
        acc_ref[...] += jnp.dot(h.astype(bf16), wd_ref[c], preferred_element_type=f32)
        return carry

    lax.fori_loop(0, wg_ref.shape[0], body, 0)
    return acc_ref[...]


def _ffn_in_kernel(alpha, seg, seg_t, x_ref, mod_ref, wg_ref, wu_ref, wd_ref, lng_ref, lnb_ref, win_ref, bin_ref,
                   wint_ref, bint_ref, x1_ref, *rest):
    out_refs, acc_ref = rest[:-1], rest[-1]
    x = x_ref[...]
    u = x * (1.0 + mod_ref[1]) + mod_ref[0]
    f = _ffn(u, wg_ref, wu_ref, wd_ref, acc_ref)
    x1 = _layer_norm(alpha * x + 0.5 * (1.0 + mod_ref[2]) * f, lng_ref[0:1, :], lnb_ref[0:1, :])
    x1_ref[...] = x1
    u2 = (x1 * (1.0 + mod_ref[4]) + mod_ref[3]).astype(bf16)
    for o_ref, (lo, hi) in zip(out_refs, seg):
        o_ref[...] = jnp.dot(u2, win_ref[:, lo:hi], preferred_element_type=f32) + bin_ref[:, lo:hi]
    for o_ref, (lo, hi) in zip(out_refs[len(seg):], seg_t):
        bias = _lanes(bint_ref[lo:hi, :], u2.shape[0] // LANES)
        o_ref[...] = lax.dot_general(wint_ref[lo:hi, :], u2, _NT, preferred_element_type=f32) + bias


_PROJ = (("rw", RW_SHIFT), ("q", NSA_WIDTH), ("kc", KV_COLS), ("ks", KV_COLS), ("kw", KV_COLS),
         ("merge", 2 * D_MODEL), ("gate", GATE_PAD))
_PROJ_T = (("q", NSA_WIDTH), ("gate", GATE_PAD), ("ks", KV_COLS), ("kw", KV_COLS))


def _segments(table, names):
    seg, lo = {}, 0
    for name, w in table:
        seg[name] = (lo, lo + w)
        lo += w
    return tuple(seg[n] for n in names)


def _ffn_in(x, mod, wts, tm, names, names_t):
    g, t, d = x.shape
    r = mod.shape[2]
    seg, seg_t = _segments(_PROJ, names), _segments(_PROJ_T, names_t)
    row = lambda w: pl.BlockSpec((None, tm, w), lambda i, j: (i, j, 0))
    col = lambda w: pl.BlockSpec((None, w, tm), lambda i, j: (i, 0, j))
    return pl.pallas_call(
        functools.partial(_ffn_in_kernel, wts["alpha"], seg, seg_t),
        out_shape=([jax.ShapeDtypeStruct((g, t, d), f32)]
                   + [jax.ShapeDtypeStruct((g, t, hi - lo), f32) for lo, hi in seg]
                   + [jax.ShapeDtypeStruct((g, hi - lo, t), f32) for lo, hi in seg_t]),
        grid=(g, t // tm),
        in_specs=[row(d),
                  pl.BlockSpec((None, 9, r, d), lambda i, j: (i, 0, 0, 0)),
                  _resident(wts["wg1"].shape), _resident(wts["wu1"].shape), _resident(wts["wd1"].shape),
                  _resident(wts["ln_g"].shape), _resident(wts["ln_b"].shape),
                  _resident(wts["w_in"].shape), _resident(wts["b_in"].shape),
                  _resident(wts["w_in_t"].shape), _resident(wts["b_in_t"].shape)],
        out_specs=[row(d)] + [row(hi - lo) for lo, hi in seg] + [col(hi - lo) for lo, hi in seg_t],
        scratch_shapes=[pltpu.VMEM((tm, d), f32)],
        compiler_params=_cparams(("parallel", "parallel")),
        name="ffn_in",
    )(x, mod, wts["wg1"], wts["wu1"], wts["wd1"], wts["ln_g"], wts["ln_b"], wts["w_in"], wts["b_in"],
      wts["w_in_t"], wts["b_in_t"])


def _mix_ffn_kernel(alpha, yb_on_lanes, x1_ref, ya_ref, yb_ref, pm_ref, mod_ref, woa_ref, wob_ref, wo_ref,
                    wg_ref, wu_ref, wd_ref, lng_ref, lnb_ref, o_ref, acc_ref):
    x1 = x1_ref[...]
    pm = pm_ref[...]
    d = x1.shape[-1]
    ga = jax.nn.sigmoid(pm[:, :d])
    gb = jax.nn.sigmoid(pm[:, d:])
    if yb_on_lanes:
        yb_out = lax.dot_general(yb_ref[...].astype(bf16), wob_ref[...], _TN, preferred_element_type=f32)
    else:
        yb_out = _dot(yb_ref[...], wob_ref[...])
    merged = ga * _dot(ya_ref[...], woa_ref[...]) + gb * yb_out
    m = _dot(merged, wo_ref[...])
    x2 = _layer_norm(alpha * x1 + (1.0 + mod_ref[5]) * m, lng_ref[1:2, :], lnb_ref[1:2, :])
    u = x2 * (1.0 + mod_ref[7]) + mod_ref[6]
    f = _ffn(u, wg_ref, wu_ref, wd_ref, acc_ref)
    o_ref[...] = _layer_norm(alpha * x2 + 0.5 * (1.0 + mod_ref[8]) * f, lng_ref[2:3, :], lnb_ref[2:3, :])


def _mix_ffn(x1, ya, yb, pm, mod, wts, tm, yb_on_lanes):
    g, t, d = x1.shape
    r = mod.shape[2]
    row = lambda w: pl.BlockSpec((None, tm, w), lambda i, j: (i, j, 0))
    yb_spec = pl.BlockSpec((None, NSA_WIDTH, tm), lambda i, j: (i, 0, j)) if yb_on_lanes else row(NSA_WIDTH)
    return pl.pallas_call(
        functools.partial(_mix_ffn_kernel, wts["alpha"], yb_on_lanes),
        out_shape=jax.ShapeDtypeStruct((g, t, d), f32),
        grid=(g, t // tm),
        in_specs=[row(d), row(ya.shape[-1]), yb_spec, row(pm.shape[-1]),
                  pl.BlockSpec((None, 9, r, d), lambda i, j: (i, 0, 0, 0)),
                  _resident(wts["w_out_a"].shape), _resident(wts["w_out_b"].shape), _resident(wts["w_o"].shape),
                  _resident(wts["wg2"].shape), _resident(wts["wu2"].shape), _resident(wts["wd2"].shape),
                  _resident(wts["ln_g"].shape), _resident(wts["ln_b"].shape)],
        out_specs=row(d),
        scratch_shapes=[pltpu.VMEM((tm, d), f32)],
        compiler_params=_cparams(("parallel", "parallel")),
        name="mix_ffn",
    )(x1, ya, yb, pm, mod, wts["w_out_a"], wts["w_out_b"], wts["w_o"], wts["wg2"], wts["wu2"], wts["wd2"],
      wts["ln_g"], wts["ln_b"])


def _rwkv_kernel(t_blk, p_ref, prev_ref, s0_ref, mu_ref, wl_ref, vec_ref, hsum_ref,
                 y_ref, sout_ref, state_ref, prevrow_ref, stage_ref, yn_ref):
    c = stage_ref.shape[0]
    w = RW_WIDTH
    ci = pl.program_id(1)

    @pl.when(ci == 0)
    def _():
        state_ref[...] = jnp.zeros(state_ref.shape, f32)
        for h in range(RW_HEADS):
            d0 = (h % RW_HPB) * RW_HEAD
            state_ref[h // RW_HPB, d0:d0 + RW_HEAD, d0:d0 + RW_HEAD] = s0_ref[h]
        prevrow_ref[...] = prev_ref[...]

    if t_blk < c:
        stage_ref[...] = jnp.zeros(stage_ref.shape, f32)
        stage_ref[0:t_blk, :] = p_ref[...]
        p = stage_ref[...]
    else:
        p = p_ref[...]
    ridx = lax.broadcasted_iota(jnp.int32, (c, 1), 0)
    valid = ridx < t_blk
    p_prev = jnp.where(ridx == 0, prevrow_ref[...], pltpu.roll(p, 1, 0))
    prevrow_ref[...] = p[c - 1:c, :]
    xs = p + (p_prev - p) * mu_ref[...]

    tail = xs[:, 3 * w:]
    li = lax.broadcasted_iota(jnp.int32, tail.shape, 1)
    act = jnp.where(li < W_LORA, jnp.tanh(tail),
                    jnp.where(li < W_LORA + A_LORA, tail, jax.nn.sigmoid(tail)))
    lora = _dot(act, wl_ref[...])
    w0, a0, k_k, k_a, r_k, ln_w, ln_b = (vec_ref[i:i + 1, :] for i in range(7))
    z = -(w0 + lora[:, :w])
    softplus = jnp.maximum(z, 0.0) + jnp.log1p(jnp.exp(-jnp.abs(z)))
    lw = -jnp.exp(-softplus - 0.5)
    a = jax.nn.sigmoid(a0 + lora[:, w:2 * w])
    g = lora[:, 2 * w:]
    r, k, v = xs[:, :w], xs[:, w:2 * w], xs[:, 2 * w:3 * w]
    hsum = hsum_ref[...]
    kk = k * k_k
    kk = kk * lax.rsqrt(jnp.maximum(_dot_sel(kk * kk, hsum), 1e-24))
    k2 = k * (1.0 + (a - 1.0) * k_a)
    bonus = _dot_sel(r * k2 * r_k, hsum) * v
    if t_blk < c:
        lw = jnp.where(valid, lw, 0.0)
        kk = jnp.where(valid, kk, 0.0)
        k2 = jnp.where(valid, k2, 0.0)
        v = jnp.where(valid, v, 0.0)
    b = kk * a

    rr = lax.broadcasted_iota(jnp.int32, (c, c), 0)
    cc = lax.broadcasted_iota(jnp.int32, (c, c), 1)
    cum = _dot_f32(jnp.where(rr >= cc, 1.0, 0.0).astype(f32), lw)
    cum_last = cum[c - 1:c, :]
    e_neg = jnp.exp(-cum)
    e_rem = jnp.exp(cum_last - cum)
    at = -kk * jnp.exp(cum - lw)
    rt = r * jnp.exp(cum)
    bt, kt = b * e_neg, k2 * e_neg
    bh, kh = b * e_rem, k2 * e_rem
    p_last = jnp.exp(cum_last)

    gw = RW_HPB * RW_HEAD
    rows = RW_HPB * c
    br = lax.broadcasted_iota(jnp.int32, (rows, gw), 0)
    bc = lax.broadcasted_iota(jnp.int32, (rows, gw), 1)
    same_head = (br // c) == (bc // RW_HEAD)
    tr = lax.broadcasted_iota(jnp.int32, (rows, rows), 0) % c
    tc = lax.broadcasted_iota(jnp.int32, (rows, rows), 1) % c
    strict = tr > tc
    incl = tr >= tc
    blocks = lambda x: jnp.where(same_head, jnp.concatenate([x] * RW_HPB, axis=0), 0.0).astype(bf16)
    nt = lambda x, y: lax.dot_general(x, y, _NT, preferred_element_type=f32)
    tn = lambda x, y: lax.dot_general(x, y, _TN, preferred_element_type=f32)
    mm = lambda x, y: jnp.dot(x, y, preferred_element_type=f32)
    n_fac = max(1, (c - 1).bit_length())

    for grp in range(RW_HEADS // RW_HPB):
        gl = slice(grp * gw, (grp + 1) * gw)
        at4, rt4, bt4, kt4, bh4, kh4, v4 = (blocks(x[:, gl]) for x in (at, rt, bt, kt, bh, kh, v))
        s_bd = state_ref[grp]
        s_b = s_bd.astype(bf16)
        low = jnp.where(strict, nt(at4, bt4), 0.0).astype(bf16)
        u = nt(at4, s_b) + mm(jnp.where(strict, nt(at4, kt4), 0.0).astype(bf16), v4)
        for i in range(n_fac):
            u = u + mm(low, u.astype(bf16))
            if i + 1 < n_fac:
                low = mm(low, low).astype(bf16)
        u_b = u.astype(bf16)
        y4 = (nt(rt4, s_b) + mm(jnp.where(incl, nt(rt4, bt4), 0.0).astype(bf16), u_b)
              + mm(jnp.where(incl, nt(rt4, kt4), 0.0).astype(bf16), v4))
        state_ref[grp] = s_bd * p_last[:, gl] + tn(u_b, bh4) + tn(v4, kh4)
        yg = y4[0:c]
        for hh in range(1, RW_HPB):
            yg = yg + y4[hh * c:(hh + 1) * c]
        yn_ref[:, gl] = yg

    yh = yn_ref[...]
    mean = _dot_sel(yh, hsum) * (1.0 / RW_HEAD)
    yc = yh - mean
    var = _dot_sel(yc * yc, hsum) * (1.0 / RW_HEAD)
    y = (yc * lax.rsqrt(var + RW_GN_EPS) * ln_w + ln_b + bonus) * g
    y_ref[...] = y[0:t_blk, :] if t_blk < c else y

    @pl.when(ci == pl.num_programs(1) - 1)
    def _():
        for h in range(RW_HEADS):
            d0 = (h % RW_HPB) * RW_HEAD
            sout_ref[h] = state_ref[h // RW_HPB, d0:d0 + RW_HEAD, d0:d0 + RW_HEAD]


def _rwkv(p_rw, prev, s0, wts):
    g, t, ws = p_rw.shape
    c = RW_CHUNK
    t_blk = min(c, t)
    assert t % t_blk == 0
    return pl.pallas_call(
        functools.partial(_rwkv_kernel, t_blk),
        out_shape=[jax.ShapeDtypeStruct((g, t, RW_WIDTH), f32), jax.ShapeDtypeStruct(s0.shape, f32)],
        grid=(g, t // t_blk),
        in_specs=[pl.BlockSpec((None, t_blk, ws), lambda i, j: (i, j, 0)),
                  pl.BlockSpec((None, 1, ws), lambda i, j: (i, 0, 0)),
                  pl.BlockSpec((None,) + s0.shape[1:], lambda i, j: (i, 0, 0, 0)),
                  _resident(wts["rw_mu"].shape), _resident(wts["rw_lora"].shape),
                  _resident(wts["rw_vec"].shape), _resident(wts["rw_hsum"].shape)],
        out_specs=[pl.BlockSpec((None, t_blk, RW_WIDTH), lambda i, j: (i, j, 0)),
                   pl.BlockSpec((None,) + s0.shape[1:], lambda i, j: (i, 0, 0, 0))],
        scratch_shapes=[pltpu.VMEM((RW_HEADS // RW_HPB, RW_HPB * RW_HEAD, RW_HPB * RW_HEAD), f32),
                        pltpu.VMEM((1, ws), f32),
                        pltpu.VMEM((c, ws), f32), pltpu.VMEM((c, RW_WIDTH), f32)],
        compiler_params=_cparams(("parallel", "arbitrary")),
        name="rwkv",
    )(p_rw, prev.reshape(g, 1, ws), s0, wts["rw_mu"], wts["rw_lora"], wts["rw_vec"], wts["rw_hsum"])


def _gelu_tanh(x):
    return x * (0.5 * (1.0 + jnp.tanh(math.sqrt(2.0 / math.pi) * (x + 0.044715 * (x * x * x)))))


def _compress_rows(x, wbig_ref, pe_ref, w1f_ref, w2_ref, o_ref, w2t_ref=None, ot_ref=None):
    n = x.shape[0]
    hid = _dot(x, wbig_ref[...])
    for kv in range(2):
        pe_term = _dot(pe_ref[kv], w1f_ref[kv])[0:1, :]
        for grp in range(NSA_KV):
            base = (kv * NSA_KV + grp) * 2 * CMP_HID
            lo = hid[:, base:base + CMP_HID]
            hi = pltpu.roll(hid[:, base + CMP_HID:base + 2 * CMP_HID], n - 1, 0)
            act = _gelu_tanh(lo + hi + pe_term).astype(bf16)
            col = (kv * NSA_KV + grp) * NSA_HEAD
            o_ref[:, col:col + NSA_HEAD] = jnp.dot(act, w2_ref[kv], preferred_element_type=f32)
            if ot_ref is not None:
                ot_ref[col:col + NSA_HEAD, :] = lax.dot_general(w2t_ref[kv], act, _NT, preferred_element_type=f32)


def _compress_dense_kernel(x_ref, wbig_ref, pe_ref, w1f_ref, w2_ref, w2t_ref, o_ref, ot_ref):
    _compress_rows(x_ref[...], wbig_ref, pe_ref, w1f_ref, w2_ref, o_ref, w2t_ref, ot_ref)


def _compress_dense(x, wts):
    b, n, _ = x.shape
    return pl.pallas_call(
        _compress_dense_kernel,
        out_shape=[jax.ShapeDtypeStruct((b, n, KV_COLS), f32), jax.ShapeDtypeStruct((b, KV_COLS, n), f32)],
        grid=(b,),
        in_specs=[pl.BlockSpec((None, n, CMP_ROW), lambda i: (i, 0, 0)),
                  _resident(wts["phi_big"].shape), _resident(wts["phi_pe"].shape),
                  _resident(wts["phi_w1f"].shape), _resident(wts["phi_w2"].shape),
                  _resident(wts["phi_w2t"].shape)],
        out_specs=[pl.BlockSpec((None, n, KV_COLS), lambda i: (i, 0, 0)),
                   pl.BlockSpec((None, KV_COLS, n), lambda i: (i, 0, 0))],
        compiler_params=_cparams(("parallel",)),
        name="compress_dense",
    )(x, wts["phi_big"], wts["phi_pe"], wts["phi_w1f"], wts["phi_w2"], wts["phi_w2t"])


def _page_copy(cache_ref, buf_ref, sem, page, i, rows):
    return pltpu.make_async_copy(cache_ref.at[page], buf_ref.at[pl.ds(i * rows, rows)], sem)


def _gather_pages(pt_ref, b, cache_ref, buf_ref, sem, n_pages, rows):
    def start(i, carry):
        _page_copy(cache_ref, buf_ref, sem, pt_ref[b, i], i, rows).start()
        return carry

    def wait(i, carry):
        _page_copy(cache_ref, buf_ref, sem, pt_ref[b, i], i, rows).wait()
        return carry

    lax.fori_loop(0, n_pages, start, 0)
    lax.fori_loop(0, n_pages, wait, 0)


def _compress_paged_kernel(pt_ref, cache_ref, wbig_ref, pe_ref, w1f_ref, w2_ref, o_ref, xbuf_ref, sem):
    n_pages = pt_ref.shape[1]
    rows = cache_ref.shape[1]
    _gather_pages(pt_ref, pl.program_id(0), cache_ref, xbuf_ref, sem, n_pages, rows)
    _compress_rows(xbuf_ref[...], wbig_ref, pe_ref, w1f_ref, w2_ref, o_ref)


def _compress_paged(cache, page_table, wts):
    b, n_pages = page_table.shape
    rows = cache.shape[1]
    n = n_pages * rows
    full = lambda a: pl.BlockSpec(a.shape, lambda i, pt: (0,) * a.ndim, pipeline_mode=pl.Buffered(1))
    return pl.pallas_call(
        _compress_paged_kernel,
        out_shape=jax.ShapeDtypeStruct((b, n, KV_COLS), f32),
        grid_spec=pltpu.PrefetchScalarGridSpec(
            num_scalar_prefetch=1, grid=(b,),
            in_specs=[pl.BlockSpec(memory_space=pl.ANY), full(wts["phi_big"]), full(wts["phi_pe"]),
                      full(wts["phi_w1f"]), full(wts["phi_w2"])],
            out_specs=pl.BlockSpec((None, n, KV_COLS), lambda i, pt: (i, 0, 0)),
            scratch_shapes=[pltpu.VMEM((n, CMP_ROW), f32), pltpu.SemaphoreType.DMA(())]),
        compiler_params=_cparams(("arbitrary",)),
        name="compress_paged",
    )(page_table, cache, wts["phi_big"], wts["phi_pe"], wts["phi_w1f"], wts["phi_w2"])


def _softmax_rows(s, mask):
    s = jnp.where(mask, s, NEG_INF)
    e = jnp.where(mask, jnp.exp(s - jnp.max(s, axis=-1, keepdims=True)), 0.0)
    return e / jnp.maximum(jnp.sum(e, axis=-1, keepdims=True), 1e-30)


def _flash_step(q, k, v, mask, m, l, acc):
    s = jnp.where(mask, _dot_nt(q, k), NEG_INF)
    m_new = jnp.maximum(m, jnp.max(s, axis=-1, keepdims=True))
    alpha = jnp.exp(m - m_new)
    e = jnp.where(mask, jnp.exp(s - m_new), 0.0)
    return m_new, alpha * l + jnp.sum(e, axis=-1, keepdims=True), alpha * acc + _dot(e, v)


def _flash_init(rows):
    return (jnp.full((rows, 1), NEG_INF, f32), jnp.zeros((rows, 1), f32), jnp.zeros((rows, NSA_HEAD), f32))


def _flash_out(m, l, acc):
    return acc / jnp.maximum(l, 1e-30)


def _importance_matrix(n_cmp_rows, n_sel_cols):
    ratio = SEL_BLOCK // CMP_STRIDE
    i = lax.broadcasted_iota(jnp.int32, (n_cmp_rows, n_sel_cols), 0)
    j = lax.broadcasted_iota(jnp.int32, (n_cmp_rows, n_sel_cols), 1)
    return jnp.where((i >= ratio * j - 1) & (i <= ratio * j + ratio - 1), 1.0, 0.0).astype(bf16)


def _block_scores(imp, j, cur):
    forced = (j == 0) | (j == cur) | (j == cur - 1)
    return jnp.where(j <= cur, jnp.where(forced, BIG, imp), -BIG)


def _stack_heads(x, grp, rows):
    return jnp.concatenate(
        [x[:, (grp * NSA_HPG + h) * NSA_HEAD:(grp * NSA_HPG + h + 1) * NSA_HEAD] for h in range(NSA_HPG)], axis=0)


def _stack_gates(gates, grp, branch):
    return jnp.concatenate(
        [gates[:, 3 * (grp * NSA_HPG + h) + branch:3 * (grp * NSA_HPG + h) + branch + 1] for h in range(NSA_HPG)],
        axis=0)


def _flash_step_t(qt, k, vt, mask, m, l, acc):
    s = jnp.where(mask, jnp.dot(k.astype(bf16), qt, preferred_element_type=f32), NEG_INF)
    m_new = jnp.maximum(m, jnp.max(s, axis=0, keepdims=True))
    alpha = jnp.exp(m - m_new)
    e = jnp.where(mask, jnp.exp(s - m_new), 0.0)
    return (m_new, alpha * l + jnp.sum(e, axis=0, keepdims=True),
            alpha * acc + jnp.dot(vt.astype(bf16), e.astype(bf16), preferred_element_type=f32))


def _flash_init_t(cols):
    return (jnp.full((1, cols), NEG_INF, f32), jnp.zeros((1, cols), f32), jnp.zeros((NSA_HEAD, cols), f32))


def _lanes(x, n):
    return jnp.concatenate([x] * n, axis=1)


def _nsa_prompt_kernel(qt_ref, gt_ref, kc_ref, vct_ref, ks_ref, vst_ref, kw_ref, vwt_ref, o_ref):
    tq = qt_ref.shape[1]
    n_cmp = kc_ref.shape[0]
    n_sel = ks_ref.shape[0] // SEL_BLOCK
    i = pl.program_id(1)
    t0 = i * tq
    cols = NSA_HPG * tq
    gates = jax.nn.sigmoid(gt_ref[...])
    q_pos = t0 + lax.broadcasted_iota(jnp.int32, (1, tq), 1)
    q_pos_c = _lanes(q_pos, NSA_HPG)
    cmp_end = lax.broadcasted_iota(jnp.int32, (n_cmp, 1), 0) * CMP_STRIDE + (CMP_LEN - 1)
    ratio = SEL_BLOCK // CMP_STRIDE
    ij = lax.broadcasted_iota(jnp.int32, (n_sel, n_cmp), 0)
    ii = lax.broadcasted_iota(jnp.int32, (n_sel, n_cmp), 1)
    imp_mat = jnp.where((ii >= ratio * ij - 1) & (ii <= ratio * ij + ratio - 1), 1.0, 0.0).astype(bf16)
    jt = lax.broadcasted_iota(jnp.int32, (n_sel, tq), 0)
    cur_t = (t0 + lax.broadcasted_iota(jnp.int32, (n_sel, tq), 1)) // SEL_BLOCK
    krow = lax.broadcasted_iota(jnp.int32, (K_TILE, 1), 0)
    er = lax.broadcasted_iota(jnp.int32, (K_TILE, n_sel), 0) // SEL_BLOCK
    ej = lax.broadcasted_iota(jnp.int32, (K_TILE, n_sel), 1)
    groups = range(NSA_KV)
    heads_of = lambda grp: range(grp * NSA_HPG, (grp + 1) * NSA_HPG)
    qts, sels, o_cs = [], [], []

    for grp in groups:
        d0 = grp * NSA_HEAD
        heads = heads_of(grp)
        qt = jnp.concatenate([qt_ref[h * NSA_HEAD:(h + 1) * NSA_HEAD, :] for h in heads], axis=1)
        qt = (qt * NSA_SCALE).astype(bf16)
        mask_c = cmp_end <= q_pos_c
        s = jnp.where(mask_c, jnp.dot(kc_ref[:, d0:d0 + NSA_HEAD].astype(bf16), qt, preferred_element_type=f32),
                      NEG_INF)
        e = jnp.where(mask_c, jnp.exp(s - jnp.max(s, axis=0, keepdims=True)), 0.0)
        p_c = e / jnp.maximum(jnp.sum(e, axis=0, keepdims=True), 1e-30)
        o_c = jnp.dot(vct_ref[d0:d0 + NSA_HEAD, :].astype(bf16), p_c.astype(bf16), preferred_element_type=f32)
        p_sum = p_c[:, 0:tq]
        for h in range(1, NSA_HPG):
            p_sum = p_sum + p_c[:, h * tq:(h + 1) * tq]
        hi = p_sum.astype(bf16)
        r1 = p_sum - hi.astype(f32)
        mid = r1.astype(bf16)
        lo = (r1 - mid.astype(f32)).astype(bf16)
        imp = sum(jnp.dot(imp_mat, part, preferred_element_type=f32) for part in (hi, mid, lo))
        score = _block_scores(imp, jt, cur_t)
        rank = jnp.zeros((n_sel, tq), f32)
        for j2 in range(n_sel):
            row = score[j2:j2 + 1, :]
            ge = jnp.where(row >= score, 1.0, 0.0)
            gt = jnp.where(row > score, 1.0, 0.0)
            rank = rank + jnp.where(jt > j2, ge, gt)
        qts.append(qt)
        sels.append(jnp.where(rank < SEL_TOPK, 1.0, 0.0).astype(bf16))
        o_cs.append(o_c)

    def sel_step(kt, carry):
        k0 = pl.multiple_of(kt * K_TILE, K_TILE)
        expand = jnp.where(ej == kt * (K_TILE // SEL_BLOCK) + er, 1.0, 0.0).astype(bf16)
        causal = (k0 + krow) <= q_pos
        out = ()
        for grp in groups:
            d0 = grp * NSA_HEAD
            chosen = jnp.where(causal, jnp.dot(expand, sels[grp], preferred_element_type=f32), 0.0)
            out += _flash_step_t(qts[grp], ks_ref[pl.ds(k0, K_TILE), d0:d0 + NSA_HEAD],
                                 vst_ref[d0:d0 + NSA_HEAD, pl.ds(k0, K_TILE)], _lanes(chosen, NSA_HPG) > 0.5,
                                 *carry[3 * grp:3 * grp + 3])
        return out

    last = (t0 + tq - 1) // K_TILE
    sel_acc = lax.fori_loop(0, last + 1, sel_step, _flash_init_t(cols) * NSA_KV)

    def win_step(kt, carry):
        k0 = pl.multiple_of(kt * K_TILE, K_TILE)
        dist = q_pos_c - (k0 + krow)
        mask = (dist >= 0) & (dist < WINDOW)
        out = ()
        for grp in groups:
            d0 = grp * NSA_HEAD
            out += _flash_step_t(qts[grp], kw_ref[pl.ds(k0, K_TILE), d0:d0 + NSA_HEAD],
                                 vwt_ref[d0:d0 + NSA_HEAD, pl.ds(k0, K_TILE)], mask, *carry[3 * grp:3 * grp + 3])
        return out

    first = jnp.maximum(t0 - WINDOW, 0) // K_TILE
    win_acc = lax.fori_loop(first, last + 1, win_step, _flash_init_t(cols) * NSA_KV)

    for grp in groups:
        heads = heads_of(grp)
        o_s = _flash_out(*sel_acc[3 * grp:3 * grp + 3])
        o_w = _flash_out(*win_acc[3 * grp:3 * grp + 3])
        gate = lambda br: jnp.concatenate([gates[3 * h + br:3 * h + br + 1, :] for h in heads], axis=1)
        o = gate(0) * o_cs[grp] + gate(1) * o_s + gate(2) * o_w
        for n, h in enumerate(heads):
            o_ref[h * NSA_HEAD:(h + 1) * NSA_HEAD, :] = o[:, n * tq:(n + 1) * tq]


def _nsa_prompt(qt, gt, kvc, kvct, kvs, kvst, kvw, kvwt):
    b, _, t = qt.shape
    n_cmp = kvc.shape[1]
    tq = Q_TILE
    assert t % tq == 0 and t % K_TILE == 0 and K_TILE % SEL_BLOCK == 0
    keys =lambda r: pl.BlockSpec((None, r, NSA_KV_COLS), lambda i, j: (i, 0, 0))
    vals = lambda c: pl.BlockSpec((None, NSA_KV_COLS, c), lambda i, j: (i, 1, 0))
    return pl.pallas_call(
        _nsa_prompt_kernel,
        out_shape=jax.ShapeDtypeStruct((b, NSA_WIDTH, t), f32),
        grid=(b, t // tq),
        in_specs=[pl.BlockSpec((None, NSA_WIDTH, tq), lambda i, j: (i, 0, j)),
                  pl.BlockSpec((None, GATE_PAD, tq), lambda i, j: (i, 0, j)),
                  keys(n_cmp), vals(n_cmp), keys(t), vals(t), keys(t), vals(t)],
        out_specs=pl.BlockSpec((None, NSA_WIDTH, tq), lambda i, j: (i, 0, j)),
        compiler_params=_cparams(("parallel", "arbitrary")),
        name="nsa_prompt",
    )(qt, gt, kvc, kvct, kvs, kvst, kvw, kvwt)


def _nsa_sample_kernel(past_len, pt_ref, q_ref, gate_ref, kvc_ref, cache_ref, kvs_new_ref, win_ref, kvw_new_ref,
                       o_ref, selbuf_ref, winbuf_ref, qbuf_ref, gbuf_ref, obuf_ref, sem):
    t_new = q_ref.shape[0]
    tp = qbuf_ref.shape[0]
    n_pages = pt_ref.shape[1]
    page = cache_ref.shape[1]
    n_cmp = kvc_ref.shape[0]
    n_win = win_ref.shape[0]
    total = past_len + t_new
    n_sel = -(-total // SEL_BLOCK)
    n_sel_pad = -(-n_sel // LANES) * LANES
    n_kt = selbuf_ref.shape[0] // S_KTILE
    rows = NSA_HPG * tp

    selbuf_ref[pl.ds(past_len, selbuf_ref.shape[0] - past_len), :] = jnp.zeros(
        (selbuf_ref.shape[0] - past_len, KV_COLS), f32)
    selbuf_ref[pl.ds(past_len, t_new), :] = kvs_new_ref[...]
    _gather_pages(pt_ref, pl.program_id(0), cache_ref, selbuf_ref, sem, n_pages, page)
    winbuf_ref[...] = jnp.zeros(winbuf_ref.shape, f32)
    winbuf_ref[0:n_win, :] = win_ref[...]
    winbuf_ref[n_win:n_win + t_new, :] = kvw_new_ref[...]
    qbuf_ref[...] = jnp.zeros(qbuf_ref.shape, f32)
    qbuf_ref[0:t_new, :] = q_ref[...]
    gbuf_ref[...] = jnp.zeros(gbuf_ref.shape, f32)
    gbuf_ref[0:t_new, :] = gate_ref[...]

    q = qbuf_ref[...] * NSA_SCALE
    gates = jax.nn.sigmoid(gbuf_ref[...])
    q_pos = past_len + lax.broadcasted_iota(jnp.int32, (tp, 1), 0)
    q_pos_r = jnp.concatenate([q_pos] * NSA_HPG, axis=0)
    cmp_end = lax.broadcasted_iota(jnp.int32, (1, n_cmp), 1) * CMP_STRIDE + (CMP_LEN - 1)
    imp_mat = _importance_matrix(n_cmp, n_sel_pad)
    jj = lax.broadcasted_iota(jnp.int32, (tp, n_sel_pad), 1)
    kcol = lax.broadcasted_iota(jnp.int32, (1, S_KTILE), 1)
    ej = lax.broadcasted_iota(jnp.int32, (n_sel_pad, S_KTILE), 0)
    ec = lax.broadcasted_iota(jnp.int32, (n_sel_pad, S_KTILE), 1) // SEL_BLOCK

    for grp in range(NSA_KV):
        kcol0, vcol0 = grp * NSA_HEAD, NSA_KV_COLS + grp * NSA_HEAD
        qs = _stack_heads(q, grp, tp).astype(bf16)
        p_c = _softmax_rows(_dot_nt(qs, kvc_ref[:, kcol0:kcol0 + NSA_HEAD]), cmp_end <= q_pos_r)
        o_c = _dot(p_c, kvc_ref[:, vcol0:vcol0 + NSA_HEAD])
        p_sum = p_c[0:tp]
        for h in range(1, NSA_HPG):
            p_sum = p_sum + p_c[h * tp:(h + 1) * tp]
        score = _block_scores(_dot_sel(p_sum, imp_mat), jj, q_pos // SEL_BLOCK)
        rank = jnp.zeros((tp, n_sel_pad), f32)
        for j2 in range(n_sel):
            col = score[:, j2:j2 + 1]
            ge = jnp.where(col >= score, 1.0, 0.0)
            gt = jnp.where(col > score, 1.0, 0.0)
            rank = rank + jnp.where(jj > j2, ge, gt)
        sel = jnp.where((rank < SEL_TOPK) & (jj < n_sel), 1.0, 0.0).astype(bf16)
        sel = jnp.concatenate([sel] * NSA_HPG, axis=0)

        def sel_step(kt, carry):
            k0 = pl.multiple_of(kt * S_KTILE, S_KTILE)
            expand = jnp.where(ej == kt * (S_KTILE // SEL_BLOCK) + ec, 1.0, 0.0).astype(bf16)
            chosen = (jnp.dot(sel, expand, preferred_element_type=f32) > 0.5) & ((k0 + kcol) <= q_pos_r)
            return _flash_step(qs, selbuf_ref[pl.ds(k0, S_KTILE), kcol0:kcol0 + NSA_HEAD],
                               selbuf_ref[pl.ds(k0, S_KTILE), vcol0:vcol0 + NSA_HEAD], chosen, *carry)

        o_s = _flash_out(*lax.fori_loop(0, n_kt, sel_step, _flash_init(rows)))

        carry = _flash_init(rows)
        for kt in range(winbuf_ref.shape[0] // S_KTILE):
            w_pos = past_len - n_win + kt * S_KTILE + kcol
            dist = jnp.where(w_pos >= 0, q_pos_r - w_pos, -1)
            carry = _flash_step(qs, winbuf_ref[kt * S_KTILE:(kt + 1) * S_KTILE, kcol0:kcol0 + NSA_HEAD],
                                winbuf_ref[kt * S_KTILE:(kt + 1) * S_KTILE, vcol0:vcol0 + NSA_HEAD],
                                (dist >= 0) & (dist < WINDOW), *carry)
        o_w = _flash_out(*carry)

        o = _stack_gates(gates, grp, 0) * o_c + _stack_gates(gates, grp, 1) * o_s + _stack_gates(gates, grp, 2) * o_w
        for h in range(NSA_HPG):
            col = (grp * NSA_HPG + h) * NSA_HEAD
            obuf_ref[:, col:col + NSA_HEAD] = o[h * tp:(h + 1) * tp]
    o_ref[...] = obuf_ref[0:t_new, :]


def _nsa_sample(p_q, p_gate, kvc_cmp, cache_sel, page_table, p_kvs, win_state, p_kvw):
    b, t_new, _ = p_q.shape
    n_pages = page_table.shape[1]
    page = cache_sel.shape[1]
    past_len = n_pages * page
    n_cmp = kvc_cmp.shape[1]
    n_win = win_state.shape[1]
    tp = 8
    assert t_new <= tp and past_len % S_KTILE == 0
    sel_rows = past_len + S_KTILE
    win_rows = -(-(n_win + t_new) // S_KTILE) * S_KTILE
    per_seq = lambda n, w: pl.BlockSpec((None, n, w), lambda i, pt: (i, 0, 0))
    return pl.pallas_call(
        functools.partial(_nsa_sample_kernel, past_len),
        out_shape=jax.ShapeDtypeStruct((b, t_new, NSA_WIDTH), f32),
        grid_spec=pltpu.PrefetchScalarGridSpec(
            num_scalar_prefetch=1, grid=(b,),
            in_specs=[per_seq(t_new, NSA_WIDTH), per_seq(t_new, GATE_PAD), per_seq(n_cmp, KV_COLS),
                      pl.BlockSpec(memory_space=pl.ANY), per_seq(t_new, KV_COLS), per_seq(n_win, KV_COLS),
                      per_seq(t_new, KV_COLS)],
            out_specs=per_seq(t_new, NSA_WIDTH),
            scratch_shapes=[pltpu.VMEM((sel_rows, KV_COLS), f32), pltpu.VMEM((win_rows, KV_COLS), f32),
                            pltpu.VMEM((tp, NSA_WIDTH), f32), pltpu.VMEM((tp, GATE_PAD), f32),
                            pltpu.VMEM((tp, NSA_WIDTH), f32), pltpu.SemaphoreType.DMA(())]),
        compiler_params=_cparams(("arbitrary",)),
        name="nsa_sample",
    )(page_table, p_q, p_gate, kvc_cmp, cache_sel, p_kvs, win_state, p_kvw)


def _prep_weights(l, depth, ln_g, ln_b, ffn_w_gate, ffn_w_up, ffn_w_down, w_in, b_in, rw_mu, rw_w0, rw_w2, rw_a0,
                  rw_a2, rw_g2, rw_k_k, rw_k_a, rw_r_k, rw_ln_w, rw_ln_b, nsa_phi_pe, nsa_phi_w1, nsa_phi_w2,
                  w_out_a, w_out_b, w_o):
    d = D_MODEL
    nc = D_FF // FF_CHUNK
    up = lambda w: w.astype(bf16).reshape(d, nc, FF_CHUNK).transpose(1, 0, 2)
    down = lambda w: w.astype(bf16).reshape(nc, FF_CHUNK, d)
    c1 = RW_SHIFT
    c2 = c1 + NSA_WIDTH
    c3 = c2 + 3 * KV_COLS
    c4 = c3 + N_GATE
    cols = lambda a: jnp.concatenate(
        [a[..., :c3], a[..., c4:], a[..., c3:c4], jnp.zeros(a.shape[:-1] + (GATE_PAD - N_GATE,), a.dtype)], axis=-1)
    z = lambda r, c: jnp.zeros((r, c), f32)
    w = RW_WIDTH
    lora = jnp.concatenate([
        jnp.concatenate([rw_w2[l], z(W_LORA, w), z(W_LORA, w)], axis=1),
        jnp.concatenate([z(A_LORA, w), rw_a2[l], z(A_LORA, w)], axis=1),
        jnp.concatenate([z(G_LORA, w), z(G_LORA, w), rw_g2[l]], axis=1)], axis=0)
    hid = lax.broadcasted_iota(jnp.int32, (w, w), 0) // RW_HEAD == lax.broadcasted_iota(jnp.int32, (w, w), 1) // RW_HEAD
    vec = jnp.stack([rw_w0[l], rw_a0[l], rw_k_k[l], rw_k_a[l], rw_r_k[l].reshape(w), rw_ln_w[l], rw_ln_b[l],
                     jnp.zeros((w,), f32)])
    w1 = nsa_phi_w1[l].reshape(2, 2, CMP_STRIDE, NSA_HEAD, CMP_HID)
    eye = jnp.eye(NSA_KV, dtype=f32)
    big = jnp.einsum("kljde,kK,gG->jkgdKGle", w1, jnp.eye(2, dtype=f32), eye)
    pe = jnp.broadcast_to(nsa_phi_pe[l].reshape(2, 1, CMP_LEN * NSA_HEAD), (2, 8, CMP_LEN * NSA_HEAD))
    w_cols, b_cols = cols(w_in[l]), cols(b_in[l])
    seg_t = _segments(_PROJ, [name for name, _ in _PROJ_T])
    w_t = jnp.concatenate([w_cols[:, lo:hi] for lo, hi in seg_t], axis=1).T
    b_t = jnp.concatenate([b_cols[lo:hi] for lo, hi in seg_t])
    return dict(
        alpha=(2 * depth) ** 0.25,
        wg1=up(ffn_w_gate[l, 0]), wu1=up(ffn_w_up[l, 0]), wd1=down(ffn_w_down[l, 0]),
        wg2=up(ffn_w_gate[l, 1]), wu2=up(ffn_w_up[l, 1]), wd2=down(ffn_w_down[l, 1]),
        ln_g=ln_g[l], ln_b=ln_b[l],
        w_in=w_cols.astype(bf16), b_in=b_cols.reshape(1, -1),
        w_in_t=w_t.astype(bf16), b_in_t=jnp.broadcast_to(b_t[:, None], (b_t.shape[0], LANES)),
        phi_w2t=jnp.swapaxes(nsa_phi_w2[l], 1, 2).astype(bf16),
        rw_mu=rw_mu[l].reshape(1, -1), rw_lora=lora.astype(bf16), rw_vec=vec, rw_hsum=hid.astype(bf16),
        phi_big=big.reshape(CMP_ROW, 2 * NSA_KV * 2 * CMP_HID).astype(bf16), phi_pe=pe,
        phi_w1f=nsa_phi_w1[l].reshape(2, CMP_LEN * NSA_HEAD, CMP_HID).astype(bf16),
        phi_w2=nsa_phi_w2[l].astype(bf16),
        w_out_a=w_out_a[l].astype(bf16), w_out_b=w_out_b[l].astype(bf16), w_o=w_o[l].astype(bf16))


def _kv_state(p):
    return p.reshape(p.shape[:-1] + (2, NSA_KV, NSA_HEAD))


def _kv_state_t(p_t):
    b, _, t = p_t.shape
    return p_t.reshape(b, 2, NSA_KV, NSA_HEAD, t).transpose(0, 4, 1, 2, 3)


def kernel(x_prompt, x_sample, c_prompt, c_sample, cache_kv_cmp, cache_kv_sel, state_kv_win, state_wkv, state_shift, page_table, w_ada, b_ada, ln_g, ln_b, ffn_w_gate, ffn_w_up, ffn_w_down, w_in, b_in, rw_mu, rw_w0, rw_w2, rw_a0, rw_a2, rw_g2, rw_k_k, rw_k_a, rw_r_k, rw_ln_w, rw_ln_b, nsa_phi_pe, nsa_phi_w1, nsa_phi_w2, w_out_a, w_out_b, w_o):
    bp, seq, d = x_prompt.shape
    bd, t_new, _ = x_sample.shape
    depth = w_ada.shape[0]
    n_phys, page = cache_kv_cmp.shape[1:3]
    n_pages = page_table.shape[1]
    n_win = state_kv_win.shape[2]
    rows_s = bd * t_new
    tm_p = 256
    assert seq % tm_p == 0 and seq % CMP_STRIDE == 0 and page % CMP_STRIDE == 0

    y_p, y_s = x_prompt, x_sample.reshape(1, rows_s, d)
    st_p, st_s = [], []
    for l in range(depth):
        wts = _prep_weights(l, depth, ln_g, ln_b, ffn_w_gate, ffn_w_up, ffn_w_down, w_in, b_in, rw_mu, rw_w0,
                            rw_w2, rw_a0, rw_a2, rw_g2, rw_k_k, rw_k_a, rw_r_k, rw_ln_w, rw_ln_b, nsa_phi_pe,
                            nsa_phi_w1, nsa_phi_w2, w_out_a, w_out_b, w_o)
        mod = _ada_mod(jnp.concatenate([c_prompt, c_sample], axis=0), w_ada[l], b_ada[l])
        mod_p = mod[:bp].reshape(bp, 9, 1, d)
        mod_s = jnp.repeat(mod[bp:].reshape(bd, 9, d), t_new, axis=0).reshape(rows_s, 9, d)
        mod_s = mod_s.transpose(1, 0, 2).reshape(1, 9, rows_s, d)

        x1, p_rw, p_kc, p_ks, p_kw, p_m, q_t, g_t, ks_t, kw_t = _ffn_in(
            y_p, mod_p, wts, tm_p, ("rw", "kc", "ks", "kw", "merge"), ("q", "gate", "ks", "kw"))
        ya, wkv_p = _rwkv(p_rw, jnp.zeros((bp, RW_SHIFT), f32), jnp.zeros((bp, RW_HEADS, RW_HEAD, RW_HEAD), f32),
                          wts)
        kvc_cmp, kvc_cmp_t = _compress_dense(p_kc.reshape(bp, seq // CMP_STRIDE, CMP_ROW), wts)
        yb_t = _nsa_prompt(q_t, g_t, kvc_cmp, kvc_cmp_t, p_ks, ks_t, p_kw, kw_t)
        y_p = _mix_ffn(x1, ya, yb_t, p_m, mod_p, wts, tm_p, True)
        n_keep = min(WINDOW, seq)
        st_p.append((_kv_state(p_kc), _kv_state_t(ks_t), _kv_state_t(kw_t[:, :, seq - n_keep:]), wkv_p,
                     p_rw[:, -1]))

        x1, p_rw, p_q, p_kc, p_ks, p_kw, p_m, p_g = _ffn_in(
            y_s, mod_s, wts, rows_s, ("rw", "q", "kc", "ks", "kw", "merge", "gate"), ())
        per_seq = lambda a: a.reshape(bd, t_new, a.shape[-1])
        p_rw, p_q, p_kc, p_ks, p_kw, p_g = map(per_seq, (p_rw, p_q, p_kc, p_ks, p_kw, p_g))
        ya, wkv_s = _rwkv(p_rw, state_shift[l], state_wkv[l], wts)
        kvc_cmp = _compress_paged(cache_kv_cmp[l].reshape(n_phys, page // CMP_STRIDE, CMP_ROW), page_table, wts)
        win_state = state_kv_win[l].reshape(bd, n_win, KV_COLS)
        yb = _nsa_sample(p_q, p_g, kvc_cmp, cache_kv_sel[l].reshape(n_phys, page, KV_COLS), page_table, p_ks,
                         win_state, p_kw)
        y_s = _mix_ffn(x1, ya.reshape(1, rows_s, RW_WIDTH), yb.reshape(1, rows_s, NSA_WIDTH), p_m, mod_s, wts,
                       rows_s, False)
        new_win = jnp.concatenate([win_state, p_kw], axis=1)[:, t_new:]
        st_s.append((_kv_state(p_kc), _kv_state(p_ks), _kv_state(new_win), wkv_s, p_rw[:, -1]))

    stack = lambda sts, i: jnp.stack([s[i] for s in sts])
    return (y_p, y_s.reshape(bd, t_new, d)) + tuple(stack(st_p, i) for i in range(5)) + tuple(
        stack(st_s, i) for i in range(5))
```

```python
import functools
import math

import jax
import jax.numpy as jnp
from jax import lax
from jax.experimental import pallas as pl
from jax.experimental.pallas import tpu as pltpu

f32 = jnp.float32
bf16 = jnp.bfloat16

D_MODEL = 1024
RW_HEAD = 64
RW_HEADS = 8
RW_WIDTH = RW_HEADS * RW_HEAD
W_LORA = 64
A_LORA = 64
G_LORA = 128
LORA_IN = W_LORA + A_LORA + G_LORA
RW_GN_EPS = 64e-5
RW_SHIFT = 3 * RW_WIDTH + LORA_IN
NSA_HEAD = 64
NSA_HEADS = 8
NSA_KV = 2
NSA_HPG = NSA_HEADS // NSA_KV
NSA_WIDTH = NSA_HEADS * NSA_HEAD
NSA_KV_COLS = NSA_KV * NSA_HEAD
KV_COLS = 2 * NSA_KV_COLS
NSA_SCALE = NSA_HEAD ** -0.5
CMP_STRIDE = 16
CMP_LEN = 2 * CMP_STRIDE
CMP_HID = 128
CMP_ROW = CMP_STRIDE * KV_COLS
SEL_BLOCK = 64
SEL_TOPK = 16
WINDOW = 512
D_FF = 2816
N_GATE = 3 * NSA_HEADS
LN_EPS = 1e-5
NEG_INF = -1e30
BIG = 1e9

LANES = 128
GATE_PAD = LANES
FF_CHUNK = 256
RW_CHUNK = 64
RW_HPB = 4
Q_TILE = 128
K_TILE = 256
S_KTILE = 2048
VMEM_LIMIT = 56 * 1024 * 1024

_NT = (((1,), (1,)), ((), ()))
_TN = (((0,), (0,)), ((), ()))
_HI = lax.Precision.HIGHEST


def _dot(a, b):
    return jnp.dot(a.astype(bf16), b.astype(bf16), preferred_element_type=f32)


def _dot_nt(a, b):
    return lax.dot_general(a.astype(bf16), b.astype(bf16), _NT, preferred_element_type=f32)


def _dot_f32(a, b, dims=None):
    if dims is None:
        return jnp.dot(a, b, preferred_element_type=f32, precision=_HI)
    return lax.dot_general(a, b, dims, preferred_element_type=f32, precision=_HI)


def _dot_sel(x, m01):
    hi = x.astype(bf16)
    r1 = x - hi.astype(f32)
    mid = r1.astype(bf16)
    lo = (r1 - mid.astype(f32)).astype(bf16)
    d = lambda t: jnp.dot(t, m01, preferred_element_type=f32)
    return d(hi) + d(mid) + d(lo)


def _layer_norm(x, g, b):
    mu = jnp.mean(x, axis=-1, keepdims=True)
    xc = x - mu
    var = jnp.mean(xc * xc, axis=-1, keepdims=True)
    return xc * lax.rsqrt(var + LN_EPS) * g + b


def _cparams(sem):
    return pltpu.CompilerParams(dimension_semantics=sem, vmem_limit_bytes=VMEM_LIMIT)


def _resident(shape):
    nd = len(shape)
    return pl.BlockSpec(shape, lambda *_: (0,) * nd, pipeline_mode=pl.Buffered(1))


def _ada_kernel(c_ref, w_ref, b_ref, o_ref):
    c = c_ref[...]
    o_ref[...] = _dot(c * jax.nn.sigmoid(c), w_ref[...]) + b_ref[...]


def _ada_mod(c, w_ada, b_ada):
    rows, d = c.shape
    n = w_ada.shape[1]
    tn = d
    return pl.pallas_call(
        _ada_kernel,
        out_shape=jax.ShapeDtypeStruct((rows, n), f32),
        grid=(n // tn,),
        in_specs=[pl.BlockSpec((rows, d), lambda j: (0, 0)),
                  pl.BlockSpec((d, tn), lambda j: (0, j)),
                  pl.BlockSpec((1, tn), lambda j: (0, j))],
        out_specs=pl.BlockSpec((rows, tn), lambda j: (0, j)),
        compiler_params=_cparams(("parallel",)),
        name="ada_mod",
    )(c, w_ada, b_ada.reshape(1, n))


def _ffn(u, wg_ref, wu_ref, wd_ref, acc_ref):
    ub = u.astype(bf16)
    acc_ref[...] = jnp.zeros(acc_ref.shape, f32)

    def body(c, carry):
        hg = jnp.dot(ub, wg_ref[c], preferred_element_type=f32)
        hu = jnp.dot(ub, wu_ref[c], preferred_element_type=f32)
        h = hg * jax.nn.sigmoid(hg) * hu
        acc_ref[...] += jnp.dot(h.astype(bf16), wd_ref[c], preferred_element_type=f32)
        return carry

    lax.fori_loop(0, wg_ref.shape[0], body, 0)
    return acc_ref[...]


def _ffn_in_kernel(alpha, seg, seg_t, x_ref, mod_ref, wg_ref, wu_ref, wd_ref, lng_ref, lnb_ref, win_ref, bin_ref,
                   wint_ref, bint_ref, x1_ref, *rest):
    out_refs, acc_ref = rest[:-1], rest[-1]
    x = x_ref[...]
    u = x * (1.0 + mod_ref[1]) + mod_ref[0]
    f = _ffn(u, wg_ref, wu_ref, wd_ref, acc_ref)
    x1 = _layer_norm(alpha * x + 0.5 * (1.0 + mod_ref[2]) * f, lng_ref[0:1, :], lnb_ref[0:1, :])
    x1_ref[...] = x1
    u2 = (x1 * (1.0 + mod_ref[4]) + mod_ref[3]).astype(bf16)
    for o_ref, (lo, hi) in zip(out_refs, seg):
        o_ref[...] = jnp.dot(u2, win_ref[:, lo:hi], preferred_element_type=f32) + bin_ref[:, lo:hi]
    for o_ref, (lo, hi) in zip(out_refs[len(seg):], seg_t):
        o_ref[...] = lax.dot_general(wint_ref[lo:hi, :], u2, _NT, preferred_element_type=f32) + bint_ref[lo:hi, :]


_PROJ = (("rw", RW_SHIFT), ("q", NSA_WIDTH), ("kc", KV_COLS), ("ks", KV_COLS), ("kw", KV_COLS),
         ("merge", 2 * D_MODEL), ("gate", GATE_PAD))
_PROJ_T = (("q", NSA_WIDTH), ("gate", GATE_PAD), ("ks", KV_COLS), ("kw", KV_COLS))


def _segments(table, names):
    seg, lo = {}, 0
    for name, w in table:
        seg[name] = (lo, lo + w)
        lo += w
    return tuple(seg[n] for n in names)


def _ffn_in(x, mod, wts, tm, names, names_t):
    g, t, d = x.shape
    r = mod.shape[2]
    seg, seg_t = _segments(_PROJ, names), _segments(_PROJ_T, names_t)
    row = lambda w: pl.BlockSpec((None, tm, w), lambda i, j: (i, j, 0))
    col = lambda w: pl.BlockSpec((None, w, tm), lambda i, j: (i, 0, j))
    return pl.pallas_call(
        functools.partial(_ffn_in_kernel, wts["alpha"], seg, seg_t),
        out_shape=([jax.ShapeDtypeStruct((g, t, d), f32)]
                   + [jax.ShapeDtypeStruct((g, t, hi - lo), f32) for lo, hi in seg]
                   + [jax.ShapeDtypeStruct((g, hi - lo, t), f32) for lo, hi in seg_t]),
        grid=(g, t // tm),
        in_specs=[row(d),
                  pl.BlockSpec((None, 9, r, d), lambda i, j: (i, 0, 0, 0)),
                  _resident(wts["wg1"].shape), _resident(wts["wu1"].shape), _resident(wts["wd1"].shape),
                  _resident(wts["ln_g"].shape), _resident(wts["ln_b"].shape),
                  _resident(wts["w_in"].shape), _resident(wts["b_in"].shape),
                  _resident(wts["w_in_t"].shape), _resident(wts["b_in_t"].shape)],
        out_specs=[row(d)] + [row(hi - lo) for lo, hi in seg] + [col(hi - lo) for lo, hi in seg_t],
        scratch_shapes=[pltpu.VMEM((tm, d), f32)],
        compiler_params=_cparams(("parallel", "parallel")),
        name="ffn_in",
    )(x, mod, wts["wg1"], wts["wu1"], wts["wd1"], wts["ln_g"], wts["ln_b"], wts["w_in"], wts["b_in"],
      wts["w_in_t"], wts["b_in_t"])


def _mix_ffn_kernel(alpha, yb_on_lanes, x1_ref, ya_ref, yb_ref, pm_ref, mod_ref, woa_ref, wob_ref, wo_ref,
                    wg_ref, wu_ref, wd_ref, lng_ref, lnb_ref, o_ref, acc_ref):
    x1 = x1_ref[...]
    pm = pm_ref[...]
    d = x1.shape[-1]
    ga = jax.nn.sigmoid(pm[:, :d])
    gb = jax.nn.sigmoid(pm[:, d:])
    if yb_on_lanes:
        yb_out = lax.dot_general(yb_ref[...].astype(bf16), wob_ref[...], _TN, preferred_element_type=f32)
    else:
        yb_out = _dot(yb_ref[...], wob_ref[...])
    merged = ga * _dot(ya_ref[...], woa_ref[...]) + gb * yb_out
    m = _dot(merged, wo_ref[...])
    x2 = _layer_norm(alpha * x1 + (1.0 + mod_ref[5]) * m, lng_ref[1:2, :], lnb_ref[1:2, :])
    u = x2 * (1.0 + mod_ref[7]) + mod_ref[6]
    f = _ffn(u, wg_ref, wu_ref, wd_ref, acc_ref)
    o_ref[...] = _layer_norm(alpha * x2 + 0.5 * (1.0 + mod_ref[8]) * f, lng_ref[2:3, :], lnb_ref[2:3, :])


def _mix_ffn(x1, ya, yb, pm, mod, wts, tm, yb_on_lanes):
    g, t, d = x1.shape
    r = mod.shape[2]
    row = lambda w: pl.BlockSpec((None, tm, w), lambda i, j: (i, j, 0))
    yb_spec = pl.BlockSpec((None, NSA_WIDTH, tm), lambda i, j: (i, 0, j)) if yb_on_lanes else row(NSA_WIDTH)
    return pl.pallas_call(
        functools.partial(_mix_ffn_kernel, wts["alpha"], yb_on_lanes),
        out_shape=jax.ShapeDtypeStruct((g, t, d), f32),
        grid=(g, t // tm),
        in_specs=[row(d), row(ya.shape[-1]), yb_spec, row(pm.shape[-1]),
                  pl.BlockSpec((None, 9, r, d), lambda i, j: (i, 0, 0, 0)),
                  _resident(wts["w_out_a"].shape), _resident(wts["w_out_b"].shape), _resident(wts["w_o"].shape),
                  _resident(wts["wg2"].shape), _resident(wts["wu2"].shape), _resident(wts["wd2"].shape),
                  _resident(wts["ln_g"].shape), _resident(wts["ln_b"].shape)],
        out_specs=row(d),
        scratch_shapes=[pltpu.VMEM((tm, d), f32)],
        compiler_params=_cparams(("parallel", "parallel")),
        name="mix_ffn",
    )(x1, ya, yb, pm, mod, wts["w_out_a"], wts["w_out_b"], wts["w_o"], wts["wg2"], wts["wu2"], wts["wd2"],
      wts["ln_g"], wts["ln_b"])


def _rwkv_kernel(t_blk, p_ref, prev_ref, s0_ref, mu_ref, wl_ref, vec_ref, hsum_ref,
                 y_ref, sout_ref, state_ref, prevrow_ref, stage_ref, yn_ref):
    c = stage_ref.shape[0]
    w = RW_WIDTH
    ci = pl.program_id(1)

    @pl.when(ci == 0)
    def _():
        state_ref[...] = jnp.zeros(state_ref.shape, f32)
        for h in range(RW_HEADS):
            d0 = (h % RW_HPB) * RW_HEAD
            state_ref[h // RW_HPB, d0:d0 + RW_HEAD, d0:d0 + RW_HEAD] = s0_ref[h]
        prevrow_ref[...] = prev_ref[...]

    if t_blk < c:
        stage_ref[...] = jnp.zeros(stage_ref.shape, f32)
        stage_ref[0:t_blk, :] = p_ref[...]
        p = stage_ref[...]
    else:
        p = p_ref[...]
    ridx = lax.broadcasted_iota(jnp.int32, (c, 1), 0)
    valid = ridx < t_blk
    p_prev = jnp.where(ridx == 0, prevrow_ref[...], pltpu.roll(p, 1, 0))
    prevrow_ref[...] = p[c - 1:c, :]
    xs = p + (p_prev - p) * mu_ref[...]

    tail = xs[:, 3 * w:]
    li = lax.broadcasted_iota(jnp.int32, tail.shape, 1)
    act = jnp.where(li < W_LORA, jnp.tanh(tail),
                    jnp.where(li < W_LORA + A_LORA, tail, jax.nn.sigmoid(tail)))
    lora = _dot(act, wl_ref[...])
    w0, a0, k_k, k_a, r_k, ln_w, ln_b = (vec_ref[i:i + 1, :] for i in range(7))
    z = -(w0 + lora[:, :w])
    softplus = jnp.maximum(z, 0.0) + jnp.log1p(jnp.exp(-jnp.abs(z)))
    lw = -jnp.exp(-softplus - 0.5)
    a = jax.nn.sigmoid(a0 + lora[:, w:2 * w])
    g = lora[:, 2 * w:]
    r, k, v = xs[:, :w], xs[:, w:2 * w], xs[:, 2 * w:3 * w]
    hsum = hsum_ref[...]
    kk = k * k_k
    kk = kk * lax.rsqrt(jnp.maximum(_dot_sel(kk * kk, hsum), 1e-24))
    k2 = k * (1.0 + (a - 1.0) * k_a)
    bonus = _dot_sel(r * k2 * r_k, hsum) * v
    if t_blk < c:
        lw = jnp.where(valid, lw, 0.0)
        kk = jnp.where(valid, kk, 0.0)
        k2 = jnp.where(valid, k2, 0.0)
        v = jnp.where(valid, v, 0.0)
    b = kk * a

    rr = lax.broadcasted_iota(jnp.int32, (c, c), 0)
    cc = lax.broadcasted_iota(jnp.int32, (c, c), 1)
    cum = _dot_f32(jnp.where(rr >= cc, 1.0, 0.0).astype(f32), lw)
    cum_last = cum[c - 1:c, :]
    e_neg = jnp.exp(-cum)
    e_rem = jnp.exp(cum_last - cum)
    at = -kk * jnp.exp(cum - lw)
    rt = r * jnp.exp(cum)
    bt, kt = b * e_neg, k2 * e_neg
    bh, kh = b * e_rem, k2 * e_rem
    p_last = jnp.exp(cum_last)

    gw = RW_HPB * RW_HEAD
    rows = RW_HPB * c
    br = lax.broadcasted_iota(jnp.int32, (rows, gw), 0)
    bc = lax.broadcasted_iota(jnp.int32, (rows, gw), 1)
    same_head = (br // c) == (bc // RW_HEAD)
    tr = lax.broadcasted_iota(jnp.int32, (rows, rows), 0) % c
    tc = lax.broadcasted_iota(jnp.int32, (rows, rows), 1) % c
    strict = tr > tc
    incl = tr >= tc
    blocks = lambda x: jnp.where(same_head, jnp.concatenate([x] * RW_HPB, axis=0), 0.0).astype(bf16)
    nt = lambda x, y: lax.dot_general(x, y, _NT, preferred_element_type=f32)
    tn = lambda x, y: lax.dot_general(x, y, _TN, preferred_element_type=f32)
    mm = lambda x, y: jnp.dot(x, y, preferred_element_type=f32)
    n_fac = max(1, (c - 1).bit_length())

    for grp in range(RW_HEADS // RW_HPB):
        gl = slice(grp * gw, (grp + 1) * gw)
        at4, rt4, bt4, kt4, bh4, kh4, v4 = (blocks(x[:, gl]) for x in (at, rt, bt, kt, bh, kh, v))
        s_bd = state_ref[grp]
        s_b = s_bd.astype(bf16)
        low = jnp.where(strict, nt(at4, bt4), 0.0).astype(bf16)
        u = nt(at4, s_b) + mm(jnp.where(strict, nt(at4, kt4), 0.0).astype(bf16), v4)
        for i in range(n_fac):
            u = u + mm(low, u.astype(bf16))
            if i + 1 < n_fac:
                low = mm(low, low).astype(bf16)
        u_b = u.astype(bf16)
        y4 = (nt(rt4, s_b) + mm(jnp.where(incl, nt(rt4, bt4), 0.0).astype(bf16), u_b)
              + mm(jnp.where(incl, nt(rt4, kt4), 0.0).astype(bf16), v4))
        state_ref[grp] = s_bd * p_last[:, gl] + tn(u_b, bh4) + tn(v4, kh4)
        yg = y4[0:c]
        for hh in range(1, RW_HPB):
            yg = yg + y4[hh * c:(hh + 1) * c]
        yn_ref[:, gl] = yg

    yh = yn_ref[...]
    mean = _dot_sel(yh, hsum) * (1.0 / RW_HEAD)
    yc = yh - mean
    var = _dot_sel(yc * yc, hsum) * (1.0 / RW_HEAD)
    y = (yc * lax.rsqrt(var + RW_GN_EPS) * ln_w + ln_b + bonus) * g
    y_ref[...] = y[0:t_blk, :] if t_blk < c else y

    @pl.when(ci == pl.num_programs(1) - 1)
    def _():
        for h in range(RW_HEADS):
            d0 = (h % RW_HPB) * RW_HEAD
            sout_ref[h] = state_ref[h // RW_HPB, d0:d0 + RW_HEAD, d0:d0 + RW_HEAD]


def _rwkv(p_rw, prev, s0, wts):
    g, t, ws = p_rw.shape
    c = RW_CHUNK
    t_blk = min(c, t)
    assert t % t_blk == 0
    return pl.pallas_call(
        functools.partial(_rwkv_kernel, t_blk),
        out_shape=[jax.ShapeDtypeStruct((g, t, RW_WIDTH), f32), jax.ShapeDtypeStruct(s0.shape, f32)],
        grid=(g, t // t_blk),
        in_specs=[pl.BlockSpec((None, t_blk, ws), lambda i, j: (i, j, 0)),
                  pl.BlockSpec((None, 1, ws), lambda i, j: (i, 0, 0)),
                  pl.BlockSpec((None,) + s0.shape[1:], lambda i, j: (i, 0, 0, 0)),
                  _resident(wts["rw_mu"].shape), _resident(wts["rw_lora"].shape),
                  _resident(wts["rw_vec"].shape), _resident(wts["rw_hsum"].shape)],
        out_specs=[pl.BlockSpec((None, t_blk, RW_WIDTH), lambda i, j: (i, j, 0)),
                   pl.BlockSpec((None,) + s0.shape[1:], lambda i, j: (i, 0, 0, 0))],
        scratch_shapes=[pltpu.VMEM((RW_HEADS // RW_HPB, RW_HPB * RW_HEAD, RW_HPB * RW_HEAD), f32),
                        pltpu.VMEM((1, ws), f32),
                        pltpu.VMEM((c, ws), f32), pltpu.VMEM((c, RW_WIDTH), f32)],
        compiler_params=_cparams(("parallel", "arbitrary")),
        name="rwkv",
    )(p_rw, prev.reshape(g, 1, ws), s0, wts["rw_mu"], wts["rw_lora"], wts["rw_vec"], wts["rw_hsum"])


def _gelu_tanh(x):
    return x * (0.5 * (1.0 + jnp.tanh(math.sqrt(2.0 / math.pi) * (x + 0.044715 * (x * x * x)))))


def _compress_rows(x, wbig_ref, pe_ref, w1f_ref, w2_ref, o_ref, w2t_ref=None, ot_ref=None):
    n = x.shape[0]
    hid = _dot(x, wbig_ref[...])
    for kv in range(2):
        pe_term = _dot(pe_ref[kv], w1f_ref[kv])[0:1, :]
        for grp in range(NSA_KV):
            base = (kv * NSA_KV + grp) * 2 * CMP_HID
            lo = hid[:, base:base + CMP_HID]
            hi = pltpu.roll(hid[:, base + CMP_HID:base + 2 * CMP_HID], n - 1, 0)
            act = _gelu_tanh(lo + hi + pe_term).astype(bf16)
            col = (kv * NSA_KV + grp) * NSA_HEAD
            o_ref[:, col:col + NSA_HEAD] = jnp.dot(act, w2_ref[kv], preferred_element_type=f32)
            if ot_ref is not None:
                ot_ref[col:col + NSA_HEAD, :] = lax.dot_general(w2t_ref[kv], act, _NT, preferred_element_type=f32)


def _compress_dense_kernel(x_ref, wbig_ref, pe_ref, w1f_ref, w2_ref, w2t_ref, o_ref, ot_ref):
    _compress_rows(x_ref[...], wbig_ref, pe_ref, w1f_ref, w2_ref, o_ref, w2t_ref, ot_ref)


def _compress_dense(x, wts):
    b, n, _ = x.shape
    return pl.pallas_call(
        _compress_dense_kernel,
        out_shape=[jax.ShapeDtypeStruct((b, n, KV_COLS), f32), jax.ShapeDtypeStruct((b, KV_COLS, n), f32)],
        grid=(b,),
        in_specs=[pl.BlockSpec((None, n, CMP_ROW), lambda i: (i, 0, 0)),
                  _resident(wts["phi_big"].shape), _resident(wts["phi_pe"].shape),
                  _resident(wts["phi_w1f"].shape), _resident(wts["phi_w2"].shape),
                  _resident(wts["phi_w2t"].shape)],
        out_specs=[pl.BlockSpec((None, n, KV_COLS), lambda i: (i, 0, 0)),
                   pl.BlockSpec((None, KV_COLS, n), lambda i: (i, 0, 0))],
        compiler_params=_cparams(("parallel",)),
        name="compress_dense",
    )(x, wts["phi_big"], wts["phi_pe"], wts["phi_w1f"], wts["phi_w2"], wts["phi_w2t"])


CMP_GROUP = 2 * LANES


def _compress_paged_kernel(pt_ref, cache_ref, wbig_ref, pe_ref, w1f_ref, w2_ref, o_ref, pbuf_ref, xbuf_ref, sem):
    b = pl.program_id(0)
    n_pages = pt_ref.shape[1]
    page = cache_ref.shape[2]
    pages_per_group = CMP_GROUP // page
    rows_per_group = CMP_GROUP // CMP_STRIDE

    def page_copy(seq, i):
        return pltpu.make_async_copy(cache_ref.at[pt_ref[seq, i]], pbuf_ref.at[:, pl.ds(i * page, page)],
                                     sem.at[i // pages_per_group])

    def start_all(seq):
        def start(i, carry):
            page_copy(seq, i).start()
            return carry
        lax.fori_loop(0, n_pages, start, 0)

    @pl.when(b == 0)
    def _():
        start_all(0)

    r = lax.broadcasted_iota(jnp.int32, (CMP_GROUP, CMP_GROUP), 0)
    c = lax.broadcasted_iota(jnp.int32, (CMP_GROUP, CMP_GROUP), 1)
    perm = jnp.where(c == CMP_STRIDE * (r % rows_per_group) + r // rows_per_group, 1.0, 0.0).astype(bf16)

    def regroup(gi, carry):
        for k in range(pages_per_group):
            page_copy(b, gi * pages_per_group + k).wait()
        t0 = pl.multiple_of(gi * CMP_GROUP, CMP_GROUP)
        tokens = pbuf_ref[:, pl.ds(t0, CMP_GROUP)].astype(bf16)
        rows = lax.dot_general(perm, tokens, _NT, preferred_element_type=f32)
        r0 = pl.multiple_of(gi * rows_per_group, rows_per_group)
        for j in range(CMP_STRIDE):
            xbuf_ref[pl.ds(r0, rows_per_group), j * KV_COLS:(j + 1) * KV_COLS] = (
                rows[j * rows_per_group:(j + 1) * rows_per_group].astype(bf16))
        return carry

    lax.fori_loop(0, n_pages // pages_per_group, regroup, 0)

    @pl.when(b + 1 < pl.num_programs(0))
    def _():
        start_all(b + 1)

    _compress_rows(xbuf_ref[...], wbig_ref, pe_ref, w1f_ref, w2_ref, o_ref)


def _compress_paged(cache_t, page_table, wts):
    b, n_pages = page_table.shape
    page = cache_t.shape[2]
    n = n_pages * page // CMP_STRIDE
    assert CMP_GROUP % page == 0 and n_pages % (CMP_GROUP // page) == 0
    full = lambda a: pl.BlockSpec(a.shape, lambda i, pt: (0,) * a.ndim, pipeline_mode=pl.Buffered(1))
    return pl.pallas_call(
        _compress_paged_kernel,
        out_shape=jax.ShapeDtypeStruct((b, n, KV_COLS), f32),
        grid_spec=pltpu.PrefetchScalarGridSpec(
            num_scalar_prefetch=1, grid=(b,),
            in_specs=[pl.BlockSpec(memory_space=pl.ANY), full(wts["phi_big"]), full(wts["phi_pe"]),
                      full(wts["phi_w1f"]), full(wts["phi_w2"])],
            out_specs=pl.BlockSpec((None, n, KV_COLS), lambda i, pt: (i, 0, 0)),
            scratch_shapes=[pltpu.VMEM((KV_COLS, n_pages * page), f32), pltpu.VMEM((n, CMP_ROW), bf16),
                            pltpu.SemaphoreType.DMA((n_pages * page // CMP_GROUP,))]),
        compiler_params=_cparams(("arbitrary",)),
        name="compress_paged",
    )(page_table, cache_t, wts["phi_big"], wts["phi_pe"], wts["phi_w1f"], wts["phi_w2"])


def _softmax_rows(s, mask):
    s = jnp.where(mask, s, NEG_INF)
    e = jnp.where(mask, jnp.exp(s - jnp.max(s, axis=-1, keepdims=True)), 0.0)
    return e / jnp.maximum(jnp.sum(e, axis=-1, keepdims=True), 1e-30)


def _flash_step(q, k, v, mask, m, l, acc):
    s = jnp.where(mask, _dot_nt(q, k), NEG_INF)
    m_new = jnp.maximum(m, jnp.max(s, axis=-1, keepdims=True))
    alpha = jnp.exp(m - m_new)
    e = jnp.where(mask, jnp.exp(s - m_new), 0.0)
    return m_new, alpha * l + jnp.sum(e, axis=-1, keepdims=True), alpha * acc + _dot(e, v)


def _flash_init(rows):
    return (jnp.full((rows, 1), NEG_INF, f32), jnp.zeros((rows, 1), f32), jnp.zeros((rows, NSA_HEAD), f32))


def _flash_out(m, l, acc):
    return acc / jnp.maximum(l, 1e-30)


def _importance_matrix(n_cmp_rows, n_sel_cols):
    ratio = SEL_BLOCK // CMP_STRIDE
    i = lax.broadcasted_iota(jnp.int32, (n_cmp_rows, n_sel_cols), 0)
    j = lax.broadcasted_iota(jnp.int32, (n_cmp_rows, n_sel_cols), 1)
    return jnp.where((i >= ratio * j - 1) & (i <= ratio * j + ratio - 1), 1.0, 0.0).astype(bf16)


def _block_scores(imp, j, cur):
    forced = (j == 0) | (j == cur) | (j == cur - 1)
    return jnp.where(j <= cur, jnp.where(forced, BIG, imp), -BIG)


def _stack_heads(x, grp, rows):
    return jnp.concatenate(
        [x[:, (grp * NSA_HPG + h) * NSA_HEAD:(grp * NSA_HPG + h + 1) * NSA_HEAD] for h in range(NSA_HPG)], axis=0)


def _stack_gates(gates, grp, branch):
    return jnp.concatenate(
        [gates[:, 3 * (grp * NSA_HPG + h) + branch:3 * (grp * NSA_HPG + h) + branch + 1] for h in range(NSA_HPG)],
        axis=0)


def _flash_step_t(qt, k, vt, mask, m, l, acc):
    s = jnp.where(mask, jnp.dot(k.astype(bf16), qt, preferred_element_type=f32), NEG_INF)
    m_new = jnp.maximum(m, jnp.max(s, axis=0, keepdims=True))
    alpha = jnp.exp(m - m_new)
    e = jnp.where(mask, jnp.exp(s - m_new), 0.0)
    return (m_new, alpha * l + jnp.sum(e, axis=0, keepdims=True),
            alpha * acc + jnp.dot(vt.astype(bf16), e.astype(bf16), preferred_element_type=f32))


def _flash_init_t(cols):
    return (jnp.full((1, cols), NEG_INF, f32), jnp.zeros((1, cols), f32), jnp.zeros((NSA_HEAD, cols), f32))


def _lanes(x, n):
    return jnp.concatenate([x] * n, axis=1)


def _nsa_prompt_kernel(qt_ref, gt_ref, kc_ref, vct_ref, ks_ref, vst_ref, kw_ref, vwt_ref, o_ref):
    tq = qt_ref.shape[1]
    n_cmp = kc_ref.shape[0]
    n_sel = ks_ref.shape[0] // SEL_BLOCK
    i = pl.program_id(1)
    t0 = i * tq
    cols = NSA_HPG * tq
    gates = jax.nn.sigmoid(gt_ref[...])
    q_pos = t0 + lax.broadcasted_iota(jnp.int32, (1, tq), 1)
    q_pos_c = _lanes(q_pos, NSA_HPG)
    cmp_end = lax.broadcasted_iota(jnp.int32, (n_cmp, 1), 0) * CMP_STRIDE + (CMP_LEN - 1)
    ratio = SEL_BLOCK // CMP_STRIDE
    ij = lax.broadcasted_iota(jnp.int32, (n_sel, n_cmp), 0)
    ii = lax.broadcasted_iota(jnp.int32, (n_sel, n_cmp), 1)
    imp_mat = jnp.where((ii >= ratio * ij - 1) & (ii <= ratio * ij + ratio - 1), 1.0, 0.0).astype(bf16)
    jt = lax.broadcasted_iota(jnp.int32, (n_sel, tq), 0)
    cur_t = (t0 + lax.broadcasted_iota(jnp.int32, (n_sel, tq), 1)) // SEL_BLOCK
    krow = lax.broadcasted_iota(jnp.int32, (K_TILE, 1), 0)
    er = lax.broadcasted_iota(jnp.int32, (K_TILE, n_sel), 0) // SEL_BLOCK
    ej = lax.broadcasted_iota(jnp.int32, (K_TILE, n_sel), 1)
    groups = range(NSA_KV)
    heads_of = lambda grp: range(grp * NSA_HPG, (grp + 1) * NSA_HPG)
    qts, sels, o_cs = [], [], []

    for grp in groups:
        d0 = grp * NSA_HEAD
        heads = heads_of(grp)
        qt = jnp.concatenate([qt_ref[h * NSA_HEAD:(h + 1) * NSA_HEAD, :] for h in heads], axis=1)
        qt = (qt * NSA_SCALE).astype(bf16)
        mask_c = cmp_end <= q_pos_c
        s = jnp.where(mask_c, jnp.dot(kc_ref[:, d0:d0 + NSA_HEAD].astype(bf16), qt, preferred_element_type=f32),
                      NEG_INF)
        e = jnp.where(mask_c, jnp.exp(s - jnp.max(s, axis=0, keepdims=True)), 0.0)
        p_c = e / jnp.maximum(jnp.sum(e, axis=0, keepdims=True), 1e-30)
        o_c = jnp.dot(vct_ref[d0:d0 + NSA_HEAD, :].astype(bf16), p_c.astype(bf16), preferred_element_type=f32)
        p_sum = p_c[:, 0:tq]
        for h in range(1, NSA_HPG):
            p_sum = p_sum + p_c[:, h * tq:(h + 1) * tq]
        hi = p_sum.astype(bf16)
        r1 = p_sum - hi.astype(f32)
        mid = r1.astype(bf16)
        lo = (r1 - mid.astype(f32)).astype(bf16)
        imp = sum(jnp.dot(imp_mat, part, preferred_element_type=f32) for part in (hi, mid, lo))
        score = _block_scores(imp, jt, cur_t)
        rank = jnp.zeros((n_sel, tq), f32)
        for j2 in range(n_sel):
            row = score[j2:j2 + 1, :]
            ge = jnp.where(row >= score, 1.0, 0.0)
            gt = jnp.where(row > score, 1.0, 0.0)
            rank = rank + jnp.where(jt > j2, ge, gt)
        qts.append(qt)
        sels.append(jnp.where(rank < SEL_TOPK, 1.0, 0.0).astype(bf16))
        o_cs.append(o_c)

    def sel_step(kt, carry):
        k0 = pl.multiple_of(kt * K_TILE, K_TILE)
        expand = jnp.where(ej == kt * (K_TILE // SEL_BLOCK) + er, 1.0, 0.0).astype(bf16)
        causal = (k0 + krow) <= q_pos
        out = ()
        for grp in groups:
            d0 = grp * NSA_HEAD
            chosen = jnp.where(causal, jnp.dot(expand, sels[grp], preferred_element_type=f32), 0.0)
            out += _flash_step_t(qts[grp], ks_ref[pl.ds(k0, K_TILE), d0:d0 + NSA_HEAD],
                                 vst_ref[d0:d0 + NSA_HEAD, pl.ds(k0, K_TILE)], _lanes(chosen, NSA_HPG) > 0.5,
                                 *carry[3 * grp:3 * grp + 3])
        return out

    last = (t0 + tq - 1) // K_TILE
    sel_acc = lax.fori_loop(0, last + 1, sel_step, _flash_init_t(cols) * NSA_KV)

    def win_step(kt, carry):
        k0 = pl.multiple_of(kt * K_TILE, K_TILE)
        dist = q_pos_c - (k0 + krow)
        mask = (dist >= 0) & (dist < WINDOW)
        out = ()
        for grp in groups:
            d0 = grp * NSA_HEAD
            out += _flash_step_t(qts[grp], kw_ref[pl.ds(k0, K_TILE), d0:d0 + NSA_HEAD],
                                 vwt_ref[d0:d0 + NSA_HEAD, pl.ds(k0, K_TILE)], mask, *carry[3 * grp:3 * grp + 3])
        return out

    first = jnp.maximum(t0 - WINDOW, 0) // K_TILE
    win_acc = lax.fori_loop(first, last + 1, win_step, _flash_init_t(cols) * NSA_KV)

    for grp in groups:
        heads = heads_of(grp)
        o_s = _flash_out(*sel_acc[3 * grp:3 * grp + 3])
        o_w = _flash_out(*win_acc[3 * grp:3 * grp + 3])
        gate = lambda br: jnp.concatenate([gates[3 * h + br:3 * h + br + 1, :] for h in heads], axis=1)
        o = gate(0) * o_cs[grp] + gate(1) * o_s + gate(2) * o_w
        for n, h in enumerate(heads):
            o_ref[h * NSA_HEAD:(h + 1) * NSA_HEAD, :] = o[:, n * tq:(n + 1) * tq]


def _nsa_prompt(qt, gt, kvc, kvct, kvs, kvst, kvw, kvwt):
    b, _, t = qt.shape
    n_cmp = kvc.shape[1]
    tq = Q_TILE
    assert t % tq == 0 and t % K_TILE == 0 and K_TILE % SEL_BLOCK == 0
    keys =lambda r: pl.BlockSpec((None, r, NSA_KV_COLS), lambda i, j: (i, 0, 0))
    vals = lambda c: pl.BlockSpec((None, NSA_KV_COLS, c), lambda i, j: (i, 1, 0))
    return pl.pallas_call(
        _nsa_prompt_kernel,
        out_shape=jax.ShapeDtypeStruct((b, NSA_WIDTH, t), f32),
        grid=(b, t // tq),
        in_specs=[pl.BlockSpec((None, NSA_WIDTH, tq), lambda i, j: (i, 0, j)),
                  pl.BlockSpec((None, GATE_PAD, tq), lambda i, j: (i, 0, j)),
                  keys(n_cmp), vals(n_cmp), keys(t), vals(t), keys(t), vals(t)],
        out_specs=pl.BlockSpec((None, NSA_WIDTH, tq), lambda i, j: (i, 0, j)),
        compiler_params=_cparams(("parallel", "arbitrary")),
        name="nsa_prompt",
    )(qt, gt, kvc, kvct, kvs, kvst, kvw, kvwt)


def _nsa_sample_kernel(past_len, pt_ref, q_ref, gate_ref, kvc_ref, cache_ref, kvs_new_ref, win_ref, kvw_new_ref,
                       o_ref, selbuf_ref, winbuf_ref, qbuf_ref, gbuf_ref, obuf_ref, sem):
    b = pl.program_id(0)
    t_new = q_ref.shape[0]
    tp = qbuf_ref.shape[0]
    n_pages = pt_ref.shape[1]
    page = cache_ref.shape[2]
    n_cmp = kvc_ref.shape[0]
    n_win = win_ref.shape[1]
    total = past_len + t_new
    n_sel = -(-total // SEL_BLOCK)
    n_sel_pad = -(-n_sel // LANES) * LANES
    tile_pages = S_KTILE // page
    tile_blocks = S_KTILE // SEL_BLOCK
    rows = NSA_HPG * tp

    def page_copy(i):
        return pltpu.make_async_copy(cache_ref.at[pt_ref[b, i]], selbuf_ref.at[:, pl.ds(i * page, page)],
                                     sem.at[i // tile_pages])

    def start(i, carry):
        page_copy(i).start()
        return carry

    lax.fori_loop(0, n_pages, start, 0)

    src = lax.broadcasted_iota(jnp.int32, (kvs_new_ref.shape[1], LANES), 0)
    dst = lax.broadcasted_iota(jnp.int32, (kvs_new_ref.shape[1], LANES), 1)
    pick = jnp.where((src == b * t_new + dst) & (dst < t_new), 1.0, 0.0).astype(bf16)
    selbuf_ref[:, past_len:past_len + LANES] = _dot(kvs_new_ref[...], pick)
    winbuf_ref[:, 0:n_win] = win_ref[...]
    winbuf_ref[:, n_win:n_win + LANES] = _dot(kvw_new_ref[...], pick)
    qbuf_ref[...] = jnp.zeros(qbuf_ref.shape, f32)
    qbuf_ref[0:t_new, :] = q_ref[...]
    gbuf_ref[...] = jnp.zeros(gbuf_ref.shape, f32)
    gbuf_ref[0:t_new, :] = gate_ref[...]

    q = qbuf_ref[...] * NSA_SCALE
    gates = jax.nn.sigmoid(gbuf_ref[...])
    q_pos = past_len + lax.broadcasted_iota(jnp.int32, (tp, 1), 0)
    q_pos_r = jnp.concatenate([q_pos] * NSA_HPG, axis=0)
    cmp_end = lax.broadcasted_iota(jnp.int32, (1, n_cmp), 1) * CMP_STRIDE + (CMP_LEN - 1)
    imp_mat = _importance_matrix(n_cmp, n_sel_pad)
    jj = lax.broadcasted_iota(jnp.int32, (tp, n_sel_pad), 1)
    expand = jnp.where(lax.broadcasted_iota(jnp.int32, (tile_blocks, S_KTILE), 0)
                       == lax.broadcasted_iota(jnp.int32, (tile_blocks, S_KTILE), 1) // SEL_BLOCK,
                       1.0, 0.0).astype(bf16)

    def flash(qs, k_t, v_t, mask, m, l, acc):
        s = jnp.where(mask, jnp.dot(qs, k_t.astype(bf16), preferred_element_type=f32), NEG_INF)
        m_new = jnp.maximum(m, jnp.max(s, axis=-1, keepdims=True))
        alpha = jnp.exp(m - m_new)
        e = jnp.where(mask, jnp.exp(s - m_new), 0.0)
        pv = lax.dot_general(e.astype(bf16), v_t.astype(bf16), _NT, preferred_element_type=f32)
        return m_new, alpha * l + jnp.sum(e, axis=-1, keepdims=True), alpha * acc + pv

    qss, sels, o_cs = [], [], []
    for grp in range(NSA_KV):
        kcol0, vcol0 = grp * NSA_HEAD, NSA_KV_COLS + grp * NSA_HEAD
        qs = _stack_heads(q, grp, tp).astype(bf16)
        p_c = _softmax_rows(_dot_nt(qs, kvc_ref[:, kcol0:kcol0 + NSA_HEAD]), cmp_end <= q_pos_r)
        o_cs.append(_dot(p_c, kvc_ref[:, vcol0:vcol0 + NSA_HEAD]))
        p_sum = p_c[0:tp]
        for h in range(1, NSA_HPG):
            p_sum = p_sum + p_c[h * tp:(h + 1) * tp]
        score = _block_scores(_dot_sel(p_sum, imp_mat), jj, q_pos // SEL_BLOCK)
        rank = jnp.zeros((tp, n_sel_pad), f32)
        for j2 in range(n_sel):
            col = score[:, j2:j2 + 1]
            ge = jnp.where(col >= score, 1.0, 0.0)
            gt = jnp.where(col > score, 1.0, 0.0)
            rank = rank + jnp.where(jj > j2, ge, gt)
        sel = jnp.where((rank < SEL_TOPK) & (jj < n_sel), 1.0, 0.0).astype(bf16)
        sels.append(jnp.concatenate([sel] * NSA_HPG, axis=0))
        qss.append(qs)

    carries = [_flash_init(rows) for _ in range(NSA_KV)]
    n_tiles = past_len // S_KTILE
    for kt in range(n_tiles + 1):
        width = S_KTILE if kt < n_tiles else LANES
        k0 = kt * S_KTILE
        if kt < n_tiles:
            def wait(i, carry):
                page_copy(i).wait()
                return carry
            lax.fori_loop(kt * tile_pages, (kt + 1) * tile_pages, wait, 0)
        k_pos = k0 + lax.broadcasted_iota(jnp.int32, (1, width), 1)
        for grp in range(NSA_KV):
            krow0, vrow0 = grp * NSA_HEAD, NSA_KV_COLS + grp * NSA_HEAD
            blocks = sels[grp][:, kt * tile_blocks:(kt + 1) * tile_blocks]
            chosen = jnp.dot(blocks, expand[:, 0:width], preferred_element_type=f32)
            carries[grp] = flash(qss[grp], selbuf_ref[krow0:krow0 + NSA_HEAD, k0:k0 + width],
                                 selbuf_ref[vrow0:vrow0 + NSA_HEAD, k0:k0 + width],
                                 (chosen > 0.5) & (k_pos <= q_pos_r), *carries[grp])

    w_pos = past_len - n_win + lax.broadcasted_iota(jnp.int32, (1, winbuf_ref.shape[1]), 1)
    dist = jnp.where(w_pos >= 0, q_pos_r - w_pos, -1)
    for grp in range(NSA_KV):
        krow0, vrow0 = grp * NSA_HEAD, NSA_KV_COLS + grp * NSA_HEAD
        o_s = _flash_out(*carries[grp])
        o_w = _flash_out(*flash(qss[grp], winbuf_ref[krow0:krow0 + NSA_HEAD, :], winbuf_ref[vrow0:vrow0 + NSA_HEAD, :],
                                (dist >= 0) & (dist < WINDOW), *_flash_init(rows)))
        o = (_stack_gates(gates, grp, 0) * o_cs[grp] + _stack_gates(gates, grp, 1) * o_s
             + _stack_gates(gates, grp, 2) * o_w)
        for h in range(NSA_HPG):
            col = (grp * NSA_HPG + h) * NSA_HEAD
            obuf_ref[:, col:col + NSA_HEAD] = o[h * tp:(h + 1) * tp]
    o_ref[...] = obuf_ref[0:t_new, :]


def _nsa_sample(p_q, p_gate, kvc_cmp, cache_sel_t, page_table, kvs_new_t, win_state_t, kvw_new_t):
    b, t_new, _ = p_q.shape
    n_pages = page_table.shape[1]
    page = cache_sel_t.shape[2]
    past_len = n_pages * page
    n_cmp = kvc_cmp.shape[1]
    n_win = win_state_t.shape[2]
    tp = 8
    assert t_new <= tp and past_len % S_KTILE == 0 and S_KTILE % page == 0 and n_win % LANES == 0
    assert b * t_new == kvs_new_t.shape[1]
    n_sel_pad = -(-(-(-(past_len + t_new) // SEL_BLOCK)) // LANES) * LANES
    assert (past_len // S_KTILE + 1) * (S_KTILE // SEL_BLOCK) <= n_sel_pad
    per_seq = lambda n, w: pl.BlockSpec((None, n, w), lambda i, pt: (i, 0, 0))
    whole = lambda a: pl.BlockSpec(a.shape, lambda i, pt: (0,) * a.ndim, pipeline_mode=pl.Buffered(1))
    return pl.pallas_call(
        functools.partial(_nsa_sample_kernel, past_len),
        out_shape=jax.ShapeDtypeStruct((b, t_new, NSA_WIDTH), f32),
        grid_spec=pltpu.PrefetchScalarGridSpec(
            num_scalar_prefetch=1, grid=(b,),
            in_specs=[per_seq(t_new, NSA_WIDTH), per_seq(t_new, GATE_PAD), per_seq(n_cmp, KV_COLS),
                      pl.BlockSpec(memory_space=pl.ANY), whole(kvs_new_t), per_seq(KV_COLS, n_win),
                      whole(kvw_new_t)],
            out_specs=per_seq(t_new, NSA_WIDTH),
            scratch_shapes=[pltpu.VMEM((KV_COLS, past_len + LANES), f32), pltpu.VMEM((KV_COLS, n_win + LANES), f32),
                            pltpu.VMEM((tp, NSA_WIDTH), f32), pltpu.VMEM((tp, GATE_PAD), f32),
                            pltpu.VMEM((tp, NSA_WIDTH), f32), pltpu.SemaphoreType.DMA((past_len // S_KTILE,))]),
        compiler_params=_cparams(("arbitrary",)),
        name="nsa_sample",
    )(page_table, p_q, p_gate, kvc_cmp, cache_sel_t, kvs_new_t, win_state_t, kvw_new_t)


def _prep_weights(l, depth, ln_g, ln_b, ffn_w_gate, ffn_w_up, ffn_w_down, w_in, b_in, rw_mu, rw_w0, rw_w2, rw_a0,
                  rw_a2, rw_g2, rw_k_k, rw_k_a, rw_r_k, rw_ln_w, rw_ln_b, nsa_phi_pe, nsa_phi_w1, nsa_phi_w2,
                  w_out_a, w_out_b, w_o):
    d = D_MODEL
    nc = D_FF // FF_CHUNK
    up = lambda w: w.astype(bf16).reshape(d, nc, FF_CHUNK).transpose(1, 0, 2)
    down = lambda w: w.astype(bf16).reshape(nc, FF_CHUNK, d)
    c1 = RW_SHIFT
    c2 = c1 + NSA_WIDTH
    c3 = c2 + 3 * KV_COLS
    c4 = c3 + N_GATE
    cols = lambda a: jnp.concatenate(
        [a[..., :c3], a[..., c4:], a[..., c3:c4], jnp.zeros(a.shape[:-1] + (GATE_PAD - N_GATE,), a.dtype)], axis=-1)
    z = lambda r, c: jnp.zeros((r, c), f32)
    w = RW_WIDTH
    lora = jnp.concatenate([
        jnp.concatenate([rw_w2[l], z(W_LORA, w), z(W_LORA, w)], axis=1),
        jnp.concatenate([z(A_LORA, w), rw_a2[l], z(A_LORA, w)], axis=1),
        jnp.concatenate([z(G_LORA, w), z(G_LORA, w), rw_g2[l]], axis=1)], axis=0)
    hid = lax.broadcasted_iota(jnp.int32, (w, w), 0) // RW_HEAD == lax.broadcasted_iota(jnp.int32, (w, w), 1) // RW_HEAD
    vec = jnp.stack([rw_w0[l], rw_a0[l], rw_k_k[l], rw_k_a[l], rw_r_k[l].reshape(w), rw_ln_w[l], rw_ln_b[l],
                     jnp.zeros((w,), f32)])
    w1 = nsa_phi_w1[l].reshape(2, 2, CMP_STRIDE, NSA_HEAD, CMP_HID)
    eye = jnp.eye(NSA_KV, dtype=f32)
    big = jnp.einsum("kljde,kK,gG->jkgdKGle", w1, jnp.eye(2, dtype=f32), eye)
    pe = jnp.broadcast_to(nsa_phi_pe[l].reshape(2, 1, CMP_LEN * NSA_HEAD), (2, 8, CMP_LEN * NSA_HEAD))
    w_cols, b_cols = cols(w_in[l]), cols(b_in[l])
    seg_t = _segments(_PROJ, [name for name, _ in _PROJ_T])
    w_t = jnp.concatenate([w_cols[:, lo:hi] for lo, hi in seg_t], axis=1).T
    b_t = jnp.concatenate([b_cols[lo:hi] for lo, hi in seg_t])
    return dict(
        alpha=(2 * depth) ** 0.25,
        wg1=up(ffn_w_gate[l, 0]), wu1=up(ffn_w_up[l, 0]), wd1=down(ffn_w_down[l, 0]),
        wg2=up(ffn_w_gate[l, 1]), wu2=up(ffn_w_up[l, 1]), wd2=down(ffn_w_down[l, 1]),
        ln_g=ln_g[l], ln_b=ln_b[l],
        w_in=w_cols.astype(bf16), b_in=b_cols.reshape(1, -1),
        w_in_t=w_t.astype(bf16), b_in_t=b_t[:, None],
        phi_w2t=jnp.swapaxes(nsa_phi_w2[l], 1, 2).astype(bf16),
        rw_mu=rw_mu[l].reshape(1, -1), rw_lora=lora.astype(bf16), rw_vec=vec, rw_hsum=hid.astype(bf16),
        phi_big=big.reshape(CMP_ROW, 2 * NSA_KV * 2 * CMP_HID).astype(bf16), phi_pe=pe,
        phi_w1f=nsa_phi_w1[l].reshape(2, CMP_LEN * NSA_HEAD, CMP_HID).astype(bf16),
        phi_w2=nsa_phi_w2[l].astype(bf16),
        w_out_a=w_out_a[l].astype(bf16), w_out_b=w_out_b[l].astype(bf16), w_o=w_o[l].astype(bf16))


def _kv_state(p):
    return p.reshape(p.shape[:-1] + (2, NSA_KV, NSA_HEAD))


def _tokens_last(a):
    n, t = a.shape[:2]
    return a.transpose(0, 2, 3, 4, 1).reshape(n, KV_COLS, t)


def _kv_state_t(p_t):
    b, _, t = p_t.shape
    return p_t.reshape(b, 2, NSA_KV, NSA_HEAD, t).transpose(0, 4, 1, 2, 3)


def kernel(x_prompt, x_sample, c_prompt, c_sample, cache_kv_cmp, cache_kv_sel, state_kv_win, state_wkv, state_shift, page_table, w_ada, b_ada, ln_g, ln_b, ffn_w_gate, ffn_w_up, ffn_w_down, w_in, b_in, rw_mu, rw_w0, rw_w2, rw_a0, rw_a2, rw_g2, rw_k_k, rw_k_a, rw_r_k, rw_ln_w, rw_ln_b, nsa_phi_pe, nsa_phi_w1, nsa_phi_w2, w_out_a, w_out_b, w_o):
    bp, seq, d = x_prompt.shape
    bd, t_new, _ = x_sample.shape
    depth = w_ada.shape[0]
    n_phys, page = cache_kv_cmp.shape[1:3]
    n_pages = page_table.shape[1]
    n_win = state_kv_win.shape[2]
    rows_s = bd * t_new
    tm_p = 256
    assert seq % tm_p == 0 and seq % CMP_STRIDE == 0 and page % CMP_STRIDE == 0

    y_p, y_s = x_prompt, x_sample.reshape(1, rows_s, d)
    st_p, st_s = [], []
    for l in range(depth):
        wts = _prep_weights(l, depth, ln_g, ln_b, ffn_w_gate, ffn_w_up, ffn_w_down, w_in, b_in, rw_mu, rw_w0,
                            rw_w2, rw_a0, rw_a2, rw_g2, rw_k_k, rw_k_a, rw_r_k, rw_ln_w, rw_ln_b, nsa_phi_pe,
                            nsa_phi_w1, nsa_phi_w2, w_out_a, w_out_b, w_o)
        mod = _ada_mod(jnp.concatenate([c_prompt, c_sample], axis=0), w_ada[l], b_ada[l])
        mod_p = mod[:bp].reshape(bp, 9, 1, d)
        mod_s = jnp.repeat(mod[bp:].reshape(bd, 9, d), t_new, axis=0).reshape(rows_s, 9, d)
        mod_s = mod_s.transpose(1, 0, 2).reshape(1, 9, rows_s, d)

        x1, p_rw, p_kc, p_ks, p_kw, p_m, q_t, g_t, ks_t, kw_t = _ffn_in(
            y_p, mod_p, wts, tm_p, ("rw", "kc", "ks", "kw", "merge"), ("q", "gate", "ks", "kw"))
        ya, wkv_p = _rwkv(p_rw, jnp.zeros((bp, RW_SHIFT), f32), jnp.zeros((bp, RW_HEADS, RW_HEAD, RW_HEAD), f32),
                          wts)
        kvc_cmp, kvc_cmp_t = _compress_dense(p_kc.reshape(bp, seq // CMP_STRIDE, CMP_ROW), wts)
        yb_t = _nsa_prompt(q_t, g_t, kvc_cmp, kvc_cmp_t, p_ks, ks_t, p_kw, kw_t)
        y_p = _mix_ffn(x1, ya, yb_t, p_m, mod_p, wts, tm_p, True)
        n_keep = min(WINDOW, seq)
        st_p.append((_kv_state(p_kc), _kv_state_t(ks_t), _kv_state_t(kw_t[:, :, seq - n_keep:]), wkv_p,
                     p_rw[:, -1]))

        x1, p_rw, p_q, p_kc, p_ks, p_m, p_g, ks_t, kw_t = _ffn_in(
            y_s, mod_s, wts, rows_s, ("rw", "q", "kc", "ks", "merge", "gate"), ("ks", "kw"))
        per_seq = lambda a: a.reshape(bd, t_new, a.shape[-1])
        p_rw, p_q, p_kc, p_ks, p_g = map(per_seq, (p_rw, p_q, p_kc, p_ks, p_g))
        ya, wkv_s = _rwkv(p_rw, state_shift[l], state_wkv[l], wts)
        kvc_cmp = _compress_paged(_tokens_last(cache_kv_cmp[l]), page_table, wts)
        win_state_t = _tokens_last(state_kv_win[l])
        yb = _nsa_sample(p_q, p_g, kvc_cmp, _tokens_last(cache_kv_sel[l]), page_table, ks_t[0], win_state_t,
                         kw_t[0])
        y_s = _mix_ffn(x1, ya.reshape(1, rows_s, RW_WIDTH), yb.reshape(1, rows_s, NSA_WIDTH), p_m, mod_s, wts,
                       rows_s, False)
        kw_new_t = kw_t[0].reshape(KV_COLS, bd, t_new).transpose(1, 0, 2)
        new_win_t = jnp.concatenate([win_state_t, kw_new_t], axis=2)[:, :, t_new:]
        st_s.append((_kv_state(p_kc), _kv_state(p_ks), _kv_state_t(new_win_t), wkv_s, p_rw[:, -1]))

    stack = lambda sts, i: jnp.stack([s[i] for s in sts])
    return (y_p, y_s.reshape(bd, t_new, d)) + tuple(stack(st_p, i) for i in range(5)) + tuple(
        stack(st_s, i) for i in range(5))
```

```python
import functools
import math

import jax
import jax.numpy as jnp
from jax import lax
from jax.experimental import pallas as pl
from jax.experimental.pallas import tpu as pltpu

f32 = jnp.float32
bf16 = jnp.bfloat16

D_MODEL = 1024
RW_HEAD = 64
RW_HEADS = 8
RW_WIDTH = RW_HEADS * RW_HEAD
W_LORA = 64
A_LORA = 64
G_LORA = 128
LORA_IN = W_LORA + A_LORA + G_LORA
RW_GN_EPS = 64e-5
RW_SHIFT = 3 * RW_WIDTH + LORA_IN
NSA_HEAD = 64
NSA_HEADS = 8
NSA_KV = 2
NSA_HPG = NSA_HEADS // NSA_KV
NSA_WIDTH = NSA_HEADS * NSA_HEAD
NSA_KV_COLS = NSA_KV * NSA_HEAD
KV_COLS = 2 * NSA_KV_COLS
NSA_SCALE = NSA_HEAD ** -0.5
CMP_STRIDE = 16
CMP_LEN = 2 * CMP_STRIDE
CMP_HID = 128
CMP_ROW = CMP_STRIDE * KV_COLS
SEL_BLOCK = 64
SEL_TOPK = 16
WINDOW = 512
D_FF = 2816
N_GATE = 3 * NSA_HEADS
LN_EPS = 1e-5
NEG_INF = -1e30
BIG = 1e9

LANES = 128
GATE_PAD = LANES
FF_CHUNK = 256
RW_CHUNK = 64
RW_HPB = 4
RW_SEQS = 4
Q_TILE = 128
K_TILE = 256
S_KTILE = 2048
VMEM_LIMIT = 56 * 1024 * 1024

_NT = (((1,), (1,)), ((), ()))
_TN = (((0,), (0,)), ((), ()))
_HI = lax.Precision.HIGHEST


def _dot(a, b):
    return jnp.dot(a.astype(bf16), b.astype(bf16), preferred_element_type=f32)


def _dot_nt(a, b):
    return lax.dot_general(a.astype(bf16), b.astype(bf16), _NT, preferred_element_type=f32)


def _dot_f32(a, b, dims=None):
    if dims is None:
        return jnp.dot(a, b, preferred_element_type=f32, precision=_HI)
    return lax.dot_general(a, b, dims, preferred_element_type=f32, precision=_HI)


def _dot_sel(x, m01):
    hi = x.astype(bf16)
    r1 = x - hi.astype(f32)
    mid = r1.astype(bf16)
    lo = (r1 - mid.astype(f32)).astype(bf16)
    d = lambda t: jnp.dot(t, m01, preferred_element_type=f32)
    return d(hi) + d(mid) + d(lo)


def _layer_norm(x, g, b):
    mu = jnp.mean(x, axis=-1, keepdims=True)
    xc = x - mu
    var = jnp.mean(xc * xc, axis=-1, keepdims=True)
    return xc * lax.rsqrt(var + LN_EPS) * g + b


def _cparams(sem):
    return pltpu.CompilerParams(dimension_semantics=sem, vmem_limit_bytes=VMEM_LIMIT)


def _resident(shape):
    nd = len(shape)
    return pl.BlockSpec(shape, lambda *_: (0,) * nd, pipeline_mode=pl.Buffered(1))


def _ada_kernel(c_ref, w_ref, b_ref, o_ref):
    c = c_ref[...]
    o_ref[...] = _dot(c * jax.nn.sigmoid(c), w_ref[...]) + b_ref[...]


def _ada_mod(c, w_ada, b_ada):
    rows, d = c.shape
    n = w_ada.shape[1]
    tn = d
    return pl.pallas_call(
        _ada_kernel,
        out_shape=jax.ShapeDtypeStruct((rows, n), f32),
        grid=(n // tn,),
        in_specs=[pl.BlockSpec((rows, d), lambda j: (0, 0)),
                  pl.BlockSpec((d, tn), lambda j: (0, j)),
                  pl.BlockSpec((1, tn), lambda j: (0, j))],
        out_specs=pl.BlockSpec((rows, tn), lambda j: (0, j)),
        compiler_params=_cparams(("parallel",)),
        name="ada_mod",
    )(c, w_ada, b_ada.reshape(1, n))


def _ffn(u, wg_ref, wu_ref, wd_ref, acc_ref):
    ub = u.astype(bf16)
    acc_ref[...] = jnp.zeros(acc_ref.shape, f32)

    def body(c, carry):
        hg = jnp.dot(ub, wg_ref[c], preferred_element_type=f32)
        hu = jnp.dot(ub, wu_ref[c], preferred_element_type=f32)
        h = hg * jax.nn.sigmoid(hg) * hu
        acc_ref[...] += jnp.dot(h.astype(bf16), wd_ref[c], preferred_element_type=f32)
        return carry

    lax.fori_loop(0, wg_ref.shape[0], body, 0)
    return acc_ref[...]


def _ffn_in_kernel(alpha, seg, seg_t, x_ref, mod_ref, wg_ref, wu_ref, wd_ref, lng_ref, lnb_ref, win_ref, bin_ref,
                   wint_ref, bint_ref, x1_ref, *rest):
    out_refs, acc_ref = rest[:-1], rest[-1]
    x = x_ref[...]
    u = x * (1.0 + mod_ref[1]) + mod_ref[0]
    f = _ffn(u, wg_ref, wu_ref, wd_ref, acc_ref)
    x1 = _layer_norm(alpha * x + 0.5 * (1.0 + mod_ref[2]) * f, lng_ref[0:1, :], lnb_ref[0:1, :])
    x1_ref[...] = x1
    u2 = (x1 * (1.0 + mod_ref[4]) + mod_ref[3]).astype(bf16)
    for o_ref, (lo, hi) in zip(out_refs, seg):
        o_ref[...] = jnp.dot(u2, win_ref[:, lo:hi], preferred_element_type=f32) + bin_ref[:, lo:hi]
    for o_ref, (lo, hi) in zip(out_refs[len(seg):], seg_t):
        o_ref[...] = lax.dot_general(wint_ref[lo:hi, :], u2, _NT, preferred_element_type=f32) + bint_ref[lo:hi, :]


_PROJ = (("rw", RW_SHIFT), ("q", NSA_WIDTH), ("kc", KV_COLS), ("ks", KV_COLS), ("kw", KV_COLS),
         ("merge", 2 * D_MODEL), ("gate", GATE_PAD))
_PROJ_T = (("q", NSA_WIDTH), ("gate", GATE_PAD), ("ks", KV_COLS), ("kw", KV_COLS))


def _segments(table, names):
    seg, lo = {}, 0
    for name, w in table:
        seg[name] = (lo, lo + w)
        lo += w
    return tuple(seg[n] for n in names)


def _ffn_in(x, mod, wts, tm, names, names_t):
    g, t, d = x.shape
    r = mod.shape[2]
    seg, seg_t = _segments(_PROJ, names), _segments(_PROJ_T, names_t)
    row = lambda w: pl.BlockSpec((None, tm, w), lambda i, j: (i, j, 0))
    col = lambda w: pl.BlockSpec((None, w, tm), lambda i, j: (i, 0, j))
    return pl.pallas_call(
        functools.partial(_ffn_in_kernel, wts["alpha"], seg, seg_t),
        out_shape=([jax.ShapeDtypeStruct((g, t, d), f32)]
                   + [jax.ShapeDtypeStruct((g, t, hi - lo), f32) for lo, hi in seg]
                   + [jax.ShapeDtypeStruct((g, hi - lo, t), f32) for lo, hi in seg_t]),
        grid=(g, t // tm),
        in_specs=[row(d),
                  pl.BlockSpec((None, 9, r, d), lambda i, j: (i, 0, 0, 0)),
                  _resident(wts["wg1"].shape), _resident(wts["wu1"].shape), _resident(wts["wd1"].shape),
                  _resident(wts["ln_g"].shape), _resident(wts["ln_b"].shape),
                  _resident(wts["w_in"].shape), _resident(wts["b_in"].shape),
                  _resident(wts["w_in_t"].shape), _resident(wts["b_in_t"].shape)],
        out_specs=[row(d)] + [row(hi - lo) for lo, hi in seg] + [col(hi - lo) for lo, hi in seg_t],
        scratch_shapes=[pltpu.VMEM((tm, d), f32)],
        compiler_params=_cparams(("parallel", "parallel")),
        name="ffn_in",
    )(x, mod, wts["wg1"], wts["wu1"], wts["wd1"], wts["ln_g"], wts["ln_b"], wts["w_in"], wts["b_in"],
      wts["w_in_t"], wts["b_in_t"])


def _mix_ffn_kernel(alpha, yb_on_lanes, x1_ref, ya_ref, yb_ref, pm_ref, mod_ref, woa_ref, wob_ref, wo_ref,
                    wg_ref, wu_ref, wd_ref, lng_ref, lnb_ref, o_ref, acc_ref):
    x1 = x1_ref[...]
    pm = pm_ref[...]
    d = x1.shape[-1]
    ga = jax.nn.sigmoid(pm[:, :d])
    gb = jax.nn.sigmoid(pm[:, d:])
    if yb_on_lanes:
        yb_out = lax.dot_general(yb_ref[...].astype(bf16), wob_ref[...], _TN, preferred_element_type=f32)
    else:
        yb_out = _dot(yb_ref[...], wob_ref[...])
    merged = ga * _dot(ya_ref[...], woa_ref[...]) + gb * yb_out
    m = _dot(merged, wo_ref[...])
    x2 = _layer_norm(alpha * x1 + (1.0 + mod_ref[5]) * m, lng_ref[1:2, :], lnb_ref[1:2, :])
    u = x2 * (1.0 + mod_ref[7]) + mod_ref[6]
    f = _ffn(u, wg_ref, wu_ref, wd_ref, acc_ref)
    o_ref[...] = _layer_norm(alpha * x2 + 0.5 * (1.0 + mod_ref[8]) * f, lng_ref[2:3, :], lnb_ref[2:3, :])


def _mix_ffn(x1, ya, yb, pm, mod, wts, tm, yb_on_lanes):
    g, t, d = x1.shape
    r = mod.shape[2]
    row = lambda w: pl.BlockSpec((None, tm, w), lambda i, j: (i, j, 0))
    yb_spec = pl.BlockSpec((None, NSA_WIDTH, tm), lambda i, j: (i, 0, j)) if yb_on_lanes else row(NSA_WIDTH)
    return pl.pallas_call(
        functools.partial(_mix_ffn_kernel, wts["alpha"], yb_on_lanes),
        out_shape=jax.ShapeDtypeStruct((g, t, d), f32),
        grid=(g, t // tm),
        in_specs=[row(d), row(ya.shape[-1]), yb_spec, row(pm.shape[-1]),
                  pl.BlockSpec((None, 9, r, d), lambda i, j: (i, 0, 0, 0)),
                  _resident(wts["w_out_a"].shape), _resident(wts["w_out_b"].shape), _resident(wts["w_o"].shape),
                  _resident(wts["wg2"].shape), _resident(wts["wu2"].shape), _resident(wts["wd2"].shape),
                  _resident(wts["ln_g"].shape), _resident(wts["ln_b"].shape)],
        out_specs=row(d),
        scratch_shapes=[pltpu.VMEM((tm, d), f32)],
        compiler_params=_cparams(("parallel", "parallel")),
        name="mix_ffn",
    )(x1, ya, yb, pm, mod, wts["w_out_a"], wts["w_out_b"], wts["w_o"], wts["wg2"], wts["wu2"], wts["wd2"],
      wts["ln_g"], wts["ln_b"])


def _rwkv_kernel(t_blk, p_ref, prev_ref, s0_ref, mu_ref, wl_ref, vec_ref, hsum_ref,
                 y_ref, sout_ref, state_ref, prevrow_ref, stage_ref, yn_ref):
    n_seq = p_ref.shape[0]
    ci = pl.program_id(1)

    @pl.when(ci == 0)
    def _():
        state_ref[...] = jnp.zeros(state_ref.shape, f32)
        for s in range(n_seq):
            for h in range(RW_HEADS):
                d0 = (h % RW_HPB) * RW_HEAD
                state_ref[s, h // RW_HPB, d0:d0 + RW_HEAD, d0:d0 + RW_HEAD] = s0_ref[s, h]
        prevrow_ref[...] = prev_ref[...]

    c = stage_ref.shape[1]
    gw = RW_HPB * RW_HEAD
    rows = RW_HPB * c
    br = lax.broadcasted_iota(jnp.int32, (rows, gw), 0)
    bc = lax.broadcasted_iota(jnp.int32, (rows, gw), 1)
    same_head = (br // c) == (bc // RW_HEAD)
    tr = lax.broadcasted_iota(jnp.int32, (rows, rows), 0) % c
    tc = lax.broadcasted_iota(jnp.int32, (rows, rows), 1) % c
    strict = tr > tc
    incl = tr >= tc
    blocks = lambda x: jnp.where(same_head, jnp.concatenate([x] * RW_HPB, axis=0), 0.0).astype(bf16)
    nt = lambda x, y: lax.dot_general(x, y, _NT, preferred_element_type=f32)
    tn = lambda x, y: lax.dot_general(x, y, _TN, preferred_element_type=f32)
    mm = lambda x, y: jnp.dot(x, y, preferred_element_type=f32)
    n_fac = max(1, (c - 1).bit_length())

    pre = [_rwkv_features(t_blk, p_ref.at[s], mu_ref, wl_ref, vec_ref, hsum_ref, prevrow_ref.at[s],
                          stage_ref.at[s]) for s in range(n_seq)]
    chains = [(s, grp) for s in range(n_seq) for grp in range(RW_HEADS // RW_HPB)]
    ops, low, u = {}, {}, {}
    for s, grp in chains:
        gl = slice(grp * gw, (grp + 1) * gw)
        ops[s, grp] = [blocks(pre[s][name][:, gl]) for name in ("at", "rt", "bt", "kt", "bh", "kh", "v")]
    for ch in chains:
        at4, rt4, bt4, kt4, bh4, kh4, v4 = ops[ch]
        s_b = state_ref[ch].astype(bf16)
        low[ch] = jnp.where(strict, nt(at4, bt4), 0.0).astype(bf16)
        u[ch] = nt(at4, s_b) + mm(jnp.where(strict, nt(at4, kt4), 0.0).astype(bf16), v4)
    for i in range(n_fac):
        for ch in chains:
            u[ch] = u[ch] + mm(low[ch], u[ch].astype(bf16))
        if i + 1 < n_fac:
            for ch in chains:
                low[ch] = mm(low[ch], low[ch]).astype(bf16)
    for ch in chains:
        s, grp = ch
        gl = slice(grp * gw, (grp + 1) * gw)
        at4, rt4, bt4, kt4, bh4, kh4, v4 = ops[ch]
        s_bd = state_ref[ch]
        u_b = u[ch].astype(bf16)
        y4 = (nt(rt4, s_bd.astype(bf16)) + mm(jnp.where(incl, nt(rt4, bt4), 0.0).astype(bf16), u_b)
              + mm(jnp.where(incl, nt(rt4, kt4), 0.0).astype(bf16), v4))
        state_ref[ch] = s_bd * pre[s]["p_last"][:, gl] + tn(u_b, bh4) + tn(v4, kh4)
        yg = y4[0:c]
        for hh in range(1, RW_HPB):
            yg = yg + y4[hh * c:(hh + 1) * c]
        yn_ref[s, :, gl] = yg

    hsum = hsum_ref[...]
    ln_w, ln_b = vec_ref[5:6, :], vec_ref[6:7, :]
    for s in range(n_seq):
        yh = yn_ref[s]
        mean = _dot_sel(yh, hsum) * (1.0 / RW_HEAD)
        yc = yh - mean
        var = _dot_sel(yc * yc, hsum) * (1.0 / RW_HEAD)
        y = (yc * lax.rsqrt(var + RW_GN_EPS) * ln_w + ln_b + pre[s]["bonus"]) * pre[s]["g"]
        y_ref[s] = y[0:t_blk, :] if t_blk < c else y

    @pl.when(ci == pl.num_programs(1) - 1)
    def _():
        for s in range(n_seq):
            for h in range(RW_HEADS):
                d0 = (h % RW_HPB) * RW_HEAD
                sout_ref[s, h] = state_ref[s, h // RW_HPB, d0:d0 + RW_HEAD, d0:d0 + RW_HEAD]


def _rwkv_features(t_blk, p_ref, mu_ref, wl_ref, vec_ref, hsum_ref, prevrow_ref, stage_ref):
    c = stage_ref.shape[0]
    w = RW_WIDTH
    if t_blk < c:
        stage_ref[...] = jnp.zeros(stage_ref.shape, f32)
        stage_ref[0:t_blk, :] = p_ref[...]
        p = stage_ref[...]
    else:
        p = p_ref[...]
    ridx = lax.broadcasted_iota(jnp.int32, (c, 1), 0)
    valid = ridx < t_blk
    p_prev = jnp.where(ridx == 0, prevrow_ref[...], pltpu.roll(p, 1, 0))
    prevrow_ref[...] = p[c - 1:c, :]
    xs = p + (p_prev - p) * mu_ref[...]

    tail = xs[:, 3 * w:]
    li = lax.broadcasted_iota(jnp.int32, tail.shape, 1)
    act = jnp.where(li < W_LORA, jnp.tanh(tail),
                    jnp.where(li < W_LORA + A_LORA, tail, jax.nn.sigmoid(tail)))
    lora = _dot(act, wl_ref[...])
    w0, a0, k_k, k_a, r_k = (vec_ref[i:i + 1, :] for i in range(5))
    z = -(w0 + lora[:, :w])
    softplus = jnp.maximum(z, 0.0) + jnp.log1p(jnp.exp(-jnp.abs(z)))
    lw = -jnp.exp(-softplus - 0.5)
    a = jax.nn.sigmoid(a0 + lora[:, w:2 * w])
    g = lora[:, 2 * w:]
    r, k, v = xs[:, :w], xs[:, w:2 * w], xs[:, 2 * w:3 * w]
    hsum = hsum_ref[...]
    kk = k * k_k
    kk = kk * lax.rsqrt(jnp.maximum(_dot_sel(kk * kk, hsum), 1e-24))
    k2 = k * (1.0 + (a - 1.0) * k_a)
    bonus = _dot_sel(r * k2 * r_k, hsum) * v
    if t_blk < c:
        lw = jnp.where(valid, lw, 0.0)
        kk = jnp.where(valid, kk, 0.0)
        k2 = jnp.where(valid, k2, 0.0)
        v = jnp.where(valid, v, 0.0)
    b = kk * a

    rr = lax.broadcasted_iota(jnp.int32, (c, c), 0)
    cc = lax.broadcasted_iota(jnp.int32, (c, c), 1)
    cum = _dot_f32(jnp.where(rr >= cc, 1.0, 0.0).astype(f32), lw)
    cum_last = cum[c - 1:c, :]
    e_neg = jnp.exp(-cum)
    e_rem = jnp.exp(cum_last - cum)
    at = -kk * jnp.exp(cum - lw)
    rt = r * jnp.exp(cum)
    bt, kt = b * e_neg, k2 * e_neg
    bh, kh = b * e_rem, k2 * e_rem
    p_last = jnp.exp(cum_last)
    return dict(at=at, rt=rt, bt=bt, kt=kt, bh=bh, kh=kh, v=v, p_last=p_last, bonus=bonus, g=g)


def _rwkv(p_rw, prev, s0, wts):
    g, t, ws = p_rw.shape
    c = RW_CHUNK
    t_blk = min(c, t)
    sb = RW_SEQS
    assert t % t_blk == 0 and g % sb == 0
    gw = RW_HPB * RW_HEAD
    return pl.pallas_call(
        functools.partial(_rwkv_kernel, t_blk),
        out_shape=[jax.ShapeDtypeStruct((g, t, RW_WIDTH), f32), jax.ShapeDtypeStruct(s0.shape, f32)],
        grid=(g // sb, t // t_blk),
        in_specs=[pl.BlockSpec((sb, t_blk, ws), lambda i, j: (i, j, 0)),
                  pl.BlockSpec((sb, 1, ws), lambda i, j: (i, 0, 0)),
                  pl.BlockSpec((sb,) + s0.shape[1:], lambda i, j: (i, 0, 0, 0)),
                  _resident(wts["rw_mu"].shape), _resident(wts["rw_lora"].shape),
                  _resident(wts["rw_vec"].shape), _resident(wts["rw_hsum"].shape)],
        out_specs=[pl.BlockSpec((sb, t_blk, RW_WIDTH), lambda i, j: (i, j, 0)),
                   pl.BlockSpec((sb,) + s0.shape[1:], lambda i, j: (i, 0, 0, 0))],
        scratch_shapes=[pltpu.VMEM((sb, RW_HEADS // RW_HPB, gw, gw), f32), pltpu.VMEM((sb, 1, ws), f32),
                        pltpu.VMEM((sb, c, ws), f32), pltpu.VMEM((sb, c, RW_WIDTH), f32)],
        compiler_params=_cparams(("parallel", "arbitrary")),
        name="rwkv",
    )(p_rw, prev.reshape(g, 1, ws), s0, wts["rw_mu"], wts["rw_lora"], wts["rw_vec"], wts["rw_hsum"])


def _gelu_tanh(x):
    return x * (0.5 * (1.0 + jnp.tanh(math.sqrt(2.0 / math.pi) * (x + 0.044715 * (x * x * x)))))


def _compress_rows(x, wbig_ref, pe_ref, w1f_ref, w2_ref, o_ref, w2t_ref=None, ot_ref=None):
    n = x.shape[0]
    hid = _dot(x, wbig_ref[...])
    for kv in range(2):
        pe_term = _dot(pe_ref[kv], w1f_ref[kv])[0:1, :]
        for grp in range(NSA_KV):
            base = (kv * NSA_KV + grp) * 2 * CMP_HID
            lo = hid[:, base:base + CMP_HID]
            hi = pltpu.roll(hid[:, base + CMP_HID:base + 2 * CMP_HID], n - 1, 0)
            act = _gelu_tanh(lo + hi + pe_term).astype(bf16)
            col = (kv * NSA_KV + grp) * NSA_HEAD
            o_ref[:, col:col + NSA_HEAD] = jnp.dot(act, w2_ref[kv], preferred_element_type=f32)
            if ot_ref is not None:
                ot_ref[col:col + NSA_HEAD, :] = lax.dot_general(w2t_ref[kv], act, _NT, preferred_element_type=f32)


def _compress_dense_kernel(x_ref, wbig_ref, pe_ref, w1f_ref, w2_ref, w2t_ref, o_ref, ot_ref):
    _compress_rows(x_ref[...], wbig_ref, pe_ref, w1f_ref, w2_ref, o_ref, w2t_ref, ot_ref)


def _compress_dense(x, wts):
    b, n, _ = x.shape
    return pl.pallas_call(
        _compress_dense_kernel,
        out_shape=[jax.ShapeDtypeStruct((b, n, KV_COLS), f32), jax.ShapeDtypeStruct((b, KV_COLS, n), f32)],
        grid=(b,),
        in_specs=[pl.BlockSpec((None, n, CMP_ROW), lambda i: (i, 0, 0)),
                  _resident(wts["phi_big"].shape), _resident(wts["phi_pe"].shape),
                  _resident(wts["phi_w1f"].shape), _resident(wts["phi_w2"].shape),
                  _resident(wts["phi_w2t"].shape)],
        out_specs=[pl.BlockSpec((None, n, KV_COLS), lambda i: (i, 0, 0)),
                   pl.BlockSpec((None, KV_COLS, n), lambda i: (i, 0, 0))],
        compiler_params=_cparams(("parallel",)),
        name="compress_dense",
    )(x, wts["phi_big"], wts["phi_pe"], wts["phi_w1f"], wts["phi_w2"], wts["phi_w2t"])


CMP_GROUP = 2 * LANES
CMP_UNROLL = 4


def _compress_paged_kernel(pt_ref, cache_ref, wbig_ref, pe_ref, w1f_ref, w2_ref, o_ref, pbuf_ref, xbuf_ref, sem):
    b = pl.program_id(0)
    n_pages = pt_ref.shape[1]
    page = cache_ref.shape[2]
    pages_per_group = CMP_GROUP // page
    rows_per_group = CMP_GROUP // CMP_STRIDE

    def page_copy(seq, i):
        return pltpu.make_async_copy(cache_ref.at[pt_ref[seq, i]], pbuf_ref.at[:, pl.ds(i * page, page)],
                                     sem.at[i // pages_per_group])

    def start_all(seq):
        def start(i, carry):
            page_copy(seq, i).start()
            return carry
        lax.fori_loop(0, n_pages, start, 0)

    @pl.when(b == 0)
    def _():
        start_all(0)

    r = lax.broadcasted_iota(jnp.int32, (CMP_GROUP, CMP_GROUP), 0)
    c = lax.broadcasted_iota(jnp.int32, (CMP_GROUP, CMP_GROUP), 1)
    perm = jnp.where(c == CMP_STRIDE * (r % rows_per_group) + r // rows_per_group, 1.0, 0.0).astype(bf16)

    def regroup(step, carry):
        for pg in range(CMP_UNROLL * pages_per_group):
            page_copy(b, step * CMP_UNROLL * pages_per_group + pg).wait()
        for k in range(CMP_UNROLL):
            gi = step * CMP_UNROLL + k
            t0 = pl.multiple_of(gi * CMP_GROUP, CMP_GROUP)
            tokens = pbuf_ref[:, pl.ds(t0, CMP_GROUP)].astype(bf16)
            rows = lax.dot_general(perm, tokens, _NT, preferred_element_type=f32)
            r0 = pl.multiple_of(gi * rows_per_group, rows_per_group)
            for j in range(CMP_STRIDE):
                xbuf_ref[pl.ds(r0, rows_per_group), j * KV_COLS:(j + 1) * KV_COLS] = (
                    rows[j * rows_per_group:(j + 1) * rows_per_group].astype(bf16))
        return carry

    n_groups = n_pages // pages_per_group
    lax.fori_loop(0, n_groups // CMP_UNROLL, regroup, 0)

    @pl.when(b + 1 < pl.num_programs(0))
    def _():
        start_all(b + 1)

    _compress_rows(xbuf_ref[...], wbig_ref, pe_ref, w1f_ref, w2_ref, o_ref)


def _compress_paged(cache_t, page_table, wts):
    b, n_pages = page_table.shape
    page = cache_t.shape[2]
    n = n_pages * page // CMP_STRIDE
    assert CMP_GROUP % page == 0 and (n_pages * page) % (CMP_GROUP * CMP_UNROLL) == 0
    full = lambda a: pl.BlockSpec(a.shape, lambda i, pt: (0,) * a.ndim, pipeline_mode=pl.Buffered(1))
    return pl.pallas_call(
        _compress_paged_kernel,
        out_shape=jax.ShapeDtypeStruct((b, n, KV_COLS), f32),
        grid_spec=pltpu.PrefetchScalarGridSpec(
            num_scalar_prefetch=1, grid=(b,),
            in_specs=[pl.BlockSpec(memory_space=pl.ANY), full(wts["phi_big"]), full(wts["phi_pe"]),
                      full(wts["phi_w1f"]), full(wts["phi_w2"])],
            out_specs=pl.BlockSpec((None, n, KV_COLS), lambda i, pt: (i, 0, 0)),
            scratch_shapes=[pltpu.VMEM((KV_COLS, n_pages * page), f32), pltpu.VMEM((n, CMP_ROW), bf16),
                            pltpu.SemaphoreType.DMA((n_pages * page // CMP_GROUP,))]),
        compiler_params=_cparams(("arbitrary",)),
        name="compress_paged",
    )(page_table, cache_t, wts["phi_big"], wts["phi_pe"], wts["phi_w1f"], wts["phi_w2"])


def _softmax_rows(s, mask):
    s = jnp.where(mask, s, NEG_INF)
    e = jnp.where(mask, jnp.exp(s - jnp.max(s, axis=-1, keepdims=True)), 0.0)
    return e / jnp.maximum(jnp.sum(e, axis=-1, keepdims=True), 1e-30)


def _flash_step(q, k, v, mask, m, l, acc):
    s = jnp.where(mask, _dot_nt(q, k), NEG_INF)
    m_new = jnp.maximum(m, jnp.max(s, axis=-1, keepdims=True))
    alpha = jnp.exp(m - m_new)
    e = jnp.where(mask, jnp.exp(s - m_new), 0.0)
    return m_new, alpha * l + jnp.sum(e, axis=-1, keepdims=True), alpha * acc + _dot(e, v)


def _flash_init(rows):
    return (jnp.full((rows, 1), NEG_INF, f32), jnp.zeros((rows, 1), f32), jnp.zeros((rows, NSA_HEAD), f32))


def _flash_out(m, l, acc):
    return acc / jnp.maximum(l, 1e-30)


def _importance_matrix(n_cmp_rows, n_sel_cols):
    ratio = SEL_BLOCK // CMP_STRIDE
    i = lax.broadcasted_iota(jnp.int32, (n_cmp_rows, n_sel_cols), 0)
    j = lax.broadcasted_iota(jnp.int32, (n_cmp_rows, n_sel_cols), 1)
    return jnp.where((i >= ratio * j - 1) & (i <= ratio * j + ratio - 1), 1.0, 0.0).astype(bf16)


def _block_scores(imp, j, cur):
    forced = (j == 0) | (j == cur) | (j == cur - 1)
    return jnp.where(j <= cur, jnp.where(forced, BIG, imp), -BIG)


def _stack_heads(x, grp, rows):
    return jnp.concatenate(
        [x[:, (grp * NSA_HPG + h) * NSA_HEAD:(grp * NSA_HPG + h + 1) * NSA_HEAD] for h in range(NSA_HPG)], axis=0)


def _stack_gates(gates, grp, branch):
    return jnp.concatenate(
        [gates[:, 3 * (grp * NSA_HPG + h) + branch:3 * (grp * NSA_HPG + h) + branch + 1] for h in range(NSA_HPG)],
        axis=0)


def _flash_step_t(qt, k, vt, bias, m, l, acc):
    s = jnp.dot(k.astype(bf16), qt, preferred_element_type=f32) + bias
    m_new = jnp.maximum(m, jnp.max(s, axis=0, keepdims=True))
    alpha = jnp.exp(m - m_new)
    e = jnp.exp(s - m_new)
    return (m_new, alpha * l + jnp.sum(e, axis=0, keepdims=True),
            alpha * acc + jnp.dot(vt.astype(bf16), e.astype(bf16), preferred_element_type=f32))


M_FLOOR = 0.1 * NEG_INF


def _flash_init_t(cols):
    return (jnp.full((1, cols), M_FLOOR, f32), jnp.zeros((1, cols), f32), jnp.zeros((NSA_HEAD, cols), f32))


def _lanes(x, n):
    return jnp.concatenate([x] * n, axis=1)


def _nsa_prompt_kernel(qt_ref, gt_ref, kc_ref, vct_ref, ks_ref, vst_ref, kw_ref, vwt_ref, o_ref):
    tq = qt_ref.shape[1]
    n_cmp = kc_ref.shape[0]
    n_sel = ks_ref.shape[0] // SEL_BLOCK
    i = pl.program_id(1)
    t0 = i * tq
    cols = NSA_HPG * tq
    gates = jax.nn.sigmoid(gt_ref[...])
    q_pos = t0 + lax.broadcasted_iota(jnp.int32, (1, tq), 1)
    q_pos_c = _lanes(q_pos, NSA_HPG)
    cmp_end = lax.broadcasted_iota(jnp.int32, (n_cmp, 1), 0) * CMP_STRIDE + (CMP_LEN - 1)
    ratio = SEL_BLOCK // CMP_STRIDE
    ij = lax.broadcasted_iota(jnp.int32, (n_sel, n_cmp), 0)
    ii = lax.broadcasted_iota(jnp.int32, (n_sel, n_cmp), 1)
    imp_mat = jnp.where((ii >= ratio * ij - 1) & (ii <= ratio * ij + ratio - 1), 1.0, 0.0).astype(bf16)
    jt = lax.broadcasted_iota(jnp.int32, (n_sel, tq), 0)
    cur_t = (t0 + lax.broadcasted_iota(jnp.int32, (n_sel, tq), 1)) // SEL_BLOCK
    krow = lax.broadcasted_iota(jnp.int32, (K_TILE, 1), 0)
    er = lax.broadcasted_iota(jnp.int32, (K_TILE, n_sel), 0) // SEL_BLOCK
    ej = lax.broadcasted_iota(jnp.int32, (K_TILE, n_sel), 1)
    groups = range(NSA_KV)
    heads_of = lambda grp: range(grp * NSA_HPG, (grp + 1) * NSA_HPG)
    qts, sels, o_cs = [], [], []

    for grp in groups:
        d0 = grp * NSA_HEAD
        heads = heads_of(grp)
        qt = jnp.concatenate([qt_ref[h * NSA_HEAD:(h + 1) * NSA_HEAD, :] for h in heads], axis=1)
        qt = (qt * NSA_SCALE).astype(bf16)
        mask_c = cmp_end <= q_pos_c
        s = jnp.where(mask_c, jnp.dot(kc_ref[:, d0:d0 + NSA_HEAD].astype(bf16), qt, preferred_element_type=f32),
                      NEG_INF)
        e = jnp.where(mask_c, jnp.exp(s - jnp.max(s, axis=0, keepdims=True)), 0.0)
        p_c = e / jnp.maximum(jnp.sum(e, axis=0, keepdims=True), 1e-30)
        o_c = jnp.dot(vct_ref[d0:d0 + NSA_HEAD, :].astype(bf16), p_c.astype(bf16), preferred_element_type=f32)
        p_sum = p_c[:, 0:tq]
        for h in range(1, NSA_HPG):
            p_sum = p_sum + p_c[:, h * tq:(h + 1) * tq]
        hi = p_sum.astype(bf16)
        r1 = p_sum - hi.astype(f32)
        mid = r1.astype(bf16)
        lo = (r1 - mid.astype(f32)).astype(bf16)
        imp = sum(jnp.dot(imp_mat, part, preferred_element_type=f32) for part in (hi, mid, lo))
        score = _block_scores(imp, jt, cur_t)
        rank = jnp.zeros((n_sel, tq), f32)
        for j2 in range(n_sel):
            row = score[j2:j2 + 1, :]
            ge = jnp.where(row >= score, 1.0, 0.0)
            gt = jnp.where(row > score, 1.0, 0.0)
            rank = rank + jnp.where(jt > j2, ge, gt)
        qts.append(qt)
        sels.append(jnp.where(rank < SEL_TOPK, 1.0, 0.0).astype(bf16))
        o_cs.append(o_c)

    def sel_step(kt, carry):
        k0 = pl.multiple_of(kt * K_TILE, K_TILE)
        expand = jnp.where(ej == kt * (K_TILE // SEL_BLOCK) + er, 1.0, 0.0).astype(bf16)
        causal = (k0 + krow) <= q_pos
        out = ()
        for grp in groups:
            d0 = grp * NSA_HEAD
            chosen = jnp.dot(expand, sels[grp], preferred_element_type=f32)
            bias = jnp.where(causal & (chosen > 0.5), 0.0, NEG_INF)
            out += _flash_step_t(qts[grp], ks_ref[pl.ds(k0, K_TILE), d0:d0 + NSA_HEAD],
                                 vst_ref[d0:d0 + NSA_HEAD, pl.ds(k0, K_TILE)], _lanes(bias, NSA_HPG),
                                 *carry[3 * grp:3 * grp + 3])
        return out

    last = (t0 + tq - 1) // K_TILE
    sel_acc = lax.fori_loop(0, last + 1, sel_step, _flash_init_t(cols) * NSA_KV)

    def win_step(kt, carry):
        k0 = pl.multiple_of(kt * K_TILE, K_TILE)
        dist = q_pos - (k0 + krow)
        bias = _lanes(jnp.where((dist >= 0) & (dist < WINDOW), 0.0, NEG_INF), NSA_HPG)
        out = ()
        for grp in groups:
            d0 = grp * NSA_HEAD
            out += _flash_step_t(qts[grp], kw_ref[pl.ds(k0, K_TILE), d0:d0 + NSA_HEAD],
                                 vwt_ref[d0:d0 + NSA_HEAD, pl.ds(k0, K_TILE)], bias, *carry[3 * grp:3 * grp + 3])
        return out

    first = jnp.maximum(t0 - WINDOW, 0) // K_TILE
    win_acc = lax.fori_loop(first, last + 1, win_step, _flash_init_t(cols) * NSA_KV)

    for grp in groups:
        heads = heads_of(grp)
        o_s = _flash_out(*sel_acc[3 * grp:3 * grp + 3])
        o_w = _flash_out(*win_acc[3 * grp:3 * grp + 3])
        gate = lambda br: jnp.concatenate([gates[3 * h + br:3 * h + br + 1, :] for h in heads], axis=1)
        o = gate(0) * o_cs[grp] + gate(1) * o_s + gate(2) * o_w
        for n, h in enumerate(heads):
            o_ref[h * NSA_HEAD:(h + 1) * NSA_HEAD, :] = o[:, n * tq:(n + 1) * tq]


def _nsa_prompt(qt, gt, kvc, kvct, kvs, kvst, kvw, kvwt):
    b, _, t = qt.shape
    n_cmp = kvc.shape[1]
    tq = Q_TILE
    assert t % tq == 0 and t % K_TILE == 0 and K_TILE % SEL_BLOCK == 0
    keys =lambda r: pl.BlockSpec((None, r, NSA_KV_COLS), lambda i, j: (i, 0, 0))
    vals = lambda c: pl.BlockSpec((None, NSA_KV_COLS, c), lambda i, j: (i, 1, 0))
    return pl.pallas_call(
        _nsa_prompt_kernel,
        out_shape=jax.ShapeDtypeStruct((b, NSA_WIDTH, t), f32),
        grid=(b, t // tq),
        in_specs=[pl.BlockSpec((None, NSA_WIDTH, tq), lambda i, j: (i, 0, j)),
                  pl.BlockSpec((None, GATE_PAD, tq), lambda i, j: (i, 0, j)),
                  keys(n_cmp), vals(n_cmp), keys(t), vals(t), keys(t), vals(t)],
        out_specs=pl.BlockSpec((None, NSA_WIDTH, tq), lambda i, j: (i, 0, j)),
        compiler_params=_cparams(("parallel", "arbitrary")),
        name="nsa_prompt",
    )(qt, gt, kvc, kvct, kvs, kvst, kvw, kvwt)


def _nsa_sample_kernel(past_len, pt_ref, q_ref, gate_ref, kvc_ref, cache_ref, kvs_new_ref, win_ref, kvw_new_ref,
                       o_ref, selbuf_ref, winbuf_ref, qbuf_ref, gbuf_ref, obuf_ref, sem):
    b = pl.program_id(0)
    t_new = q_ref.shape[0]
    tp = qbuf_ref.shape[0]
    n_pages = pt_ref.shape[1]
    page = cache_ref.shape[2]
    n_cmp = kvc_ref.shape[0]
    n_win = win_ref.shape[1]
    total = past_len + t_new
    n_sel = -(-total // SEL_BLOCK)
    n_sel_pad = -(-n_sel // LANES) * LANES
    tile_pages = S_KTILE // page
    tile_blocks = S_KTILE // SEL_BLOCK
    rows = NSA_HPG * tp

    def page_copy(i):
        return pltpu.make_async_copy(cache_ref.at[pt_ref[b, i]], selbuf_ref.at[:, pl.ds(i * page, page)],
                                     sem.at[i // tile_pages])

    def start(i, carry):
        page_copy(i).start()
        return carry

    lax.fori_loop(0, n_pages, start, 0)

    src = lax.broadcasted_iota(jnp.int32, (kvs_new_ref.shape[1], LANES), 0)
    dst = lax.broadcasted_iota(jnp.int32, (kvs_new_ref.shape[1], LANES), 1)
    pick = jnp.where((src == b * t_new + dst) & (dst < t_new), 1.0, 0.0).astype(bf16)
    selbuf_ref[:, past_len:past_len + LANES] = _dot(kvs_new_ref[...], pick)
    winbuf_ref[:, 0:n_win] = win_ref[...]
    winbuf_ref[:, n_win:n_win + LANES] = _dot(kvw_new_ref[...], pick)
    qbuf_ref[...] = jnp.zeros(qbuf_ref.shape, f32)
    qbuf_ref[0:t_new, :] = q_ref[...]
    gbuf_ref[...] = jnp.zeros(gbuf_ref.shape, f32)
    gbuf_ref[0:t_new, :] = gate_ref[...]

    q = qbuf_ref[...] * NSA_SCALE
    gates = jax.nn.sigmoid(gbuf_ref[...])
    q_pos = past_len + lax.broadcasted_iota(jnp.int32, (tp, 1), 0)
    q_pos_r = jnp.concatenate([q_pos] * NSA_HPG, axis=0)
    cmp_end = lax.broadcasted_iota(jnp.int32, (1, n_cmp), 1) * CMP_STRIDE + (CMP_LEN - 1)
    imp_mat = _importance_matrix(n_cmp, n_sel_pad)
    jj = lax.broadcasted_iota(jnp.int32, (tp, n_sel_pad), 1)
    expand = jnp.where(lax.broadcasted_iota(jnp.int32, (tile_blocks, S_KTILE), 0)
                       == lax.broadcasted_iota(jnp.int32, (tile_blocks, S_KTILE), 1) // SEL_BLOCK,
                       1.0, 0.0).astype(bf16)

    def flash(qs, k_t, v_t, mask, m, l, acc):
        s = jnp.where(mask, jnp.dot(qs, k_t.astype(bf16), preferred_element_type=f32), NEG_INF)
        m_new = jnp.maximum(m, jnp.max(s, axis=-1, keepdims=True))
        alpha = jnp.exp(m - m_new)
        e = jnp.where(mask, jnp.exp(s - m_new), 0.0)
        pv = lax.dot_general(e.astype(bf16), v_t.astype(bf16), _NT, preferred_element_type=f32)
        return m_new, alpha * l + jnp.sum(e, axis=-1, keepdims=True), alpha * acc + pv

    qss, sels, o_cs = [], [], []
    for grp in range(NSA_KV):
        kcol0, vcol0 = grp * NSA_HEAD, NSA_KV_COLS + grp * NSA_HEAD
        qs = _stack_heads(q, grp, tp).astype(bf16)
        p_c = _softmax_rows(_dot_nt(qs, kvc_ref[:, kcol0:kcol0 + NSA_HEAD]), cmp_end <= q_pos_r)
        o_cs.append(_dot(p_c, kvc_ref[:, vcol0:vcol0 + NSA_HEAD]))
        p_sum = p_c[0:tp]
        for h in range(1, NSA_HPG):
            p_sum = p_sum + p_c[h * tp:(h + 1) * tp]
        score = _block_scores(_dot_sel(p_sum, imp_mat), jj, q_pos // SEL_BLOCK)
        rank = jnp.zeros((tp, n_sel_pad), f32)
        for j2 in range(n_sel):
            col = score[:, j2:j2 + 1]
            ge = jnp.where(col >= score, 1.0, 0.0)
            gt = jnp.where(col > score, 1.0, 0.0)
            rank = rank + jnp.where(jj > j2, ge, gt)
        sel = jnp.where((rank < SEL_TOPK) & (jj < n_sel), 1.0, 0.0).astype(bf16)
        sels.append(jnp.concatenate([sel] * NSA_HPG, axis=0))
        qss.append(qs)

    carries = [_flash_init(rows) for _ in range(NSA_KV)]
    n_tiles = past_len // S_KTILE
    for kt in range(n_tiles + 1):
        width = S_KTILE if kt < n_tiles else LANES
        k0 = kt * S_KTILE
        if kt < n_tiles:
            def wait(i, carry):
                page_copy(i).wait()
                return carry
            lax.fori_loop(kt * tile_pages, (kt + 1) * tile_pages, wait, 0)
        k_pos = k0 + lax.broadcasted_iota(jnp.int32, (1, width), 1)
        for grp in range(NSA_KV):
            krow0, vrow0 = grp * NSA_HEAD, NSA_KV_COLS + grp * NSA_HEAD
            blocks = sels[grp][:, kt * tile_blocks:(kt + 1) * tile_blocks]
            chosen = jnp.dot(blocks, expand[:, 0:width], preferred_element_type=f32)
            carries[grp] = flash(qss[grp], selbuf_ref[krow0:krow0 + NSA_HEAD, k0:k0 + width],
                                 selbuf_ref[vrow0:vrow0 + NSA_HEAD, k0:k0 + width],
                                 (chosen > 0.5) & (k_pos <= q_pos_r), *carries[grp])

    w_pos = past_len - n_win + lax.broadcasted_iota(jnp.int32, (1, winbuf_ref.shape[1]), 1)
    dist = jnp.where(w_pos >= 0, q_pos_r - w_pos, -1)
    for grp in range(NSA_KV):
        krow0, vrow0 = grp * NSA_HEAD, NSA_KV_COLS + grp * NSA_HEAD
        o_s = _flash_out(*carries[grp])
        o_w = _flash_out(*flash(qss[grp], winbuf_ref[krow0:krow0 + NSA_HEAD, :], winbuf_ref[vrow0:vrow0 + NSA_HEAD, :],
                                (dist >= 0) & (dist < WINDOW), *_flash_init(rows)))
        o = (_stack_gates(gates, grp, 0) * o_cs[grp] + _stack_gates(gates, grp, 1) * o_s
             + _stack_gates(gates, grp, 2) * o_w)
        for h in range(NSA_HPG):
            col = (grp * NSA_HPG + h) * NSA_HEAD
            obuf_ref[:, col:col + NSA_HEAD] = o[h * tp:(h + 1) * tp]
    o_ref[...] = obuf_ref[0:t_new, :]


def _nsa_sample(p_q, p_gate, kvc_cmp, cache_sel_t, page_table, kvs_new_t, win_state_t, kvw_new_t):
    b, t_new, _ = p_q.shape
    n_pages = page_table.shape[1]
    page = cache_sel_t.shape[2]
    past_len = n_pages * page
    n_cmp = kvc_cmp.shape[1]
    n_win = win_state_t.shape[2]
    tp = 8
    assert t_new <= tp and past_len % S_KTILE == 0 and S_KTILE % page == 0 and n_win % LANES == 0
    assert b * t_new == kvs_new_t.shape[1]
    n_sel_pad = -(-(-(-(past_len + t_new) // SEL_BLOCK)) // LANES) * LANES
    assert (past_len // S_KTILE + 1) * (S_KTILE // SEL_BLOCK) <= n_sel_pad
    per_seq = lambda n, w: pl.BlockSpec((None, n, w), lambda i, pt: (i, 0, 0))
    whole = lambda a: pl.BlockSpec(a.shape, lambda i, pt: (0,) * a.ndim, pipeline_mode=pl.Buffered(1))
    return pl.pallas_call(
        functools.partial(_nsa_sample_kernel, past_len),
        out_shape=jax.ShapeDtypeStruct((b, t_new, NSA_WIDTH), f32),
        grid_spec=pltpu.PrefetchScalarGridSpec(
            num_scalar_prefetch=1, grid=(b,),
            in_specs=[per_seq(t_new, NSA_WIDTH), per_seq(t_new, GATE_PAD), per_seq(n_cmp, KV_COLS),
                      pl.BlockSpec(memory_space=pl.ANY), whole(kvs_new_t), per_seq(KV_COLS, n_win),
                      whole(kvw_new_t)],
            out_specs=per_seq(t_new, NSA_WIDTH),
            scratch_shapes=[pltpu.VMEM((KV_COLS, past_len + LANES), f32), pltpu.VMEM((KV_COLS, n_win + LANES), f32),
                            pltpu.VMEM((tp, NSA_WIDTH), f32), pltpu.VMEM((tp, GATE_PAD), f32),
                            pltpu.VMEM((tp, NSA_WIDTH), f32), pltpu.SemaphoreType.DMA((past_len // S_KTILE,))]),
        compiler_params=_cparams(("arbitrary",)),
        name="nsa_sample",
    )(page_table, p_q, p_gate, kvc_cmp, cache_sel_t, kvs_new_t, win_state_t, kvw_new_t)


def _prep_weights(l, depth, ln_g, ln_b, ffn_w_gate, ffn_w_up, ffn_w_down, w_in, b_in, rw_mu, rw_w0, rw_w2, rw_a0,
                  rw_a2, rw_g2, rw_k_k, rw_k_a, rw_r_k, rw_ln_w, rw_ln_b, nsa_phi_pe, nsa_phi_w1, nsa_phi_w2,
                  w_out_a, w_out_b, w_o):
    d = D_MODEL
    nc = D_FF // FF_CHUNK
    up = lambda w: w.astype(bf16).reshape(d, nc, FF_CHUNK).transpose(1, 0, 2)
    down = lambda w: w.astype(bf16).reshape(nc, FF_CHUNK, d)
    c1 = RW_SHIFT
    c2 = c1 + NSA_WIDTH
    c3 = c2 + 3 * KV_COLS
    c4 = c3 + N_GATE
    cols = lambda a: jnp.concatenate(
        [a[..., :c3], a[..., c4:], a[..., c3:c4], jnp.zeros(a.shape[:-1] + (GATE_PAD - N_GATE,), a.dtype)], axis=-1)
    z = lambda r, c: jnp.zeros((r, c), f32)
    w = RW_WIDTH
    lora = jnp.concatenate([
        jnp.concatenate([rw_w2[l], z(W_LORA, w), z(W_LORA, w)], axis=1),
        jnp.concatenate([z(A_LORA, w), rw_a2[l], z(A_LORA, w)], axis=1),
        jnp.concatenate([z(G_LORA, w), z(G_LORA, w), rw_g2[l]], axis=1)], axis=0)
    hid = lax.broadcasted_iota(jnp.int32, (w, w), 0) // RW_HEAD == lax.broadcasted_iota(jnp.int32, (w, w), 1) // RW_HEAD
    vec = jnp.stack([rw_w0[l], rw_a0[l], rw_k_k[l], rw_k_a[l], rw_r_k[l].reshape(w), rw_ln_w[l], rw_ln_b[l],
                     jnp.zeros((w,), f32)])
    w1 = nsa_phi_w1[l].reshape(2, 2, CMP_STRIDE, NSA_HEAD, CMP_HID)
    eye = jnp.eye(NSA_KV, dtype=f32)
    big = jnp.einsum("kljde,kK,gG->jkgdKGle", w1, jnp.eye(2, dtype=f32), eye)
    pe = jnp.broadcast_to(nsa_phi_pe[l].reshape(2, 1, CMP_LEN * NSA_HEAD), (2, 8, CMP_LEN * NSA_HEAD))
    w_cols, b_cols = cols(w_in[l]), cols(b_in[l])
    seg_t = _segments(_PROJ, [name for name, _ in _PROJ_T])
    w_t = jnp.concatenate([w_cols[:, lo:hi] for lo, hi in seg_t], axis=1).T
    b_t = jnp.concatenate([b_cols[lo:hi] for lo, hi in seg_t])
    return dict(
        alpha=(2 * depth) ** 0.25,
        wg1=up(ffn_w_gate[l, 0]), wu1=up(ffn_w_up[l, 0]), wd1=down(ffn_w_down[l, 0]),
        wg2=up(ffn_w_gate[l, 1]), wu2=up(ffn_w_up[l, 1]), wd2=down(ffn_w_down[l, 1]),
        ln_g=ln_g[l], ln_b=ln_b[l],
        w_in=w_cols.astype(bf16), b_in=b_cols.reshape(1, -1),
        w_in_t=w_t.astype(bf16), b_in_t=b_t[:, None],
        phi_w2t=jnp.swapaxes(nsa_phi_w2[l], 1, 2).astype(bf16),
        rw_mu=rw_mu[l].reshape(1, -1), rw_lora=lora.astype(bf16), rw_vec=vec, rw_hsum=hid.astype(bf16),
        phi_big=big.reshape(CMP_ROW, 2 * NSA_KV * 2 * CMP_HID).astype(bf16), phi_pe=pe,
        phi_w1f=nsa_phi_w1[l].reshape(2, CMP_LEN * NSA_HEAD, CMP_HID).astype(bf16),
        phi_w2=nsa_phi_w2[l].astype(bf16),
        w_out_a=w_out_a[l].astype(bf16), w_out_b=w_out_b[l].astype(bf16), w_o=w_o[l].astype(bf16))


def _kv_state(p):
    return p.reshape(p.shape[:-1] + (2, NSA_KV, NSA_HEAD))


def _tokens_last(a):
    n, t = a.shape[:2]
    return a.transpose(0, 2, 3, 4, 1).reshape(n, KV_COLS, t)


def _kv_state_t(p_t):
    b, _, t = p_t.shape
    return p_t.reshape(b, 2, NSA_KV, NSA_HEAD, t).transpose(0, 4, 1, 2, 3)


def kernel(x_prompt, x_sample, c_prompt, c_sample, cache_kv_cmp, cache_kv_sel, state_kv_win, state_wkv, state_shift, page_table, w_ada, b_ada, ln_g, ln_b, ffn_w_gate, ffn_w_up, ffn_w_down, w_in, b_in, rw_mu, rw_w0, rw_w2, rw_a0, rw_a2, rw_g2, rw_k_k, rw_k_a, rw_r_k, rw_ln_w, rw_ln_b, nsa_phi_pe, nsa_phi_w1, nsa_phi_w2, w_out_a, w_out_b, w_o):
    bp, seq, d = x_prompt.shape
    bd, t_new, _ = x_sample.shape
    depth = w_ada.shape[0]
    n_phys, page = cache_kv_cmp.shape[1:3]
    n_pages = page_table.shape[1]
    n_win = state_kv_win.shape[2]
    rows_s = bd * t_new
    tm_p = 256
    assert seq % tm_p == 0 and seq % CMP_STRIDE == 0 and page % CMP_STRIDE == 0

    y_p, y_s = x_prompt, x_sample.reshape(1, rows_s, d)
    st_p, st_s = [], []
    for l in range(depth):
        wts = _prep_weights(l, depth, ln_g, ln_b, ffn_w_gate, ffn_w_up, ffn_w_down, w_in, b_in, rw_mu, rw_w0,
                            rw_w2, rw_a0, rw_a2, rw_g2, rw_k_k, rw_k_a, rw_r_k, rw_ln_w, rw_ln_b, nsa_phi_pe,
                            nsa_phi_w1, nsa_phi_w2, w_out_a, w_out_b, w_o)
        mod = _ada_mod(jnp.concatenate([c_prompt, c_sample], axis=0), w_ada[l], b_ada[l])
        mod_p = mod[:bp].reshape(bp, 9, 1, d)
        mod_s = jnp.repeat(mod[bp:].reshape(bd, 9, d), t_new, axis=0).reshape(rows_s, 9, d)
        mod_s = mod_s.transpose(1, 0, 2).reshape(1, 9, rows_s, d)

        x1, p_rw, p_kc, p_ks, p_kw, p_m, q_t, g_t, ks_t, kw_t = _ffn_in(
            y_p, mod_p, wts, tm_p, ("rw", "kc", "ks", "kw", "merge"), ("q", "gate", "ks", "kw"))
        ya, wkv_p = _rwkv(p_rw, jnp.zeros((bp, RW_SHIFT), f32), jnp.zeros((bp, RW_HEADS, RW_HEAD, RW_HEAD), f32),
                          wts)
        kvc_cmp, kvc_cmp_t = _compress_dense(p_kc.reshape(bp, seq // CMP_STRIDE, CMP_ROW), wts)
        yb_t = _nsa_prompt(q_t, g_t, kvc_cmp, kvc_cmp_t, p_ks, ks_t, p_kw, kw_t)
        y_p = _mix_ffn(x1, ya, yb_t, p_m, mod_p, wts, tm_p, True)
        n_keep = min(WINDOW, seq)
        st_p.append((_kv_state(p_kc), _kv_state_t(ks_t), _kv_state_t(kw_t[:, :, seq - n_keep:]), wkv_p,
                     p_rw[:, -1]))

        x1, p_rw, p_q, p_kc, p_ks, p_m, p_g, ks_t, kw_t = _ffn_in(
            y_s, mod_s, wts, rows_s, ("rw", "q", "kc", "ks", "merge", "gate"), ("ks", "kw"))
        per_seq = lambda a: a.reshape(bd, t_new, a.shape[-1])
        p_rw, p_q, p_kc, p_ks, p_g = map(per_seq, (p_rw, p_q, p_kc, p_ks, p_g))
        ya, wkv_s = _rwkv(p_rw, state_shift[l], state_wkv[l], wts)
        kvc_cmp = _compress_paged(_tokens_last(cache_kv_cmp[l]), page_table, wts)
        win_state_t = _tokens_last(state_kv_win[l])
        yb = _nsa_sample(p_q, p_g, kvc_cmp, _tokens_last(cache_kv_sel[l]), page_table, ks_t[0], win_state_t,
                         kw_t[0])
        y_s = _mix_ffn(x1, ya.reshape(1, rows_s, RW_WIDTH), yb.reshape(1, rows_s, NSA_WIDTH), p_m, mod_s, wts,
                       rows_s, False)
        kw_new_t = kw_t[0].reshape(KV_COLS, bd, t_new).transpose(1, 0, 2)
        new_win_t = jnp.concatenate([win_state_t, kw_new_t], axis=2)[:, :, t_new:]
        st_s.append((_kv_state(p_kc), _kv_state(p_ks), _kv_state_t(new_win_t), wkv_s, p_rw[:, -1]))

    stack = lambda sts, i: jnp.stack([s[i] for s in sts])
    return (y_p, y_s.reshape(bd, t_new, d)) + tuple(stack(st_p, i) for i in range(5)) + tuple(
        stack(st_s, i) for i in range(5))
```

```python
import functools
import math

import jax
import jax.numpy as jnp
from jax import lax
from jax.experimental import pallas as pl
from jax.experimental.pallas import tpu as pltpu

f32 = jnp.float32
bf16 = jnp.bfloat16

D_MODEL = 1024
RW_HEAD = 64
RW_HEADS = 8
RW_WIDTH = RW_HEADS * RW_HEAD
W_LORA = 64
A_LORA = 64
G_LORA = 128
LORA_IN = W_LORA + A_LORA + G_LORA
RW_GN_EPS = 64e-5
RW_SHIFT = 3 * RW_WIDTH + LORA_IN
NSA_HEAD = 64
NSA_HEADS = 8
NSA_KV = 2
NSA_HPG = NSA_HEADS // NSA_KV
NSA_WIDTH = NSA_HEADS * NSA_HEAD
NSA_KV_COLS = NSA_KV * NSA_HEAD
KV_COLS = 2 * NSA_KV_COLS
NSA_SCALE = NSA_HEAD ** -0.5
CMP_STRIDE = 16
CMP_LEN = 2 * CMP_STRIDE
CMP_HID = 128
CMP_ROW = CMP_STRIDE * KV_COLS
SEL_BLOCK = 64
SEL_TOPK = 16
WINDOW = 512
D_FF = 2816
N_GATE = 3 * NSA_HEADS
LN_EPS = 1e-5
NEG_INF = -1e30
BIG = 1e9

LANES = 128
GATE_PAD = LANES
FF_CHUNK = 256
RW_CHUNK = 64
RW_HPB = 4
RW_SEQS = 4
Q_TILE = 256
K_TILE = 256
S_KTILE = 2048
VMEM_LIMIT = 56 * 1024 * 1024

_NT = (((1,), (1,)), ((), ()))
_TN = (((0,), (0,)), ((), ()))
_HI = lax.Precision.HIGHEST


def _dot(a, b):
    return jnp.dot(a.astype(bf16), b.astype(bf16), preferred_element_type=f32)


def _dot_nt(a, b):
    return lax.dot_general(a.astype(bf16), b.astype(bf16), _NT, preferred_element_type=f32)


def _dot_f32(a, b, dims=None):
    if dims is None:
        return jnp.dot(a, b, preferred_element_type=f32, precision=_HI)
    return lax.dot_general(a, b, dims, preferred_element_type=f32, precision=_HI)


def _dot_sel(x, m01):
    hi = x.astype(bf16)
    r1 = x - hi.astype(f32)
    mid = r1.astype(bf16)
    lo = (r1 - mid.astype(f32)).astype(bf16)
    d = lambda t: jnp.dot(t, m01, preferred_element_type=f32)
    return d(hi) + d(mid) + d(lo)


def _layer_norm(x, g, b):
    mu = jnp.mean(x, axis=-1, keepdims=True)
    xc = x - mu
    var = jnp.mean(xc * xc, axis=-1, keepdims=True)
    return xc * lax.rsqrt(var + LN_EPS) * g + b


def _cparams(sem):
    return pltpu.CompilerParams(dimension_semantics=sem, vmem_limit_bytes=VMEM_LIMIT)


def _resident(shape):
    nd = len(shape)
    return pl.BlockSpec(shape, lambda *_: (0,) * nd, pipeline_mode=pl.Buffered(1))


def _ada_kernel(c_ref, w_ref, b_ref, o_ref):
    c = c_ref[...]
    o_ref[...] = _dot(c * jax.nn.sigmoid(c), w_ref[...]) + b_ref[...]


def _ada_mod(c, w_ada, b_ada):
    rows, d = c.shape
    n = w_ada.shape[1]
    tn = d
    return pl.pallas_call(
        _ada_kernel,
        out_shape=jax.ShapeDtypeStruct((rows, n), f32),
        grid=(n // tn,),
        in_specs=[pl.BlockSpec((rows, d), lambda j: (0, 0)),
                  pl.BlockSpec((d, tn), lambda j: (0, j)),
                  pl.BlockSpec((1, tn), lambda j: (0, j))],
        out_specs=pl.BlockSpec((rows, tn), lambda j: (0, j)),
        compiler_params=_cparams(("parallel",)),
        name="ada_mod",
    )(c, w_ada, b_ada.reshape(1, n))


def _ffn(u, wg_ref, wu_ref, wd_ref, acc_ref):
    ub = u.astype(bf16)
    acc_ref[...] = jnp.zeros(acc_ref.shape, f32)

    def body(c, carry):
        hg = jnp.dot(ub, wg_ref[c], preferred_element_type=f32)
        hu = jnp.dot(ub, wu_ref[c], preferred_element_type=f32)
        h = hg * jax.nn.sigmoid(hg) * hu
        acc_ref[...] += jnp.dot(h.astype(bf16), wd_ref[c], preferred_element_type=f32)
        return carry

    lax.fori_loop(0, wg_ref.shape[0], body, 0)
    return acc_ref[...]


def _ffn_in_kernel(alpha, seg, seg_t, x_ref, mod_ref, wg_ref, wu_ref, wd_ref, lng_ref, lnb_ref, win_ref, bin_ref,
                   wint_ref, bint_ref, x1_ref, *rest):
    out_refs, acc_ref = rest[:-1], rest[-1]
    x = x_ref[...]
    u = x * (1.0 + mod_ref[1]) + mod_ref[0]
    f = _ffn(u, wg_ref, wu_ref, wd_ref, acc_ref)
    x1 = _layer_norm(alpha * x + 0.5 * (1.0 + mod_ref[2]) * f, lng_ref[0:1, :], lnb_ref[0:1, :])
    x1_ref[...] = x1
    u2 = (x1 * (1.0 + mod_ref[4]) + mod_ref[3]).astype(bf16)
    for o_ref, (lo, hi) in zip(out_refs, seg):
        o_ref[...] = jnp.dot(u2, win_ref[:, lo:hi], preferred_element_type=f32) + bin_ref[:, lo:hi]
    for o_ref, (lo, hi) in zip(out_refs[len(seg):], seg_t):
        o_ref[...] = lax.dot_general(wint_ref[lo:hi, :], u2, _NT, preferred_element_type=f32) + bint_ref[lo:hi, :]


_PROJ = (("rw", RW_SHIFT), ("q", NSA_WIDTH), ("kc", KV_COLS), ("ks", KV_COLS), ("kw", KV_COLS),
         ("merge", 2 * D_MODEL), ("gate", GATE_PAD))
_PROJ_T = (("q", NSA_WIDTH), ("gate", GATE_PAD), ("ks", KV_COLS), ("kw", KV_COLS))


def _segments(table, names):
    seg, lo = {}, 0
    for name, w in table:
        seg[name] = (lo, lo + w)
        lo += w
    return tuple(seg[n] for n in names)


def _ffn_in(x, mod, wts, tm, names, names_t):
    g, t, d = x.shape
    r = mod.shape[2]
    seg, seg_t = _segments(_PROJ, names), _segments(_PROJ_T, names_t)
    row = lambda w: pl.BlockSpec((None, tm, w), lambda i, j: (i, j, 0))
    col = lambda w: pl.BlockSpec((None, w, tm), lambda i, j: (i, 0, j))
    return pl.pallas_call(
        functools.partial(_ffn_in_kernel, wts["alpha"], seg, seg_t),
        out_shape=([jax.ShapeDtypeStruct((g, t, d), f32)]
                   + [jax.ShapeDtypeStruct((g, t, hi - lo), f32) for lo, hi in seg]
                   + [jax.ShapeDtypeStruct((g, hi - lo, t), f32) for lo, hi in seg_t]),
        grid=(g, t // tm),
        in_specs=[row(d),
                  pl.BlockSpec((None, 9, r, d), lambda i, j: (i, 0, 0, 0)),
                  _resident(wts["wg1"].shape), _resident(wts["wu1"].shape), _resident(wts["wd1"].shape),
                  _resident(wts["ln_g"].shape), _resident(wts["ln_b"].shape),
                  _resident(wts["w_in"].shape), _resident(wts["b_in"].shape),
                  _resident(wts["w_in_t"].shape), _resident(wts["b_in_t"].shape)],
        out_specs=[row(d)] + [row(hi - lo) for lo, hi in seg] + [col(hi - lo) for lo, hi in seg_t],
        scratch_shapes=[pltpu.VMEM((tm, d), f32)],
        compiler_params=_cparams(("parallel", "parallel")),
        name="ffn_in",
    )(x, mod, wts["wg1"], wts["wu1"], wts["wd1"], wts["ln_g"], wts["ln_b"], wts["w_in"], wts["b_in"],
      wts["w_in_t"], wts["b_in_t"])


def _mix_ffn_kernel(alpha, yb_on_lanes, x1_ref, ya_ref, yb_ref, pm_ref, mod_ref, woa_ref, wob_ref, wo_ref,
                    wg_ref, wu_ref, wd_ref, lng_ref, lnb_ref, o_ref, acc_ref):
    x1 = x1_ref[...]
    pm = pm_ref[...]
    d = x1.shape[-1]
    ga = jax.nn.sigmoid(pm[:, :d])
    gb = jax.nn.sigmoid(pm[:, d:])
    if yb_on_lanes:
        yb_out = lax.dot_general(yb_ref[...].astype(bf16), wob_ref[...], _TN, preferred_element_type=f32)
    else:
        yb_out = _dot(yb_ref[...], wob_ref[...])
    merged = ga * _dot(ya_ref[...], woa_ref[...]) + gb * yb_out
    m = _dot(merged, wo_ref[...])
    x2 = _layer_norm(alpha * x1 + (1.0 + mod_ref[5]) * m, lng_ref[1:2, :], lnb_ref[1:2, :])
    u = x2 * (1.0 + mod_ref[7]) + mod_ref[6]
    f = _ffn(u, wg_ref, wu_ref, wd_ref, acc_ref)
    o_ref[...] = _layer_norm(alpha * x2 + 0.5 * (1.0 + mod_ref[8]) * f, lng_ref[2:3, :], lnb_ref[2:3, :])


def _mix_ffn(x1, ya, yb, pm, mod, wts, tm, yb_on_lanes):
    g, t, d = x1.shape
    r = mod.shape[2]
    row = lambda w: pl.BlockSpec((None, tm, w), lambda i, j: (i, j, 0))
    yb_spec = pl.BlockSpec((None, NSA_WIDTH, tm), lambda i, j: (i, 0, j)) if yb_on_lanes else row(NSA_WIDTH)
    return pl.pallas_call(
        functools.partial(_mix_ffn_kernel, wts["alpha"], yb_on_lanes),
        out_shape=jax.ShapeDtypeStruct((g, t, d), f32),
        grid=(g, t // tm),
        in_specs=[row(d), row(ya.shape[-1]), yb_spec, row(pm.shape[-1]),
                  pl.BlockSpec((None, 9, r, d), lambda i, j: (i, 0, 0, 0)),
                  _resident(wts["w_out_a"].shape), _resident(wts["w_out_b"].shape), _resident(wts["w_o"].shape),
                  _resident(wts["wg2"].shape), _resident(wts["wu2"].shape), _resident(wts["wd2"].shape),
                  _resident(wts["ln_g"].shape), _resident(wts["ln_b"].shape)],
        out_specs=row(d),
        scratch_shapes=[pltpu.VMEM((tm, d), f32)],
        compiler_params=_cparams(("parallel", "parallel")),
        name="mix_ffn",
    )(x1, ya, yb, pm, mod, wts["w_out_a"], wts["w_out_b"], wts["w_o"], wts["wg2"], wts["wu2"], wts["wd2"],
      wts["ln_g"], wts["ln_b"])


def _rwkv_kernel(t_blk, p_ref, prev_ref, s0_ref, mu_ref, wl_ref, vec_ref, hsum_ref,
                 y_ref, sout_ref, state_ref, prevrow_ref, stage_ref, yn_ref):
    n_seq = p_ref.shape[0]
    ci = pl.program_id(1)

    @pl.when(ci == 0)
    def _():
        state_ref[...] = jnp.zeros(state_ref.shape, f32)
        for s in range(n_seq):
            for h in range(RW_HEADS):
                d0 = (h % RW_HPB) * RW_HEAD
                state_ref[s, h // RW_HPB, d0:d0 + RW_HEAD, d0:d0 + RW_HEAD] = s0_ref[s, h]
        prevrow_ref[...] = prev_ref[...]

    c = stage_ref.shape[1]
    gw = RW_HPB * RW_HEAD
    rows = RW_HPB * c
    br = lax.broadcasted_iota(jnp.int32, (rows, gw), 0)
    bc = lax.broadcasted_iota(jnp.int32, (rows, gw), 1)
    same_head = (br // c) == (bc // RW_HEAD)
    tr = lax.broadcasted_iota(jnp.int32, (rows, rows), 0) % c
    tc = lax.broadcasted_iota(jnp.int32, (rows, rows), 1) % c
    strict = tr > tc
    incl = tr >= tc
    blocks = lambda x: jnp.where(same_head, jnp.concatenate([x] * RW_HPB, axis=0), 0.0).astype(bf16)
    nt = lambda x, y: lax.dot_general(x, y, _NT, preferred_element_type=f32)
    tn = lambda x, y: lax.dot_general(x, y, _TN, preferred_element_type=f32)
    mm = lambda x, y: jnp.dot(x, y, preferred_element_type=f32)
    n_fac = max(1, (c - 1).bit_length())

    pre = [_rwkv_features(t_blk, p_ref.at[s], mu_ref, wl_ref, vec_ref, hsum_ref, prevrow_ref.at[s],
                          stage_ref.at[s]) for s in range(n_seq)]
    chains = [(s, grp) for s in range(n_seq) for grp in range(RW_HEADS // RW_HPB)]
    ops, low, u = {}, {}, {}
    for s, grp in chains:
        gl = slice(grp * gw, (grp + 1) * gw)
        ops[s, grp] = [blocks(pre[s][name][:, gl]) for name in ("at", "rt", "bt", "kt", "bh", "kh", "v")]
    for ch in chains:
        at4, rt4, bt4, kt4, bh4, kh4, v4 = ops[ch]
        s_b = state_ref[ch].astype(bf16)
        low[ch] = jnp.where(strict, nt(at4, bt4), 0.0).astype(bf16)
        u[ch] = nt(at4, s_b) + mm(jnp.where(strict, nt(at4, kt4), 0.0).astype(bf16), v4)
    for i in range(n_fac):
        for ch in chains:
            u[ch] = u[ch] + mm(low[ch], u[ch].astype(bf16))
        if i + 1 < n_fac:
            for ch in chains:
                low[ch] = mm(low[ch], low[ch]).astype(bf16)
    for ch in chains:
        s, grp = ch
        gl = slice(grp * gw, (grp + 1) * gw)
        at4, rt4, bt4, kt4, bh4, kh4, v4 = ops[ch]
        s_bd = state_ref[ch]
        u_b = u[ch].astype(bf16)
        y4 = (nt(rt4, s_bd.astype(bf16)) + mm(jnp.where(incl, nt(rt4, bt4), 0.0).astype(bf16), u_b)
              + mm(jnp.where(incl, nt(rt4, kt4), 0.0).astype(bf16), v4))
        state_ref[ch] = s_bd * pre[s]["p_last"][:, gl] + tn(u_b, bh4) + tn(v4, kh4)
        yg = y4[0:c]
        for hh in range(1, RW_HPB):
            yg = yg + y4[hh * c:(hh + 1) * c]
        yn_ref[s, :, gl] = yg

    hsum = hsum_ref[...]
    ln_w, ln_b = vec_ref[5:6, :], vec_ref[6:7, :]
    for s in range(n_seq):
        yh = yn_ref[s]
        mean = _dot_sel(yh, hsum) * (1.0 / RW_HEAD)
        yc = yh - mean
        var = _dot_sel(yc * yc, hsum) * (1.0 / RW_HEAD)
        y = (yc * lax.rsqrt(var + RW_GN_EPS) * ln_w + ln_b + pre[s]["bonus"]) * pre[s]["g"]
        y_ref[s] = y[0:t_blk, :] if t_blk < c else y

    @pl.when(ci == pl.num_programs(1) - 1)
    def _():
        for s in range(n_seq):
            for h in range(RW_HEADS):
                d0 = (h % RW_HPB) * RW_HEAD
                sout_ref[s, h] = state_ref[s, h // RW_HPB, d0:d0 + RW_HEAD, d0:d0 + RW_HEAD]


def _rwkv_features(t_blk, p_ref, mu_ref, wl_ref, vec_ref, hsum_ref, prevrow_ref, stage_ref):
    c = stage_ref.shape[0]
    w = RW_WIDTH
    if t_blk < c:
        stage_ref[...] = jnp.zeros(stage_ref.shape, f32)
        stage_ref[0:t_blk, :] = p_ref[...]
        p = stage_ref[...]
    else:
        p = p_ref[...]
    ridx = lax.broadcasted_iota(jnp.int32, (c, 1), 0)
    valid = ridx < t_blk
    p_prev = jnp.where(ridx == 0, prevrow_ref[...], pltpu.roll(p, 1, 0))
    prevrow_ref[...] = p[c - 1:c, :]
    xs = p + (p_prev - p) * mu_ref[...]

    tail = xs[:, 3 * w:]
    li = lax.broadcasted_iota(jnp.int32, tail.shape, 1)
    act = jnp.where(li < W_LORA, jnp.tanh(tail),
                    jnp.where(li < W_LORA + A_LORA, tail, jax.nn.sigmoid(tail)))
    lora = _dot(act, wl_ref[...])
    w0, a0, k_k, k_a, r_k = (vec_ref[i:i + 1, :] for i in range(5))
    z = -(w0 + lora[:, :w])
    softplus = jnp.maximum(z, 0.0) + jnp.log1p(jnp.exp(-jnp.abs(z)))
    lw = -jnp.exp(-softplus - 0.5)
    a = jax.nn.sigmoid(a0 + lora[:, w:2 * w])
    g = lora[:, 2 * w:]
    r, k, v = xs[:, :w], xs[:, w:2 * w], xs[:, 2 * w:3 * w]
    hsum = hsum_ref[...]
    kk = k * k_k
    kk = kk * lax.rsqrt(jnp.maximum(_dot_sel(kk * kk, hsum), 1e-24))
    k2 = k * (1.0 + (a - 1.0) * k_a)
    bonus = _dot_sel(r * k2 * r_k, hsum) * v
    if t_blk < c:
        lw = jnp.where(valid, lw, 0.0)
        kk = jnp.where(valid, kk, 0.0)
        k2 = jnp.where(valid, k2, 0.0)
        v = jnp.where(valid, v, 0.0)
    b = kk * a

    rr = lax.broadcasted_iota(jnp.int32, (c, c), 0)
    cc = lax.broadcasted_iota(jnp.int32, (c, c), 1)
    cum = _dot_f32(jnp.where(rr >= cc, 1.0, 0.0).astype(f32), lw)
    cum_last = cum[c - 1:c, :]
    e_neg = jnp.exp(-cum)
    e_rem = jnp.exp(cum_last - cum)
    at = -kk * jnp.exp(cum - lw)
    rt = r * jnp.exp(cum)
    bt, kt = b * e_neg, k2 * e_neg
    bh, kh = b * e_rem, k2 * e_rem
    p_last = jnp.exp(cum_last)
    return dict(at=at, rt=rt, bt=bt, kt=kt, bh=bh, kh=kh, v=v, p_last=p_last, bonus=bonus, g=g)


def _rwkv(p_rw, prev, s0, wts):
    g, t, ws = p_rw.shape
    c = RW_CHUNK
    t_blk = min(c, t)
    sb = RW_SEQS
    assert t % t_blk == 0 and g % sb == 0
    gw = RW_HPB * RW_HEAD
    return pl.pallas_call(
        functools.partial(_rwkv_kernel, t_blk),
        out_shape=[jax.ShapeDtypeStruct((g, t, RW_WIDTH), f32), jax.ShapeDtypeStruct(s0.shape, f32)],
        grid=(g // sb, t // t_blk),
        in_specs=[pl.BlockSpec((sb, t_blk, ws), lambda i, j: (i, j, 0)),
                  pl.BlockSpec((sb, 1, ws), lambda i, j: (i, 0, 0)),
                  pl.BlockSpec((sb,) + s0.shape[1:], lambda i, j: (i, 0, 0, 0)),
                  _resident(wts["rw_mu"].shape), _resident(wts["rw_lora"].shape),
                  _resident(wts["rw_vec"].shape), _resident(wts["rw_hsum"].shape)],
        out_specs=[pl.BlockSpec((sb, t_blk, RW_WIDTH), lambda i, j: (i, j, 0)),
                   pl.BlockSpec((sb,) + s0.shape[1:], lambda i, j: (i, 0, 0, 0))],
        scratch_shapes=[pltpu.VMEM((sb, RW_HEADS // RW_HPB, gw, gw), f32), pltpu.VMEM((sb, 1, ws), f32),
                        pltpu.VMEM((sb, c, ws), f32), pltpu.VMEM((sb, c, RW_WIDTH), f32)],
        compiler_params=_cparams(("parallel", "arbitrary")),
        name="rwkv",
    )(p_rw, prev.reshape(g, 1, ws), s0, wts["rw_mu"], wts["rw_lora"], wts["rw_vec"], wts["rw_hsum"])


def _gelu_tanh(x):
    return x * (0.5 * (1.0 + jnp.tanh(math.sqrt(2.0 / math.pi) * (x + 0.044715 * (x * x * x)))))


def _compress_rows(hids, pe_ref, w1f_ref, w2_ref, o_ref, w2t_ref=None, ot_ref=None):
    n = hids[0].shape[0]
    for kv in range(2):
        pe_term = _dot(pe_ref[kv], w1f_ref[kv])[0:1, :]
        for grp in range(NSA_KV):
            base = grp * 2 * CMP_HID
            lo = hids[kv][:, base:base + CMP_HID]
            hi = pltpu.roll(hids[kv][:, base + CMP_HID:base + 2 * CMP_HID], n - 1, 0)
            act = _gelu_tanh(lo + hi + pe_term).astype(bf16)
            col = (kv * NSA_KV + grp) * NSA_HEAD
            o_ref[:, col:col + NSA_HEAD] = jnp.dot(act, w2_ref[kv], preferred_element_type=f32)
            if ot_ref is not None:
                ot_ref[col:col + NSA_HEAD, :] = lax.dot_general(w2t_ref[kv], act, _NT, preferred_element_type=f32)


def _compress_dense_kernel(x_ref, wbig_ref, pe_ref, w1f_ref, w2_ref, w2t_ref, o_ref, ot_ref):
    hid = _dot(x_ref[...], wbig_ref[...])
    half = hid.shape[1] // 2
    _compress_rows([hid[:, :half], hid[:, half:]], pe_ref, w1f_ref, w2_ref, o_ref, w2t_ref, ot_ref)


def _compress_dense(x, wts):
    b, n, _ = x.shape
    return pl.pallas_call(
        _compress_dense_kernel,
        out_shape=[jax.ShapeDtypeStruct((b, n, KV_COLS), f32), jax.ShapeDtypeStruct((b, KV_COLS, n), f32)],
        grid=(b,),
        in_specs=[pl.BlockSpec((None, n, CMP_ROW), lambda i: (i, 0, 0)),
                  _resident(wts["phi_big"].shape), _resident(wts["phi_pe"].shape),
                  _resident(wts["phi_w1f"].shape), _resident(wts["phi_w2"].shape),
                  _resident(wts["phi_w2t"].shape)],
        out_specs=[pl.BlockSpec((None, n, KV_COLS), lambda i: (i, 0, 0)),
                   pl.BlockSpec((None, KV_COLS, n), lambda i: (i, 0, 0))],
        compiler_params=_cparams(("parallel",)),
        name="compress_dense",
    )(x, wts["phi_big"], wts["phi_pe"], wts["phi_w1f"], wts["phi_w2"], wts["phi_w2t"])


CMP_GROUP = 2 * LANES
CMP_UNROLL = 4


def _compress_paged_kernel(pt_ref, cache_ref, wkv_ref, pe_ref, w1f_ref, w2_ref, o_ref, pbuf_ref, xbuf_ref, sem):
    b = pl.program_id(0)
    n_pages = pt_ref.shape[1]
    page = cache_ref.shape[2]
    pages_per_group = CMP_GROUP // page
    rows_per_group = CMP_GROUP // CMP_STRIDE

    def page_copy(seq, i):
        return pltpu.make_async_copy(cache_ref.at[pt_ref[seq, i]], pbuf_ref.at[:, pl.ds(i * page, page)],
                                     sem.at[i // pages_per_group])

    def start_all(seq):
        def start(i, carry):
            page_copy(seq, i).start()
            return carry
        lax.fori_loop(0, n_pages, start, 0)

    @pl.when(b == 0)
    def _():
        start_all(0)

    r = lax.broadcasted_iota(jnp.int32, (CMP_GROUP, CMP_GROUP), 0)
    c = lax.broadcasted_iota(jnp.int32, (CMP_GROUP, CMP_GROUP), 1)
    perm = jnp.where(c == CMP_STRIDE * (r % rows_per_group) + r // rows_per_group, 1.0, 0.0).astype(bf16)

    def regroup(step, carry):
        for pg in range(CMP_UNROLL * pages_per_group):
            page_copy(b, step * CMP_UNROLL * pages_per_group + pg).wait()
        for k in range(CMP_UNROLL):
            gi = step * CMP_UNROLL + k
            t0 = pl.multiple_of(gi * CMP_GROUP, CMP_GROUP)
            tokens = pbuf_ref[:, pl.ds(t0, CMP_GROUP)].astype(bf16)
            rows = lax.dot_general(perm, tokens, _NT, preferred_element_type=f32)
            r0 = pl.multiple_of(gi * rows_per_group, rows_per_group)
            for j in range(CMP_STRIDE):
                for kv in range(2):
                    col = (kv * CMP_STRIDE + j) * NSA_KV_COLS
                    xbuf_ref[pl.ds(r0, rows_per_group), col:col + NSA_KV_COLS] = (
                        rows[j * rows_per_group:(j + 1) * rows_per_group,
                             kv * NSA_KV_COLS:(kv + 1) * NSA_KV_COLS].astype(bf16))
        return carry

    n_groups = n_pages // pages_per_group
    lax.fori_loop(0, n_groups // CMP_UNROLL, regroup, 0)

    @pl.when(b + 1 < pl.num_programs(0))
    def _():
        start_all(b + 1)

    half = CMP_STRIDE * NSA_KV_COLS
    hids = [jnp.dot(xbuf_ref[:, kv * half:(kv + 1) * half], wkv_ref[kv], preferred_element_type=f32)
            for kv in range(2)]
    _compress_rows(hids, pe_ref, w1f_ref, w2_ref, o_ref)


def _compress_paged(cache_t, page_table, wts):
    b, n_pages = page_table.shape
    page = cache_t.shape[2]
    n = n_pages * page // CMP_STRIDE
    assert CMP_GROUP % page == 0 and (n_pages * page) % (CMP_GROUP * CMP_UNROLL) == 0
    full = lambda a: pl.BlockSpec(a.shape, lambda i, pt: (0,) * a.ndim, pipeline_mode=pl.Buffered(1))
    return pl.pallas_call(
        _compress_paged_kernel,
        out_shape=jax.ShapeDtypeStruct((b, n, KV_COLS), f32),
        grid_spec=pltpu.PrefetchScalarGridSpec(
            num_scalar_prefetch=1, grid=(b,),
            in_specs=[pl.BlockSpec(memory_space=pl.ANY), full(wts["phi_kv"]), full(wts["phi_pe"]),
                      full(wts["phi_w1f"]), full(wts["phi_w2"])],
            out_specs=pl.BlockSpec((None, n, KV_COLS), lambda i, pt: (i, 0, 0)),
            scratch_shapes=[pltpu.VMEM((KV_COLS, n_pages * page), f32), pltpu.VMEM((n, CMP_ROW), bf16),
                            pltpu.SemaphoreType.DMA((n_pages * page // CMP_GROUP,))]),
        compiler_params=_cparams(("arbitrary",)),
        name="compress_paged",
    )(page_table, cache_t, wts["phi_kv"], wts["phi_pe"], wts["phi_w1f"], wts["phi_w2"])


def _softmax_rows(s, mask):
    s = jnp.where(mask, s, NEG_INF)
    e = jnp.where(mask, jnp.exp(s - jnp.max(s, axis=-1, keepdims=True)), 0.0)
    return e / jnp.maximum(jnp.sum(e, axis=-1, keepdims=True), 1e-30)


def _flash_step(q, k, v, mask, m, l, acc):
    s = jnp.where(mask, _dot_nt(q, k), NEG_INF)
    m_new = jnp.maximum(m, jnp.max(s, axis=-1, keepdims=True))
    alpha = jnp.exp(m - m_new)
    e = jnp.where(mask, jnp.exp(s - m_new), 0.0)
    return m_new, alpha * l + jnp.sum(e, axis=-1, keepdims=True), alpha * acc + _dot(e, v)


def _flash_init(rows):
    return (jnp.full((rows, 1), NEG_INF, f32), jnp.zeros((rows, 1), f32), jnp.zeros((rows, NSA_HEAD), f32))


def _flash_out(m, l, acc):
    return acc / jnp.maximum(l, 1e-30)


def _importance_matrix(n_cmp_rows, n_sel_cols):
    ratio = SEL_BLOCK // CMP_STRIDE
    i = lax.broadcasted_iota(jnp.int32, (n_cmp_rows, n_sel_cols), 0)
    j = lax.broadcasted_iota(jnp.int32, (n_cmp_rows, n_sel_cols), 1)
    return jnp.where((i >= ratio * j - 1) & (i <= ratio * j + ratio - 1), 1.0, 0.0).astype(bf16)


def _block_scores(imp, j, cur):
    forced = (j == 0) | (j == cur) | (j == cur - 1)
    return jnp.where(j <= cur, jnp.where(forced, BIG, imp), -BIG)


def _stack_heads(x, grp, rows):
    return jnp.concatenate(
        [x[:, (grp * NSA_HPG + h) * NSA_HEAD:(grp * NSA_HPG + h + 1) * NSA_HEAD] for h in range(NSA_HPG)], axis=0)


def _stack_gates(gates, grp, branch):
    return jnp.concatenate(
        [gates[:, 3 * (grp * NSA_HPG + h) + branch:3 * (grp * NSA_HPG + h) + branch + 1] for h in range(NSA_HPG)],
        axis=0)


def _scores_t(qts, ks, biases):
    return tuple(jnp.dot(k.astype(bf16), qt, preferred_element_type=f32) + bias
                 for qt, k, bias in zip(qts, ks, biases))


def _flash_update_t(ss, vts, carry):
    n = len(ss)
    ms, ls, accs = carry[0::3], carry[1::3], carry[2::3]
    m_new = [jnp.maximum(ms[i], jnp.max(ss[i], axis=0, keepdims=True)) for i in range(n)]
    es = [jnp.exp2(ss[i] - m_new[i]) for i in range(n)]
    pvs = [jnp.dot(vts[i].astype(bf16), es[i].astype(bf16), preferred_element_type=f32) for i in range(n)]
    out = ()
    for i in range(n):
        alpha = jnp.exp2(ms[i] - m_new[i])
        out += (m_new[i], alpha * ls[i] + jnp.sum(es[i], axis=0, keepdims=True), alpha * accs[i] + pvs[i])
    return out


M_FLOOR = 0.1 * NEG_INF
LOG2_E = math.log2(math.e)


def _flash_init_t(cols):
    return (jnp.full((1, cols), M_FLOOR, f32), jnp.zeros((1, cols), f32), jnp.zeros((NSA_HEAD, cols), f32))


def _lanes(x, n):
    return jnp.concatenate([x] * n, axis=1)


def _nsa_prompt_kernel(qt_ref, gt_ref, kc_ref, vct_ref, ks_ref, vst_ref, kw_ref, vwt_ref, o_ref):
    tq = qt_ref.shape[1]
    n_cmp = kc_ref.shape[0]
    n_sel = ks_ref.shape[0] // SEL_BLOCK
    i = pl.program_id(1)
    t0 = i * tq
    cols = NSA_HPG * tq
    gates = jax.nn.sigmoid(gt_ref[...])
    q_pos = t0 + lax.broadcasted_iota(jnp.int32, (1, tq), 1)
    q_pos_c = _lanes(q_pos, NSA_HPG)
    cmp_end = lax.broadcasted_iota(jnp.int32, (n_cmp, 1), 0) * CMP_STRIDE + (CMP_LEN - 1)
    ratio = SEL_BLOCK // CMP_STRIDE
    ij = lax.broadcasted_iota(jnp.int32, (n_sel, n_cmp), 0)
    ii = lax.broadcasted_iota(jnp.int32, (n_sel, n_cmp), 1)
    imp_mat = jnp.where((ii >= ratio * ij - 1) & (ii <= ratio * ij + ratio - 1), 1.0, 0.0).astype(bf16)
    jt = lax.broadcasted_iota(jnp.int32, (n_sel, tq), 0)
    cur_t = (t0 + lax.broadcasted_iota(jnp.int32, (n_sel, tq), 1)) // SEL_BLOCK
    krow = lax.broadcasted_iota(jnp.int32, (K_TILE, 1), 0)
    er = lax.broadcasted_iota(jnp.int32, (K_TILE, n_sel), 0) // SEL_BLOCK
    ej = lax.broadcasted_iota(jnp.int32, (K_TILE, n_sel), 1)
    groups = range(NSA_KV)
    heads_of = lambda grp: range(grp * NSA_HPG, (grp + 1) * NSA_HPG)
    qts, sels, o_cs = [], [], []

    for grp in groups:
        d0 = grp * NSA_HEAD
        heads = heads_of(grp)
        q_all = jnp.concatenate([qt_ref[h * NSA_HEAD:(h + 1) * NSA_HEAD, :] for h in heads], axis=1)
        qt = (q_all * NSA_SCALE).astype(bf16)
        mask_c = cmp_end <= q_pos_c
        s = jnp.where(mask_c, jnp.dot(kc_ref[:, d0:d0 + NSA_HEAD].astype(bf16), qt, preferred_element_type=f32),
                      NEG_INF)
        e = jnp.where(mask_c, jnp.exp(s - jnp.max(s, axis=0, keepdims=True)), 0.0)
        p_c = e / jnp.maximum(jnp.sum(e, axis=0, keepdims=True), 1e-30)
        o_c = jnp.dot(vct_ref[d0:d0 + NSA_HEAD, :].astype(bf16), p_c.astype(bf16), preferred_element_type=f32)
        p_sum = p_c[:, 0:tq]
        for h in range(1, NSA_HPG):
            p_sum = p_sum + p_c[:, h * tq:(h + 1) * tq]
        hi = p_sum.astype(bf16)
        r1 = p_sum - hi.astype(f32)
        mid = r1.astype(bf16)
        lo = (r1 - mid.astype(f32)).astype(bf16)
        imp = sum(jnp.dot(imp_mat, part, preferred_element_type=f32) for part in (hi, mid, lo))
        score = _block_scores(imp, jt, cur_t)
        rank = jnp.zeros((n_sel, tq), f32)
        for j2 in range(n_sel):
            row = score[j2:j2 + 1, :]
            ge = jnp.where(row >= score, 1.0, 0.0)
            gt = jnp.where(row > score, 1.0, 0.0)
            rank = rank + jnp.where(jt > j2, ge, gt)
        qts.append((q_all * (NSA_SCALE * LOG2_E)).astype(bf16))
        sels.append(jnp.where(rank < SEL_TOPK, 1.0, 0.0).astype(bf16))
        o_cs.append(o_c)

    last = (t0 + tq - 1) // K_TILE
    key_cols = lambda ref, k0: [ref[pl.ds(k0, K_TILE), g * NSA_HEAD:(g + 1) * NSA_HEAD] for g in groups]
    val_rows = lambda ref, k0: [ref[g * NSA_HEAD:(g + 1) * NSA_HEAD, pl.ds(k0, K_TILE)] for g in groups]

    def sel_step(kt, carry):
        k0 = pl.multiple_of(kt * K_TILE, K_TILE)
        expand = jnp.where(ej == kt * (K_TILE // SEL_BLOCK) + er, 1.0, 0.0).astype(bf16)
        causal = (k0 + krow) <= q_pos
        biases = []
        for grp in groups:
            chosen = jnp.dot(expand, sels[grp], preferred_element_type=f32)
            biases.append(_lanes(jnp.where(causal & (chosen > 0.5), 0.0, NEG_INF), NSA_HPG))
        return _flash_update_t(_scores_t(qts, key_cols(ks_ref, k0), biases), val_rows(vst_ref, k0), carry)

    sel_acc = lax.fori_loop(0, last + 1, sel_step, _flash_init_t(cols) * NSA_KV)

    def win_step(kt, carry):
        k0 = pl.multiple_of(kt * K_TILE, K_TILE)
        dist = q_pos - (k0 + krow)
        bias = _lanes(jnp.where((dist >= 0) & (dist < WINDOW), 0.0, NEG_INF), NSA_HPG)
        return _flash_update_t(_scores_t(qts, key_cols(kw_ref, k0), [bias] * NSA_KV), val_rows(vwt_ref, k0), carry)

    win_acc = lax.fori_loop(jnp.maximum(t0 - WINDOW, 0) // K_TILE, last + 1, win_step,
                            _flash_init_t(cols) * NSA_KV)

    for grp in groups:
        heads = heads_of(grp)
        o_s = _flash_out(*sel_acc[3 * grp:3 * grp + 3])
        o_w = _flash_out(*win_acc[3 * grp:3 * grp + 3])
        gate = lambda br: jnp.concatenate([gates[3 * h + br:3 * h + br + 1, :] for h in heads], axis=1)
        o = gate(0) * o_cs[grp] + gate(1) * o_s + gate(2) * o_w
        for n, h in enumerate(heads):
            o_ref[h * NSA_HEAD:(h + 1) * NSA_HEAD, :] = o[:, n * tq:(n + 1) * tq]


def _nsa_prompt(qt, gt, kvc, kvct, kvs, kvst, kvw, kvwt):
    b, _, t = qt.shape
    n_cmp = kvc.shape[1]
    tq = Q_TILE
    assert t % tq == 0 and t % K_TILE == 0 and K_TILE % SEL_BLOCK == 0
    keys =lambda r: pl.BlockSpec((None, r, NSA_KV_COLS), lambda i, j: (i, 0, 0))
    vals = lambda c: pl.BlockSpec((None, NSA_KV_COLS, c), lambda i, j: (i, 1, 0))
    return pl.pallas_call(
        _nsa_prompt_kernel,
        out_shape=jax.ShapeDtypeStruct((b, NSA_WIDTH, t), f32),
        grid=(b, t // tq),
        in_specs=[pl.BlockSpec((None, NSA_WIDTH, tq), lambda i, j: (i, 0, j)),
                  pl.BlockSpec((None, GATE_PAD, tq), lambda i, j: (i, 0, j)),
                  keys(n_cmp), vals(n_cmp), keys(t), vals(t), keys(t), vals(t)],
        out_specs=pl.BlockSpec((None, NSA_WIDTH, tq), lambda i, j: (i, 0, j)),
        compiler_params=_cparams(("parallel", "arbitrary")),
        name="nsa_prompt",
    )(qt, gt, kvc, kvct, kvs, kvst, kvw, kvwt)


def _nsa_sample_kernel(past_len, pt_ref, q_ref, gate_ref, kvc_ref, cache_ref, kvs_new_ref, win_ref, kvw_new_ref,
                       o_ref, selbuf_ref, winbuf_ref, qbuf_ref, gbuf_ref, obuf_ref, sem):
    b = pl.program_id(0)
    t_new = q_ref.shape[0]
    tp = qbuf_ref.shape[0]
    n_pages = pt_ref.shape[1]
    page = cache_ref.shape[2]
    n_cmp = kvc_ref.shape[0]
    n_win = win_ref.shape[1]
    total = past_len + t_new
    n_sel = -(-total // SEL_BLOCK)
    n_sel_pad = -(-n_sel // LANES) * LANES
    tile_pages = S_KTILE // page
    tile_blocks = S_KTILE // SEL_BLOCK
    rows = NSA_HPG * tp

    def page_copy(i):
        return pltpu.make_async_copy(cache_ref.at[pt_ref[b, i]], selbuf_ref.at[:, pl.ds(i * page, page)],
                                     sem.at[i // tile_pages])

    def start(i, carry):
        page_copy(i).start()
        return carry

    lax.fori_loop(0, n_pages, start, 0)

    src = lax.broadcasted_iota(jnp.int32, (kvs_new_ref.shape[1], LANES), 0)
    dst = lax.broadcasted_iota(jnp.int32, (kvs_new_ref.shape[1], LANES), 1)
    pick = jnp.where((src == b * t_new + dst) & (dst < t_new), 1.0, 0.0).astype(bf16)
    selbuf_ref[:, past_len:past_len + LANES] = _dot(kvs_new_ref[...], pick)
    winbuf_ref[:, 0:n_win] = win_ref[...]
    winbuf_ref[:, n_win:n_win + LANES] = _dot(kvw_new_ref[...], pick)
    qbuf_ref[...] = jnp.zeros(qbuf_ref.shape, f32)
    qbuf_ref[0:t_new, :] = q_ref[...]
    gbuf_ref[...] = jnp.zeros(gbuf_ref.shape, f32)
    gbuf_ref[0:t_new, :] = gate_ref[...]

    q = qbuf_ref[...] * NSA_SCALE
    gates = jax.nn.sigmoid(gbuf_ref[...])
    q_pos = past_len + lax.broadcasted_iota(jnp.int32, (tp, 1), 0)
    q_pos_r = jnp.concatenate([q_pos] * NSA_HPG, axis=0)
    cmp_end = lax.broadcasted_iota(jnp.int32, (1, n_cmp), 1) * CMP_STRIDE + (CMP_LEN - 1)
    imp_mat = _importance_matrix(n_cmp, n_sel_pad)
    jj = lax.broadcasted_iota(jnp.int32, (tp, n_sel_pad), 1)
    expand = jnp.where(lax.broadcasted_iota(jnp.int32, (tile_blocks, S_KTILE), 0)
                       == lax.broadcasted_iota(jnp.int32, (tile_blocks, S_KTILE), 1) // SEL_BLOCK,
                       1.0, 0.0).astype(bf16)

    def flash(qs, k_t, v_t, mask, m, l, acc):
        s = jnp.where(mask, jnp.dot(qs, k_t.astype(bf16), preferred_element_type=f32), NEG_INF)
        m_new = jnp.maximum(m, jnp.max(s, axis=-1, keepdims=True))
        alpha = jnp.exp(m - m_new)
        e = jnp.where(mask, jnp.exp(s - m_new), 0.0)
        pv = lax.dot_general(e.astype(bf16), v_t.astype(bf16), _NT, preferred_element_type=f32)
        return m_new, alpha * l + jnp.sum(e, axis=-1, keepdims=True), alpha * acc + pv

    qss, sels, o_cs = [], [], []
    for grp in range(NSA_KV):
        kcol0, vcol0 = grp * NSA_HEAD, NSA_KV_COLS + grp * NSA_HEAD
        qs = _stack_heads(q, grp, tp).astype(bf16)
        p_c = _softmax_rows(_dot_nt(qs, kvc_ref[:, kcol0:kcol0 + NSA_HEAD]), cmp_end <= q_pos_r)
        o_cs.append(_dot(p_c, kvc_ref[:, vcol0:vcol0 + NSA_HEAD]))
        p_sum = p_c[0:tp]
        for h in range(1, NSA_HPG):
            p_sum = p_sum + p_c[h * tp:(h + 1) * tp]
        score = _block_scores(_dot_sel(p_sum, imp_mat), jj, q_pos // SEL_BLOCK)
        rank = jnp.zeros((tp, n_sel_pad), f32)
        for j2 in range(n_sel):
            col = score[:, j2:j2 + 1]
            ge = jnp.where(col >= score, 1.0, 0.0)
            gt = jnp.where(col > score, 1.0, 0.0)
            rank = rank + jnp.where(jj > j2, ge, gt)
        sel = jnp.where((rank < SEL_TOPK) & (jj < n_sel), 1.0, 0.0).astype(bf16)
        sels.append(jnp.concatenate([sel] * NSA_HPG, axis=0))
        qss.append(qs)

    carries = [_flash_init(rows) for _ in range(NSA_KV)]
    n_tiles = past_len // S_KTILE
    for kt in range(n_tiles + 1):
        width = S_KTILE if kt < n_tiles else LANES
        k0 = kt * S_KTILE
        if kt < n_tiles:
            def wait(i, carry):
                page_copy(i).wait()
                return carry
            lax.fori_loop(kt * tile_pages, (kt + 1) * tile_pages, wait, 0)
        k_pos = k0 + lax.broadcasted_iota(jnp.int32, (1, width), 1)
        for grp in range(NSA_KV):
            krow0, vrow0 = grp * NSA_HEAD, NSA_KV_COLS + grp * NSA_HEAD
            blocks = sels[grp][:, kt * tile_blocks:(kt + 1) * tile_blocks]
            chosen = jnp.dot(blocks, expand[:, 0:width], preferred_element_type=f32)
            carries[grp] = flash(qss[grp], selbuf_ref[krow0:krow0 + NSA_HEAD, k0:k0 + width],
                                 selbuf_ref[vrow0:vrow0 + NSA_HEAD, k0:k0 + width],
                                 (chosen > 0.5) & (k_pos <= q_pos_r), *carries[grp])

    w_pos = past_len - n_win + lax.broadcasted_iota(jnp.int32, (1, winbuf_ref.shape[1]), 1)
    dist = jnp.where(w_pos >= 0, q_pos_r - w_pos, -1)
    for grp in range(NSA_KV):
        krow0, vrow0 = grp * NSA_HEAD, NSA_KV_COLS + grp * NSA_HEAD
        o_s = _flash_out(*carries[grp])
        o_w = _flash_out(*flash(qss[grp], winbuf_ref[krow0:krow0 + NSA_HEAD, :], winbuf_ref[vrow0:vrow0 + NSA_HEAD, :],
                                (dist >= 0) & (dist < WINDOW), *_flash_init(rows)))
        o = (_stack_gates(gates, grp, 0) * o_cs[grp] + _stack_gates(gates, grp, 1) * o_s
             + _stack_gates(gates, grp, 2) * o_w)
        for h in range(NSA_HPG):
            col = (grp * NSA_HPG + h) * NSA_HEAD
            obuf_ref[:, col:col + NSA_HEAD] = o[h * tp:(h + 1) * tp]
    o_ref[...] = obuf_ref[0:t_new, :]


def _nsa_sample(p_q, p_gate, kvc_cmp, cache_sel_t, page_table, kvs_new_t, win_state_t, kvw_new_t):
    b, t_new, _ = p_q.shape
    n_pages = page_table.shape[1]
    page = cache_sel_t.shape[2]
    past_len = n_pages * page
    n_cmp = kvc_cmp.shape[1]
    n_win = win_state_t.shape[2]
    tp = 8
    assert t_new <= tp and past_len % S_KTILE == 0 and S_KTILE % page == 0 and n_win % LANES == 0
    assert b * t_new == kvs_new_t.shape[1]
    n_sel_pad = -(-(-(-(past_len + t_new) // SEL_BLOCK)) // LANES) * LANES
    assert (past_len // S_KTILE + 1) * (S_KTILE // SEL_BLOCK) <= n_sel_pad
    per_seq = lambda n, w: pl.BlockSpec((None, n, w), lambda i, pt: (i, 0, 0))
    whole = lambda a: pl.BlockSpec(a.shape, lambda i, pt: (0,) * a.ndim, pipeline_mode=pl.Buffered(1))
    return pl.pallas_call(
        functools.partial(_nsa_sample_kernel, past_len),
        out_shape=jax.ShapeDtypeStruct((b, t_new, NSA_WIDTH), f32),
        grid_spec=pltpu.PrefetchScalarGridSpec(
            num_scalar_prefetch=1, grid=(b,),
            in_specs=[per_seq(t_new, NSA_WIDTH), per_seq(t_new, GATE_PAD), per_seq(n_cmp, KV_COLS),
                      pl.BlockSpec(memory_space=pl.ANY), whole(kvs_new_t), per_seq(KV_COLS, n_win),
                      whole(kvw_new_t)],
            out_specs=per_seq(t_new, NSA_WIDTH),
            scratch_shapes=[pltpu.VMEM((KV_COLS, past_len + LANES), f32), pltpu.VMEM((KV_COLS, n_win + LANES), f32),
                            pltpu.VMEM((tp, NSA_WIDTH), f32), pltpu.VMEM((tp, GATE_PAD), f32),
                            pltpu.VMEM((tp, NSA_WIDTH), f32), pltpu.SemaphoreType.DMA((past_len // S_KTILE,))]),
        compiler_params=_cparams(("arbitrary",)),
        name="nsa_sample",
    )(page_table, p_q, p_gate, kvc_cmp, cache_sel_t, kvs_new_t, win_state_t, kvw_new_t)


def _prep_weights(l, depth, ln_g, ln_b, ffn_w_gate, ffn_w_up, ffn_w_down, w_in, b_in, rw_mu, rw_w0, rw_w2, rw_a0,
                  rw_a2, rw_g2, rw_k_k, rw_k_a, rw_r_k, rw_ln_w, rw_ln_b, nsa_phi_pe, nsa_phi_w1, nsa_phi_w2,
                  w_out_a, w_out_b, w_o):
    d = D_MODEL
    nc = D_FF // FF_CHUNK
    up = lambda w: w.astype(bf16).reshape(d, nc, FF_CHUNK).transpose(1, 0, 2)
    down = lambda w: w.astype(bf16).reshape(nc, FF_CHUNK, d)
    c1 = RW_SHIFT
    c2 = c1 + NSA_WIDTH
    c3 = c2 + 3 * KV_COLS
    c4 = c3 + N_GATE
    cols = lambda a: jnp.concatenate(
        [a[..., :c3], a[..., c4:], a[..., c3:c4], jnp.zeros(a.shape[:-1] + (GATE_PAD - N_GATE,), a.dtype)], axis=-1)
    z = lambda r, c: jnp.zeros((r, c), f32)
    w = RW_WIDTH
    lora = jnp.concatenate([
        jnp.concatenate([rw_w2[l], z(W_LORA, w), z(W_LORA, w)], axis=1),
        jnp.concatenate([z(A_LORA, w), rw_a2[l], z(A_LORA, w)], axis=1),
        jnp.concatenate([z(G_LORA, w), z(G_LORA, w), rw_g2[l]], axis=1)], axis=0)
    hid = lax.broadcasted_iota(jnp.int32, (w, w), 0) // RW_HEAD == lax.broadcasted_iota(jnp.int32, (w, w), 1) // RW_HEAD
    vec = jnp.stack([rw_w0[l], rw_a0[l], rw_k_k[l], rw_k_a[l], rw_r_k[l].reshape(w), rw_ln_w[l], rw_ln_b[l],
                     jnp.zeros((w,), f32)])
    w1 = nsa_phi_w1[l].reshape(2, 2, CMP_STRIDE, NSA_HEAD, CMP_HID)
    eye = jnp.eye(NSA_KV, dtype=f32)
    big = jnp.einsum("kljde,kK,gG->jkgdKGle", w1, jnp.eye(2, dtype=f32), eye)
    per_kv = jnp.einsum("kljde,gG->kjgdGle", w1, eye)
    pe = jnp.broadcast_to(nsa_phi_pe[l].reshape(2, 1, CMP_LEN * NSA_HEAD), (2, 8, CMP_LEN * NSA_HEAD))
    w_cols, b_cols = cols(w_in[l]), cols(b_in[l])
    seg_t = _segments(_PROJ, [name for name, _ in _PROJ_T])
    w_t = jnp.concatenate([w_cols[:, lo:hi] for lo, hi in seg_t], axis=1).T
    b_t = jnp.concatenate([b_cols[lo:hi] for lo, hi in seg_t])
    return dict(
        alpha=(2 * depth) ** 0.25,
        wg1=up(ffn_w_gate[l, 0]), wu1=up(ffn_w_up[l, 0]), wd1=down(ffn_w_down[l, 0]),
        wg2=up(ffn_w_gate[l, 1]), wu2=up(ffn_w_up[l, 1]), wd2=down(ffn_w_down[l, 1]),
        ln_g=ln_g[l], ln_b=ln_b[l],
        w_in=w_cols.astype(bf16), b_in=b_cols.reshape(1, -1),
        w_in_t=w_t.astype(bf16), b_in_t=b_t[:, None],
        phi_w2t=jnp.swapaxes(nsa_phi_w2[l], 1, 2).astype(bf16),
        rw_mu=rw_mu[l].reshape(1, -1), rw_lora=lora.astype(bf16), rw_vec=vec, rw_hsum=hid.astype(bf16),
        phi_big=big.reshape(CMP_ROW, 2 * NSA_KV * 2 * CMP_HID).astype(bf16), phi_pe=pe,
        phi_kv=per_kv.reshape(2, CMP_STRIDE * NSA_KV_COLS, NSA_KV * 2 * CMP_HID).astype(bf16),
        phi_w1f=nsa_phi_w1[l].reshape(2, CMP_LEN * NSA_HEAD, CMP_HID).astype(bf16),
        phi_w2=nsa_phi_w2[l].astype(bf16),
        w_out_a=w_out_a[l].astype(bf16), w_out_b=w_out_b[l].astype(bf16), w_o=w_o[l].astype(bf16))


def _kv_state(p):
    return p.reshape(p.shape[:-1] + (2, NSA_KV, NSA_HEAD))


def _tokens_last(a):
    n, t = a.shape[:2]
    return a.transpose(0, 2, 3, 4, 1).reshape(n, KV_COLS, t)


def _kv_state_t(p_t):
    b, _, t = p_t.shape
    return p_t.reshape(b, 2, NSA_KV, NSA_HEAD, t).transpose(0, 4, 1, 2, 3)


def kernel(x_prompt, x_sample, c_prompt, c_sample, cache_kv_cmp, cache_kv_sel, state_kv_win, state_wkv, state_shift, page_table, w_ada, b_ada, ln_g, ln_b, ffn_w_gate, ffn_w_up, ffn_w_down, w_in, b_in, rw_mu, rw_w0, rw_w2, rw_a0, rw_a2, rw_g2, rw_k_k, rw_k_a, rw_r_k, rw_ln_w, rw_ln_b, nsa_phi_pe, nsa_phi_w1, nsa_phi_w2, w_out_a, w_out_b, w_o):
    bp, seq, d = x_prompt.shape
    bd, t_new, _ = x_sample.shape
    depth = w_ada.shape[0]
    n_phys, page = cache_kv_cmp.shape[1:3]
    n_pages = page_table.shape[1]
    n_win = state_kv_win.shape[2]
    rows_s = bd * t_new
    tm_p = 256
    assert seq % tm_p == 0 and seq % CMP_STRIDE == 0 and page % CMP_STRIDE == 0

    y_p, y_s = x_prompt, x_sample.reshape(1, rows_s, d)
    st_p, st_s = [], []
    for l in range(depth):
        wts = _prep_weights(l, depth, ln_g, ln_b, ffn_w_gate, ffn_w_up, ffn_w_down, w_in, b_in, rw_mu, rw_w0,
                            rw_w2, rw_a0, rw_a2, rw_g2, rw_k_k, rw_k_a, rw_r_k, rw_ln_w, rw_ln_b, nsa_phi_pe,
                            nsa_phi_w1, nsa_phi_w2, w_out_a, w_out_b, w_o)
        mod = _ada_mod(jnp.concatenate([c_prompt, c_sample], axis=0), w_ada[l], b_ada[l])
        mod_p = mod[:bp].reshape(bp, 9, 1, d)
        mod_s = jnp.repeat(mod[bp:].reshape(bd, 9, d), t_new, axis=0).reshape(rows_s, 9, d)
        mod_s = mod_s.transpose(1, 0, 2).reshape(1, 9, rows_s, d)

        x1, p_rw, p_kc, p_ks, p_kw, p_m, q_t, g_t, ks_t, kw_t = _ffn_in(
            y_p, mod_p, wts, tm_p, ("rw", "kc", "ks", "kw", "merge"), ("q", "gate", "ks", "kw"))
        ya, wkv_p = _rwkv(p_rw, jnp.zeros((bp, RW_SHIFT), f32), jnp.zeros((bp, RW_HEADS, RW_HEAD, RW_HEAD), f32),
                          wts)
        kvc_cmp, kvc_cmp_t = _compress_dense(p_kc.reshape(bp, seq // CMP_STRIDE, CMP_ROW), wts)
        yb_t = _nsa_prompt(q_t, g_t, kvc_cmp, kvc_cmp_t, p_ks, ks_t, p_kw, kw_t)
        y_p = _mix_ffn(x1, ya, yb_t, p_m, mod_p, wts, tm_p, True)
        n_keep = min(WINDOW, seq)
        st_p.append((_kv_state(p_kc), _kv_state_t(ks_t), _kv_state_t(kw_t[:, :, seq - n_keep:]), wkv_p,
                     p_rw[:, -1]))

        x1, p_rw, p_q, p_kc, p_ks, p_m, p_g, ks_t, kw_t = _ffn_in(
            y_s, mod_s, wts, rows_s, ("rw", "q", "kc", "ks", "merge", "gate"), ("ks", "kw"))
        per_seq = lambda a: a.reshape(bd, t_new, a.shape[-1])
        p_rw, p_q, p_kc, p_ks, p_g = map(per_seq, (p_rw, p_q, p_kc, p_ks, p_g))
        ya, wkv_s = _rwkv(p_rw, state_shift[l], state_wkv[l], wts)
        kvc_cmp = _compress_paged(_tokens_last(cache_kv_cmp[l]), page_table, wts)
        win_state_t = _tokens_last(state_kv_win[l])
        yb = _nsa_sample(p_q, p_g, kvc_cmp, _tokens_last(cache_kv_sel[l]), page_table, ks_t[0], win_state_t,
                         kw_t[0])
        y_s = _mix_ffn(x1, ya.reshape(1, rows_s, RW_WIDTH), yb.reshape(1, rows_s, NSA_WIDTH), p_m, mod_s, wts,
                       rows_s, False)
        kw_new_t = kw_t[0].reshape(KV_COLS, bd, t_new).transpose(1, 0, 2)
        new_win_t = jnp.concatenate([win_state_t, kw_new_t], axis=2)[:, :, t_new:]
        st_s.append((_kv_state(p_kc), _kv_state(p_ks), _kv_state_t(new_win_t), wkv_s, p_rw[:, -1]))

    stack = lambda sts, i: jnp.stack([s[i] for s in sts])
    return (y_p, y_s.reshape(bd, t_new, d)) + tuple(stack(st_p, i) for i in range(5)) + tuple(
        stack(st_s, i) for i in range(5))
```

```python
import functools
import math

import jax
import jax.numpy as jnp
from jax import lax
from jax.experimental import pallas as pl
from jax.experimental.pallas import tpu as pltpu

f32 = jnp.float32
bf16 = jnp.bfloat16

D_MODEL = 1024
RW_HEAD = 64
RW_HEADS = 8
RW_WIDTH = RW_HEADS * RW_HEAD
W_LORA = 64
A_LORA = 64
G_LORA = 128
LORA_IN = W_LORA + A_LORA + G_LORA
RW_GN_EPS = 64e-5
RW_SHIFT = 3 * RW_WIDTH + LORA_IN
NSA_HEAD = 64
NSA_HEADS = 8
NSA_KV = 2
NSA_HPG = NSA_HEADS // NSA_KV
NSA_WIDTH = NSA_HEADS * NSA_HEAD
NSA_KV_COLS = NSA_KV * NSA_HEAD
KV_COLS = 2 * NSA_KV_COLS
NSA_SCALE = NSA_HEAD ** -0.5
CMP_STRIDE = 16
CMP_LEN = 2 * CMP_STRIDE
CMP_HID = 128
CMP_ROW = CMP_STRIDE * KV_COLS
SEL_BLOCK = 64
SEL_TOPK = 16
WINDOW = 512
D_FF = 2816
N_GATE = 3 * NSA_HEADS
LN_EPS = 1e-5
NEG_INF = -1e30
BIG = 1e9

LANES = 128
GATE_PAD = LANES
FF_CHUNK = 256
TM_FFN = 1024
TM_PROJ = 512
RW_CHUNK = 64
RW_HPB = 4
RW_SEQS = 4
Q_TILE = 256
K_TILE = 256
S_KTILE = 2048
VMEM_LIMIT = 56 * 1024 * 1024

_NT = (((1,), (1,)), ((), ()))
_TN = (((0,), (0,)), ((), ()))
_HI = lax.Precision.HIGHEST


def _dot(a, b):
    return jnp.dot(a.astype(bf16), b.astype(bf16), preferred_element_type=f32)


def _dot_nt(a, b):
    return lax.dot_general(a.astype(bf16), b.astype(bf16), _NT, preferred_element_type=f32)


def _dot_f32(a, b, dims=None):
    if dims is None:
        return jnp.dot(a, b, preferred_element_type=f32, precision=_HI)
    return lax.dot_general(a, b, dims, preferred_element_type=f32, precision=_HI)


def _dot_sel(x, m01):
    hi = x.astype(bf16)
    r1 = x - hi.astype(f32)
    mid = r1.astype(bf16)
    lo = (r1 - mid.astype(f32)).astype(bf16)
    d = lambda t: jnp.dot(t, m01, preferred_element_type=f32)
    return d(hi) + d(mid) + d(lo)


def _layer_norm(x, g, b):
    mu = jnp.mean(x, axis=-1, keepdims=True)
    xc = x - mu
    var = jnp.mean(xc * xc, axis=-1, keepdims=True)
    return xc * lax.rsqrt(var + LN_EPS) * g + b


def _cparams(sem):
    return pltpu.CompilerParams(dimension_semantics=sem, vmem_limit_bytes=VMEM_LIMIT)


def _resident(shape):
    nd = len(shape)
    return pl.BlockSpec(shape, lambda *_: (0,) * nd, pipeline_mode=pl.Buffered(1))


def _ada_kernel(c_ref, w_ref, b_ref, o_ref):
    c = c_ref[...]
    o_ref[...] = _dot(c * jax.nn.sigmoid(c), w_ref[...]) + b_ref[...]


def _ada_mod(c, w_ada, b_ada):
    rows, d = c.shape
    n = w_ada.shape[1]
    tn = d
    return pl.pallas_call(
        _ada_kernel,
        out_shape=jax.ShapeDtypeStruct((rows, n), f32),
        grid=(n // tn,),
        in_specs=[pl.BlockSpec((rows, d), lambda j: (0, 0)),
                  pl.BlockSpec((d, tn), lambda j: (0, j)),
                  pl.BlockSpec((1, tn), lambda j: (0, j))],
        out_specs=pl.BlockSpec((rows, tn), lambda j: (0, j)),
        compiler_params=_cparams(("parallel",)),
        name="ada_mod",
    )(c, w_ada, b_ada.reshape(1, n))


def _ffn(u, wg_ref, wu_ref, wd_ref, acc_ref):
    ub = u.astype(bf16)
    acc_ref[...] = jnp.zeros(acc_ref.shape, f32)

    def body(c, carry):
        hg = jnp.dot(ub, wg_ref[c], preferred_element_type=f32)
        hu = jnp.dot(ub, wu_ref[c], preferred_element_type=f32)
        h = hg * jax.nn.sigmoid(hg) * hu
        acc_ref[...] += jnp.dot(h.astype(bf16), wd_ref[c], preferred_element_type=f32)
        return carry

    lax.fori_loop(0, wg_ref.shape[0], body, 0)
    return acc_ref[...]


def _ffn_block_kernel(alpha, half, x_ref, mod_ref, wg_ref, wu_ref, wd_ref, lng_ref, lnb_ref, o_ref, acc_ref):
    m0 = 6 * half
    ln = 2 * half
    x = x_ref[...]
    u = x * (1.0 + mod_ref[m0 + 1]) + mod_ref[m0]
    f = _ffn(u, wg_ref, wu_ref, wd_ref, acc_ref)
    o_ref[...] = _layer_norm(alpha * x + 0.5 * (1.0 + mod_ref[m0 + 2]) * f, lng_ref[ln:ln + 1, :],
                             lnb_ref[ln:ln + 1, :])


def _ffn_block(x, mod, wts, tm, half):
    g, t, d = x.shape
    r = mod.shape[2]
    row = pl.BlockSpec((None, tm, d), lambda i, j: (i, j, 0))
    names = ("wg1", "wu1", "wd1") if half == 0 else ("wg2", "wu2", "wd2")
    weights = [wts[n] for n in names] + [wts["ln_g"], wts["ln_b"]]
    return pl.pallas_call(
        functools.partial(_ffn_block_kernel, wts["alpha"], half),
        out_shape=jax.ShapeDtypeStruct((g, t, d), f32),
        grid=(g, t // tm),
        in_specs=[row, pl.BlockSpec((None, 9, r, d), lambda i, j: (i, 0, 0, 0))] + [_resident(w.shape) for w in weights],
        out_specs=row,
        scratch_shapes=[pltpu.VMEM((tm, d), f32)],
        compiler_params=_cparams(("parallel", "parallel")),
        name="ffn_block",
    )(x, mod, *weights)


def _in_proj_kernel(seg, seg_t, x1_ref, mod_ref, win_ref, bin_ref, wint_ref, bint_ref, *out_refs):
    u2 = (x1_ref[...] * (1.0 + mod_ref[4]) + mod_ref[3]).astype(bf16)
    for o_ref, (lo, hi) in zip(out_refs, seg):
        o_ref[...] = jnp.dot(u2, win_ref[:, lo:hi], preferred_element_type=f32) + bin_ref[:, lo:hi]
    for o_ref, (lo, hi) in zip(out_refs[len(seg):], seg_t):
        o_ref[...] = lax.dot_general(wint_ref[lo:hi, :], u2, _NT, preferred_element_type=f32) + bint_ref[lo:hi, :]


_PROJ = (("rw", RW_SHIFT), ("q", NSA_WIDTH), ("kc", KV_COLS), ("ks", KV_COLS), ("kw", KV_COLS),
         ("merge", 2 * D_MODEL), ("gate", GATE_PAD))
_PROJ_T = (("q", NSA_WIDTH), ("gate", GATE_PAD), ("ks", KV_COLS), ("kw", KV_COLS))


def _segments(table, names):
    seg, lo = {}, 0
    for name, w in table:
        seg[name] = (lo, lo + w)
        lo += w
    return tuple(seg[n] for n in names)


def _in_proj(x1, mod, wts, tm, names, names_t):
    g, t, d = x1.shape
    r = mod.shape[2]
    seg, seg_t = _segments(_PROJ, names), _segments(_PROJ_T, names_t)
    row = lambda w: pl.BlockSpec((None, tm, w), lambda i, j: (i, j, 0))
    col = lambda w: pl.BlockSpec((None, w, tm), lambda i, j: (i, 0, j))
    weights = [wts["w_in"], wts["b_in"], wts["w_in_t"], wts["b_in_t"]]
    return pl.pallas_call(
        functools.partial(_in_proj_kernel, seg, seg_t),
        out_shape=([jax.ShapeDtypeStruct((g, t, hi - lo), f32) for lo, hi in seg]
                   + [jax.ShapeDtypeStruct((g, hi - lo, t), f32) for lo, hi in seg_t]),
        grid=(g, t // tm),
        in_specs=[row(d), pl.BlockSpec((None, 9, r, d), lambda i, j: (i, 0, 0, 0))]
                 + [_resident(w.shape) for w in weights],
        out_specs=[row(hi - lo) for lo, hi in seg] + [col(hi - lo) for lo, hi in seg_t],
        compiler_params=_cparams(("parallel", "parallel")),
        name="in_proj",
    )(x1, mod, *weights)


def _mix_kernel(alpha, yb_on_lanes, x1_ref, ya_ref, yb_ref, pm_ref, mod_ref, woa_ref, wob_ref, wo_ref,
                lng_ref, lnb_ref, o_ref):
    x1 = x1_ref[...]
    pm = pm_ref[...]
    d = x1.shape[-1]
    ga = jax.nn.sigmoid(pm[:, :d])
    gb = jax.nn.sigmoid(pm[:, d:])
    if yb_on_lanes:
        yb_out = lax.dot_general(yb_ref[...].astype(bf16), wob_ref[...], _TN, preferred_element_type=f32)
    else:
        yb_out = _dot(yb_ref[...], wob_ref[...])
    merged = ga * _dot(ya_ref[...], woa_ref[...]) + gb * yb_out
    m = _dot(merged, wo_ref[...])
    o_ref[...] = _layer_norm(alpha * x1 + (1.0 + mod_ref[5]) * m, lng_ref[1:2, :], lnb_ref[1:2, :])


def _mix(x1, ya, yb, pm, mod, wts, tm, yb_on_lanes):
    g, t, d = x1.shape
    r = mod.shape[2]
    row = lambda w: pl.BlockSpec((None, tm, w), lambda i, j: (i, j, 0))
    yb_spec = pl.BlockSpec((None, NSA_WIDTH, tm), lambda i, j: (i, 0, j)) if yb_on_lanes else row(NSA_WIDTH)
    weights = [wts["w_out_a"], wts["w_out_b"], wts["w_o"], wts["ln_g"], wts["ln_b"]]
    return pl.pallas_call(
        functools.partial(_mix_kernel, wts["alpha"], yb_on_lanes),
        out_shape=jax.ShapeDtypeStruct((g, t, d), f32),
        grid=(g, t // tm),
        in_specs=[row(d), row(ya.shape[-1]), yb_spec, row(pm.shape[-1]),
                  pl.BlockSpec((None, 9, r, d), lambda i, j: (i, 0, 0, 0))] + [_resident(w.shape) for w in weights],
        out_specs=row(d),
        compiler_params=_cparams(("parallel", "parallel")),
        name="mix",
    )(x1, ya, yb, pm, mod, *weights)


def _rwkv_kernel(t_blk, p_ref, prev_ref, s0_ref, mu_ref, wl_ref, vec_ref, hsum_ref,
                 y_ref, sout_ref, state_ref, prevrow_ref, stage_ref, yn_ref):
    n_seq = p_ref.shape[0]
    ci = pl.program_id(1)

    @pl.when(ci == 0)
    def _():
        state_ref[...] = jnp.zeros(state_ref.shape, f32)
        for s in range(n_seq):
            for h in range(RW_HEADS):
                d0 = (h % RW_HPB) * RW_HEAD
                state_ref[s, h // RW_HPB, d0:d0 + RW_HEAD, d0:d0 + RW_HEAD] = s0_ref[s, h]
        prevrow_ref[...] = prev_ref[...]

    c = stage_ref.shape[1]
    gw = RW_HPB * RW_HEAD
    rows = RW_HPB * c
    br = lax.broadcasted_iota(jnp.int32, (rows, gw), 0)
    bc = lax.broadcasted_iota(jnp.int32, (rows, gw), 1)
    same_head = (br // c) == (bc // RW_HEAD)
    tr = lax.broadcasted_iota(jnp.int32, (rows, rows), 0) % c
    tc = lax.broadcasted_iota(jnp.int32, (rows, rows), 1) % c
    strict = tr > tc
    incl = tr >= tc
    blocks = lambda x: jnp.where(same_head, jnp.concatenate([x] * RW_HPB, axis=0), 0.0).astype(bf16)
    nt = lambda x, y: lax.dot_general(x, y, _NT, preferred_element_type=f32)
    tn = lambda x, y: lax.dot_general(x, y, _TN, preferred_element_type=f32)
    mm = lambda x, y: jnp.dot(x, y, preferred_element_type=f32)
    n_fac = max(1, (c - 1).bit_length())

    pre = [_rwkv_features(t_blk, p_ref.at[s], mu_ref, wl_ref, vec_ref, hsum_ref, prevrow_ref.at[s],
                          stage_ref.at[s]) for s in range(n_seq)]
    chains = [(s, grp) for s in range(n_seq) for grp in range(RW_HEADS // RW_HPB)]
    ops, low, u = {}, {}, {}
    for s, grp in chains:
        gl = slice(grp * gw, (grp + 1) * gw)
        ops[s, grp] = [blocks(pre[s][name][:, gl]) for name in ("at", "rt", "bt", "kt", "bh", "kh", "v")]
    for ch in chains:
        at4, rt4, bt4, kt4, bh4, kh4, v4 = ops[ch]
        s_b = state_ref[ch].astype(bf16)
        low[ch] = jnp.where(strict, nt(at4, bt4), 0.0).astype(bf16)
        u[ch] = nt(at4, s_b) + mm(jnp.where(strict, nt(at4, kt4), 0.0).astype(bf16), v4)
    for i in range(n_fac):
        for ch in chains:
            u[ch] = u[ch] + mm(low[ch], u[ch].astype(bf16))
        if i + 1 < n_fac:
            for ch in chains:
                low[ch] = mm(low[ch], low[ch]).astype(bf16)
    for ch in chains:
        s, grp = ch
        gl = slice(grp * gw, (grp + 1) * gw)
        at4, rt4, bt4, kt4, bh4, kh4, v4 = ops[ch]
        s_bd = state_ref[ch]
        u_b = u[ch].astype(bf16)
        y4 = (nt(rt4, s_bd.astype(bf16)) + mm(jnp.where(incl, nt(rt4, bt4), 0.0).astype(bf16), u_b)
              + mm(jnp.where(incl, nt(rt4, kt4), 0.0).astype(bf16), v4))
        state_ref[ch] = s_bd * pre[s]["p_last"][:, gl] + tn(u_b, bh4) + tn(v4, kh4)
        yg = y4[0:c]
        for hh in range(1, RW_HPB):
            yg = yg + y4[hh * c:(hh + 1) * c]
        yn_ref[s, :, gl] = yg

    hsum = hsum_ref[...]
    ln_w, ln_b = vec_ref[5:6, :], vec_ref[6:7, :]
    for s in range(n_seq):
        yh = yn_ref[s]
        mean = _dot_sel(yh, hsum) * (1.0 / RW_HEAD)
        yc = yh - mean
        var = _dot_sel(yc * yc, hsum) * (1.0 / RW_HEAD)
        y = (yc * lax.rsqrt(var + RW_GN_EPS) * ln_w + ln_b + pre[s]["bonus"]) * pre[s]["g"]
        y_ref[s] = y[0:t_blk, :] if t_blk < c else y

    @pl.when(ci == pl.num_programs(1) - 1)
    def _():
        for s in range(n_seq):
            for h in range(RW_HEADS):
                d0 = (h % RW_HPB) * RW_HEAD
                sout_ref[s, h] = state_ref[s, h // RW_HPB, d0:d0 + RW_HEAD, d0:d0 + RW_HEAD]


def _rwkv_features(t_blk, p_ref, mu_ref, wl_ref, vec_ref, hsum_ref, prevrow_ref, stage_ref):
    c = stage_ref.shape[0]
    w = RW_WIDTH
    if t_blk < c:
        stage_ref[...] = jnp.zeros(stage_ref.shape, f32)
        stage_ref[0:t_blk, :] = p_ref[...]
        p = stage_ref[...]
    else:
        p = p_ref[...]
    ridx = lax.broadcasted_iota(jnp.int32, (c, 1), 0)
    valid = ridx < t_blk
    p_prev = jnp.where(ridx == 0, prevrow_ref[...], pltpu.roll(p, 1, 0))
    prevrow_ref[...] = p[c - 1:c, :]
    xs = p + (p_prev - p) * mu_ref[...]

    tail = xs[:, 3 * w:]
    li = lax.broadcasted_iota(jnp.int32, tail.shape, 1)
    act = jnp.where(li < W_LORA, jnp.tanh(tail),
                    jnp.where(li < W_LORA + A_LORA, tail, jax.nn.sigmoid(tail)))
    lora = _dot(act, wl_ref[...])
    w0, a0, k_k, k_a, r_k = (vec_ref[i:i + 1, :] for i in range(5))
    z = -(w0 + lora[:, :w])
    softplus = jnp.maximum(z, 0.0) + jnp.log1p(jnp.exp(-jnp.abs(z)))
    lw = -jnp.exp(-softplus - 0.5)
    a = jax.nn.sigmoid(a0 + lora[:, w:2 * w])
    g = lora[:, 2 * w:]
    r, k, v = xs[:, :w], xs[:, w:2 * w], xs[:, 2 * w:3 * w]
    hsum = hsum_ref[...]
    kk = k * k_k
    kk = kk * lax.rsqrt(jnp.maximum(_dot_sel(kk * kk, hsum), 1e-24))
    k2 = k * (1.0 + (a - 1.0) * k_a)
    bonus = _dot_sel(r * k2 * r_k, hsum) * v
    if t_blk < c:
        lw = jnp.where(valid, lw, 0.0)
        kk = jnp.where(valid, kk, 0.0)
        k2 = jnp.where(valid, k2, 0.0)
        v = jnp.where(valid, v, 0.0)
    b = kk * a

    rr = lax.broadcasted_iota(jnp.int32, (c, c), 0)
    cc = lax.broadcasted_iota(jnp.int32, (c, c), 1)
    cum = _dot_f32(jnp.where(rr >= cc, 1.0, 0.0).astype(f32), lw)
    cum_last = cum[c - 1:c, :]
    e_neg = jnp.exp(-cum)
    e_rem = jnp.exp(cum_last - cum)
    at = -kk * jnp.exp(cum - lw)
    rt = r * jnp.exp(cum)
    bt, kt = b * e_neg, k2 * e_neg
    bh, kh = b * e_rem, k2 * e_rem
    p_last = jnp.exp(cum_last)
    return dict(at=at, rt=rt, bt=bt, kt=kt, bh=bh, kh=kh, v=v, p_last=p_last, bonus=bonus, g=g)


def _rwkv(p_rw, prev, s0, wts):
    g, t, ws = p_rw.shape
    c = RW_CHUNK
    t_blk = min(c, t)
    sb = RW_SEQS
    assert t % t_blk == 0 and g % sb == 0
    gw = RW_HPB * RW_HEAD
    return pl.pallas_call(
        functools.partial(_rwkv_kernel, t_blk),
        out_shape=[jax.ShapeDtypeStruct((g, t, RW_WIDTH), f32), jax.ShapeDtypeStruct(s0.shape, f32)],
        grid=(g // sb, t // t_blk),
        in_specs=[pl.BlockSpec((sb, t_blk, ws), lambda i, j: (i, j, 0)),
                  pl.BlockSpec((sb, 1, ws), lambda i, j: (i, 0, 0)),
                  pl.BlockSpec((sb,) + s0.shape[1:], lambda i, j: (i, 0, 0, 0)),
                  _resident(wts["rw_mu"].shape), _resident(wts["rw_lora"].shape),
                  _resident(wts["rw_vec"].shape), _resident(wts["rw_hsum"].shape)],
        out_specs=[pl.BlockSpec((sb, t_blk, RW_WIDTH), lambda i, j: (i, j, 0)),
                   pl.BlockSpec((sb,) + s0.shape[1:], lambda i, j: (i, 0, 0, 0))],
        scratch_shapes=[pltpu.VMEM((sb, RW_HEADS // RW_HPB, gw, gw), f32), pltpu.VMEM((sb, 1, ws), f32),
                        pltpu.VMEM((sb, c, ws), f32), pltpu.VMEM((sb, c, RW_WIDTH), f32)],
        compiler_params=_cparams(("parallel", "arbitrary")),
        name="rwkv",
    )(p_rw, prev.reshape(g, 1, ws), s0, wts["rw_mu"], wts["rw_lora"], wts["rw_vec"], wts["rw_hsum"])


def _gelu_tanh(x):
    return x * (0.5 * (1.0 + jnp.tanh(math.sqrt(2.0 / math.pi) * (x + 0.044715 * (x * x * x)))))


def _compress_rows(hids, pe_ref, w1f_ref, w2_ref, o_ref, w2t_ref=None, ot_ref=None):
    n = hids[0].shape[0]
    for kv in range(2):
        pe_term = _dot(pe_ref[kv], w1f_ref[kv])[0:1, :]
        for grp in range(NSA_KV):
            base = grp * 2 * CMP_HID
            lo = hids[kv][:, base:base + CMP_HID]
            hi = pltpu.roll(hids[kv][:, base + CMP_HID:base + 2 * CMP_HID], n - 1, 0)
            act = _gelu_tanh(lo + hi + pe_term).astype(bf16)
            col = (kv * NSA_KV + grp) * NSA_HEAD
            o_ref[:, col:col + NSA_HEAD] = jnp.dot(act, w2_ref[kv], preferred_element_type=f32)
            if ot_ref is not None:
                ot_ref[col:col + NSA_HEAD, :] = lax.dot_general(w2t_ref[kv], act, _NT, preferred_element_type=f32)


def _compress_dense_kernel(x_ref, wbig_ref, pe_ref, w1f_ref, w2_ref, w2t_ref, o_ref, ot_ref):
    hid = _dot(x_ref[...], wbig_ref[...])
    half = hid.shape[1] // 2
    _compress_rows([hid[:, :half], hid[:, half:]], pe_ref, w1f_ref, w2_ref, o_ref, w2t_ref, ot_ref)


def _compress_dense(x, wts):
    b, n, _ = x.shape
    return pl.pallas_call(
        _compress_dense_kernel,
        out_shape=[jax.ShapeDtypeStruct((b, n, KV_COLS), f32), jax.ShapeDtypeStruct((b, KV_COLS, n), f32)],
        grid=(b,),
        in_specs=[pl.BlockSpec((None, n, CMP_ROW), lambda i: (i, 0, 0)),
                  _resident(wts["phi_big"].shape), _resident(wts["phi_pe"].shape),
                  _resident(wts["phi_w1f"].shape), _resident(wts["phi_w2"].shape),
                  _resident(wts["phi_w2t"].shape)],
        out_specs=[pl.BlockSpec((None, n, KV_COLS), lambda i: (i, 0, 0)),
                   pl.BlockSpec((None, KV_COLS, n), lambda i: (i, 0, 0))],
        compiler_params=_cparams(("parallel",)),
        name="compress_dense",
    )(x, wts["phi_big"], wts["phi_pe"], wts["phi_w1f"], wts["phi_w2"], wts["phi_w2t"])


CMP_GROUP = 2 * LANES
CMP_UNROLL = 4


def _compress_paged_kernel(pt_ref, cache_ref, wkv_ref, pe_ref, w1f_ref, w2_ref, o_ref, pbuf_ref, xbuf_ref, sem):
    b = pl.program_id(0)
    n_pages = pt_ref.shape[1]
    page = cache_ref.shape[2]
    pages_per_group = CMP_GROUP // page
    rows_per_group = CMP_GROUP // CMP_STRIDE

    def page_copy(seq, i):
        return pltpu.make_async_copy(cache_ref.at[pt_ref[seq, i]], pbuf_ref.at[:, pl.ds(i * page, page)],
                                     sem.at[i // pages_per_group])

    def start_all(seq):
        def start(i, carry):
            page_copy(seq, i).start()
            return carry
        lax.fori_loop(0, n_pages, start, 0)

    @pl.when(b == 0)
    def _():
        start_all(0)

    r = lax.broadcasted_iota(jnp.int32, (CMP_GROUP, CMP_GROUP), 0)
    c = lax.broadcasted_iota(jnp.int32, (CMP_GROUP, CMP_GROUP), 1)
    perm = jnp.where(c == CMP_STRIDE * (r % rows_per_group) + r // rows_per_group, 1.0, 0.0).astype(bf16)

    def regroup(step, carry):
        for pg in range(CMP_UNROLL * pages_per_group):
            page_copy(b, step * CMP_UNROLL * pages_per_group + pg).wait()
        for k in range(CMP_UNROLL):
            gi = step * CMP_UNROLL + k
            t0 = pl.multiple_of(gi * CMP_GROUP, CMP_GROUP)
            tokens = pbuf_ref[:, pl.ds(t0, CMP_GROUP)].astype(bf16)
            rows = lax.dot_general(perm, tokens, _NT, preferred_element_type=f32)
            r0 = pl.multiple_of(gi * rows_per_group, rows_per_group)
            for j in range(CMP_STRIDE):
                for kv in range(2):
                    col = (kv * CMP_STRIDE + j) * NSA_KV_COLS
                    xbuf_ref[pl.ds(r0, rows_per_group), col:col + NSA_KV_COLS] = (
                        rows[j * rows_per_group:(j + 1) * rows_per_group,
                             kv * NSA_KV_COLS:(kv + 1) * NSA_KV_COLS].astype(bf16))
        return carry

    n_groups = n_pages // pages_per_group
    lax.fori_loop(0, n_groups // CMP_UNROLL, regroup, 0)

    @pl.when(b + 1 < pl.num_programs(0))
    def _():
        start_all(b + 1)

    half = CMP_STRIDE * NSA_KV_COLS
    hids = [jnp.dot(xbuf_ref[:, kv * half:(kv + 1) * half], wkv_ref[kv], preferred_element_type=f32)
            for kv in range(2)]
    _compress_rows(hids, pe_ref, w1f_ref, w2_ref, o_ref)


def _compress_paged(cache_t, page_table, wts):
    b, n_pages = page_table.shape
    page = cache_t.shape[2]
    n = n_pages * page // CMP_STRIDE
    assert CMP_GROUP % page == 0 and (n_pages * page) % (CMP_GROUP * CMP_UNROLL) == 0
    full = lambda a: pl.BlockSpec(a.shape, lambda i, pt: (0,) * a.ndim, pipeline_mode=pl.Buffered(1))
    return pl.pallas_call(
        _compress_paged_kernel,
        out_shape=jax.ShapeDtypeStruct((b, n, KV_COLS), f32),
        grid_spec=pltpu.PrefetchScalarGridSpec(
            num_scalar_prefetch=1, grid=(b,),
            in_specs=[pl.BlockSpec(memory_space=pl.ANY), full(wts["phi_kv"]), full(wts["phi_pe"]),
                      full(wts["phi_w1f"]), full(wts["phi_w2"])],
            out_specs=pl.BlockSpec((None, n, KV_COLS), lambda i, pt: (i, 0, 0)),
            scratch_shapes=[pltpu.VMEM((KV_COLS, n_pages * page), f32), pltpu.VMEM((n, CMP_ROW), bf16),
                            pltpu.SemaphoreType.DMA((n_pages * page // CMP_GROUP,))]),
        compiler_params=_cparams(("arbitrary",)),
        name="compress_paged",
    )(page_table, cache_t, wts["phi_kv"], wts["phi_pe"], wts["phi_w1f"], wts["phi_w2"])


def _softmax_rows(s, mask):
    s = jnp.where(mask, s, NEG_INF)
    e = jnp.where(mask, jnp.exp(s - jnp.max(s, axis=-1, keepdims=True)), 0.0)
    return e / jnp.maximum(jnp.sum(e, axis=-1, keepdims=True), 1e-30)


def _flash_step(q, k, v, mask, m, l, acc):
    s = jnp.where(mask, _dot_nt(q, k), NEG_INF)
    m_new = jnp.maximum(m, jnp.max(s, axis=-1, keepdims=True))
    alpha = jnp.exp(m - m_new)
    e = jnp.where(mask, jnp.exp(s - m_new), 0.0)
    return m_new, alpha * l + jnp.sum(e, axis=-1, keepdims=True), alpha * acc + _dot(e, v)


def _flash_init(rows):
    return (jnp.full((rows, 1), NEG_INF, f32), jnp.zeros((rows, 1), f32), jnp.zeros((rows, NSA_HEAD), f32))


def _flash_out(m, l, acc):
    return acc / jnp.maximum(l, 1e-30)


def _importance_matrix(n_cmp_rows, n_sel_cols):
    ratio = SEL_BLOCK // CMP_STRIDE
    i = lax.broadcasted_iota(jnp.int32, (n_cmp_rows, n_sel_cols), 0)
    j = lax.broadcasted_iota(jnp.int32, (n_cmp_rows, n_sel_cols), 1)
    return jnp.where((i >= ratio * j - 1) & (i <= ratio * j + ratio - 1), 1.0, 0.0).astype(bf16)


def _block_scores(imp, j, cur):
    forced = (j == 0) | (j == cur) | (j == cur - 1)
    return jnp.where(j <= cur, jnp.where(forced, BIG, imp), -BIG)


def _stack_heads(x, grp, rows):
    return jnp.concatenate(
        [x[:, (grp * NSA_HPG + h) * NSA_HEAD:(grp * NSA_HPG + h + 1) * NSA_HEAD] for h in range(NSA_HPG)], axis=0)


def _stack_gates(gates, grp, branch):
    return jnp.concatenate(
        [gates[:, 3 * (grp * NSA_HPG + h) + branch:3 * (grp * NSA_HPG + h) + branch + 1] for h in range(NSA_HPG)],
        axis=0)


def _scores_t(qts, ks, biases):
    return tuple(jnp.dot(k.astype(bf16), qt, preferred_element_type=f32) + bias
                 for qt, k, bias in zip(qts, ks, biases))


def _flash_update_t(ss, vts, carry):
    n = len(ss)
    ms, ls, accs = carry[0::3], carry[1::3], carry[2::3]
    m_new = [jnp.maximum(ms[i], jnp.max(ss[i], axis=0, keepdims=True)) for i in range(n)]
    es = [jnp.exp2(ss[i] - m_new[i]) for i in range(n)]
    pvs = [jnp.dot(vts[i].astype(bf16), es[i].astype(bf16), preferred_element_type=f32) for i in range(n)]
    out = ()
    for i in range(n):
        alpha = jnp.exp2(ms[i] - m_new[i])
        out += (m_new[i], alpha * ls[i] + jnp.sum(es[i], axis=0, keepdims=True), alpha * accs[i] + pvs[i])
    return out


M_FLOOR = 0.1 * NEG_INF
LOG2_E = math.log2(math.e)


def _flash_init_t(cols):
    return (jnp.full((1, cols), M_FLOOR, f32), jnp.zeros((1, cols), f32), jnp.zeros((NSA_HEAD, cols), f32))


def _lanes(x, n):
    return jnp.concatenate([x] * n, axis=1)


def _nsa_prompt_kernel(qt_ref, gt_ref, kc_ref, vct_ref, ks_ref, vst_ref, kw_ref, vwt_ref, o_ref):
    tq = qt_ref.shape[1]
    n_cmp = kc_ref.shape[0]
    n_sel = ks_ref.shape[0] // SEL_BLOCK
    i = pl.program_id(1)
    t0 = i * tq
    cols = NSA_HPG * tq
    gates = jax.nn.sigmoid(gt_ref[...])
    q_pos = t0 + lax.broadcasted_iota(jnp.int32, (1, tq), 1)
    q_pos_c = _lanes(q_pos, NSA_HPG)
    cmp_end = lax.broadcasted_iota(jnp.int32, (n_cmp, 1), 0) * CMP_STRIDE + (CMP_LEN - 1)
    ratio = SEL_BLOCK // CMP_STRIDE
    ij = lax.broadcasted_iota(jnp.int32, (n_sel, n_cmp), 0)
    ii = lax.broadcasted_iota(jnp.int32, (n_sel, n_cmp), 1)
    imp_mat = jnp.where((ii >= ratio * ij - 1) & (ii <= ratio * ij + ratio - 1), 1.0, 0.0).astype(bf16)
    jt = lax.broadcasted_iota(jnp.int32, (n_sel, tq), 0)
    cur_t = (t0 + lax.broadcasted_iota(jnp.int32, (n_sel, tq), 1)) // SEL_BLOCK
    krow = lax.broadcasted_iota(jnp.int32, (K_TILE, 1), 0)
    er = lax.broadcasted_iota(jnp.int32, (K_TILE, n_sel), 0) // SEL_BLOCK
    ej = lax.broadcasted_iota(jnp.int32, (K_TILE, n_sel), 1)
    groups = range(NSA_KV)
    heads_of = lambda grp: range(grp * NSA_HPG, (grp + 1) * NSA_HPG)
    qts, sels, o_cs = [], [], []

    for grp in groups:
        d0 = grp * NSA_HEAD
        heads = heads_of(grp)
        q_all = jnp.concatenate([qt_ref[h * NSA_HEAD:(h + 1) * NSA_HEAD, :] for h in heads], axis=1)
        qt = (q_all * NSA_SCALE).astype(bf16)
        mask_c = cmp_end <= q_pos_c
        s = jnp.where(mask_c, jnp.dot(kc_ref[:, d0:d0 + NSA_HEAD].astype(bf16), qt, preferred_element_type=f32),
                      NEG_INF)
        e = jnp.where(mask_c, jnp.exp(s - jnp.max(s, axis=0, keepdims=True)), 0.0)
        p_c = e / jnp.maximum(jnp.sum(e, axis=0, keepdims=True), 1e-30)
        o_c = jnp.dot(vct_ref[d0:d0 + NSA_HEAD, :].astype(bf16), p_c.astype(bf16), preferred_element_type=f32)
        p_sum = p_c[:, 0:tq]
        for h in range(1, NSA_HPG):
            p_sum = p_sum + p_c[:, h * tq:(h + 1) * tq]
        hi = p_sum.astype(bf16)
        r1 = p_sum - hi.astype(f32)
        mid = r1.astype(bf16)
        lo = (r1 - mid.astype(f32)).astype(bf16)
        imp = sum(jnp.dot(imp_mat, part, preferred_element_type=f32) for part in (hi, mid, lo))
        score = _block_scores(imp, jt, cur_t)
        rank = jnp.zeros((n_sel, tq), f32)
        for j2 in range(n_sel):
            row = score[j2:j2 + 1, :]
            ge = jnp.where(row >= score, 1.0, 0.0)
            gt = jnp.where(row > score, 1.0, 0.0)
            rank = rank + jnp.where(jt > j2, ge, gt)
        qts.append((q_all * (NSA_SCALE * LOG2_E)).astype(bf16))
        sels.append(jnp.where(rank < SEL_TOPK, 1.0, 0.0).astype(bf16))
        o_cs.append(o_c)

    last = (t0 + tq - 1) // K_TILE
    key_cols = lambda ref, k0: [ref[pl.ds(k0, K_TILE), g * NSA_HEAD:(g + 1) * NSA_HEAD] for g in groups]
    val_rows = lambda ref, k0: [ref[g * NSA_HEAD:(g + 1) * NSA_HEAD, pl.ds(k0, K_TILE)] for g in groups]

    def sel_step(kt, carry):
        k0 = pl.multiple_of(kt * K_TILE, K_TILE)
        expand = jnp.where(ej == kt * (K_TILE // SEL_BLOCK) + er, 1.0, 0.0).astype(bf16)
        causal = (k0 + krow) <= q_pos
        biases = []
        for grp in groups:
            chosen = jnp.dot(expand, sels[grp], preferred_element_type=f32)
            biases.append(_lanes(jnp.where(causal & (chosen > 0.5), 0.0, NEG_INF), NSA_HPG))
        return _flash_update_t(_scores_t(qts, key_cols(ks_ref, k0), biases), val_rows(vst_ref, k0), carry)

    sel_acc = lax.fori_loop(0, last + 1, sel_step, _flash_init_t(cols) * NSA_KV)

    def win_step(kt, carry):
        k0 = pl.multiple_of(kt * K_TILE, K_TILE)
        dist = q_pos - (k0 + krow)
        bias = _lanes(jnp.where((dist >= 0) & (dist < WINDOW), 0.0, NEG_INF), NSA_HPG)
        return _flash_update_t(_scores_t(qts, key_cols(kw_ref, k0), [bias] * NSA_KV), val_rows(vwt_ref, k0), carry)

    win_acc = lax.fori_loop(jnp.maximum(t0 - WINDOW, 0) // K_TILE, last + 1, win_step,
                            _flash_init_t(cols) * NSA_KV)

    for grp in groups:
        heads = heads_of(grp)
        o_s = _flash_out(*sel_acc[3 * grp:3 * grp + 3])
        o_w = _flash_out(*win_acc[3 * grp:3 * grp + 3])
        gate = lambda br: jnp.concatenate([gates[3 * h + br:3 * h + br + 1, :] for h in heads], axis=1)
        o = gate(0) * o_cs[grp] + gate(1) * o_s + gate(2) * o_w
        for n, h in enumerate(heads):
            o_ref[h * NSA_HEAD:(h + 1) * NSA_HEAD, :] = o[:, n * tq:(n + 1) * tq]


def _nsa_prompt(qt, gt, kvc, kvct, kvs, kvst, kvw, kvwt):
    b, _, t = qt.shape
    n_cmp = kvc.shape[1]
    tq = Q_TILE
    assert t % tq == 0 and t % K_TILE == 0 and K_TILE % SEL_BLOCK == 0
    keys =lambda r: pl.BlockSpec((None, r, NSA_KV_COLS), lambda i, j: (i, 0, 0))
    vals = lambda c: pl.BlockSpec((None, NSA_KV_COLS, c), lambda i, j: (i, 1, 0))
    return pl.pallas_call(
        _nsa_prompt_kernel,
        out_shape=jax.ShapeDtypeStruct((b, NSA_WIDTH, t), f32),
        grid=(b, t // tq),
        in_specs=[pl.BlockSpec((None, NSA_WIDTH, tq), lambda i, j: (i, 0, j)),
                  pl.BlockSpec((None, GATE_PAD, tq), lambda i, j: (i, 0, j)),
                  keys(n_cmp), vals(n_cmp), keys(t), vals(t), keys(t), vals(t)],
        out_specs=pl.BlockSpec((None, NSA_WIDTH, tq), lambda i, j: (i, 0, j)),
        compiler_params=_cparams(("parallel", "arbitrary")),
        name="nsa_prompt",
    )(qt, gt, kvc, kvct, kvs, kvst, kvw, kvwt)


def _nsa_sample_kernel(past_len, pt_ref, q_ref, gate_ref, kvc_ref, cache_ref, kvs_new_ref, win_ref, kvw_new_ref,
                       o_ref, selbuf_ref, winbuf_ref, qbuf_ref, gbuf_ref, obuf_ref, sem):
    b = pl.program_id(0)
    t_new = q_ref.shape[0]
    tp = qbuf_ref.shape[0]
    n_pages = pt_ref.shape[1]
    page = cache_ref.shape[2]
    n_cmp = kvc_ref.shape[0]
    n_win = win_ref.shape[1]
    total = past_len + t_new
    n_sel = -(-total // SEL_BLOCK)
    n_sel_pad = -(-n_sel // LANES) * LANES
    tile_pages = S_KTILE // page
    tile_blocks = S_KTILE // SEL_BLOCK
    rows = NSA_HPG * tp

    def page_copy(i):
        return pltpu.make_async_copy(cache_ref.at[pt_ref[b, i]], selbuf_ref.at[:, pl.ds(i * page, page)],
                                     sem.at[i // tile_pages])

    def start(i, carry):
        page_copy(i).start()
        return carry

    lax.fori_loop(0, n_pages, start, 0)

    src = lax.broadcasted_iota(jnp.int32, (kvs_new_ref.shape[1], LANES), 0)
    dst = lax.broadcasted_iota(jnp.int32, (kvs_new_ref.shape[1], LANES), 1)
    pick = jnp.where((src == b * t_new + dst) & (dst < t_new), 1.0, 0.0).astype(bf16)
    selbuf_ref[:, past_len:past_len + LANES] = _dot(kvs_new_ref[...], pick)
    winbuf_ref[:, 0:n_win] = win_ref[...]
    winbuf_ref[:, n_win:n_win + LANES] = _dot(kvw_new_ref[...], pick)
    qbuf_ref[...] = jnp.zeros(qbuf_ref.shape, f32)
    qbuf_ref[0:t_new, :] = q_ref[...]
    gbuf_ref[...] = jnp.zeros(gbuf_ref.shape, f32)
    gbuf_ref[0:t_new, :] = gate_ref[...]

    q = qbuf_ref[...] * NSA_SCALE
    gates = jax.nn.sigmoid(gbuf_ref[...])
    q_pos = past_len + lax.broadcasted_iota(jnp.int32, (tp, 1), 0)
    q_pos_r = jnp.concatenate([q_pos] * NSA_HPG, axis=0)
    cmp_end = lax.broadcasted_iota(jnp.int32, (1, n_cmp), 1) * CMP_STRIDE + (CMP_LEN - 1)
    imp_mat = _importance_matrix(n_cmp, n_sel_pad)
    jj = lax.broadcasted_iota(jnp.int32, (tp, n_sel_pad), 1)
    expand = jnp.where(lax.broadcasted_iota(jnp.int32, (tile_blocks, S_KTILE), 0)
                       == lax.broadcasted_iota(jnp.int32, (tile_blocks, S_KTILE), 1) // SEL_BLOCK,
                       1.0, 0.0).astype(bf16)

    def flash(qs, k_t, v_t, mask, m, l, acc):
        s = jnp.where(mask, jnp.dot(qs, k_t.astype(bf16), preferred_element_type=f32), NEG_INF)
        m_new = jnp.maximum(m, jnp.max(s, axis=-1, keepdims=True))
        alpha = jnp.exp(m - m_new)
        e = jnp.where(mask, jnp.exp(s - m_new), 0.0)
        pv = lax.dot_general(e.astype(bf16), v_t.astype(bf16), _NT, preferred_element_type=f32)
        return m_new, alpha * l + jnp.sum(e, axis=-1, keepdims=True), alpha * acc + pv

    qss, sels, o_cs = [], [], []
    for grp in range(NSA_KV):
        kcol0, vcol0 = grp * NSA_HEAD, NSA_KV_COLS + grp * NSA_HEAD
        qs = _stack_heads(q, grp, tp).astype(bf16)
        p_c = _softmax_rows(_dot_nt(qs, kvc_ref[:, kcol0:kcol0 + NSA_HEAD]), cmp_end <= q_pos_r)
        o_cs.append(_dot(p_c, kvc_ref[:, vcol0:vcol0 + NSA_HEAD]))
        p_sum = p_c[0:tp]
        for h in range(1, NSA_HPG):
            p_sum = p_sum + p_c[h * tp:(h + 1) * tp]
        score = _block_scores(_dot_sel(p_sum, imp_mat), jj, q_pos // SEL_BLOCK)
        rank = jnp.zeros((tp, n_sel_pad), f32)
        for j2 in range(n_sel):
            col = score[:, j2:j2 + 1]
            ge = jnp.where(col >= score, 1.0, 0.0)
            gt = jnp.where(col > score, 1.0, 0.0)
            rank = rank + jnp.where(jj > j2, ge, gt)
        sel = jnp.where((rank < SEL_TOPK) & (jj < n_sel), 1.0, 0.0).astype(bf16)
        sels.append(jnp.concatenate([sel] * NSA_HPG, axis=0))
        qss.append(qs)

    carries = [_flash_init(rows) for _ in range(NSA_KV)]
    n_tiles = past_len // S_KTILE
    for kt in range(n_tiles + 1):
        width = S_KTILE if kt < n_tiles else LANES
        k0 = kt * S_KTILE
        if kt < n_tiles:
            def wait(i, carry):
                page_copy(i).wait()
                return carry
            lax.fori_loop(kt * tile_pages, (kt + 1) * tile_pages, wait, 0)
        k_pos = k0 + lax.broadcasted_iota(jnp.int32, (1, width), 1)
        for grp in range(NSA_KV):
            krow0, vrow0 = grp * NSA_HEAD, NSA_KV_COLS + grp * NSA_HEAD
            blocks = sels[grp][:, kt * tile_blocks:(kt + 1) * tile_blocks]
            chosen = jnp.dot(blocks, expand[:, 0:width], preferred_element_type=f32)
            carries[grp] = flash(qss[grp], selbuf_ref[krow0:krow0 + NSA_HEAD, k0:k0 + width],
                                 selbuf_ref[vrow0:vrow0 + NSA_HEAD, k0:k0 + width],
                                 (chosen > 0.5) & (k_pos <= q_pos_r), *carries[grp])

    w_pos = past_len - n_win + lax.broadcasted_iota(jnp.int32, (1, winbuf_ref.shape[1]), 1)
    dist = jnp.where(w_pos >= 0, q_pos_r - w_pos, -1)
    for grp in range(NSA_KV):
        krow0, vrow0 = grp * NSA_HEAD, NSA_KV_COLS + grp * NSA_HEAD
        o_s = _flash_out(*carries[grp])
        o_w = _flash_out(*flash(qss[grp], winbuf_ref[krow0:krow0 + NSA_HEAD, :], winbuf_ref[vrow0:vrow0 + NSA_HEAD, :],
                                (dist >= 0) & (dist < WINDOW), *_flash_init(rows)))
        o = (_stack_gates(gates, grp, 0) * o_cs[grp] + _stack_gates(gates, grp, 1) * o_s
             + _stack_gates(gates, grp, 2) * o_w)
        for h in range(NSA_HPG):
            col = (grp * NSA_HPG + h) * NSA_HEAD
            obuf_ref[:, col:col + NSA_HEAD] = o[h * tp:(h + 1) * tp]
    o_ref[...] = obuf_ref[0:t_new, :]


def _nsa_sample(p_q, p_gate, kvc_cmp, cache_sel_t, page_table, kvs_new_t, win_state_t, kvw_new_t):
    b, t_new, _ = p_q.shape
    n_pages = page_table.shape[1]
    page = cache_sel_t.shape[2]
    past_len = n_pages * page
    n_cmp = kvc_cmp.shape[1]
    n_win = win_state_t.shape[2]
    tp = 8
    assert t_new <= tp and past_len % S_KTILE == 0 and S_KTILE % page == 0 and n_win % LANES == 0
    assert b * t_new == kvs_new_t.shape[1]
    n_sel_pad = -(-(-(-(past_len + t_new) // SEL_BLOCK)) // LANES) * LANES
    assert (past_len // S_KTILE + 1) * (S_KTILE // SEL_BLOCK) <= n_sel_pad
    per_seq = lambda n, w: pl.BlockSpec((None, n, w), lambda i, pt: (i, 0, 0))
    whole = lambda a: pl.BlockSpec(a.shape, lambda i, pt: (0,) * a.ndim, pipeline_mode=pl.Buffered(1))
    return pl.pallas_call(
        functools.partial(_nsa_sample_kernel, past_len),
        out_shape=jax.ShapeDtypeStruct((b, t_new, NSA_WIDTH), f32),
        grid_spec=pltpu.PrefetchScalarGridSpec(
            num_scalar_prefetch=1, grid=(b,),
            in_specs=[per_seq(t_new, NSA_WIDTH), per_seq(t_new, GATE_PAD), per_seq(n_cmp, KV_COLS),
                      pl.BlockSpec(memory_space=pl.ANY), whole(kvs_new_t), per_seq(KV_COLS, n_win),
                      whole(kvw_new_t)],
            out_specs=per_seq(t_new, NSA_WIDTH),
            scratch_shapes=[pltpu.VMEM((KV_COLS, past_len + LANES), f32), pltpu.VMEM((KV_COLS, n_win + LANES), f32),
                            pltpu.VMEM((tp, NSA_WIDTH), f32), pltpu.VMEM((tp, GATE_PAD), f32),
                            pltpu.VMEM((tp, NSA_WIDTH), f32), pltpu.SemaphoreType.DMA((past_len // S_KTILE,))]),
        compiler_params=_cparams(("arbitrary",)),
        name="nsa_sample",
    )(page_table, p_q, p_gate, kvc_cmp, cache_sel_t, kvs_new_t, win_state_t, kvw_new_t)


def _prep_weights(l, depth, ln_g, ln_b, ffn_w_gate, ffn_w_up, ffn_w_down, w_in, b_in, rw_mu, rw_w0, rw_w2, rw_a0,
                  rw_a2, rw_g2, rw_k_k, rw_k_a, rw_r_k, rw_ln_w, rw_ln_b, nsa_phi_pe, nsa_phi_w1, nsa_phi_w2,
                  w_out_a, w_out_b, w_o):
    d = D_MODEL
    nc = D_FF // FF_CHUNK
    up = lambda w: w.astype(bf16).reshape(d, nc, FF_CHUNK).transpose(1, 0, 2)
    down = lambda w: w.astype(bf16).reshape(nc, FF_CHUNK, d)
    c1 = RW_SHIFT
    c2 = c1 + NSA_WIDTH
    c3 = c2 + 3 * KV_COLS
    c4 = c3 + N_GATE
    cols = lambda a: jnp.concatenate(
        [a[..., :c3], a[..., c4:], a[..., c3:c4], jnp.zeros(a.shape[:-1] + (GATE_PAD - N_GATE,), a.dtype)], axis=-1)
    z = lambda r, c: jnp.zeros((r, c), f32)
    w = RW_WIDTH
    lora = jnp.concatenate([
        jnp.concatenate([rw_w2[l], z(W_LORA, w), z(W_LORA, w)], axis=1),
        jnp.concatenate([z(A_LORA, w), rw_a2[l], z(A_LORA, w)], axis=1),
        jnp.concatenate([z(G_LORA, w), z(G_LORA, w), rw_g2[l]], axis=1)], axis=0)
    hid = lax.broadcasted_iota(jnp.int32, (w, w), 0) // RW_HEAD == lax.broadcasted_iota(jnp.int32, (w, w), 1) // RW_HEAD
    vec = jnp.stack([rw_w0[l], rw_a0[l], rw_k_k[l], rw_k_a[l], rw_r_k[l].reshape(w), rw_ln_w[l], rw_ln_b[l],
                     jnp.zeros((w,), f32)])
    w1 = nsa_phi_w1[l].reshape(2, 2, CMP_STRIDE, NSA_HEAD, CMP_HID)
    eye = jnp.eye(NSA_KV, dtype=f32)
    big = jnp.einsum("kljde,kK,gG->jkgdKGle", w1, jnp.eye(2, dtype=f32), eye)
    per_kv = jnp.einsum("kljde,gG->kjgdGle", w1, eye)
    pe = jnp.broadcast_to(nsa_phi_pe[l].reshape(2, 1, CMP_LEN * NSA_HEAD), (2, 8, CMP_LEN * NSA_HEAD))
    w_cols, b_cols = cols(w_in[l]), cols(b_in[l])
    seg_t = _segments(_PROJ, [name for name, _ in _PROJ_T])
    w_t = jnp.concatenate([w_cols[:, lo:hi] for lo, hi in seg_t], axis=1).T
    b_t = jnp.concatenate([b_cols[lo:hi] for lo, hi in seg_t])
    return dict(
        alpha=(2 * depth) ** 0.25,
        wg1=up(ffn_w_gate[l, 0]), wu1=up(ffn_w_up[l, 0]), wd1=down(ffn_w_down[l, 0]),
        wg2=up(ffn_w_gate[l, 1]), wu2=up(ffn_w_up[l, 1]), wd2=down(ffn_w_down[l, 1]),
        ln_g=ln_g[l], ln_b=ln_b[l],
        w_in=w_cols.astype(bf16), b_in=b_cols.reshape(1, -1),
        w_in_t=w_t.astype(bf16), b_in_t=b_t[:, None],
        phi_w2t=jnp.swapaxes(nsa_phi_w2[l], 1, 2).astype(bf16),
        rw_mu=rw_mu[l].reshape(1, -1), rw_lora=lora.astype(bf16), rw_vec=vec, rw_hsum=hid.astype(bf16),
        phi_big=big.reshape(CMP_ROW, 2 * NSA_KV * 2 * CMP_HID).astype(bf16), phi_pe=pe,
        phi_kv=per_kv.reshape(2, CMP_STRIDE * NSA_KV_COLS, NSA_KV * 2 * CMP_HID).astype(bf16),
        phi_w1f=nsa_phi_w1[l].reshape(2, CMP_LEN * NSA_HEAD, CMP_HID).astype(bf16),
        phi_w2=nsa_phi_w2[l].astype(bf16),
        w_out_a=w_out_a[l].astype(bf16), w_out_b=w_out_b[l].astype(bf16), w_o=w_o[l].astype(bf16))


def _kv_state(p):
    return p.reshape(p.shape[:-1] + (2, NSA_KV, NSA_HEAD))


def _tokens_last(a):
    n, t = a.shape[:2]
    return a.transpose(0, 2, 3, 4, 1).reshape(n, KV_COLS, t)


def _kv_state_t(p_t):
    b, _, t = p_t.shape
    return p_t.reshape(b, 2, NSA_KV, NSA_HEAD, t).transpose(0, 4, 1, 2, 3)


def kernel(x_prompt, x_sample, c_prompt, c_sample, cache_kv_cmp, cache_kv_sel, state_kv_win, state_wkv, state_shift, page_table, w_ada, b_ada, ln_g, ln_b, ffn_w_gate, ffn_w_up, ffn_w_down, w_in, b_in, rw_mu, rw_w0, rw_w2, rw_a0, rw_a2, rw_g2, rw_k_k, rw_k_a, rw_r_k, rw_ln_w, rw_ln_b, nsa_phi_pe, nsa_phi_w1, nsa_phi_w2, w_out_a, w_out_b, w_o):
    bp, seq, d = x_prompt.shape
    bd, t_new, _ = x_sample.shape
    depth = w_ada.shape[0]
    n_phys, page = cache_kv_cmp.shape[1:3]
    n_pages = page_table.shape[1]
    n_win = state_kv_win.shape[2]
    rows_s = bd * t_new
    tm_p = min(TM_PROJ, seq)
    tm_ffn = min(TM_FFN, seq)
    assert seq % tm_p == 0 and seq % tm_ffn == 0 and seq % CMP_STRIDE == 0 and page % CMP_STRIDE == 0

    y_p, y_s = x_prompt, x_sample.reshape(1, rows_s, d)
    st_p, st_s = [], []
    for l in range(depth):
        wts = _prep_weights(l, depth, ln_g, ln_b, ffn_w_gate, ffn_w_up, ffn_w_down, w_in, b_in, rw_mu, rw_w0,
                            rw_w2, rw_a0, rw_a2, rw_g2, rw_k_k, rw_k_a, rw_r_k, rw_ln_w, rw_ln_b, nsa_phi_pe,
                            nsa_phi_w1, nsa_phi_w2, w_out_a, w_out_b, w_o)
        mod = _ada_mod(jnp.concatenate([c_prompt, c_sample], axis=0), w_ada[l], b_ada[l])
        mod_p = mod[:bp].reshape(bp, 9, 1, d)
        mod_s = jnp.repeat(mod[bp:].reshape(bd, 9, d), t_new, axis=0).reshape(rows_s, 9, d)
        mod_s = mod_s.transpose(1, 0, 2).reshape(1, 9, rows_s, d)

        x1 = _ffn_block(y_p, mod_p, wts, tm_ffn, 0)
        p_rw, p_kc, p_ks, p_kw, p_m, q_t, g_t, ks_t, kw_t = _in_proj(
            x1, mod_p, wts, tm_p, ("rw", "kc", "ks", "kw", "merge"), ("q", "gate", "ks", "kw"))
        ya, wkv_p = _rwkv(p_rw, jnp.zeros((bp, RW_SHIFT), f32), jnp.zeros((bp, RW_HEADS, RW_HEAD, RW_HEAD), f32),
                          wts)
        kvc_cmp, kvc_cmp_t = _compress_dense(p_kc.reshape(bp, seq // CMP_STRIDE, CMP_ROW), wts)
        yb_t = _nsa_prompt(q_t, g_t, kvc_cmp, kvc_cmp_t, p_ks, ks_t, p_kw, kw_t)
        y_p = _ffn_block(_mix(x1, ya, yb_t, p_m, mod_p, wts, tm_p, True), mod_p, wts, tm_ffn, 1)
        n_keep = min(WINDOW, seq)
        st_p.append((_kv_state(p_kc), _kv_state_t(ks_t), _kv_state_t(kw_t[:, :, seq - n_keep:]), wkv_p,
                     p_rw[:, -1]))

        x1 = _ffn_block(y_s, mod_s, wts, rows_s, 0)
        p_rw, p_q, p_kc, p_ks, p_m, p_g, ks_t, kw_t = _in_proj(
            x1, mod_s, wts, rows_s, ("rw", "q", "kc", "ks", "merge", "gate"), ("ks", "kw"))
        per_seq = lambda a: a.reshape(bd, t_new, a.shape[-1])
        p_rw, p_q, p_kc, p_ks, p_g = map(per_seq, (p_rw, p_q, p_kc, p_ks, p_g))
        ya, wkv_s = _rwkv(p_rw, state_shift[l], state_wkv[l], wts)
        kvc_cmp = _compress_paged(_tokens_last(cache_kv_cmp[l]), page_table, wts)
        win_state_t = _tokens_last(state_kv_win[l])
        yb = _nsa_sample(p_q, p_g, kvc_cmp, _tokens_last(cache_kv_sel[l]), page_table, ks_t[0], win_state_t,
                         kw_t[0])
        x2 = _mix(x1, ya.reshape(1, rows_s, RW_WIDTH), yb.reshape(1, rows_s, NSA_WIDTH), p_m, mod_s, wts, rows_s,
                  False)
        y_s = _ffn_block(x2, mod_s, wts, rows_s, 1)
        kw_new_t = kw_t[0].reshape(KV_COLS, bd, t_new).transpose(1, 0, 2)
        new_win_t = jnp.concatenate([win_state_t, kw_new_t], axis=2)[:, :, t_new:]
        st_s.append((_kv_state(p_kc), _kv_state(p_ks), _kv_state_t(new_win_t), wkv_s, p_rw[:, -1]))

    stack = lambda sts, i: jnp.stack([s[i] for s in sts])
    return (y_p, y_s.reshape(bd, t_new, d)) + tuple(stack(st_p, i) for i in range(5)) + tuple(
        stack(st_s, i) for i in range(5))
```

```python
import functools
import math

import jax
import jax.numpy as jnp
from jax import lax
from jax.experimental import pallas as pl
from jax.experimental.pallas import tpu as pltpu

f32 = jnp.float32
bf16 = jnp.bfloat16

D_MODEL = 1024
RW_HEAD = 64
RW_HEADS = 8
RW_WIDTH = RW_HEADS * RW_HEAD
W_LORA = 64
A_LORA = 64
G_LORA = 128
LORA_IN = W_LORA + A_LORA + G_LORA
RW_GN_EPS = 64e-5
RW_SHIFT = 3 * RW_WIDTH + LORA_IN
NSA_HEAD = 64
NSA_HEADS = 8
NSA_KV = 2
NSA_HPG = NSA_HEADS // NSA_KV
NSA_WIDTH = NSA_HEADS * NSA_HEAD
NSA_KV_COLS = NSA_KV * NSA_HEAD
KV_COLS = 2 * NSA_KV_COLS
NSA_SCALE = NSA_HEAD ** -0.5
CMP_STRIDE = 16
CMP_LEN = 2 * CMP_STRIDE
CMP_HID = 128
CMP_ROW = CMP_STRIDE * KV_COLS
SEL_BLOCK = 64
SEL_TOPK = 16
WINDOW = 512
D_FF = 2816
N_GATE = 3 * NSA_HEADS
LN_EPS = 1e-5
NEG_INF = -1e30
BIG = 1e9

LANES = 128
GATE_PAD = LANES
FF_CHUNK = 256
TM_FFN = 1024
TM_PROJ = 512
RW_CHUNK = 64
RW_HPB = 4
RW_SEQS = 4
Q_TILE = 256
K_TILE = 256
S_KTILE = 2048
VMEM_LIMIT = 56 * 1024 * 1024

_NT = (((1,), (1,)), ((), ()))
_TN = (((0,), (0,)), ((), ()))
_HI = lax.Precision.HIGHEST


def _dot(a, b):
    return jnp.dot(a.astype(bf16), b.astype(bf16), preferred_element_type=f32)


def _dot_nt(a, b):
    return lax.dot_general(a.astype(bf16), b.astype(bf16), _NT, preferred_element_type=f32)


def _dot_f32(a, b, dims=None):
    if dims is None:
        return jnp.dot(a, b, preferred_element_type=f32, precision=_HI)
    return lax.dot_general(a, b, dims, preferred_element_type=f32, precision=_HI)


def _dot_sel(x, m01):
    hi = x.astype(bf16)
    r1 = x - hi.astype(f32)
    mid = r1.astype(bf16)
    lo = (r1 - mid.astype(f32)).astype(bf16)
    d = lambda t: jnp.dot(t, m01, preferred_element_type=f32)
    return d(hi) + d(mid) + d(lo)


def _layer_norm(x, g, b):
    mu = jnp.mean(x, axis=-1, keepdims=True)
    xc = x - mu
    var = jnp.mean(xc * xc, axis=-1, keepdims=True)
    return xc * lax.rsqrt(var + LN_EPS) * g + b


def _cparams(sem):
    return pltpu.CompilerParams(dimension_semantics=sem, vmem_limit_bytes=VMEM_LIMIT)


def _resident(shape):
    nd = len(shape)
    return pl.BlockSpec(shape, lambda *_: (0,) * nd, pipeline_mode=pl.Buffered(1))


def _ada_kernel(c_ref, w_ref, b_ref, o_ref):
    c = c_ref[...]
    o_ref[...] = _dot(c * jax.nn.sigmoid(c), w_ref[...]) + b_ref[...]


def _ada_mod(c, w_ada, b_ada):
    rows, d = c.shape
    n = w_ada.shape[1]
    tn = d
    return pl.pallas_call(
        _ada_kernel,
        out_shape=jax.ShapeDtypeStruct((rows, n), f32),
        grid=(n // tn,),
        in_specs=[pl.BlockSpec((rows, d), lambda j: (0, 0)),
                  pl.BlockSpec((d, tn), lambda j: (0, j)),
                  pl.BlockSpec((1, tn), lambda j: (0, j))],
        out_specs=pl.BlockSpec((rows, tn), lambda j: (0, j)),
        compiler_params=_cparams(("parallel",)),
        name="ada_mod",
    )(c, w_ada, b_ada.reshape(1, n))


def _ffn(u, wg_ref, wu_ref, wd_ref, acc_ref):
    ub = u.astype(bf16)
    acc_ref[...] = jnp.zeros(acc_ref.shape, f32)

    def body(c, carry):
        hg = jnp.dot(ub, wg_ref[c], preferred_element_type=f32)
        hu = jnp.dot(ub, wu_ref[c], preferred_element_type=f32)
        h = hg * jax.nn.sigmoid(hg) * hu
        acc_ref[...] += jnp.dot(h.astype(bf16), wd_ref[c], preferred_element_type=f32)
        return carry

    lax.fori_loop(0, wg_ref.shape[0], body, 0)
    return acc_ref[...]


def _ffn_block_kernel(alpha, half, x_ref, mod_ref, wg_ref, wu_ref, wd_ref, lng_ref, lnb_ref, o_ref, acc_ref):
    m0 = 6 * half
    ln = 2 * half
    x = x_ref[...]
    u = x * (1.0 + mod_ref[m0 + 1]) + mod_ref[m0]
    f = _ffn(u, wg_ref, wu_ref, wd_ref, acc_ref)
    o_ref[...] = _layer_norm(alpha * x + 0.5 * (1.0 + mod_ref[m0 + 2]) * f, lng_ref[ln:ln + 1, :],
                             lnb_ref[ln:ln + 1, :])


def _ffn_block(x, mod, wts, tm, half):
    g, t, d = x.shape
    r = mod.shape[2]
    row = pl.BlockSpec((None, tm, d), lambda i, j: (i, j, 0))
    names = ("wg1", "wu1", "wd1") if half == 0 else ("wg2", "wu2", "wd2")
    weights = [wts[n] for n in names] + [wts["ln_g"], wts["ln_b"]]
    return pl.pallas_call(
        functools.partial(_ffn_block_kernel, wts["alpha"], half),
        out_shape=jax.ShapeDtypeStruct((g, t, d), f32),
        grid=(g, t // tm),
        in_specs=[row, pl.BlockSpec((None, 9, r, d), lambda i, j: (i, 0, 0, 0))] + [_resident(w.shape) for w in weights],
        out_specs=row,
        scratch_shapes=[pltpu.VMEM((tm, d), f32)],
        compiler_params=_cparams(("parallel", "parallel")),
        name="ffn_block",
    )(x, mod, *weights)


def _in_proj_kernel(seg, seg_t, x1_ref, mod_ref, win_ref, bin_ref, wint_ref, bint_ref, *out_refs):
    u2 = (x1_ref[...] * (1.0 + mod_ref[4]) + mod_ref[3]).astype(bf16)
    for o_ref, (lo, hi) in zip(out_refs, seg):
        o_ref[...] = jnp.dot(u2, win_ref[:, lo:hi], preferred_element_type=f32) + bin_ref[:, lo:hi]
    for o_ref, (lo, hi) in zip(out_refs[len(seg):], seg_t):
        o_ref[...] = lax.dot_general(wint_ref[lo:hi, :], u2, _NT, preferred_element_type=f32) + bint_ref[lo:hi, :]


_PROJ = (("rw", RW_SHIFT), ("q", NSA_WIDTH), ("kc", KV_COLS), ("ks", KV_COLS), ("kw", KV_COLS),
         ("merge", 2 * D_MODEL), ("gate", GATE_PAD))
_PROJ_T = (("q", NSA_WIDTH), ("gate", GATE_PAD), ("ks", KV_COLS), ("kw", KV_COLS))


def _segments(table, names):
    seg, lo = {}, 0
    for name, w in table:
        seg[name] = (lo, lo + w)
        lo += w
    return tuple(seg[n] for n in names)


def _in_proj(x1, mod, wts, tm, names, names_t):
    g, t, d = x1.shape
    r = mod.shape[2]
    seg, seg_t = _segments(_PROJ, names), _segments(_PROJ_T, names_t)
    row = lambda w: pl.BlockSpec((None, tm, w), lambda i, j: (i, j, 0))
    col = lambda w: pl.BlockSpec((None, w, tm), lambda i, j: (i, 0, j))
    weights = [wts["w_in"], wts["b_in"], wts["w_in_t"], wts["b_in_t"]]
    return pl.pallas_call(
        functools.partial(_in_proj_kernel, seg, seg_t),
        out_shape=([jax.ShapeDtypeStruct((g, t, hi - lo), f32) for lo, hi in seg]
                   + [jax.ShapeDtypeStruct((g, hi - lo, t), f32) for lo, hi in seg_t]),
        grid=(g, t // tm),
        in_specs=[row(d), pl.BlockSpec((None, 9, r, d), lambda i, j: (i, 0, 0, 0))]
                 + [_resident(w.shape) for w in weights],
        out_specs=[row(hi - lo) for lo, hi in seg] + [col(hi - lo) for lo, hi in seg_t],
        compiler_params=_cparams(("parallel", "parallel")),
        name="in_proj",
    )(x1, mod, *weights)


def _mix_kernel(alpha, yb_on_lanes, x1_ref, ya_ref, yb_ref, pm_ref, mod_ref, woa_ref, wob_ref, wo_ref,
                lng_ref, lnb_ref, o_ref):
    x1 = x1_ref[...]
    pm = pm_ref[...]
    d = x1.shape[-1]
    ga = jax.nn.sigmoid(pm[:, :d])
    gb = jax.nn.sigmoid(pm[:, d:])
    if yb_on_lanes:
        yb_out = lax.dot_general(yb_ref[...].astype(bf16), wob_ref[...], _TN, preferred_element_type=f32)
    else:
        yb_out = _dot(yb_ref[...], wob_ref[...])
    merged = ga * _dot(ya_ref[...], woa_ref[...]) + gb * yb_out
    m = _dot(merged, wo_ref[...])
    o_ref[...] = _layer_norm(alpha * x1 + (1.0 + mod_ref[5]) * m, lng_ref[1:2, :], lnb_ref[1:2, :])


def _mix(x1, ya, yb, pm, mod, wts, tm, yb_on_lanes):
    g, t, d = x1.shape
    r = mod.shape[2]
    row = lambda w: pl.BlockSpec((None, tm, w), lambda i, j: (i, j, 0))
    yb_spec = pl.BlockSpec((None, NSA_WIDTH, tm), lambda i, j: (i, 0, j)) if yb_on_lanes else row(NSA_WIDTH)
    weights = [wts["w_out_a"], wts["w_out_b"], wts["w_o"], wts["ln_g"], wts["ln_b"]]
    return pl.pallas_call(
        functools.partial(_mix_kernel, wts["alpha"], yb_on_lanes),
        out_shape=jax.ShapeDtypeStruct((g, t, d), f32),
        grid=(g, t // tm),
        in_specs=[row(d), row(ya.shape[-1]), yb_spec, row(pm.shape[-1]),
                  pl.BlockSpec((None, 9, r, d), lambda i, j: (i, 0, 0, 0))] + [_resident(w.shape) for w in weights],
        out_specs=row(d),
        compiler_params=_cparams(("parallel", "parallel")),
        name="mix",
    )(x1, ya, yb, pm, mod, *weights)


def _rwkv_kernel(t_blk, p_ref, prev_ref, s0_ref, mu_ref, wl_ref, vec_ref, hsum_ref,
                 y_ref, sout_ref, state_ref, prevrow_ref, stage_ref, yn_ref):
    n_seq = p_ref.shape[0]
    ci = pl.program_id(1)

    @pl.when(ci == 0)
    def _():
        state_ref[...] = jnp.zeros(state_ref.shape, f32)
        for s in range(n_seq):
            for h in range(RW_HEADS):
                d0 = (h % RW_HPB) * RW_HEAD
                state_ref[s, h // RW_HPB, d0:d0 + RW_HEAD, d0:d0 + RW_HEAD] = s0_ref[s, h]
        prevrow_ref[...] = prev_ref[...]

    c = stage_ref.shape[1]
    gw = RW_HPB * RW_HEAD
    rows = RW_HPB * c
    br = lax.broadcasted_iota(jnp.int32, (rows, gw), 0)
    bc = lax.broadcasted_iota(jnp.int32, (rows, gw), 1)
    same_head = (br // c) == (bc // RW_HEAD)
    tr = lax.broadcasted_iota(jnp.int32, (rows, rows), 0) % c
    tc = lax.broadcasted_iota(jnp.int32, (rows, rows), 1) % c
    strict = tr > tc
    incl = tr >= tc
    blocks = lambda x: jnp.where(same_head, jnp.concatenate([x] * RW_HPB, axis=0), 0.0).astype(bf16)
    nt = lambda x, y: lax.dot_general(x, y, _NT, preferred_element_type=f32)
    tn = lambda x, y: lax.dot_general(x, y, _TN, preferred_element_type=f32)
    mm = lambda x, y: jnp.dot(x, y, preferred_element_type=f32)
    n_fac = max(1, (c - 1).bit_length())

    pre = [_rwkv_features(t_blk, p_ref.at[s], mu_ref, wl_ref, vec_ref, hsum_ref, prevrow_ref.at[s],
                          stage_ref.at[s]) for s in range(n_seq)]
    chains = [(s, grp) for s in range(n_seq) for grp in range(RW_HEADS // RW_HPB)]
    ops, low, u = {}, {}, {}
    for s, grp in chains:
        gl = slice(grp * gw, (grp + 1) * gw)
        ops[s, grp] = [blocks(pre[s][name][:, gl]) for name in ("at", "rt", "bt", "kt", "bh", "kh", "v")]
    for ch in chains:
        at4, rt4, bt4, kt4, bh4, kh4, v4 = ops[ch]
        s_b = state_ref[ch].astype(bf16)
        low[ch] = jnp.where(strict, nt(at4, bt4), 0.0).astype(bf16)
        u[ch] = nt(at4, s_b) + mm(jnp.where(strict, nt(at4, kt4), 0.0).astype(bf16), v4)
    for i in range(n_fac):
        for ch in chains:
            u[ch] = u[ch] + mm(low[ch], u[ch].astype(bf16))
        if i + 1 < n_fac:
            for ch in chains:
                low[ch] = mm(low[ch], low[ch]).astype(bf16)
    for ch in chains:
        s, grp = ch
        gl = slice(grp * gw, (grp + 1) * gw)
        at4, rt4, bt4, kt4, bh4, kh4, v4 = ops[ch]
        s_bd = state_ref[ch]
        u_b = u[ch].astype(bf16)
        y4 = (nt(rt4, s_bd.astype(bf16)) + mm(jnp.where(incl, nt(rt4, bt4), 0.0).astype(bf16), u_b)
              + mm(jnp.where(incl, nt(rt4, kt4), 0.0).astype(bf16), v4))
        state_ref[ch] = s_bd * pre[s]["p_last"][:, gl] + tn(u_b, bh4) + tn(v4, kh4)
        yg = y4[0:c]
        for hh in range(1, RW_HPB):
            yg = yg + y4[hh * c:(hh + 1) * c]
        yn_ref[s, :, gl] = yg

    hsum = hsum_ref[...]
    ln_w, ln_b = vec_ref[5:6, :], vec_ref[6:7, :]
    for s in range(n_seq):
        yh = yn_ref[s]
        mean = _dot_sel(yh, hsum) * (1.0 / RW_HEAD)
        yc = yh - mean
        var = _dot_sel(yc * yc, hsum) * (1.0 / RW_HEAD)
        y = (yc * lax.rsqrt(var + RW_GN_EPS) * ln_w + ln_b + pre[s]["bonus"]) * pre[s]["g"]
        y_ref[s] = y[0:t_blk, :] if t_blk < c else y

    @pl.when(ci == pl.num_programs(1) - 1)
    def _():
        for s in range(n_seq):
            for h in range(RW_HEADS):
                d0 = (h % RW_HPB) * RW_HEAD
                sout_ref[s, h] = state_ref[s, h // RW_HPB, d0:d0 + RW_HEAD, d0:d0 + RW_HEAD]


def _rwkv_features(t_blk, p_ref, mu_ref, wl_ref, vec_ref, hsum_ref, prevrow_ref, stage_ref):
    c = stage_ref.shape[0]
    w = RW_WIDTH
    if t_blk < c:
        stage_ref[...] = jnp.zeros(stage_ref.shape, f32)
        stage_ref[0:t_blk, :] = p_ref[...]
        p = stage_ref[...]
    else:
        p = p_ref[...]
    ridx = lax.broadcasted_iota(jnp.int32, (c, 1), 0)
    valid = ridx < t_blk
    p_prev = jnp.where(ridx == 0, prevrow_ref[...], pltpu.roll(p, 1, 0))
    prevrow_ref[...] = p[c - 1:c, :]
    xs = p + (p_prev - p) * mu_ref[...]

    tail = xs[:, 3 * w:]
    li = lax.broadcasted_iota(jnp.int32, tail.shape, 1)
    act = jnp.where(li < W_LORA, jnp.tanh(tail),
                    jnp.where(li < W_LORA + A_LORA, tail, jax.nn.sigmoid(tail)))
    lora = _dot(act, wl_ref[...])
    w0, a0, k_k, k_a, r_k = (vec_ref[i:i + 1, :] for i in range(5))
    z = -(w0 + lora[:, :w])
    softplus = jnp.maximum(z, 0.0) + jnp.log1p(jnp.exp(-jnp.abs(z)))
    lw = -jnp.exp(-softplus - 0.5)
    a = jax.nn.sigmoid(a0 + lora[:, w:2 * w])
    g = lora[:, 2 * w:]
    r, k, v = xs[:, :w], xs[:, w:2 * w], xs[:, 2 * w:3 * w]
    hsum = hsum_ref[...]
    kk = k * k_k
    kk = kk * lax.rsqrt(jnp.maximum(_dot_sel(kk * kk, hsum), 1e-24))
    k2 = k * (1.0 + (a - 1.0) * k_a)
    bonus = _dot_sel(r * k2 * r_k, hsum) * v
    if t_blk < c:
        lw = jnp.where(valid, lw, 0.0)
        kk = jnp.where(valid, kk, 0.0)
        k2 = jnp.where(valid, k2, 0.0)
        v = jnp.where(valid, v, 0.0)
    b = kk * a

    rr = lax.broadcasted_iota(jnp.int32, (c, c), 0)
    cc = lax.broadcasted_iota(jnp.int32, (c, c), 1)
    cum = _dot_f32(jnp.where(rr >= cc, 1.0, 0.0).astype(f32), lw)
    cum_last = cum[c - 1:c, :]
    e_neg = jnp.exp(-cum)
    e_rem = jnp.exp(cum_last - cum)
    at = -kk * jnp.exp(cum - lw)
    rt = r * jnp.exp(cum)
    bt, kt = b * e_neg, k2 * e_neg
    bh, kh = b * e_rem, k2 * e_rem
    p_last = jnp.exp(cum_last)
    return dict(at=at, rt=rt, bt=bt, kt=kt, bh=bh, kh=kh, v=v, p_last=p_last, bonus=bonus, g=g)


def _rwkv(p_rw, prev, s0, wts):
    g, t, ws = p_rw.shape
    c = RW_CHUNK
    t_blk = min(c, t)
    sb = RW_SEQS
    assert t % t_blk == 0 and g % sb == 0
    gw = RW_HPB * RW_HEAD
    return pl.pallas_call(
        functools.partial(_rwkv_kernel, t_blk),
        out_shape=[jax.ShapeDtypeStruct((g, t, RW_WIDTH), f32), jax.ShapeDtypeStruct(s0.shape, f32)],
        grid=(g // sb, t // t_blk),
        in_specs=[pl.BlockSpec((sb, t_blk, ws), lambda i, j: (i, j, 0)),
                  pl.BlockSpec((sb, 1, ws), lambda i, j: (i, 0, 0)),
                  pl.BlockSpec((sb,) + s0.shape[1:], lambda i, j: (i, 0, 0, 0)),
                  _resident(wts["rw_mu"].shape), _resident(wts["rw_lora"].shape),
                  _resident(wts["rw_vec"].shape), _resident(wts["rw_hsum"].shape)],
        out_specs=[pl.BlockSpec((sb, t_blk, RW_WIDTH), lambda i, j: (i, j, 0)),
                   pl.BlockSpec((sb,) + s0.shape[1:], lambda i, j: (i, 0, 0, 0))],
        scratch_shapes=[pltpu.VMEM((sb, RW_HEADS // RW_HPB, gw, gw), f32), pltpu.VMEM((sb, 1, ws), f32),
                        pltpu.VMEM((sb, c, ws), f32), pltpu.VMEM((sb, c, RW_WIDTH), f32)],
        compiler_params=_cparams(("parallel", "arbitrary")),
        name="rwkv",
    )(p_rw, prev.reshape(g, 1, ws), s0, wts["rw_mu"], wts["rw_lora"], wts["rw_vec"], wts["rw_hsum"])


def _gelu_tanh(x):
    return x * (0.5 * (1.0 + jnp.tanh(math.sqrt(2.0 / math.pi) * (x + 0.044715 * (x * x * x)))))


def _compress_rows(hids, pe_ref, w1f_ref, w2_ref, o_ref, w2t_ref=None, ot_ref=None):
    n = hids[0].shape[0]
    for kv in range(2):
        pe_term = _dot(pe_ref[kv], w1f_ref[kv])[0:1, :]
        for grp in range(NSA_KV):
            base = grp * 2 * CMP_HID
            lo = hids[kv][:, base:base + CMP_HID]
            hi = pltpu.roll(hids[kv][:, base + CMP_HID:base + 2 * CMP_HID], n - 1, 0)
            act = _gelu_tanh(lo + hi + pe_term).astype(bf16)
            col = (kv * NSA_KV + grp) * NSA_HEAD
            o_ref[:, col:col + NSA_HEAD] = jnp.dot(act, w2_ref[kv], preferred_element_type=f32)
            if ot_ref is not None:
                ot_ref[col:col + NSA_HEAD, :] = lax.dot_general(w2t_ref[kv], act, _NT, preferred_element_type=f32)


def _compress_dense_kernel(x_ref, wbig_ref, pe_ref, w1f_ref, w2_ref, w2t_ref, o_ref, ot_ref):
    hid = _dot(x_ref[...], wbig_ref[...])
    half = hid.shape[1] // 2
    _compress_rows([hid[:, :half], hid[:, half:]], pe_ref, w1f_ref, w2_ref, o_ref, w2t_ref, ot_ref)


def _compress_dense(x, wts):
    b, n, _ = x.shape
    return pl.pallas_call(
        _compress_dense_kernel,
        out_shape=[jax.ShapeDtypeStruct((b, n, KV_COLS), f32), jax.ShapeDtypeStruct((b, KV_COLS, n), f32)],
        grid=(b,),
        in_specs=[pl.BlockSpec((None, n, CMP_ROW), lambda i: (i, 0, 0)),
                  _resident(wts["phi_big"].shape), _resident(wts["phi_pe"].shape),
                  _resident(wts["phi_w1f"].shape), _resident(wts["phi_w2"].shape),
                  _resident(wts["phi_w2t"].shape)],
        out_specs=[pl.BlockSpec((None, n, KV_COLS), lambda i: (i, 0, 0)),
                   pl.BlockSpec((None, KV_COLS, n), lambda i: (i, 0, 0))],
        compiler_params=_cparams(("parallel",)),
        name="compress_dense",
    )(x, wts["phi_big"], wts["phi_pe"], wts["phi_w1f"], wts["phi_w2"], wts["phi_w2t"])


CMP_GROUP = 2 * LANES
CMP_UNROLL = 4


def _compress_paged_kernel(pt_ref, cache_ref, wkv_ref, pe_ref, w1f_ref, w2_ref, o_ref, pbuf_ref, xbuf_ref, sem):
    b = pl.program_id(0)
    n_pages = pt_ref.shape[1]
    page = cache_ref.shape[2]
    pages_per_group = CMP_GROUP // page
    rows_per_group = CMP_GROUP // CMP_STRIDE

    def page_copy(seq, i):
        return pltpu.make_async_copy(cache_ref.at[pt_ref[seq, i]], pbuf_ref.at[:, pl.ds(i * page, page)],
                                     sem.at[i // pages_per_group])

    def start_all(seq):
        def start(i, carry):
            page_copy(seq, i).start()
            return carry
        lax.fori_loop(0, n_pages, start, 0)

    @pl.when(b == 0)
    def _():
        start_all(0)

    r = lax.broadcasted_iota(jnp.int32, (CMP_GROUP, CMP_GROUP), 0)
    c = lax.broadcasted_iota(jnp.int32, (CMP_GROUP, CMP_GROUP), 1)
    perm = jnp.where(c == CMP_STRIDE * (r % rows_per_group) + r // rows_per_group, 1.0, 0.0).astype(bf16)

    def regroup(step, carry):
        for pg in range(CMP_UNROLL * pages_per_group):
            page_copy(b, step * CMP_UNROLL * pages_per_group + pg).wait()
        for k in range(CMP_UNROLL):
            gi = step * CMP_UNROLL + k
            t0 = pl.multiple_of(gi * CMP_GROUP, CMP_GROUP)
            tokens = pbuf_ref[:, pl.ds(t0, CMP_GROUP)].astype(bf16)
            rows = lax.dot_general(perm, tokens, _NT, preferred_element_type=f32)
            r0 = pl.multiple_of(gi * rows_per_group, rows_per_group)
            for j in range(CMP_STRIDE):
                for kv in range(2):
                    col = (kv * CMP_STRIDE + j) * NSA_KV_COLS
                    xbuf_ref[pl.ds(r0, rows_per_group), col:col + NSA_KV_COLS] = (
                        rows[j * rows_per_group:(j + 1) * rows_per_group,
                             kv * NSA_KV_COLS:(kv + 1) * NSA_KV_COLS].astype(bf16))
        return carry

    n_groups = n_pages // pages_per_group
    lax.fori_loop(0, n_groups // CMP_UNROLL, regroup, 0)

    @pl.when(b + 1 < pl.num_programs(0))
    def _():
        start_all(b + 1)

    half = CMP_STRIDE * NSA_KV_COLS
    hids = [jnp.dot(xbuf_ref[:, kv * half:(kv + 1) * half], wkv_ref[kv], preferred_element_type=f32)
            for kv in range(2)]
    _compress_rows(hids, pe_ref, w1f_ref, w2_ref, o_ref)


def _compress_paged(cache_t, page_table, wts):
    b, n_pages = page_table.shape
    page = cache_t.shape[2]
    n = n_pages * page // CMP_STRIDE
    assert CMP_GROUP % page == 0 and (n_pages * page) % (CMP_GROUP * CMP_UNROLL) == 0
    full = lambda a: pl.BlockSpec(a.shape, lambda i, pt: (0,) * a.ndim, pipeline_mode=pl.Buffered(1))
    return pl.pallas_call(
        _compress_paged_kernel,
        out_shape=jax.ShapeDtypeStruct((b, n, KV_COLS), f32),
        grid_spec=pltpu.PrefetchScalarGridSpec(
            num_scalar_prefetch=1, grid=(b,),
            in_specs=[pl.BlockSpec(memory_space=pl.ANY), full(wts["phi_kv"]), full(wts["phi_pe"]),
                      full(wts["phi_w1f"]), full(wts["phi_w2"])],
            out_specs=pl.BlockSpec((None, n, KV_COLS), lambda i, pt: (i, 0, 0)),
            scratch_shapes=[pltpu.VMEM((KV_COLS, n_pages * page), f32), pltpu.VMEM((n, CMP_ROW), bf16),
                            pltpu.SemaphoreType.DMA((n_pages * page // CMP_GROUP,))]),
        compiler_params=_cparams(("arbitrary",)),
        name="compress_paged",
    )(page_table, cache_t, wts["phi_kv"], wts["phi_pe"], wts["phi_w1f"], wts["phi_w2"])


def _softmax_rows(s, mask):
    s = jnp.where(mask, s, NEG_INF)
    e = jnp.where(mask, jnp.exp(s - jnp.max(s, axis=-1, keepdims=True)), 0.0)
    return e / jnp.maximum(jnp.sum(e, axis=-1, keepdims=True), 1e-30)


def _flash_step(q, k, v, mask, m, l, acc):
    s = jnp.where(mask, _dot_nt(q, k), NEG_INF)
    m_new = jnp.maximum(m, jnp.max(s, axis=-1, keepdims=True))
    alpha = jnp.exp(m - m_new)
    e = jnp.where(mask, jnp.exp(s - m_new), 0.0)
    return m_new, alpha * l + jnp.sum(e, axis=-1, keepdims=True), alpha * acc + _dot(e, v)


def _flash_init(rows):
    return (jnp.full((rows, 1), NEG_INF, f32), jnp.zeros((rows, 1), f32), jnp.zeros((rows, NSA_HEAD), f32))


def _flash_out(m, l, acc):
    return acc / jnp.maximum(l, 1e-30)


def _importance_matrix(n_cmp_rows, n_sel_cols):
    ratio = SEL_BLOCK // CMP_STRIDE
    i = lax.broadcasted_iota(jnp.int32, (n_cmp_rows, n_sel_cols), 0)
    j = lax.broadcasted_iota(jnp.int32, (n_cmp_rows, n_sel_cols), 1)
    return jnp.where((i >= ratio * j - 1) & (i <= ratio * j + ratio - 1), 1.0, 0.0).astype(bf16)


def _block_scores(imp, j, cur):
    forced = (j == 0) | (j == cur) | (j == cur - 1)
    return jnp.where(j <= cur, jnp.where(forced, BIG, imp), -BIG)


def _stack_heads(x, grp, rows):
    return jnp.concatenate(
        [x[:, (grp * NSA_HPG + h) * NSA_HEAD:(grp * NSA_HPG + h + 1) * NSA_HEAD] for h in range(NSA_HPG)], axis=0)


def _stack_gates(gates, grp, branch):
    return jnp.concatenate(
        [gates[:, 3 * (grp * NSA_HPG + h) + branch:3 * (grp * NSA_HPG + h) + branch + 1] for h in range(NSA_HPG)],
        axis=0)


def _scores_t(qts, ks, biases):
    return tuple(jnp.dot(k.astype(bf16), qt, preferred_element_type=f32) + bias
                 for qt, k, bias in zip(qts, ks, biases))


def _flash_update_t(ss, vts, carry):
    n = len(ss)
    ms, ls, accs = carry[0::3], carry[1::3], carry[2::3]
    m_new = [jnp.maximum(ms[i], jnp.max(ss[i], axis=0, keepdims=True)) for i in range(n)]
    es = [jnp.exp2(ss[i] - m_new[i]) for i in range(n)]
    pvs = [jnp.dot(vts[i].astype(bf16), es[i].astype(bf16), preferred_element_type=f32) for i in range(n)]
    out = ()
    for i in range(n):
        alpha = jnp.exp2(ms[i] - m_new[i])
        out += (m_new[i], alpha * ls[i] + jnp.sum(es[i], axis=0, keepdims=True), alpha * accs[i] + pvs[i])
    return out


M_FLOOR = 0.1 * NEG_INF
LOG2_E = math.log2(math.e)


def _flash_init_t(cols):
    return (jnp.full((1, cols), M_FLOOR, f32), jnp.zeros((1, cols), f32), jnp.zeros((NSA_HEAD, cols), f32))


def _lanes(x, n):
    return jnp.concatenate([x] * n, axis=1)


def _nsa_prompt_kernel(qt_ref, gt_ref, kc_ref, vct_ref, ks_ref, vst_ref, kw_ref, vwt_ref, o_ref):
    tq = qt_ref.shape[1]
    n_cmp = kc_ref.shape[0]
    n_sel = ks_ref.shape[0] // SEL_BLOCK
    i = pl.program_id(1)
    t0 = i * tq
    cols = NSA_HPG * tq
    gates = jax.nn.sigmoid(gt_ref[...])
    q_pos = t0 + lax.broadcasted_iota(jnp.int32, (1, tq), 1)
    q_pos_c = _lanes(q_pos, NSA_HPG)
    cmp_end = lax.broadcasted_iota(jnp.int32, (n_cmp, 1), 0) * CMP_STRIDE + (CMP_LEN - 1)
    ratio = SEL_BLOCK // CMP_STRIDE
    ij = lax.broadcasted_iota(jnp.int32, (n_sel, n_cmp), 0)
    ii = lax.broadcasted_iota(jnp.int32, (n_sel, n_cmp), 1)
    imp_mat = jnp.where((ii >= ratio * ij - 1) & (ii <= ratio * ij + ratio - 1), 1.0, 0.0).astype(bf16)
    jt = lax.broadcasted_iota(jnp.int32, (n_sel, tq), 0)
    cur_t = (t0 + lax.broadcasted_iota(jnp.int32, (n_sel, tq), 1)) // SEL_BLOCK
    krow = lax.broadcasted_iota(jnp.int32, (K_TILE, 1), 0)
    er = lax.broadcasted_iota(jnp.int32, (K_TILE, n_sel), 0) // SEL_BLOCK
    ej = lax.broadcasted_iota(jnp.int32, (K_TILE, n_sel), 1)
    groups = range(NSA_KV)
    heads_of = lambda grp: range(grp * NSA_HPG, (grp + 1) * NSA_HPG)
    qts, sels, o_cs = [], [], []

    for grp in groups:
        d0 = grp * NSA_HEAD
        heads = heads_of(grp)
        q_all = jnp.concatenate([qt_ref[h * NSA_HEAD:(h + 1) * NSA_HEAD, :] for h in heads], axis=1)
        qt = (q_all * NSA_SCALE).astype(bf16)
        mask_c = cmp_end <= q_pos_c
        s = jnp.where(mask_c, jnp.dot(kc_ref[:, d0:d0 + NSA_HEAD].astype(bf16), qt, preferred_element_type=f32),
                      NEG_INF)
        e = jnp.where(mask_c, jnp.exp(s - jnp.max(s, axis=0, keepdims=True)), 0.0)
        p_c = e / jnp.maximum(jnp.sum(e, axis=0, keepdims=True), 1e-30)
        o_c = jnp.dot(vct_ref[d0:d0 + NSA_HEAD, :].astype(bf16), p_c.astype(bf16), preferred_element_type=f32)
        p_sum = p_c[:, 0:tq]
        for h in range(1, NSA_HPG):
            p_sum = p_sum + p_c[:, h * tq:(h + 1) * tq]
        hi = p_sum.astype(bf16)
        r1 = p_sum - hi.astype(f32)
        mid = r1.astype(bf16)
        lo = (r1 - mid.astype(f32)).astype(bf16)
        imp = sum(jnp.dot(imp_mat, part, preferred_element_type=f32) for part in (hi, mid, lo))
        score = _block_scores(imp, jt, cur_t)
        rank = jnp.zeros((n_sel, tq), f32)
        for j2 in range(n_sel):
            row = score[j2:j2 + 1, :]
            ge = jnp.where(row >= score, 1.0, 0.0)
            gt = jnp.where(row > score, 1.0, 0.0)
            rank = rank + jnp.where(jt > j2, ge, gt)
        qts.append((q_all * (NSA_SCALE * LOG2_E)).astype(bf16))
        sels.append(jnp.where(rank < SEL_TOPK, 1.0, 0.0).astype(bf16))
        o_cs.append(o_c)

    last = (t0 + tq - 1) // K_TILE
    key_cols = lambda ref, k0: [ref[pl.ds(k0, K_TILE), g * NSA_HEAD:(g + 1) * NSA_HEAD] for g in groups]
    val_rows = lambda ref, k0: [ref[g * NSA_HEAD:(g + 1) * NSA_HEAD, pl.ds(k0, K_TILE)] for g in groups]

    def sel_step(kt, carry):
        k0 = pl.multiple_of(kt * K_TILE, K_TILE)
        expand = jnp.where(ej == kt * (K_TILE // SEL_BLOCK) + er, 1.0, 0.0).astype(bf16)
        causal = (k0 + krow) <= q_pos
        biases = []
        for grp in groups:
            chosen = jnp.dot(expand, sels[grp], preferred_element_type=f32)
            biases.append(_lanes(jnp.where(causal & (chosen > 0.5), 0.0, NEG_INF), NSA_HPG))
        return _flash_update_t(_scores_t(qts, key_cols(ks_ref, k0), biases), val_rows(vst_ref, k0), carry)

    sel_acc = lax.fori_loop(0, last + 1, sel_step, _flash_init_t(cols) * NSA_KV)

    def win_step(kt, carry):
        k0 = pl.multiple_of(kt * K_TILE, K_TILE)
        dist = q_pos - (k0 + krow)
        bias = _lanes(jnp.where((dist >= 0) & (dist < WINDOW), 0.0, NEG_INF), NSA_HPG)
        return _flash_update_t(_scores_t(qts, key_cols(kw_ref, k0), [bias] * NSA_KV), val_rows(vwt_ref, k0), carry)

    win_acc = lax.fori_loop(jnp.maximum(t0 - WINDOW, 0) // K_TILE, last + 1, win_step,
                            _flash_init_t(cols) * NSA_KV)

    for grp in groups:
        heads = heads_of(grp)
        o_s = _flash_out(*sel_acc[3 * grp:3 * grp + 3])
        o_w = _flash_out(*win_acc[3 * grp:3 * grp + 3])
        gate = lambda br: jnp.concatenate([gates[3 * h + br:3 * h + br + 1, :] for h in heads], axis=1)
        o = gate(0) * o_cs[grp] + gate(1) * o_s + gate(2) * o_w
        for n, h in enumerate(heads):
            o_ref[h * NSA_HEAD:(h + 1) * NSA_HEAD, :] = o[:, n * tq:(n + 1) * tq]


def _nsa_prompt(qt, gt, kvc, kvct, kvs, kvst, kvw, kvwt):
    b, _, t = qt.shape
    n_cmp = kvc.shape[1]
    tq = Q_TILE
    assert t % tq == 0 and t % K_TILE == 0 and K_TILE % SEL_BLOCK == 0
    keys =lambda r: pl.BlockSpec((None, r, NSA_KV_COLS), lambda i, j: (i, 0, 0))
    vals = lambda c: pl.BlockSpec((None, NSA_KV_COLS, c), lambda i, j: (i, 1, 0))
    return pl.pallas_call(
        _nsa_prompt_kernel,
        out_shape=jax.ShapeDtypeStruct((b, NSA_WIDTH, t), f32),
        grid=(b, t // tq),
        in_specs=[pl.BlockSpec((None, NSA_WIDTH, tq), lambda i, j: (i, 0, j)),
                  pl.BlockSpec((None, GATE_PAD, tq), lambda i, j: (i, 0, j)),
                  keys(n_cmp), vals(n_cmp), keys(t), vals(t), keys(t), vals(t)],
        out_specs=pl.BlockSpec((None, NSA_WIDTH, tq), lambda i, j: (i, 0, j)),
        compiler_params=_cparams(("parallel", "arbitrary")),
        name="nsa_prompt",
    )(qt, gt, kvc, kvct, kvs, kvst, kvw, kvwt)


def _nsa_sample_kernel(past_len, pt_ref, q_ref, gate_ref, kvc_ref, cache_ref, kvs_new_ref, win_ref, kvw_new_ref,
                       o_ref, selbufs_ref, winbuf_ref, qbuf_ref, gbuf_ref, obuf_ref, sem):
    b = pl.program_id(0)
    t_new = q_ref.shape[0]
    tp = qbuf_ref.shape[0]
    n_pages = pt_ref.shape[1]
    page = cache_ref.shape[2]
    n_cmp = kvc_ref.shape[0]
    n_win = win_ref.shape[1]
    total = past_len + t_new
    n_sel = -(-total // SEL_BLOCK)
    n_sel_pad = -(-n_sel // LANES) * LANES
    tile_pages = S_KTILE // page
    tile_blocks = S_KTILE // SEL_BLOCK
    rows = NSA_HPG * tp

    slot = b % 2
    selbuf_ref = selbufs_ref.at[slot]

    def page_copy(seq, i, into):
        return pltpu.make_async_copy(cache_ref.at[pt_ref[seq, i]], selbufs_ref.at[into, :, pl.ds(i * page, page)],
                                     sem.at[into, i // tile_pages])

    def start_all(seq, into):
        def start(i, carry):
            page_copy(seq, i, into).start()
            return carry
        lax.fori_loop(0, n_pages, start, 0)

    @pl.when(b == 0)
    def _():
        start_all(0, 0)

    @pl.when(b + 1 < pl.num_programs(0))
    def _():
        start_all(b + 1, 1 - slot)

    src = lax.broadcasted_iota(jnp.int32, (kvs_new_ref.shape[1], LANES), 0)
    dst = lax.broadcasted_iota(jnp.int32, (kvs_new_ref.shape[1], LANES), 1)
    pick = jnp.where((src == b * t_new + dst) & (dst < t_new), 1.0, 0.0).astype(bf16)
    selbuf_ref[:, past_len:past_len + LANES] = _dot(kvs_new_ref[...], pick)
    winbuf_ref[:, 0:n_win] = win_ref[...]
    winbuf_ref[:, n_win:n_win + LANES] = _dot(kvw_new_ref[...], pick)
    qbuf_ref[...] = jnp.zeros(qbuf_ref.shape, f32)
    qbuf_ref[0:t_new, :] = q_ref[...]
    gbuf_ref[...] = jnp.zeros(gbuf_ref.shape, f32)
    gbuf_ref[0:t_new, :] = gate_ref[...]

    q = qbuf_ref[...] * NSA_SCALE
    gates = jax.nn.sigmoid(gbuf_ref[...])
    q_pos = past_len + lax.broadcasted_iota(jnp.int32, (tp, 1), 0)
    q_pos_r = jnp.concatenate([q_pos] * NSA_HPG, axis=0)
    cmp_end = lax.broadcasted_iota(jnp.int32, (1, n_cmp), 1) * CMP_STRIDE + (CMP_LEN - 1)
    imp_mat = _importance_matrix(n_cmp, n_sel_pad)
    jj = lax.broadcasted_iota(jnp.int32, (tp, n_sel_pad), 1)
    expand = jnp.where(lax.broadcasted_iota(jnp.int32, (tile_blocks, S_KTILE), 0)
                       == lax.broadcasted_iota(jnp.int32, (tile_blocks, S_KTILE), 1) // SEL_BLOCK,
                       1.0, 0.0).astype(bf16)

    def flash(qs, k_t, v_t, mask, m, l, acc):
        s = jnp.where(mask, jnp.dot(qs, k_t.astype(bf16), preferred_element_type=f32), NEG_INF)
        m_new = jnp.maximum(m, jnp.max(s, axis=-1, keepdims=True))
        alpha = jnp.exp(m - m_new)
        e = jnp.where(mask, jnp.exp(s - m_new), 0.0)
        pv = lax.dot_general(e.astype(bf16), v_t.astype(bf16), _NT, preferred_element_type=f32)
        return m_new, alpha * l + jnp.sum(e, axis=-1, keepdims=True), alpha * acc + pv

    qss, sels, o_cs = [], [], []
    for grp in range(NSA_KV):
        kcol0, vcol0 = grp * NSA_HEAD, NSA_KV_COLS + grp * NSA_HEAD
        qs = _stack_heads(q, grp, tp).astype(bf16)
        p_c = _softmax_rows(_dot_nt(qs, kvc_ref[:, kcol0:kcol0 + NSA_HEAD]), cmp_end <= q_pos_r)
        o_cs.append(_dot(p_c, kvc_ref[:, vcol0:vcol0 + NSA_HEAD]))
        p_sum = p_c[0:tp]
        for h in range(1, NSA_HPG):
            p_sum = p_sum + p_c[h * tp:(h + 1) * tp]
        score = _block_scores(_dot_sel(p_sum, imp_mat), jj, q_pos // SEL_BLOCK)
        rank = jnp.zeros((tp, n_sel_pad), f32)
        for j2 in range(n_sel):
            col = score[:, j2:j2 + 1]
            ge = jnp.where(col >= score, 1.0, 0.0)
            gt = jnp.where(col > score, 1.0, 0.0)
            rank = rank + jnp.where(jj > j2, ge, gt)
        sel = jnp.where((rank < SEL_TOPK) & (jj < n_sel), 1.0, 0.0).astype(bf16)
        sels.append(jnp.concatenate([sel] * NSA_HPG, axis=0))
        qss.append(qs)

    def wait(i, carry):
        page_copy(b, i, slot).wait()
        return carry

    lax.fori_loop(0, n_pages, wait, 0)
    n_tiles = past_len // S_KTILE
    tiles = [(kt * S_KTILE, S_KTILE if kt < n_tiles else LANES) for kt in range(n_tiles + 1)]
    probs = [(grp, kt) for kt in range(n_tiles + 1) for grp in range(NSA_KV)]
    masks, ss = {}, {}
    for grp, kt in probs:
        k0, width = tiles[kt]
        k_pos = k0 + lax.broadcasted_iota(jnp.int32, (1, width), 1)
        blocks = sels[grp][:, kt * tile_blocks:(kt + 1) * tile_blocks]
        chosen = jnp.dot(blocks, expand[:, 0:width], preferred_element_type=f32)
        masks[grp, kt] = (chosen > 0.5) & (k_pos <= q_pos_r)
        k_t = selbuf_ref[grp * NSA_HEAD:(grp + 1) * NSA_HEAD, k0:k0 + width]
        ss[grp, kt] = jnp.where(masks[grp, kt], jnp.dot(qss[grp], k_t.astype(bf16), preferred_element_type=f32),
                                NEG_INF)
    row_max = [functools.reduce(jnp.maximum, [jnp.max(ss[grp, kt], axis=-1, keepdims=True)
                                               for kt in range(n_tiles + 1)]) for grp in range(NSA_KV)]
    es = {p: jnp.where(masks[p], jnp.exp(ss[p] - row_max[p[0]]), 0.0) for p in probs}
    pvs = {}
    for grp, kt in probs:
        k0, width = tiles[kt]
        v_t = selbuf_ref[NSA_KV_COLS + grp * NSA_HEAD:NSA_KV_COLS + (grp + 1) * NSA_HEAD, k0:k0 + width]
        pvs[grp, kt] = lax.dot_general(es[grp, kt].astype(bf16), v_t.astype(bf16), _NT, preferred_element_type=f32)

    w_pos = past_len - n_win + lax.broadcasted_iota(jnp.int32, (1, winbuf_ref.shape[1]), 1)
    dist = jnp.where(w_pos >= 0, q_pos_r - w_pos, -1)
    for grp in range(NSA_KV):
        krow0, vrow0 = grp * NSA_HEAD, NSA_KV_COLS + grp * NSA_HEAD
        denom = sum(jnp.sum(es[grp, kt], axis=-1, keepdims=True) for kt in range(n_tiles + 1))
        o_s = sum(pvs[grp, kt] for kt in range(n_tiles + 1)) / jnp.maximum(denom, 1e-30)
        o_w = _flash_out(*flash(qss[grp], winbuf_ref[krow0:krow0 + NSA_HEAD, :], winbuf_ref[vrow0:vrow0 + NSA_HEAD, :],
                                (dist >= 0) & (dist < WINDOW), *_flash_init(rows)))
        o = (_stack_gates(gates, grp, 0) * o_cs[grp] + _stack_gates(gates, grp, 1) * o_s
             + _stack_gates(gates, grp, 2) * o_w)
        for h in range(NSA_HPG):
            col = (grp * NSA_HPG + h) * NSA_HEAD
            obuf_ref[:, col:col + NSA_HEAD] = o[h * tp:(h + 1) * tp]
    o_ref[...] = obuf_ref[0:t_new, :]


def _nsa_sample(p_q, p_gate, kvc_cmp, cache_sel_t, page_table, kvs_new_t, win_state_t, kvw_new_t):
    b, t_new, _ = p_q.shape
    n_pages = page_table.shape[1]
    page = cache_sel_t.shape[2]
    past_len = n_pages * page
    n_cmp = kvc_cmp.shape[1]
    n_win = win_state_t.shape[2]
    tp = 8
    assert t_new <= tp and past_len % S_KTILE == 0 and S_KTILE % page == 0 and n_win % LANES == 0
    assert b * t_new == kvs_new_t.shape[1]
    n_sel_pad = -(-(-(-(past_len + t_new) // SEL_BLOCK)) // LANES) * LANES
    assert (past_len // S_KTILE + 1) * (S_KTILE // SEL_BLOCK) <= n_sel_pad
    per_seq = lambda n, w: pl.BlockSpec((None, n, w), lambda i, pt: (i, 0, 0))
    whole = lambda a: pl.BlockSpec(a.shape, lambda i, pt: (0,) * a.ndim, pipeline_mode=pl.Buffered(1))
    return pl.pallas_call(
        functools.partial(_nsa_sample_kernel, past_len),
        out_shape=jax.ShapeDtypeStruct((b, t_new, NSA_WIDTH), f32),
        grid_spec=pltpu.PrefetchScalarGridSpec(
            num_scalar_prefetch=1, grid=(b,),
            in_specs=[per_seq(t_new, NSA_WIDTH), per_seq(t_new, GATE_PAD), per_seq(n_cmp, KV_COLS),
                      pl.BlockSpec(memory_space=pl.ANY), whole(kvs_new_t), per_seq(KV_COLS, n_win),
                      whole(kvw_new_t)],
            out_specs=per_seq(t_new, NSA_WIDTH),
            scratch_shapes=[pltpu.VMEM((2, KV_COLS, past_len + LANES), f32), pltpu.VMEM((KV_COLS, n_win + LANES), f32),
                            pltpu.VMEM((tp, NSA_WIDTH), f32), pltpu.VMEM((tp, GATE_PAD), f32),
                            pltpu.VMEM((tp, NSA_WIDTH), f32), pltpu.SemaphoreType.DMA((2, past_len // S_KTILE))]),
        compiler_params=_cparams(("arbitrary",)),
        name="nsa_sample",
    )(page_table, p_q, p_gate, kvc_cmp, cache_sel_t, kvs_new_t, win_state_t, kvw_new_t)


def _prep_weights(l, depth, ln_g, ln_b, ffn_w_gate, ffn_w_up, ffn_w_down, w_in, b_in, rw_mu, rw_w0, rw_w2, rw_a0,
                  rw_a2, rw_g2, rw_k_k, rw_k_a, rw_r_k, rw_ln_w, rw_ln_b, nsa_phi_pe, nsa_phi_w1, nsa_phi_w2,
                  w_out_a, w_out_b, w_o):
    d = D_MODEL
    nc = D_FF // FF_CHUNK
    up = lambda w: w.astype(bf16).reshape(d, nc, FF_CHUNK).transpose(1, 0, 2)
    down = lambda w: w.astype(bf16).reshape(nc, FF_CHUNK, d)
    c1 = RW_SHIFT
    c2 = c1 + NSA_WIDTH
    c3 = c2 + 3 * KV_COLS
    c4 = c3 + N_GATE
    cols = lambda a: jnp.concatenate(
        [a[..., :c3], a[..., c4:], a[..., c3:c4], jnp.zeros(a.shape[:-1] + (GATE_PAD - N_GATE,), a.dtype)], axis=-1)
    z = lambda r, c: jnp.zeros((r, c), f32)
    w = RW_WIDTH
    lora = jnp.concatenate([
        jnp.concatenate([rw_w2[l], z(W_LORA, w), z(W_LORA, w)], axis=1),
        jnp.concatenate([z(A_LORA, w), rw_a2[l], z(A_LORA, w)], axis=1),
        jnp.concatenate([z(G_LORA, w), z(G_LORA, w), rw_g2[l]], axis=1)], axis=0)
    hid = lax.broadcasted_iota(jnp.int32, (w, w), 0) // RW_HEAD == lax.broadcasted_iota(jnp.int32, (w, w), 1) // RW_HEAD
    vec = jnp.stack([rw_w0[l], rw_a0[l], rw_k_k[l], rw_k_a[l], rw_r_k[l].reshape(w), rw_ln_w[l], rw_ln_b[l],
                     jnp.zeros((w,), f32)])
    w1 = nsa_phi_w1[l].reshape(2, 2, CMP_STRIDE, NSA_HEAD, CMP_HID)
    eye = jnp.eye(NSA_KV, dtype=f32)
    big = jnp.einsum("kljde,kK,gG->jkgdKGle", w1, jnp.eye(2, dtype=f32), eye)
    per_kv = jnp.einsum("kljde,gG->kjgdGle", w1, eye)
    pe = jnp.broadcast_to(nsa_phi_pe[l].reshape(2, 1, CMP_LEN * NSA_HEAD), (2, 8, CMP_LEN * NSA_HEAD))
    w_cols, b_cols = cols(w_in[l]), cols(b_in[l])
    seg_t = _segments(_PROJ, [name for name, _ in _PROJ_T])
    w_t = jnp.concatenate([w_cols[:, lo:hi] for lo, hi in seg_t], axis=1).T
    b_t = jnp.concatenate([b_cols[lo:hi] for lo, hi in seg_t])
    return dict(
        alpha=(2 * depth) ** 0.25,
        wg1=up(ffn_w_gate[l, 0]), wu1=up(ffn_w_up[l, 0]), wd1=down(ffn_w_down[l, 0]),
        wg2=up(ffn_w_gate[l, 1]), wu2=up(ffn_w_up[l, 1]), wd2=down(ffn_w_down[l, 1]),
        ln_g=ln_g[l], ln_b=ln_b[l],
        w_in=w_cols.astype(bf16), b_in=b_cols.reshape(1, -1),
        w_in_t=w_t.astype(bf16), b_in_t=b_t[:, None],
        phi_w2t=jnp.swapaxes(nsa_phi_w2[l], 1, 2).astype(bf16),
        rw_mu=rw_mu[l].reshape(1, -1), rw_lora=lora.astype(bf16), rw_vec=vec, rw_hsum=hid.astype(bf16),
        phi_big=big.reshape(CMP_ROW, 2 * NSA_KV * 2 * CMP_HID).astype(bf16), phi_pe=pe,
        phi_kv=per_kv.reshape(2, CMP_STRIDE * NSA_KV_COLS, NSA_KV * 2 * CMP_HID).astype(bf16),
        phi_w1f=nsa_phi_w1[l].reshape(2, CMP_LEN * NSA_HEAD, CMP_HID).astype(bf16),
        phi_w2=nsa_phi_w2[l].astype(bf16),
        w_out_a=w_out_a[l].astype(bf16), w_out_b=w_out_b[l].astype(bf16), w_o=w_o[l].astype(bf16))


def _kv_state(p):
    return p.reshape(p.shape[:-1] + (2, NSA_KV, NSA_HEAD))


def _tokens_last(a):
    n, t = a.shape[:2]
    return a.transpose(0, 2, 3, 4, 1).reshape(n, KV_COLS, t)


def _kv_state_t(p_t):
    b, _, t = p_t.shape
    return p_t.reshape(b, 2, NSA_KV, NSA_HEAD, t).transpose(0, 4, 1, 2, 3)


def kernel(x_prompt, x_sample, c_prompt, c_sample, cache_kv_cmp, cache_kv_sel, state_kv_win, state_wkv, state_shift, page_table, w_ada, b_ada, ln_g, ln_b, ffn_w_gate, ffn_w_up, ffn_w_down, w_in, b_in, rw_mu, rw_w0, rw_w2, rw_a0, rw_a2, rw_g2, rw_k_k, rw_k_a, rw_r_k, rw_ln_w, rw_ln_b, nsa_phi_pe, nsa_phi_w1, nsa_phi_w2, w_out_a, w_out_b, w_o):
    bp, seq, d = x_prompt.shape
    bd, t_new, _ = x_sample.shape
    depth = w_ada.shape[0]
    n_phys, page = cache_kv_cmp.shape[1:3]
    n_pages = page_table.shape[1]
    n_win = state_kv_win.shape[2]
    rows_s = bd * t_new
    tm_p = min(TM_PROJ, seq)
    tm_ffn = min(TM_FFN, seq)
    assert seq % tm_p == 0 and seq % tm_ffn == 0 and seq % CMP_STRIDE == 0 and page % CMP_STRIDE == 0

    y_p, y_s = x_prompt, x_sample.reshape(1, rows_s, d)
    st_p, st_s = [], []
    for l in range(depth):
        wts = _prep_weights(l, depth, ln_g, ln_b, ffn_w_gate, ffn_w_up, ffn_w_down, w_in, b_in, rw_mu, rw_w0,
                            rw_w2, rw_a0, rw_a2, rw_g2, rw_k_k, rw_k_a, rw_r_k, rw_ln_w, rw_ln_b, nsa_phi_pe,
                            nsa_phi_w1, nsa_phi_w2, w_out_a, w_out_b, w_o)
        mod = _ada_mod(jnp.concatenate([c_prompt, c_sample], axis=0), w_ada[l], b_ada[l])
        mod_p = mod[:bp].reshape(bp, 9, 1, d)
        mod_s = jnp.repeat(mod[bp:].reshape(bd, 9, d), t_new, axis=0).reshape(rows_s, 9, d)
        mod_s = mod_s.transpose(1, 0, 2).reshape(1, 9, rows_s, d)

        x1 = _ffn_block(y_p, mod_p, wts, tm_ffn, 0)
        p_rw, p_kc, p_ks, p_kw, p_m, q_t, g_t, ks_t, kw_t = _in_proj(
            x1, mod_p, wts, tm_p, ("rw", "kc", "ks", "kw", "merge"), ("q", "gate", "ks", "kw"))
        ya, wkv_p = _rwkv(p_rw, jnp.zeros((bp, RW_SHIFT), f32), jnp.zeros((bp, RW_HEADS, RW_HEAD, RW_HEAD), f32),
                          wts)
        kvc_cmp, kvc_cmp_t = _compress_dense(p_kc.reshape(bp, seq // CMP_STRIDE, CMP_ROW), wts)
        yb_t = _nsa_prompt(q_t, g_t, kvc_cmp, kvc_cmp_t, p_ks, ks_t, p_kw, kw_t)
        y_p = _ffn_block(_mix(x1, ya, yb_t, p_m, mod_p, wts, tm_p, True), mod_p, wts, tm_ffn, 1)
        n_keep = min(WINDOW, seq)
        st_p.append((_kv_state(p_kc), _kv_state_t(ks_t), _kv_state_t(kw_t[:, :, seq - n_keep:]), wkv_p,
                     p_rw[:, -1]))

        x1 = _ffn_block(y_s, mod_s, wts, rows_s, 0)
        p_rw, p_q, p_kc, p_ks, p_m, p_g, ks_t, kw_t = _in_proj(
            x1, mod_s, wts, rows_s, ("rw", "q", "kc", "ks", "merge", "gate"), ("ks", "kw"))
        per_seq = lambda a: a.reshape(bd, t_new, a.shape[-1])
        p_rw, p_q, p_kc, p_ks, p_g = map(per_seq, (p_rw, p_q, p_kc, p_ks, p_g))
        ya, wkv_s = _rwkv(p_rw, state_shift[l], state_wkv[l], wts)
        kvc_cmp = _compress_paged(_tokens_last(cache_kv_cmp[l]), page_table, wts)
        win_state_t = _tokens_last(state_kv_win[l])
        yb = _nsa_sample(p_q, p_g, kvc_cmp, _tokens_last(cache_kv_sel[l]), page_table, ks_t[0], win_state_t,
                         kw_t[0])
        x2 = _mix(x1, ya.reshape(1, rows_s, RW_WIDTH), yb.reshape(1, rows_s, NSA_WIDTH), p_m, mod_s, wts, rows_s,
                  False)
        y_s = _ffn_block(x2, mod_s, wts, rows_s, 1)
        kw_new_t = kw_t[0].reshape(KV_COLS, bd, t_new).transpose(1, 0, 2)
        new_win_t = jnp.concatenate([win_state_t, kw_new_t], axis=2)[:, :, t_new:]
        st_s.append((_kv_state(p_kc), _kv_state(p_ks), _kv_state_t(new_win_t), wkv_s, p_rw[:, -1]))

    stack = lambda sts, i: jnp.stack([s[i] for s in sts])
    return (y_p, y_s.reshape(bd, t_new, d)) + tuple(stack(st_p, i) for i in range(5)) + tuple(
        stack(st_s, i) for i in range(5))
```

```python
import functools
import math

import jax
import jax.numpy as jnp
from jax import lax
from jax.experimental import pallas as pl
from jax.experimental.pallas import tpu as pltpu

f32 = jnp.float32
bf16 = jnp.bfloat16

D_MODEL = 1024
RW_HEAD = 64
RW_HEADS = 8
RW_WIDTH = RW_HEADS * RW_HEAD
W_LORA = 64
A_LORA = 64
G_LORA = 128
LORA_IN = W_LORA + A_LORA + G_LORA
RW_GN_EPS = 64e-5
RW_SHIFT = 3 * RW_WIDTH + LORA_IN
NSA_HEAD = 64
NSA_HEADS = 8
NSA_KV = 2
NSA_HPG = NSA_HEADS // NSA_KV
NSA_WIDTH = NSA_HEADS * NSA_HEAD
NSA_KV_COLS = NSA_KV * NSA_HEAD
KV_COLS = 2 * NSA_KV_COLS
NSA_SCALE = NSA_HEAD ** -0.5
CMP_STRIDE = 16
CMP_LEN = 2 * CMP_STRIDE
CMP_HID = 128
CMP_ROW = CMP_STRIDE * KV_COLS
SEL_BLOCK = 64
SEL_TOPK = 16
WINDOW = 512
D_FF = 2816
N_GATE = 3 * NSA_HEADS
LN_EPS = 1e-5
NEG_INF = -1e30
BIG = 1e9

LANES = 128
GATE_PAD = LANES
FF_CHUNK = 256
TM_FFN = 1024
TM_PROJ = 512
RW_CHUNK = 64
RW_HPB = 4
RW_SEQS = 4
Q_TILE = 256
K_TILE = 256
S_KTILE = 2048
VMEM_LIMIT = 56 * 1024 * 1024

_NT = (((1,), (1,)), ((), ()))
_TN = (((0,), (0,)), ((), ()))
_HI = lax.Precision.HIGHEST


def _dot(a, b):
    return jnp.dot(a.astype(bf16), b.astype(bf16), preferred_element_type=f32)


def _dot_nt(a, b):
    return lax.dot_general(a.astype(bf16), b.astype(bf16), _NT, preferred_element_type=f32)


def _dot_f32(a, b, dims=None):
    if dims is None:
        return jnp.dot(a, b, preferred_element_type=f32, precision=_HI)
    return lax.dot_general(a, b, dims, preferred_element_type=f32, precision=_HI)


def _dot_sel(x, m01):
    hi = x.astype(bf16)
    r1 = x - hi.astype(f32)
    mid = r1.astype(bf16)
    lo = (r1 - mid.astype(f32)).astype(bf16)
    d = lambda t: jnp.dot(t, m01, preferred_element_type=f32)
    return d(hi) + d(mid) + d(lo)


def _layer_norm(x, g, b):
    mu = jnp.mean(x, axis=-1, keepdims=True)
    xc = x - mu
    var = jnp.mean(xc * xc, axis=-1, keepdims=True)
    return xc * lax.rsqrt(var + LN_EPS) * g + b


def _cparams(sem):
    return pltpu.CompilerParams(dimension_semantics=sem, vmem_limit_bytes=VMEM_LIMIT)


def _resident(shape):
    nd = len(shape)
    return pl.BlockSpec(shape, lambda *_: (0,) * nd, pipeline_mode=pl.Buffered(1))


def _ada_kernel(c_ref, w_ref, b_ref, o_ref):
    c = c_ref[...]
    o_ref[...] = _dot(c * jax.nn.sigmoid(c), w_ref[...]) + b_ref[...]


def _ada_mod(c, w_ada, b_ada):
    rows, d = c.shape
    n = w_ada.shape[1]
    tn = d
    return pl.pallas_call(
        _ada_kernel,
        out_shape=jax.ShapeDtypeStruct((rows, n), f32),
        grid=(n // tn,),
        in_specs=[pl.BlockSpec((rows, d), lambda j: (0, 0)),
                  pl.BlockSpec((d, tn), lambda j: (0, j)),
                  pl.BlockSpec((1, tn), lambda j: (0, j))],
        out_specs=pl.BlockSpec((rows, tn), lambda j: (0, j)),
        compiler_params=_cparams(("parallel",)),
        name="ada_mod",
    )(c, w_ada, b_ada.reshape(1, n))


def _ffn(u, wg_ref, wu_ref, wd_ref, acc_ref):
    ub = u.astype(bf16)
    acc_ref[...] = jnp.zeros(acc_ref.shape, f32)

    def body(c, carry):
        hg = jnp.dot(ub, wg_ref[c], preferred_element_type=f32)
        hu = jnp.dot(ub, wu_ref[c], preferred_element_type=f32)
        h = hg * jax.nn.sigmoid(hg) * hu
        acc_ref[...] += jnp.dot(h.astype(bf16), wd_ref[c], preferred_element_type=f32)
        return carry

    lax.fori_loop(0, wg_ref.shape[0], body, 0)
    return acc_ref[...]


def _ffn_block_kernel(alpha, half, x_ref, mod_ref, wg_ref, wu_ref, wd_ref, lng_ref, lnb_ref, o_ref, acc_ref):
    m0 = 6 * half
    ln = 2 * half
    x = x_ref[...]
    u = x * (1.0 + mod_ref[m0 + 1]) + mod_ref[m0]
    f = _ffn(u, wg_ref, wu_ref, wd_ref, acc_ref)
    o_ref[...] = _layer_norm(alpha * x + 0.5 * (1.0 + mod_ref[m0 + 2]) * f, lng_ref[ln:ln + 1, :],
                             lnb_ref[ln:ln + 1, :])


def _ffn_block(x, mod, wts, tm, half):
    g, t, d = x.shape
    r = mod.shape[2]
    row = pl.BlockSpec((None, tm, d), lambda i, j: (i, j, 0))
    names = ("wg1", "wu1", "wd1") if half == 0 else ("wg2", "wu2", "wd2")
    weights = [wts[n] for n in names] + [wts["ln_g"], wts["ln_b"]]
    return pl.pallas_call(
        functools.partial(_ffn_block_kernel, wts["alpha"], half),
        out_shape=jax.ShapeDtypeStruct((g, t, d), f32),
        grid=(g, t // tm),
        in_specs=[row, pl.BlockSpec((None, 9, r, d), lambda i, j: (i, 0, 0, 0))] + [_resident(w.shape) for w in weights],
        out_specs=row,
        scratch_shapes=[pltpu.VMEM((tm, d), f32)],
        compiler_params=_cparams(("parallel", "parallel")),
        name="ffn_block",
    )(x, mod, *weights)


def _in_proj_kernel(seg, seg_t, x1_ref, mod_ref, win_ref, bin_ref, wint_ref, bint_ref, *out_refs):
    u2 = (x1_ref[...] * (1.0 + mod_ref[4]) + mod_ref[3]).astype(bf16)
    for o_ref, (lo, hi) in zip(out_refs, seg):
        o_ref[...] = jnp.dot(u2, win_ref[:, lo:hi], preferred_element_type=f32) + bin_ref[:, lo:hi]
    for o_ref, (lo, hi) in zip(out_refs[len(seg):], seg_t):
        o_ref[...] = lax.dot_general(wint_ref[lo:hi, :], u2, _NT, preferred_element_type=f32) + bint_ref[lo:hi, :]


_PROJ = (("rw", RW_SHIFT), ("q", NSA_WIDTH), ("kc", KV_COLS), ("ks", KV_COLS), ("kw", KV_COLS),
         ("merge", 2 * D_MODEL), ("gate", GATE_PAD))
_PROJ_T = (("q", NSA_WIDTH), ("gate", GATE_PAD), ("ks", KV_COLS), ("kw", KV_COLS))


def _segments(table, names):
    seg, lo = {}, 0
    for name, w in table:
        seg[name] = (lo, lo + w)
        lo += w
    return tuple(seg[n] for n in names)


def _in_proj(x1, mod, wts, tm, names, names_t):
    g, t, d = x1.shape
    r = mod.shape[2]
    seg, seg_t = _segments(_PROJ, names), _segments(_PROJ_T, names_t)
    row = lambda w: pl.BlockSpec((None, tm, w), lambda i, j: (i, j, 0))
    col = lambda w: pl.BlockSpec((None, w, tm), lambda i, j: (i, 0, j))
    weights = [wts["w_in"], wts["b_in"], wts["w_in_t"], wts["b_in_t"]]
    return pl.pallas_call(
        functools.partial(_in_proj_kernel, seg, seg_t),
        out_shape=([jax.ShapeDtypeStruct((g, t, hi - lo), f32) for lo, hi in seg]
                   + [jax.ShapeDtypeStruct((g, hi - lo, t), f32) for lo, hi in seg_t]),
        grid=(g, t // tm),
        in_specs=[row(d), pl.BlockSpec((None, 9, r, d), lambda i, j: (i, 0, 0, 0))]
                 + [_resident(w.shape) for w in weights],
        out_specs=[row(hi - lo) for lo, hi in seg] + [col(hi - lo) for lo, hi in seg_t],
        compiler_params=_cparams(("parallel", "parallel")),
        name="in_proj",
    )(x1, mod, *weights)


def _mix_kernel(alpha, yb_on_lanes, x1_ref, ya_ref, yb_ref, pm_ref, mod_ref, woa_ref, wob_ref, wo_ref,
                lng_ref, lnb_ref, o_ref):
    x1 = x1_ref[...]
    pm = pm_ref[...]
    d = x1.shape[-1]
    ga = jax.nn.sigmoid(pm[:, :d])
    gb = jax.nn.sigmoid(pm[:, d:])
    if yb_on_lanes:
        yb_out = lax.dot_general(yb_ref[...].astype(bf16), wob_ref[...], _TN, preferred_element_type=f32)
    else:
        yb_out = _dot(yb_ref[...], wob_ref[...])
    merged = ga * _dot(ya_ref[...], woa_ref[...]) + gb * yb_out
    m = _dot(merged, wo_ref[...])
    o_ref[...] = _layer_norm(alpha * x1 + (1.0 + mod_ref[5]) * m, lng_ref[1:2, :], lnb_ref[1:2, :])


def _mix(x1, ya, yb, pm, mod, wts, tm, yb_on_lanes):
    g, t, d = x1.shape
    r = mod.shape[2]
    row = lambda w: pl.BlockSpec((None, tm, w), lambda i, j: (i, j, 0))
    yb_spec = pl.BlockSpec((None, NSA_WIDTH, tm), lambda i, j: (i, 0, j)) if yb_on_lanes else row(NSA_WIDTH)
    weights = [wts["w_out_a"], wts["w_out_b"], wts["w_o"], wts["ln_g"], wts["ln_b"]]
    return pl.pallas_call(
        functools.partial(_mix_kernel, wts["alpha"], yb_on_lanes),
        out_shape=jax.ShapeDtypeStruct((g, t, d), f32),
        grid=(g, t // tm),
        in_specs=[row(d), row(ya.shape[-1]), yb_spec, row(pm.shape[-1]),
                  pl.BlockSpec((None, 9, r, d), lambda i, j: (i, 0, 0, 0))] + [_resident(w.shape) for w in weights],
        out_specs=row(d),
        compiler_params=_cparams(("parallel", "parallel")),
        name="mix",
    )(x1, ya, yb, pm, mod, *weights)


def _rwkv_kernel(t_blk, p_ref, prev_ref, s0_ref, mu_ref, wl_ref, vec_ref, hsum_ref,
                 y_ref, sout_ref, state_ref, prevrow_ref, stage_ref, yn_ref):
    n_seq = p_ref.shape[0]
    ci = pl.program_id(1)

    @pl.when(ci == 0)
    def _():
        state_ref[...] = jnp.zeros(state_ref.shape, f32)
        for s in range(n_seq):
            for h in range(RW_HEADS):
                d0 = (h % RW_HPB) * RW_HEAD
                state_ref[s, h // RW_HPB, d0:d0 + RW_HEAD, d0:d0 + RW_HEAD] = s0_ref[s, h]
        prevrow_ref[...] = prev_ref[...]

    c = stage_ref.shape[1]
    gw = RW_HPB * RW_HEAD
    rows = RW_HPB * c
    br = lax.broadcasted_iota(jnp.int32, (rows, gw), 0)
    bc = lax.broadcasted_iota(jnp.int32, (rows, gw), 1)
    same_head = (br // c) == (bc // RW_HEAD)
    tr = lax.broadcasted_iota(jnp.int32, (rows, rows), 0) % c
    tc = lax.broadcasted_iota(jnp.int32, (rows, rows), 1) % c
    strict = tr > tc
    incl = tr >= tc
    blocks = lambda x: jnp.where(same_head, jnp.concatenate([x] * RW_HPB, axis=0), 0.0).astype(bf16)
    nt = lambda x, y: lax.dot_general(x, y, _NT, preferred_element_type=f32)
    tn = lambda x, y: lax.dot_general(x, y, _TN, preferred_element_type=f32)
    mm = lambda x, y: jnp.dot(x, y, preferred_element_type=f32)
    n_fac = max(1, (c - 1).bit_length())

    pre = [_rwkv_features(t_blk, p_ref.at[s], mu_ref, wl_ref, vec_ref, hsum_ref, prevrow_ref.at[s],
                          stage_ref.at[s]) for s in range(n_seq)]
    chains = [(s, grp) for s in range(n_seq) for grp in range(RW_HEADS // RW_HPB)]
    ops, low, u = {}, {}, {}
    for s, grp in chains:
        gl = slice(grp * gw, (grp + 1) * gw)
        ops[s, grp] = [blocks(pre[s][name][:, gl]) for name in ("at", "rt", "bt", "kt", "bh", "kh", "v")]
    for ch in chains:
        at4, rt4, bt4, kt4, bh4, kh4, v4 = ops[ch]
        s_b = state_ref[ch].astype(bf16)
        low[ch] = jnp.where(strict, nt(at4, bt4), 0.0).astype(bf16)
        u[ch] = nt(at4, s_b) + mm(jnp.where(strict, nt(at4, kt4), 0.0).astype(bf16), v4)
    for i in range(n_fac):
        for ch in chains:
            u[ch] = u[ch] + mm(low[ch], u[ch].astype(bf16))
        if i + 1 < n_fac:
            for ch in chains:
                low[ch] = mm(low[ch], low[ch]).astype(bf16)
    for ch in chains:
        s, grp = ch
        gl = slice(grp * gw, (grp + 1) * gw)
        at4, rt4, bt4, kt4, bh4, kh4, v4 = ops[ch]
        s_bd = state_ref[ch]
        u_b = u[ch].astype(bf16)
        y4 = (nt(rt4, s_bd.astype(bf16)) + mm(jnp.where(incl, nt(rt4, bt4), 0.0).astype(bf16), u_b)
              + mm(jnp.where(incl, nt(rt4, kt4), 0.0).astype(bf16), v4))
        state_ref[ch] = s_bd * pre[s]["p_last"][:, gl] + tn(u_b, bh4) + tn(v4, kh4)
        yg = y4[0:c]
        for hh in range(1, RW_HPB):
            yg = yg + y4[hh * c:(hh + 1) * c]
        yn_ref[s, :, gl] = yg

    hsum = hsum_ref[...]
    ln_w, ln_b = vec_ref[5:6, :], vec_ref[6:7, :]
    for s in range(n_seq):
        yh = yn_ref[s]
        mean = _dot_sel(yh, hsum) * (1.0 / RW_HEAD)
        yc = yh - mean
        var = _dot_sel(yc * yc, hsum) * (1.0 / RW_HEAD)
        y = (yc * lax.rsqrt(var + RW_GN_EPS) * ln_w + ln_b + pre[s]["bonus"]) * pre[s]["g"]
        y_ref[s] = y[0:t_blk, :] if t_blk < c else y

    @pl.when(ci == pl.num_programs(1) - 1)
    def _():
        for s in range(n_seq):
            for h in range(RW_HEADS):
                d0 = (h % RW_HPB) * RW_HEAD
                sout_ref[s, h] = state_ref[s, h // RW_HPB, d0:d0 + RW_HEAD, d0:d0 + RW_HEAD]


def _rwkv_features(t_blk, p_ref, mu_ref, wl_ref, vec_ref, hsum_ref, prevrow_ref, stage_ref):
    c = stage_ref.shape[0]
    w = RW_WIDTH
    if t_blk < c:
        stage_ref[...] = jnp.zeros(stage_ref.shape, f32)
        stage_ref[0:t_blk, :] = p_ref[...]
        p = stage_ref[...]
    else:
        p = p_ref[...]
    ridx = lax.broadcasted_iota(jnp.int32, (c, 1), 0)
    valid = ridx < t_blk
    p_prev = jnp.where(ridx == 0, prevrow_ref[...], pltpu.roll(p, 1, 0))
    prevrow_ref[...] = p[c - 1:c, :]
    xs = p + (p_prev - p) * mu_ref[...]

    tail = xs[:, 3 * w:]
    li = lax.broadcasted_iota(jnp.int32, tail.shape, 1)
    act = jnp.where(li < W_LORA, jnp.tanh(tail),
                    jnp.where(li < W_LORA + A_LORA, tail, jax.nn.sigmoid(tail)))
    lora = _dot(act, wl_ref[...])
    w0, a0, k_k, k_a, r_k = (vec_ref[i:i + 1, :] for i in range(5))
    z = -(w0 + lora[:, :w])
    softplus = jnp.maximum(z, 0.0) + jnp.log1p(jnp.exp(-jnp.abs(z)))
    lw = -jnp.exp(-softplus - 0.5)
    a = jax.nn.sigmoid(a0 + lora[:, w:2 * w])
    g = lora[:, 2 * w:]
    r, k, v = xs[:, :w], xs[:, w:2 * w], xs[:, 2 * w:3 * w]
    hsum = hsum_ref[...]
    kk = k * k_k
    kk = kk * lax.rsqrt(jnp.maximum(_dot_sel(kk * kk, hsum), 1e-24))
    k2 = k * (1.0 + (a - 1.0) * k_a)
    bonus = _dot_sel(r * k2 * r_k, hsum) * v
    if t_blk < c:
        lw = jnp.where(valid, lw, 0.0)
        kk = jnp.where(valid, kk, 0.0)
        k2 = jnp.where(valid, k2, 0.0)
        v = jnp.where(valid, v, 0.0)
    b = kk * a

    rr = lax.broadcasted_iota(jnp.int32, (c, c), 0)
    cc = lax.broadcasted_iota(jnp.int32, (c, c), 1)
    cum = _dot_f32(jnp.where(rr >= cc, 1.0, 0.0).astype(f32), lw)
    cum_last = cum[c - 1:c, :]
    e_neg = jnp.exp(-cum)
    e_rem = jnp.exp(cum_last - cum)
    at = -kk * jnp.exp(cum - lw)
    rt = r * jnp.exp(cum)
    bt, kt = b * e_neg, k2 * e_neg
    bh, kh = b * e_rem, k2 * e_rem
    p_last = jnp.exp(cum_last)
    return dict(at=at, rt=rt, bt=bt, kt=kt, bh=bh, kh=kh, v=v, p_last=p_last, bonus=bonus, g=g)


def _rwkv(p_rw, prev, s0, wts):
    g, t, ws = p_rw.shape
    c = RW_CHUNK
    t_blk = min(c, t)
    sb = RW_SEQS
    assert t % t_blk == 0 and g % sb == 0
    gw = RW_HPB * RW_HEAD
    return pl.pallas_call(
        functools.partial(_rwkv_kernel, t_blk),
        out_shape=[jax.ShapeDtypeStruct((g, t, RW_WIDTH), f32), jax.ShapeDtypeStruct(s0.shape, f32)],
        grid=(g // sb, t // t_blk),
        in_specs=[pl.BlockSpec((sb, t_blk, ws), lambda i, j: (i, j, 0)),
                  pl.BlockSpec((sb, 1, ws), lambda i, j: (i, 0, 0)),
                  pl.BlockSpec((sb,) + s0.shape[1:], lambda i, j: (i, 0, 0, 0)),
                  _resident(wts["rw_mu"].shape), _resident(wts["rw_lora"].shape),
                  _resident(wts["rw_vec"].shape), _resident(wts["rw_hsum"].shape)],
        out_specs=[pl.BlockSpec((sb, t_blk, RW_WIDTH), lambda i, j: (i, j, 0)),
                   pl.BlockSpec((sb,) + s0.shape[1:], lambda i, j: (i, 0, 0, 0))],
        scratch_shapes=[pltpu.VMEM((sb, RW_HEADS // RW_HPB, gw, gw), f32), pltpu.VMEM((sb, 1, ws), f32),
                        pltpu.VMEM((sb, c, ws), f32), pltpu.VMEM((sb, c, RW_WIDTH), f32)],
        compiler_params=_cparams(("parallel", "arbitrary")),
        name="rwkv",
    )(p_rw, prev.reshape(g, 1, ws), s0, wts["rw_mu"], wts["rw_lora"], wts["rw_vec"], wts["rw_hsum"])


def _gelu_tanh(x):
    return x * (0.5 * (1.0 + jnp.tanh(math.sqrt(2.0 / math.pi) * (x + 0.044715 * (x * x * x)))))


def _compress_rows(hids, pe_ref, w1f_ref, w2_ref, o_ref, w2t_ref=None, ot_ref=None):
    n = hids[0].shape[0]
    for kv in range(2):
        pe_term = _dot(pe_ref[kv], w1f_ref[kv])[0:1, :]
        for grp in range(NSA_KV):
            base = grp * 2 * CMP_HID
            lo = hids[kv][:, base:base + CMP_HID]
            hi = pltpu.roll(hids[kv][:, base + CMP_HID:base + 2 * CMP_HID], n - 1, 0)
            act = _gelu_tanh(lo + hi + pe_term).astype(bf16)
            col = (kv * NSA_KV + grp) * NSA_HEAD
            o_ref[:, col:col + NSA_HEAD] = jnp.dot(act, w2_ref[kv], preferred_element_type=f32)
            if ot_ref is not None:
                ot_ref[col:col + NSA_HEAD, :] = lax.dot_general(w2t_ref[kv], act, _NT, preferred_element_type=f32)


def _compress_dense_kernel(x_ref, wbig_ref, pe_ref, w1f_ref, w2_ref, w2t_ref, o_ref, ot_ref):
    hid = _dot(x_ref[...], wbig_ref[...])
    half = hid.shape[1] // 2
    _compress_rows([hid[:, :half], hid[:, half:]], pe_ref, w1f_ref, w2_ref, o_ref, w2t_ref, ot_ref)


def _compress_dense(x, wts):
    b, n, _ = x.shape
    return pl.pallas_call(
        _compress_dense_kernel,
        out_shape=[jax.ShapeDtypeStruct((b, n, KV_COLS), f32), jax.ShapeDtypeStruct((b, KV_COLS, n), f32)],
        grid=(b,),
        in_specs=[pl.BlockSpec((None, n, CMP_ROW), lambda i: (i, 0, 0)),
                  _resident(wts["phi_big"].shape), _resident(wts["phi_pe"].shape),
                  _resident(wts["phi_w1f"].shape), _resident(wts["phi_w2"].shape),
                  _resident(wts["phi_w2t"].shape)],
        out_specs=[pl.BlockSpec((None, n, KV_COLS), lambda i: (i, 0, 0)),
                   pl.BlockSpec((None, KV_COLS, n), lambda i: (i, 0, 0))],
        compiler_params=_cparams(("parallel",)),
        name="compress_dense",
    )(x, wts["phi_big"], wts["phi_pe"], wts["phi_w1f"], wts["phi_w2"], wts["phi_w2t"])


CMP_GROUP = 2 * LANES
CMP_UNROLL = 4


def _compress_paged_kernel(pt_ref, cache_ref, wkv_ref, pe_ref, w1f_ref, w2_ref, o_ref, pbuf_ref, xbuf_ref, sem):
    b = pl.program_id(0)
    n_pages = pt_ref.shape[1]
    page = cache_ref.shape[2]
    pages_per_group = CMP_GROUP // page
    rows_per_group = CMP_GROUP // CMP_STRIDE

    def page_copy(seq, i):
        return pltpu.make_async_copy(cache_ref.at[pt_ref[seq, i]], pbuf_ref.at[i], sem.at[i // pages_per_group])

    def start_all(seq):
        def start(i, carry):
            page_copy(seq, i).start()
            return carry
        lax.fori_loop(0, n_pages, start, 0)

    @pl.when(b == 0)
    def _():
        start_all(0)

    r = lax.broadcasted_iota(jnp.int32, (CMP_GROUP, CMP_GROUP), 0)
    c = lax.broadcasted_iota(jnp.int32, (CMP_GROUP, CMP_GROUP), 1)
    perm = jnp.where(c == CMP_STRIDE * (r % rows_per_group) + r // rows_per_group, 1.0, 0.0).astype(bf16)

    def regroup(step, carry):
        for pg in range(CMP_UNROLL * pages_per_group):
            page_copy(b, step * CMP_UNROLL * pages_per_group + pg).wait()
        for k in range(CMP_UNROLL):
            gi = step * CMP_UNROLL + k
            tokens = jnp.concatenate([pbuf_ref[gi * pages_per_group + pg] for pg in range(pages_per_group)],
                                     axis=1).astype(bf16)
            rows = lax.dot_general(perm, tokens, _NT, preferred_element_type=f32)
            r0 = pl.multiple_of(gi * rows_per_group, rows_per_group)
            for j in range(CMP_STRIDE):
                for kv in range(2):
                    col = (kv * CMP_STRIDE + j) * NSA_KV_COLS
                    xbuf_ref[pl.ds(r0, rows_per_group), col:col + NSA_KV_COLS] = (
                        rows[j * rows_per_group:(j + 1) * rows_per_group,
                             kv * NSA_KV_COLS:(kv + 1) * NSA_KV_COLS].astype(bf16))
        return carry

    n_groups = n_pages // pages_per_group
    lax.fori_loop(0, n_groups // CMP_UNROLL, regroup, 0)

    @pl.when(b + 1 < pl.num_programs(0))
    def _():
        start_all(b + 1)

    half = CMP_STRIDE * NSA_KV_COLS
    hids = [jnp.dot(xbuf_ref[:, kv * half:(kv + 1) * half], wkv_ref[kv], preferred_element_type=f32)
            for kv in range(2)]
    _compress_rows(hids, pe_ref, w1f_ref, w2_ref, o_ref)


def _compress_paged(cache_t, page_table, wts):
    b, n_pages = page_table.shape
    page = cache_t.shape[2]
    n = n_pages * page // CMP_STRIDE
    assert CMP_GROUP % page == 0 and (n_pages * page) % (CMP_GROUP * CMP_UNROLL) == 0
    full = lambda a: pl.BlockSpec(a.shape, lambda i, pt: (0,) * a.ndim, pipeline_mode=pl.Buffered(1))
    return pl.pallas_call(
        _compress_paged_kernel,
        out_shape=jax.ShapeDtypeStruct((b, n, KV_COLS), f32),
        grid_spec=pltpu.PrefetchScalarGridSpec(
            num_scalar_prefetch=1, grid=(b,),
            in_specs=[pl.BlockSpec(memory_space=pl.ANY), full(wts["phi_kv"]), full(wts["phi_pe"]),
                      full(wts["phi_w1f"]), full(wts["phi_w2"])],
            out_specs=pl.BlockSpec((None, n, KV_COLS), lambda i, pt: (i, 0, 0)),
            scratch_shapes=[pltpu.VMEM((n_pages, KV_COLS, page), f32), pltpu.VMEM((n, CMP_ROW), bf16),
                            pltpu.SemaphoreType.DMA((n_pages * page // CMP_GROUP,))]),
        compiler_params=_cparams(("arbitrary",)),
        name="compress_paged",
    )(page_table, cache_t, wts["phi_kv"], wts["phi_pe"], wts["phi_w1f"], wts["phi_w2"])


def _softmax_rows(s, mask):
    s = jnp.where(mask, s, NEG_INF)
    e = jnp.where(mask, jnp.exp(s - jnp.max(s, axis=-1, keepdims=True)), 0.0)
    return e / jnp.maximum(jnp.sum(e, axis=-1, keepdims=True), 1e-30)


def _flash_step(q, k, v, mask, m, l, acc):
    s = jnp.where(mask, _dot_nt(q, k), NEG_INF)
    m_new = jnp.maximum(m, jnp.max(s, axis=-1, keepdims=True))
    alpha = jnp.exp(m - m_new)
    e = jnp.where(mask, jnp.exp(s - m_new), 0.0)
    return m_new, alpha * l + jnp.sum(e, axis=-1, keepdims=True), alpha * acc + _dot(e, v)


def _flash_init(rows):
    return (jnp.full((rows, 1), NEG_INF, f32), jnp.zeros((rows, 1), f32), jnp.zeros((rows, NSA_HEAD), f32))


def _flash_out(m, l, acc):
    return acc / jnp.maximum(l, 1e-30)


def _importance_matrix(n_cmp_rows, n_sel_cols):
    ratio = SEL_BLOCK // CMP_STRIDE
    i = lax.broadcasted_iota(jnp.int32, (n_cmp_rows, n_sel_cols), 0)
    j = lax.broadcasted_iota(jnp.int32, (n_cmp_rows, n_sel_cols), 1)
    return jnp.where((i >= ratio * j - 1) & (i <= ratio * j + ratio - 1), 1.0, 0.0).astype(bf16)


def _block_scores(imp, j, cur):
    forced = (j == 0) | (j == cur) | (j == cur - 1)
    return jnp.where(j <= cur, jnp.where(forced, BIG, imp), -BIG)


def _stack_heads(x, grp, rows):
    return jnp.concatenate(
        [x[:, (grp * NSA_HPG + h) * NSA_HEAD:(grp * NSA_HPG + h + 1) * NSA_HEAD] for h in range(NSA_HPG)], axis=0)


def _stack_gates(gates, grp, branch):
    return jnp.concatenate(
        [gates[:, 3 * (grp * NSA_HPG + h) + branch:3 * (grp * NSA_HPG + h) + branch + 1] for h in range(NSA_HPG)],
        axis=0)


def _scores_t(qts, ks, biases):
    return tuple(jnp.dot(k.astype(bf16), qt, preferred_element_type=f32) + bias
                 for qt, k, bias in zip(qts, ks, biases))


def _flash_update_t(ss, vts, carry):
    n = len(ss)
    ms, ls, accs = carry[0::3], carry[1::3], carry[2::3]
    m_new = [jnp.maximum(ms[i], jnp.max(ss[i], axis=0, keepdims=True)) for i in range(n)]
    es = [jnp.exp2(ss[i] - m_new[i]) for i in range(n)]
    pvs = [jnp.dot(vts[i].astype(bf16), es[i].astype(bf16), preferred_element_type=f32) for i in range(n)]
    out = ()
    for i in range(n):
        alpha = jnp.exp2(ms[i] - m_new[i])
        out += (m_new[i], alpha * ls[i] + jnp.sum(es[i], axis=0, keepdims=True), alpha * accs[i] + pvs[i])
    return out


M_FLOOR = 0.1 * NEG_INF
LOG2_E = math.log2(math.e)


def _flash_init_t(cols):
    return (jnp.full((1, cols), M_FLOOR, f32), jnp.zeros((1, cols), f32), jnp.zeros((NSA_HEAD, cols), f32))


def _lanes(x, n):
    return jnp.concatenate([x] * n, axis=1)


def _nsa_prompt_kernel(qt_ref, gt_ref, kc_ref, vct_ref, ks_ref, vst_ref, kw_ref, vwt_ref, o_ref):
    tq = qt_ref.shape[1]
    n_cmp = kc_ref.shape[0]
    n_sel = ks_ref.shape[0] // SEL_BLOCK
    i = pl.program_id(1)
    t0 = i * tq
    cols = NSA_HPG * tq
    gates = jax.nn.sigmoid(gt_ref[...])
    q_pos = t0 + lax.broadcasted_iota(jnp.int32, (1, tq), 1)
    q_pos_c = _lanes(q_pos, NSA_HPG)
    cmp_end = lax.broadcasted_iota(jnp.int32, (n_cmp, 1), 0) * CMP_STRIDE + (CMP_LEN - 1)
    ratio = SEL_BLOCK // CMP_STRIDE
    ij = lax.broadcasted_iota(jnp.int32, (n_sel, n_cmp), 0)
    ii = lax.broadcasted_iota(jnp.int32, (n_sel, n_cmp), 1)
    imp_mat = jnp.where((ii >= ratio * ij - 1) & (ii <= ratio * ij + ratio - 1), 1.0, 0.0).astype(bf16)
    jt = lax.broadcasted_iota(jnp.int32, (n_sel, tq), 0)
    cur_t = (t0 + lax.broadcasted_iota(jnp.int32, (n_sel, tq), 1)) // SEL_BLOCK
    krow = lax.broadcasted_iota(jnp.int32, (K_TILE, 1), 0)
    er = lax.broadcasted_iota(jnp.int32, (K_TILE, n_sel), 0) // SEL_BLOCK
    ej = lax.broadcasted_iota(jnp.int32, (K_TILE, n_sel), 1)
    groups = range(NSA_KV)
    heads_of = lambda grp: range(grp * NSA_HPG, (grp + 1) * NSA_HPG)
    qts, sels, o_cs = [], [], []

    for grp in groups:
        d0 = grp * NSA_HEAD
        heads = heads_of(grp)
        q_all = jnp.concatenate([qt_ref[h * NSA_HEAD:(h + 1) * NSA_HEAD, :] for h in heads], axis=1)
        qt = (q_all * NSA_SCALE).astype(bf16)
        mask_c = cmp_end <= q_pos_c
        s = jnp.where(mask_c, jnp.dot(kc_ref[:, d0:d0 + NSA_HEAD].astype(bf16), qt, preferred_element_type=f32),
                      NEG_INF)
        e = jnp.where(mask_c, jnp.exp(s - jnp.max(s, axis=0, keepdims=True)), 0.0)
        p_c = e / jnp.maximum(jnp.sum(e, axis=0, keepdims=True), 1e-30)
        o_c = jnp.dot(vct_ref[d0:d0 + NSA_HEAD, :].astype(bf16), p_c.astype(bf16), preferred_element_type=f32)
        p_sum = p_c[:, 0:tq]
        for h in range(1, NSA_HPG):
            p_sum = p_sum + p_c[:, h * tq:(h + 1) * tq]
        hi = p_sum.astype(bf16)
        r1 = p_sum - hi.astype(f32)
        mid = r1.astype(bf16)
        lo = (r1 - mid.astype(f32)).astype(bf16)
        imp = sum(jnp.dot(imp_mat, part, preferred_element_type=f32) for part in (hi, mid, lo))
        score = _block_scores(imp, jt, cur_t)
        rank = jnp.zeros((n_sel, tq), f32)
        for j2 in range(n_sel):
            row = score[j2:j2 + 1, :]
            ge = jnp.where(row >= score, 1.0, 0.0)
            gt = jnp.where(row > score, 1.0, 0.0)
            rank = rank + jnp.where(jt > j2, ge, gt)
        qts.append((q_all * (NSA_SCALE * LOG2_E)).astype(bf16))
        sels.append(jnp.where(rank < SEL_TOPK, 1.0, 0.0).astype(bf16))
        o_cs.append(o_c)

    last = (t0 + tq - 1) // K_TILE
    key_cols = lambda ref, k0: [ref[pl.ds(k0, K_TILE), g * NSA_HEAD:(g + 1) * NSA_HEAD] for g in groups]
    val_rows = lambda ref, k0: [ref[g * NSA_HEAD:(g + 1) * NSA_HEAD, pl.ds(k0, K_TILE)] for g in groups]

    def sel_step(kt, carry):
        k0 = pl.multiple_of(kt * K_TILE, K_TILE)
        expand = jnp.where(ej == kt * (K_TILE // SEL_BLOCK) + er, 1.0, 0.0).astype(bf16)
        causal = (k0 + krow) <= q_pos
        biases = []
        for grp in groups:
            chosen = jnp.dot(expand, sels[grp], preferred_element_type=f32)
            biases.append(_lanes(jnp.where(causal & (chosen > 0.5), 0.0, NEG_INF), NSA_HPG))
        return _flash_update_t(_scores_t(qts, key_cols(ks_ref, k0), biases), val_rows(vst_ref, k0), carry)

    sel_acc = lax.fori_loop(0, last + 1, sel_step, _flash_init_t(cols) * NSA_KV)

    def win_step(kt, carry):
        k0 = pl.multiple_of(kt * K_TILE, K_TILE)
        dist = q_pos - (k0 + krow)
        bias = _lanes(jnp.where((dist >= 0) & (dist < WINDOW), 0.0, NEG_INF), NSA_HPG)
        return _flash_update_t(_scores_t(qts, key_cols(kw_ref, k0), [bias] * NSA_KV), val_rows(vwt_ref, k0), carry)

    win_acc = lax.fori_loop(jnp.maximum(t0 - WINDOW, 0) // K_TILE, last + 1, win_step,
                            _flash_init_t(cols) * NSA_KV)

    for grp in groups:
        heads = heads_of(grp)
        o_s = _flash_out(*sel_acc[3 * grp:3 * grp + 3])
        o_w = _flash_out(*win_acc[3 * grp:3 * grp + 3])
        gate = lambda br: jnp.concatenate([gates[3 * h + br:3 * h + br + 1, :] for h in heads], axis=1)
        o = gate(0) * o_cs[grp] + gate(1) * o_s + gate(2) * o_w
        for n, h in enumerate(heads):
            o_ref[h * NSA_HEAD:(h + 1) * NSA_HEAD, :] = o[:, n * tq:(n + 1) * tq]


def _nsa_prompt(qt, gt, kvc, kvct, kvs, kvst, kvw, kvwt):
    b, _, t = qt.shape
    n_cmp = kvc.shape[1]
    tq = Q_TILE
    assert t % tq == 0 and t % K_TILE == 0 and K_TILE % SEL_BLOCK == 0
    keys =lambda r: pl.BlockSpec((None, r, NSA_KV_COLS), lambda i, j: (i, 0, 0))
    vals = lambda c: pl.BlockSpec((None, NSA_KV_COLS, c), lambda i, j: (i, 1, 0))
    return pl.pallas_call(
        _nsa_prompt_kernel,
        out_shape=jax.ShapeDtypeStruct((b, NSA_WIDTH, t), f32),
        grid=(b, t // tq),
        in_specs=[pl.BlockSpec((None, NSA_WIDTH, tq), lambda i, j: (i, 0, j)),
                  pl.BlockSpec((None, GATE_PAD, tq), lambda i, j: (i, 0, j)),
                  keys(n_cmp), vals(n_cmp), keys(t), vals(t), keys(t), vals(t)],
        out_specs=pl.BlockSpec((None, NSA_WIDTH, tq), lambda i, j: (i, 0, j)),
        compiler_params=_cparams(("parallel", "arbitrary")),
        name="nsa_prompt",
    )(qt, gt, kvc, kvct, kvs, kvst, kvw, kvwt)


def _nsa_sample_kernel(past_len, pt_ref, q_ref, gate_ref, kvc_ref, cache_ref, kvs_new_ref, win_ref, kvw_new_ref,
                       o_ref, selbufs_ref, winbuf_ref, qbuf_ref, gbuf_ref, obuf_ref, sem):
    b = pl.program_id(0)
    t_new = q_ref.shape[0]
    tp = qbuf_ref.shape[0]
    n_pages = pt_ref.shape[1]
    page = cache_ref.shape[2]
    n_cmp = kvc_ref.shape[0]
    n_win = win_ref.shape[1]
    total = past_len + t_new
    n_sel = -(-total // SEL_BLOCK)
    n_sel_pad = -(-n_sel // LANES) * LANES
    tile_pages = S_KTILE // page
    tile_blocks = S_KTILE // SEL_BLOCK
    rows = NSA_HPG * tp

    slot = b % 2
    selbuf_ref = selbufs_ref.at[slot]

    def page_copy(seq, i, into):
        return pltpu.make_async_copy(cache_ref.at[pt_ref[seq, i]], selbufs_ref.at[into, i],
                                     sem.at[into, i // tile_pages])

    def start_all(seq, into):
        def start(i, carry):
            page_copy(seq, i, into).start()
            return carry
        lax.fori_loop(0, n_pages, start, 0)

    @pl.when(b == 0)
    def _():
        start_all(0, 0)

    @pl.when(b + 1 < pl.num_programs(0))
    def _():
        start_all(b + 1, 1 - slot)

    src = lax.broadcasted_iota(jnp.int32, (kvs_new_ref.shape[1], LANES), 0)
    dst = lax.broadcasted_iota(jnp.int32, (kvs_new_ref.shape[1], LANES), 1)
    pick = jnp.where((src == b * t_new + dst) & (dst < t_new), 1.0, 0.0).astype(bf16)
    selbuf_ref[n_pages] = _dot(kvs_new_ref[...], pick)
    winbuf_ref[:, 0:n_win] = win_ref[...]
    winbuf_ref[:, n_win:n_win + LANES] = _dot(kvw_new_ref[...], pick)
    qbuf_ref[...] = jnp.zeros(qbuf_ref.shape, f32)
    qbuf_ref[0:t_new, :] = q_ref[...]
    gbuf_ref[...] = jnp.zeros(gbuf_ref.shape, f32)
    gbuf_ref[0:t_new, :] = gate_ref[...]

    q = qbuf_ref[...] * NSA_SCALE
    gates = jax.nn.sigmoid(gbuf_ref[...])
    q_pos = past_len + lax.broadcasted_iota(jnp.int32, (tp, 1), 0)
    q_pos_r = jnp.concatenate([q_pos] * NSA_HPG, axis=0)
    cmp_end = lax.broadcasted_iota(jnp.int32, (1, n_cmp), 1) * CMP_STRIDE + (CMP_LEN - 1)
    imp_mat = _importance_matrix(n_cmp, n_sel_pad)
    jj = lax.broadcasted_iota(jnp.int32, (tp, n_sel_pad), 1)
    expand = jnp.where(lax.broadcasted_iota(jnp.int32, (tile_blocks, S_KTILE), 0)
                       == lax.broadcasted_iota(jnp.int32, (tile_blocks, S_KTILE), 1) // SEL_BLOCK,
                       1.0, 0.0).astype(bf16)

    def flash(qs, k_t, v_t, mask, m, l, acc):
        s = jnp.where(mask, jnp.dot(qs, k_t.astype(bf16), preferred_element_type=f32), NEG_INF)
        m_new = jnp.maximum(m, jnp.max(s, axis=-1, keepdims=True))
        alpha = jnp.exp(m - m_new)
        e = jnp.where(mask, jnp.exp(s - m_new), 0.0)
        pv = lax.dot_general(e.astype(bf16), v_t.astype(bf16), _NT, preferred_element_type=f32)
        return m_new, alpha * l + jnp.sum(e, axis=-1, keepdims=True), alpha * acc + pv

    qss, sels, o_cs = [], [], []
    for grp in range(NSA_KV):
        kcol0, vcol0 = grp * NSA_HEAD, NSA_KV_COLS + grp * NSA_HEAD
        qs = _stack_heads(q, grp, tp).astype(bf16)
        p_c = _softmax_rows(_dot_nt(qs, kvc_ref[:, kcol0:kcol0 + NSA_HEAD]), cmp_end <= q_pos_r)
        o_cs.append(_dot(p_c, kvc_ref[:, vcol0:vcol0 + NSA_HEAD]))
        p_sum = p_c[0:tp]
        for h in range(1, NSA_HPG):
            p_sum = p_sum + p_c[h * tp:(h + 1) * tp]
        score = _block_scores(_dot_sel(p_sum, imp_mat), jj, q_pos // SEL_BLOCK)
        rank = jnp.zeros((tp, n_sel_pad), f32)
        for j2 in range(n_sel):
            col = score[:, j2:j2 + 1]
            ge = jnp.where(col >= score, 1.0, 0.0)
            gt = jnp.where(col > score, 1.0, 0.0)
            rank = rank + jnp.where(jj > j2, ge, gt)
        sel = jnp.where((rank < SEL_TOPK) & (jj < n_sel), 1.0, 0.0).astype(bf16)
        sels.append(jnp.concatenate([sel] * NSA_HPG, axis=0))
        qss.append(qs)

    def wait(i, carry):
        page_copy(b, i, slot).wait()
        return carry

    lax.fori_loop(0, n_pages, wait, 0)
    n_tiles = past_len // S_KTILE
    tiles = [(kt * S_KTILE, S_KTILE if kt < n_tiles else LANES) for kt in range(n_tiles + 1)]
    probs = [(grp, kt) for kt in range(n_tiles + 1) for grp in range(NSA_KV)]

    def tile_rows(kt, r0):
        pages = range(kt * tile_pages, (kt + 1) * tile_pages) if kt < n_tiles else [n_pages]
        return jnp.concatenate([selbuf_ref[p, r0:r0 + NSA_HEAD, :] for p in pages], axis=1)

    masks, ss = {}, {}
    for grp, kt in probs:
        k0, width = tiles[kt]
        k_pos = k0 + lax.broadcasted_iota(jnp.int32, (1, width), 1)
        blocks = sels[grp][:, kt * tile_blocks:(kt + 1) * tile_blocks]
        chosen = jnp.dot(blocks, expand[:, 0:width], preferred_element_type=f32)
        masks[grp, kt] = (chosen > 0.5) & (k_pos <= q_pos_r)
        k_t = tile_rows(kt, grp * NSA_HEAD)
        ss[grp, kt] = jnp.where(masks[grp, kt], jnp.dot(qss[grp], k_t.astype(bf16), preferred_element_type=f32),
                                NEG_INF)
    row_max = [functools.reduce(jnp.maximum, [jnp.max(ss[grp, kt], axis=-1, keepdims=True)
                                               for kt in range(n_tiles + 1)]) for grp in range(NSA_KV)]
    es = {p: jnp.where(masks[p], jnp.exp(ss[p] - row_max[p[0]]), 0.0) for p in probs}
    pvs = {}
    for grp, kt in probs:
        k0, width = tiles[kt]
        v_t = tile_rows(kt, NSA_KV_COLS + grp * NSA_HEAD)
        pvs[grp, kt] = lax.dot_general(es[grp, kt].astype(bf16), v_t.astype(bf16), _NT, preferred_element_type=f32)

    w_pos = past_len - n_win + lax.broadcasted_iota(jnp.int32, (1, winbuf_ref.shape[1]), 1)
    dist = jnp.where(w_pos >= 0, q_pos_r - w_pos, -1)
    for grp in range(NSA_KV):
        krow0, vrow0 = grp * NSA_HEAD, NSA_KV_COLS + grp * NSA_HEAD
        denom = sum(jnp.sum(es[grp, kt], axis=-1, keepdims=True) for kt in range(n_tiles + 1))
        o_s = sum(pvs[grp, kt] for kt in range(n_tiles + 1)) / jnp.maximum(denom, 1e-30)
        o_w = _flash_out(*flash(qss[grp], winbuf_ref[krow0:krow0 + NSA_HEAD, :], winbuf_ref[vrow0:vrow0 + NSA_HEAD, :],
                                (dist >= 0) & (dist < WINDOW), *_flash_init(rows)))
        o = (_stack_gates(gates, grp, 0) * o_cs[grp] + _stack_gates(gates, grp, 1) * o_s
             + _stack_gates(gates, grp, 2) * o_w)
        for h in range(NSA_HPG):
            col = (grp * NSA_HPG + h) * NSA_HEAD
            obuf_ref[:, col:col + NSA_HEAD] = o[h * tp:(h + 1) * tp]
    o_ref[...] = obuf_ref[0:t_new, :]


def _nsa_sample(p_q, p_gate, kvc_cmp, cache_sel_t, page_table, kvs_new_t, win_state_t, kvw_new_t):
    b, t_new, _ = p_q.shape
    n_pages = page_table.shape[1]
    page = cache_sel_t.shape[2]
    past_len = n_pages * page
    n_cmp = kvc_cmp.shape[1]
    n_win = win_state_t.shape[2]
    tp = 8
    assert t_new <= tp and past_len % S_KTILE == 0 and S_KTILE % page == 0 and n_win % LANES == 0
    assert b * t_new == kvs_new_t.shape[1] and page == LANES
    n_sel_pad = -(-(-(-(past_len + t_new) // SEL_BLOCK)) // LANES) * LANES
    assert (past_len // S_KTILE + 1) * (S_KTILE // SEL_BLOCK) <= n_sel_pad
    per_seq = lambda n, w: pl.BlockSpec((None, n, w), lambda i, pt: (i, 0, 0))
    whole = lambda a: pl.BlockSpec(a.shape, lambda i, pt: (0,) * a.ndim, pipeline_mode=pl.Buffered(1))
    return pl.pallas_call(
        functools.partial(_nsa_sample_kernel, past_len),
        out_shape=jax.ShapeDtypeStruct((b, t_new, NSA_WIDTH), f32),
        grid_spec=pltpu.PrefetchScalarGridSpec(
            num_scalar_prefetch=1, grid=(b,),
            in_specs=[per_seq(t_new, NSA_WIDTH), per_seq(t_new, GATE_PAD), per_seq(n_cmp, KV_COLS),
                      pl.BlockSpec(memory_space=pl.ANY), whole(kvs_new_t), per_seq(KV_COLS, n_win),
                      whole(kvw_new_t)],
            out_specs=per_seq(t_new, NSA_WIDTH),
            scratch_shapes=[pltpu.VMEM((2, n_pages + 1, KV_COLS, page), f32), pltpu.VMEM((KV_COLS, n_win + LANES), f32),
                            pltpu.VMEM((tp, NSA_WIDTH), f32), pltpu.VMEM((tp, GATE_PAD), f32),
                            pltpu.VMEM((tp, NSA_WIDTH), f32), pltpu.SemaphoreType.DMA((2, past_len // S_KTILE))]),
        compiler_params=_cparams(("arbitrary",)),
        name="nsa_sample",
    )(page_table, p_q, p_gate, kvc_cmp, cache_sel_t, kvs_new_t, win_state_t, kvw_new_t)


def _prep_weights(l, depth, ln_g, ln_b, ffn_w_gate, ffn_w_up, ffn_w_down, w_in, b_in, rw_mu, rw_w0, rw_w2, rw_a0,
                  rw_a2, rw_g2, rw_k_k, rw_k_a, rw_r_k, rw_ln_w, rw_ln_b, nsa_phi_pe, nsa_phi_w1, nsa_phi_w2,
                  w_out_a, w_out_b, w_o):
    d = D_MODEL
    nc = D_FF // FF_CHUNK
    up = lambda w: w.astype(bf16).reshape(d, nc, FF_CHUNK).transpose(1, 0, 2)
    down = lambda w: w.astype(bf16).reshape(nc, FF_CHUNK, d)
    c1 = RW_SHIFT
    c2 = c1 + NSA_WIDTH
    c3 = c2 + 3 * KV_COLS
    c4 = c3 + N_GATE
    cols = lambda a: jnp.concatenate(
        [a[..., :c3], a[..., c4:], a[..., c3:c4], jnp.zeros(a.shape[:-1] + (GATE_PAD - N_GATE,), a.dtype)], axis=-1)
    z = lambda r, c: jnp.zeros((r, c), f32)
    w = RW_WIDTH
    lora = jnp.concatenate([
        jnp.concatenate([rw_w2[l], z(W_LORA, w), z(W_LORA, w)], axis=1),
        jnp.concatenate([z(A_LORA, w), rw_a2[l], z(A_LORA, w)], axis=1),
        jnp.concatenate([z(G_LORA, w), z(G_LORA, w), rw_g2[l]], axis=1)], axis=0)
    hid = lax.broadcasted_iota(jnp.int32, (w, w), 0) // RW_HEAD == lax.broadcasted_iota(jnp.int32, (w, w), 1) // RW_HEAD
    vec = jnp.stack([rw_w0[l], rw_a0[l], rw_k_k[l], rw_k_a[l], rw_r_k[l].reshape(w), rw_ln_w[l], rw_ln_b[l],
                     jnp.zeros((w,), f32)])
    w1 = nsa_phi_w1[l].reshape(2, 2, CMP_STRIDE, NSA_HEAD, CMP_HID)
    eye = jnp.eye(NSA_KV, dtype=f32)
    big = jnp.einsum("kljde,kK,gG->jkgdKGle", w1, jnp.eye(2, dtype=f32), eye)
    per_kv = jnp.einsum("kljde,gG->kjgdGle", w1, eye)
    pe = jnp.broadcast_to(nsa_phi_pe[l].reshape(2, 1, CMP_LEN * NSA_HEAD), (2, 8, CMP_LEN * NSA_HEAD))
    w_cols, b_cols = cols(w_in[l]), cols(b_in[l])
    seg_t = _segments(_PROJ, [name for name, _ in _PROJ_T])
    w_t = jnp.concatenate([w_cols[:, lo:hi] for lo, hi in seg_t], axis=1).T
    b_t = jnp.concatenate([b_cols[lo:hi] for lo, hi in seg_t])
    return dict(
        alpha=(2 * depth) ** 0.25,
        wg1=up(ffn_w_gate[l, 0]), wu1=up(ffn_w_up[l, 0]), wd1=down(ffn_w_down[l, 0]),
        wg2=up(ffn_w_gate[l, 1]), wu2=up(ffn_w_up[l, 1]), wd2=down(ffn_w_down[l, 1]),
        ln_g=ln_g[l], ln_b=ln_b[l],
        w_in=w_cols.astype(bf16), b_in=b_cols.reshape(1, -1),
        w_in_t=w_t.astype(bf16), b_in_t=b_t[:, None],
        phi_w2t=jnp.swapaxes(nsa_phi_w2[l], 1, 2).astype(bf16),
        rw_mu=rw_mu[l].reshape(1, -1), rw_lora=lora.astype(bf16), rw_vec=vec, rw_hsum=hid.astype(bf16),
        phi_big=big.reshape(CMP_ROW, 2 * NSA_KV * 2 * CMP_HID).astype(bf16), phi_pe=pe,
        phi_kv=per_kv.reshape(2, CMP_STRIDE * NSA_KV_COLS, NSA_KV * 2 * CMP_HID).astype(bf16),
        phi_w1f=nsa_phi_w1[l].reshape(2, CMP_LEN * NSA_HEAD, CMP_HID).astype(bf16),
        phi_w2=nsa_phi_w2[l].astype(bf16),
        w_out_a=w_out_a[l].astype(bf16), w_out_b=w_out_b[l].astype(bf16), w_o=w_o[l].astype(bf16))


def _kv_state(p):
    return p.reshape(p.shape[:-1] + (2, NSA_KV, NSA_HEAD))


def _tokens_last(a):
    n, t = a.shape[:2]
    return a.transpose(0, 2, 3, 4, 1).reshape(n, KV_COLS, t)


def _kv_state_t(p_t):
    b, _, t = p_t.shape
    return p_t.reshape(b, 2, NSA_KV, NSA_HEAD, t).transpose(0, 4, 1, 2, 3)


def kernel(x_prompt, x_sample, c_prompt, c_sample, cache_kv_cmp, cache_kv_sel, state_kv_win, state_wkv, state_shift, page_table, w_ada, b_ada, ln_g, ln_b, ffn_w_gate, ffn_w_up, ffn_w_down, w_in, b_in, rw_mu, rw_w0, rw_w2, rw_a0, rw_a2, rw_g2, rw_k_k, rw_k_a, rw_r_k, rw_ln_w, rw_ln_b, nsa_phi_pe, nsa_phi_w1, nsa_phi_w2, w_out_a, w_out_b, w_o):
    bp, seq, d = x_prompt.shape
    bd, t_new, _ = x_sample.shape
    depth = w_ada.shape[0]
    n_phys, page = cache_kv_cmp.shape[1:3]
    n_pages = page_table.shape[1]
    n_win = state_kv_win.shape[2]
    rows_s = bd * t_new
    tm_p = min(TM_PROJ, seq)
    tm_ffn = min(TM_FFN, seq)
    assert seq % tm_p == 0 and seq % tm_ffn == 0 and seq % CMP_STRIDE == 0 and page % CMP_STRIDE == 0

    y_p, y_s = x_prompt, x_sample.reshape(1, rows_s, d)
    st_p, st_s = [], []
    for l in range(depth):
        wts = _prep_weights(l, depth, ln_g, ln_b, ffn_w_gate, ffn_w_up, ffn_w_down, w_in, b_in, rw_mu, rw_w0,
                            rw_w2, rw_a0, rw_a2, rw_g2, rw_k_k, rw_k_a, rw_r_k, rw_ln_w, rw_ln_b, nsa_phi_pe,
                            nsa_phi_w1, nsa_phi_w2, w_out_a, w_out_b, w_o)
        mod = _ada_mod(jnp.concatenate([c_prompt, c_sample], axis=0), w_ada[l], b_ada[l])
        mod_p = mod[:bp].reshape(bp, 9, 1, d)
        mod_s = jnp.repeat(mod[bp:].reshape(bd, 9, d), t_new, axis=0).reshape(rows_s, 9, d)
        mod_s = mod_s.transpose(1, 0, 2).reshape(1, 9, rows_s, d)

        x1 = _ffn_block(y_p, mod_p, wts, tm_ffn, 0)
        p_rw, p_kc, p_ks, p_kw, p_m, q_t, g_t, ks_t, kw_t = _in_proj(
            x1, mod_p, wts, tm_p, ("rw", "kc", "ks", "kw", "merge"), ("q", "gate", "ks", "kw"))
        ya, wkv_p = _rwkv(p_rw, jnp.zeros((bp, RW_SHIFT), f32), jnp.zeros((bp, RW_HEADS, RW_HEAD, RW_HEAD), f32),
                          wts)
        kvc_cmp, kvc_cmp_t = _compress_dense(p_kc.reshape(bp, seq // CMP_STRIDE, CMP_ROW), wts)
        yb_t = _nsa_prompt(q_t, g_t, kvc_cmp, kvc_cmp_t, p_ks, ks_t, p_kw, kw_t)
        y_p = _ffn_block(_mix(x1, ya, yb_t, p_m, mod_p, wts, tm_p, True), mod_p, wts, tm_ffn, 1)
        n_keep = min(WINDOW, seq)
        st_p.append((_kv_state(p_kc), _kv_state_t(ks_t), _kv_state_t(kw_t[:, :, seq - n_keep:]), wkv_p,
                     p_rw[:, -1]))

        x1 = _ffn_block(y_s, mod_s, wts, rows_s, 0)
        p_rw, p_q, p_kc, p_ks, p_m, p_g, ks_t, kw_t = _in_proj(
            x1, mod_s, wts, rows_s, ("rw", "q", "kc", "ks", "merge", "gate"), ("ks", "kw"))
        per_seq = lambda a: a.reshape(bd, t_new, a.shape[-1])
        p_rw, p_q, p_kc, p_ks, p_g = map(per_seq, (p_rw, p_q, p_kc, p_ks, p_g))
        ya, wkv_s = _rwkv(p_rw, state_shift[l], state_wkv[l], wts)
        kvc_cmp = _compress_paged(_tokens_last(cache_kv_cmp[l]), page_table, wts)
        win_state_t = _tokens_last(state_kv_win[l])
        yb = _nsa_sample(p_q, p_g, kvc_cmp, _tokens_last(cache_kv_sel[l]), page_table, ks_t[0], win_state_t,
                         kw_t[0])
        x2 = _mix(x1, ya.reshape(1, rows_s, RW_WIDTH), yb.reshape(1, rows_s, NSA_WIDTH), p_m, mod_s, wts, rows_s,
                  False)
        y_s = _ffn_block(x2, mod_s, wts, rows_s, 1)
        kw_new_t = kw_t[0].reshape(KV_COLS, bd, t_new).transpose(1, 0, 2)
        new_win_t = jnp.concatenate([win_state_t, kw_new_t], axis=2)[:, :, t_new:]
        st_s.append((_kv_state(p_kc), _kv_state(p_ks), _kv_state_t(new_win_t), wkv_s, p_rw[:, -1]))

    stack = lambda sts, i: jnp.stack([s[i] for s in sts])
    return (y_p, y_s.reshape(bd, t_new, d)) + tuple(stack(st_p, i) for i in range(5)) + tuple(
        stack(st_s, i) for i in range(5))
```

```python
import functools
import math

import jax
import jax.numpy as jnp
from jax import lax
from jax.experimental import pallas as pl
from jax.experimental.pallas import tpu as pltpu

f32 = jnp.float32
bf16 = jnp.bfloat16

D_MODEL = 1024
RW_HEAD = 64
RW_HEADS = 8
RW_WIDTH = RW_HEADS * RW_HEAD
W_LORA = 64
A_LORA = 64
G_LORA = 128
LORA_IN = W_LORA + A_LORA + G_LORA
RW_GN_EPS = 64e-5
RW_SHIFT = 3 * RW_WIDTH + LORA_IN
NSA_HEAD = 64
NSA_HEADS = 8
NSA_KV = 2
NSA_HPG = NSA_HEADS // NSA_KV
NSA_WIDTH = NSA_HEADS * NSA_HEAD
NSA_KV_COLS = NSA_KV * NSA_HEAD
KV_COLS = 2 * NSA_KV_COLS
NSA_SCALE = NSA_HEAD ** -0.5
CMP_STRIDE = 16
CMP_LEN = 2 * CMP_STRIDE
CMP_HID = 128
CMP_ROW = CMP_STRIDE * KV_COLS
SEL_BLOCK = 64
SEL_TOPK = 16
WINDOW = 512
D_FF = 2816
N_GATE = 3 * NSA_HEADS
LN_EPS = 1e-5
NEG_INF = -1e30
BIG = 1e9

LANES = 128
GATE_PAD = LANES
FF_CHUNK = 256
TM_FFN = 1024
TM_PROJ = 512
RW_CHUNK = 64
RW_HPB = 4
RW_SEQS = 4
Q_TILE = 256
K_TILE = 256
FLASH_HEADS = 8
S_KTILE = 2048
VMEM_LIMIT = 56 * 1024 * 1024

_NT = (((1,), (1,)), ((), ()))
_TN = (((0,), (0,)), ((), ()))
_HI = lax.Precision.HIGHEST


def _dot(a, b):
    return jnp.dot(a.astype(bf16), b.astype(bf16), preferred_element_type=f32)


def _dot_nt(a, b):
    return lax.dot_general(a.astype(bf16), b.astype(bf16), _NT, preferred_element_type=f32)


def _dot_f32(a, b, dims=None):
    if dims is None:
        return jnp.dot(a, b, preferred_element_type=f32, precision=_HI)
    return lax.dot_general(a, b, dims, preferred_element_type=f32, precision=_HI)


def _dot_sel(x, m01):
    hi = x.astype(bf16)
    r1 = x - hi.astype(f32)
    mid = r1.astype(bf16)
    lo = (r1 - mid.astype(f32)).astype(bf16)
    d = lambda t: jnp.dot(t, m01, preferred_element_type=f32)
    return d(hi) + d(mid) + d(lo)


def _layer_norm(x, g, b):
    mu = jnp.mean(x, axis=-1, keepdims=True)
    xc = x - mu
    var = jnp.mean(xc * xc, axis=-1, keepdims=True)
    return xc * lax.rsqrt(var + LN_EPS) * g + b


def _cparams(sem):
    return pltpu.CompilerParams(dimension_semantics=sem, vmem_limit_bytes=VMEM_LIMIT)


def _resident(shape):
    nd = len(shape)
    return pl.BlockSpec(shape, lambda *_: (0,) * nd, pipeline_mode=pl.Buffered(1))


def _ada_kernel(c_ref, w_ref, b_ref, o_ref):
    c = c_ref[...]
    o_ref[...] = _dot(c * jax.nn.sigmoid(c), w_ref[...]) + b_ref[...]


def _ada_mod(c, w_ada, b_ada):
    rows, d = c.shape
    n = w_ada.shape[1]
    tn = d
    return pl.pallas_call(
        _ada_kernel,
        out_shape=jax.ShapeDtypeStruct((rows, n), f32),
        grid=(n // tn,),
        in_specs=[pl.BlockSpec((rows, d), lambda j: (0, 0)),
                  pl.BlockSpec((d, tn), lambda j: (0, j)),
                  pl.BlockSpec((1, tn), lambda j: (0, j))],
        out_specs=pl.BlockSpec((rows, tn), lambda j: (0, j)),
        compiler_params=_cparams(("parallel",)),
        name="ada_mod",
    )(c, w_ada, b_ada.reshape(1, n))


def _ffn(u, wg_ref, wu_ref, wd_ref, acc_ref):
    ub = u.astype(bf16)
    acc_ref[...] = jnp.zeros(acc_ref.shape, f32)

    def body(c, carry):
        hg = jnp.dot(ub, wg_ref[c], preferred_element_type=f32)
        hu = jnp.dot(ub, wu_ref[c], preferred_element_type=f32)
        h = hg * jax.nn.sigmoid(hg) * hu
        acc_ref[...] += jnp.dot(h.astype(bf16), wd_ref[c], preferred_element_type=f32)
        return carry

    lax.fori_loop(0, wg_ref.shape[0], body, 0)
    return acc_ref[...]


def _ffn_block_kernel(alpha, half, x_ref, mod_ref, wg_ref, wu_ref, wd_ref, lng_ref, lnb_ref, o_ref, acc_ref):
    m0 = 6 * half
    ln = 2 * half
    x = x_ref[...]
    u = x * (1.0 + mod_ref[m0 + 1]) + mod_ref[m0]
    f = _ffn(u, wg_ref, wu_ref, wd_ref, acc_ref)
    o_ref[...] = _layer_norm(alpha * x + 0.5 * (1.0 + mod_ref[m0 + 2]) * f, lng_ref[ln:ln + 1, :],
                             lnb_ref[ln:ln + 1, :])


def _ffn_block(x, mod, wts, tm, half):
    g, t, d = x.shape
    r = mod.shape[2]
    row = pl.BlockSpec((None, tm, d), lambda i, j: (i, j, 0))
    names = ("wg1", "wu1", "wd1") if half == 0 else ("wg2", "wu2", "wd2")
    weights = [wts[n] for n in names] + [wts["ln_g"], wts["ln_b"]]
    return pl.pallas_call(
        functools.partial(_ffn_block_kernel, wts["alpha"], half),
        out_shape=jax.ShapeDtypeStruct((g, t, d), f32),
        grid=(g, t // tm),
        in_specs=[row, pl.BlockSpec((None, 9, r, d), lambda i, j: (i, 0, 0, 0))] + [_resident(w.shape) for w in weights],
        out_specs=row,
        scratch_shapes=[pltpu.VMEM((tm, d), f32)],
        compiler_params=_cparams(("parallel", "parallel")),
        name="ffn_block",
    )(x, mod, *weights)


def _in_proj_kernel(seg, seg_t, x1_ref, mod_ref, win_ref, bin_ref, wint_ref, bint_ref, *out_refs):
    u2 = (x1_ref[...] * (1.0 + mod_ref[4]) + mod_ref[3]).astype(bf16)
    for o_ref, (lo, hi) in zip(out_refs, seg):
        o_ref[...] = jnp.dot(u2, win_ref[:, lo:hi], preferred_element_type=f32) + bin_ref[:, lo:hi]
    for o_ref, (lo, hi) in zip(out_refs[len(seg):], seg_t):
        o_ref[...] = lax.dot_general(wint_ref[lo:hi, :], u2, _NT, preferred_element_type=f32) + bint_ref[lo:hi, :]


_PROJ = (("rw", RW_SHIFT), ("q", NSA_WIDTH), ("kc", KV_COLS), ("ks", KV_COLS), ("kw", KV_COLS),
         ("merge", 2 * D_MODEL), ("gate", GATE_PAD))
_PROJ_T = (("q", NSA_WIDTH), ("gate", GATE_PAD), ("ks", KV_COLS), ("kw", KV_COLS))


def _segments(table, names):
    seg, lo = {}, 0
    for name, w in table:
        seg[name] = (lo, lo + w)
        lo += w
    return tuple(seg[n] for n in names)


def _in_proj(x1, mod, wts, tm, names, names_t):
    g, t, d = x1.shape
    r = mod.shape[2]
    seg, seg_t = _segments(_PROJ, names), _segments(_PROJ_T, names_t)
    row = lambda w: pl.BlockSpec((None, tm, w), lambda i, j: (i, j, 0))
    col = lambda w: pl.BlockSpec((None, w, tm), lambda i, j: (i, 0, j))
    weights = [wts["w_in"], wts["b_in"], wts["w_in_t"], wts["b_in_t"]]
    return pl.pallas_call(
        functools.partial(_in_proj_kernel, seg, seg_t),
        out_shape=([jax.ShapeDtypeStruct((g, t, hi - lo), f32) for lo, hi in seg]
                   + [jax.ShapeDtypeStruct((g, hi - lo, t), f32) for lo, hi in seg_t]),
        grid=(g, t // tm),
        in_specs=[row(d), pl.BlockSpec((None, 9, r, d), lambda i, j: (i, 0, 0, 0))]
                 + [_resident(w.shape) for w in weights],
        out_specs=[row(hi - lo) for lo, hi in seg] + [col(hi - lo) for lo, hi in seg_t],
        compiler_params=_cparams(("parallel", "parallel")),
        name="in_proj",
    )(x1, mod, *weights)


def _mix_kernel(alpha, yb_on_lanes, x1_ref, ya_ref, yb_ref, pm_ref, mod_ref, woa_ref, wob_ref, wo_ref,
                lng_ref, lnb_ref, o_ref):
    x1 = x1_ref[...]
    pm = pm_ref[...]
    d = x1.shape[-1]
    ga = jax.nn.sigmoid(pm[:, :d])
    gb = jax.nn.sigmoid(pm[:, d:])
    if yb_on_lanes:
        yb_out = lax.dot_general(yb_ref[...].astype(bf16), wob_ref[...], _TN, preferred_element_type=f32)
    else:
        yb_out = _dot(yb_ref[...], wob_ref[...])
    merged = ga * _dot(ya_ref[...], woa_ref[...]) + gb * yb_out
    m = _dot(merged, wo_ref[...])
    o_ref[...] = _layer_norm(alpha * x1 + (1.0 + mod_ref[5]) * m, lng_ref[1:2, :], lnb_ref[1:2, :])


def _mix(x1, ya, yb, pm, mod, wts, tm, yb_on_lanes):
    g, t, d = x1.shape
    r = mod.shape[2]
    row = lambda w: pl.BlockSpec((None, tm, w), lambda i, j: (i, j, 0))
    yb_spec = pl.BlockSpec((None, NSA_WIDTH, tm), lambda i, j: (i, 0, j)) if yb_on_lanes else row(NSA_WIDTH)
    weights = [wts["w_out_a"], wts["w_out_b"], wts["w_o"], wts["ln_g"], wts["ln_b"]]
    return pl.pallas_call(
        functools.partial(_mix_kernel, wts["alpha"], yb_on_lanes),
        out_shape=jax.ShapeDtypeStruct((g, t, d), f32),
        grid=(g, t // tm),
        in_specs=[row(d), row(ya.shape[-1]), yb_spec, row(pm.shape[-1]),
                  pl.BlockSpec((None, 9, r, d), lambda i, j: (i, 0, 0, 0))] + [_resident(w.shape) for w in weights],
        out_specs=row(d),
        compiler_params=_cparams(("parallel", "parallel")),
        name="mix",
    )(x1, ya, yb, pm, mod, *weights)


def _rwkv_kernel(t_blk, p_ref, prev_ref, s0_ref, mu_ref, wl_ref, vec_ref, hsum_ref,
                 y_ref, sout_ref, state_ref, prevrow_ref, stage_ref, yn_ref):
    n_seq = p_ref.shape[0]
    ci = pl.program_id(1)

    @pl.when(ci == 0)
    def _():
        state_ref[...] = jnp.zeros(state_ref.shape, f32)
        for s in range(n_seq):
            for h in range(RW_HEADS):
                d0 = (h % RW_HPB) * RW_HEAD
                state_ref[s, h // RW_HPB, d0:d0 + RW_HEAD, d0:d0 + RW_HEAD] = s0_ref[s, h]
        prevrow_ref[...] = prev_ref[...]

    c = stage_ref.shape[1]
    gw = RW_HPB * RW_HEAD
    rows = RW_HPB * c
    br = lax.broadcasted_iota(jnp.int32, (rows, gw), 0)
    bc = lax.broadcasted_iota(jnp.int32, (rows, gw), 1)
    same_head = (br // c) == (bc // RW_HEAD)
    tr = lax.broadcasted_iota(jnp.int32, (rows, rows), 0) % c
    tc = lax.broadcasted_iota(jnp.int32, (rows, rows), 1) % c
    strict = tr > tc
    incl = tr >= tc
    blocks = lambda x: jnp.where(same_head, jnp.concatenate([x] * RW_HPB, axis=0), 0.0).astype(bf16)
    nt = lambda x, y: lax.dot_general(x, y, _NT, preferred_element_type=f32)
    tn = lambda x, y: lax.dot_general(x, y, _TN, preferred_element_type=f32)
    mm = lambda x, y: jnp.dot(x, y, preferred_element_type=f32)
    n_fac = max(1, (c - 1).bit_length())

    hsum = hsum_ref[...]

    def head_sums(xs):
        out = _dot_sel(jnp.concatenate(xs, axis=0), hsum)
        return [out[i * c:(i + 1) * c] for i in range(len(xs))]

    raw = [_rwkv_inputs(t_blk, p_ref.at[s], mu_ref, wl_ref, vec_ref, prevrow_ref.at[s], stage_ref.at[s])
           for s in range(n_seq)]
    sums = head_sums([d["kk"] * d["kk"] for d in raw] + [d["rk"] for d in raw])
    pre = [_rwkv_features(t_blk, raw[s], sums[s], sums[n_seq + s]) for s in range(n_seq)]
    chains = [(s, grp) for s in range(n_seq) for grp in range(RW_HEADS // RW_HPB)]
    ops, low, u = {}, {}, {}
    for s, grp in chains:
        gl = slice(grp * gw, (grp + 1) * gw)
        ops[s, grp] = [blocks(pre[s][name][:, gl]) for name in ("at", "rt", "bt", "kt", "bh", "kh", "v")]
    for ch in chains:
        at4, rt4, bt4, kt4, bh4, kh4, v4 = ops[ch]
        s_b = state_ref[ch].astype(bf16)
        low[ch] = jnp.where(strict, nt(at4, bt4), 0.0).astype(bf16)
        u[ch] = nt(at4, s_b) + mm(jnp.where(strict, nt(at4, kt4), 0.0).astype(bf16), v4)
    for i in range(n_fac):
        for ch in chains:
            u[ch] = u[ch] + mm(low[ch], u[ch].astype(bf16))
        if i + 1 < n_fac:
            for ch in chains:
                low[ch] = mm(low[ch], low[ch]).astype(bf16)
    for ch in chains:
        s, grp = ch
        gl = slice(grp * gw, (grp + 1) * gw)
        at4, rt4, bt4, kt4, bh4, kh4, v4 = ops[ch]
        s_bd = state_ref[ch]
        u_b = u[ch].astype(bf16)
        y4 = (nt(rt4, s_bd.astype(bf16)) + mm(jnp.where(incl, nt(rt4, bt4), 0.0).astype(bf16), u_b)
              + mm(jnp.where(incl, nt(rt4, kt4), 0.0).astype(bf16), v4))
        state_ref[ch] = s_bd * pre[s]["p_last"][:, gl] + tn(u_b, bh4) + tn(v4, kh4)
        yg = y4[0:c]
        for hh in range(1, RW_HPB):
            yg = yg + y4[hh * c:(hh + 1) * c]
        yn_ref[s, :, gl] = yg

    ln_w, ln_b = vec_ref[5:6, :], vec_ref[6:7, :]
    means = head_sums([yn_ref[s] for s in range(n_seq)])
    ycs = [yn_ref[s] - means[s] * (1.0 / RW_HEAD) for s in range(n_seq)]
    sqs = head_sums([yc * yc for yc in ycs])
    for s in range(n_seq):
        var = sqs[s] * (1.0 / RW_HEAD)
        y = (ycs[s] * lax.rsqrt(var + RW_GN_EPS) * ln_w + ln_b + pre[s]["bonus"]) * pre[s]["g"]
        y_ref[s] = y[0:t_blk, :] if t_blk < c else y

    @pl.when(ci == pl.num_programs(1) - 1)
    def _():
        for s in range(n_seq):
            for h in range(RW_HEADS):
                d0 = (h % RW_HPB) * RW_HEAD
                sout_ref[s, h] = state_ref[s, h // RW_HPB, d0:d0 + RW_HEAD, d0:d0 + RW_HEAD]


def _rwkv_inputs(t_blk, p_ref, mu_ref, wl_ref, vec_ref, prevrow_ref, stage_ref):
    c = stage_ref.shape[0]
    w = RW_WIDTH
    if t_blk < c:
        stage_ref[...] = jnp.zeros(stage_ref.shape, f32)
        stage_ref[0:t_blk, :] = p_ref[...]
        p = stage_ref[...]
    else:
        p = p_ref[...]
    ridx = lax.broadcasted_iota(jnp.int32, (c, 1), 0)
    valid = ridx < t_blk
    p_prev = jnp.where(ridx == 0, prevrow_ref[...], pltpu.roll(p, 1, 0))
    prevrow_ref[...] = p[c - 1:c, :]
    xs = p + (p_prev - p) * mu_ref[...]

    tail = xs[:, 3 * w:]
    li = lax.broadcasted_iota(jnp.int32, tail.shape, 1)
    act = jnp.where(li < W_LORA, jnp.tanh(tail),
                    jnp.where(li < W_LORA + A_LORA, tail, jax.nn.sigmoid(tail)))
    lora = _dot(act, wl_ref[...])
    w0, a0, k_k, k_a, r_k = (vec_ref[i:i + 1, :] for i in range(5))
    z = -(w0 + lora[:, :w])
    softplus = jnp.maximum(z, 0.0) + jnp.log1p(jnp.exp(-jnp.abs(z)))
    lw = -jnp.exp(-softplus - 0.5)
    a = jax.nn.sigmoid(a0 + lora[:, w:2 * w])
    g = lora[:, 2 * w:]
    r, k, v = xs[:, :w], xs[:, w:2 * w], xs[:, 2 * w:3 * w]
    k2 = k * (1.0 + (a - 1.0) * k_a)
    return dict(r=r, v=v, lw=lw, a=a, g=g, kk=k * k_k, k2=k2, rk=r * k2 * r_k, valid=valid)


def _rwkv_features(t_blk, raw, kk_sq_sum, rk_sum):
    r, v, lw, a, g, k2, valid = (raw[n] for n in ("r", "v", "lw", "a", "g", "k2", "valid"))
    c = r.shape[0]
    kk = raw["kk"] * lax.rsqrt(jnp.maximum(kk_sq_sum, 1e-24))
    bonus = rk_sum * v
    if t_blk < c:
        lw = jnp.where(valid, lw, 0.0)
        kk = jnp.where(valid, kk, 0.0)
        k2 = jnp.where(valid, k2, 0.0)
        v = jnp.where(valid, v, 0.0)
    b = kk * a

    rr = lax.broadcasted_iota(jnp.int32, (c, c), 0)
    cc = lax.broadcasted_iota(jnp.int32, (c, c), 1)
    cum = _dot_f32(jnp.where(rr >= cc, 1.0, 0.0).astype(f32), lw)
    cum_last = cum[c - 1:c, :]
    e_neg = jnp.exp(-cum)
    e_rem = jnp.exp(cum_last - cum)
    at = -kk * jnp.exp(cum - lw)
    rt = r * jnp.exp(cum)
    bt, kt = b * e_neg, k2 * e_neg
    bh, kh = b * e_rem, k2 * e_rem
    p_last = jnp.exp(cum_last)
    return dict(at=at, rt=rt, bt=bt, kt=kt, bh=bh, kh=kh, v=v, p_last=p_last, bonus=bonus, g=g)


def _rwkv(p_rw, prev, s0, wts):
    g, t, ws = p_rw.shape
    c = RW_CHUNK
    t_blk = min(c, t)
    sb = RW_SEQS
    assert t % t_blk == 0 and g % sb == 0
    gw = RW_HPB * RW_HEAD
    return pl.pallas_call(
        functools.partial(_rwkv_kernel, t_blk),
        out_shape=[jax.ShapeDtypeStruct((g, t, RW_WIDTH), f32), jax.ShapeDtypeStruct(s0.shape, f32)],
        grid=(g // sb, t // t_blk),
        in_specs=[pl.BlockSpec((sb, t_blk, ws), lambda i, j: (i, j, 0)),
                  pl.BlockSpec((sb, 1, ws), lambda i, j: (i, 0, 0)),
                  pl.BlockSpec((sb,) + s0.shape[1:], lambda i, j: (i, 0, 0, 0)),
                  _resident(wts["rw_mu"].shape), _resident(wts["rw_lora"].shape),
                  _resident(wts["rw_vec"].shape), _resident(wts["rw_hsum"].shape)],
        out_specs=[pl.BlockSpec((sb, t_blk, RW_WIDTH), lambda i, j: (i, j, 0)),
                   pl.BlockSpec((sb,) + s0.shape[1:], lambda i, j: (i, 0, 0, 0))],
        scratch_shapes=[pltpu.VMEM((sb, RW_HEADS // RW_HPB, gw, gw), f32), pltpu.VMEM((sb, 1, ws), f32),
                        pltpu.VMEM((sb, c, ws), f32), pltpu.VMEM((sb, c, RW_WIDTH), f32)],
        compiler_params=_cparams(("parallel", "arbitrary")),
        name="rwkv",
    )(p_rw, prev.reshape(g, 1, ws), s0, wts["rw_mu"], wts["rw_lora"], wts["rw_vec"], wts["rw_hsum"])


def _gelu_tanh(x):
    return x * (0.5 * (1.0 + jnp.tanh(math.sqrt(2.0 / math.pi) * (x + 0.044715 * (x * x * x)))))


def _compress_rows(hids, pe_ref, w1f_ref, w2_ref, o_ref, w2t_ref=None, ot_ref=None):
    n = hids[0].shape[0]
    for kv in range(2):
        pe_term = _dot(pe_ref[kv], w1f_ref[kv])[0:1, :]
        for grp in range(NSA_KV):
            base = grp * 2 * CMP_HID
            lo = hids[kv][:, base:base + CMP_HID]
            hi = pltpu.roll(hids[kv][:, base + CMP_HID:base + 2 * CMP_HID], n - 1, 0)
            act = _gelu_tanh(lo + hi + pe_term).astype(bf16)
            col = (kv * NSA_KV + grp) * NSA_HEAD
            o_ref[:, col:col + NSA_HEAD] = jnp.dot(act, w2_ref[kv], preferred_element_type=f32)
            if ot_ref is not None:
                ot_ref[col:col + NSA_HEAD, :] = lax.dot_general(w2t_ref[kv], act, _NT, preferred_element_type=f32)


def _compress_dense_kernel(x_ref, wbig_ref, pe_ref, w1f_ref, w2_ref, w2t_ref, o_ref, ot_ref):
    hid = _dot(x_ref[...], wbig_ref[...])
    half = hid.shape[1] // 2
    _compress_rows([hid[:, :half], hid[:, half:]], pe_ref, w1f_ref, w2_ref, o_ref, w2t_ref, ot_ref)


def _compress_dense(x, wts):
    b, n, _ = x.shape
    return pl.pallas_call(
        _compress_dense_kernel,
        out_shape=[jax.ShapeDtypeStruct((b, n, KV_COLS), f32), jax.ShapeDtypeStruct((b, KV_COLS, n), f32)],
        grid=(b,),
        in_specs=[pl.BlockSpec((None, n, CMP_ROW), lambda i: (i, 0, 0)),
                  _resident(wts["phi_big"].shape), _resident(wts["phi_pe"].shape),
                  _resident(wts["phi_w1f"].shape), _resident(wts["phi_w2"].shape),
                  _resident(wts["phi_w2t"].shape)],
        out_specs=[pl.BlockSpec((None, n, KV_COLS), lambda i: (i, 0, 0)),
                   pl.BlockSpec((None, KV_COLS, n), lambda i: (i, 0, 0))],
        compiler_params=_cparams(("parallel",)),
        name="compress_dense",
    )(x, wts["phi_big"], wts["phi_pe"], wts["phi_w1f"], wts["phi_w2"], wts["phi_w2t"])


CMP_GROUP = 2 * LANES
CMP_UNROLL = 4


def _compress_paged_kernel(pt_ref, cache_ref, wkv_ref, pe_ref, w1f_ref, w2_ref, o_ref, pbuf_ref, xbuf_ref, sem):
    b = pl.program_id(0)
    n_pages = pt_ref.shape[1]
    page = cache_ref.shape[2]
    pages_per_group = CMP_GROUP // page
    rows_per_group = CMP_GROUP // CMP_STRIDE

    def page_copy(seq, i):
        return pltpu.make_async_copy(cache_ref.at[pt_ref[seq, i]], pbuf_ref.at[i], sem.at[i // pages_per_group])

    def start_all(seq):
        def start(i, carry):
            page_copy(seq, i).start()
            return carry
        lax.fori_loop(0, n_pages, start, 0)

    @pl.when(b == 0)
    def _():
        start_all(0)

    r = lax.broadcasted_iota(jnp.int32, (CMP_GROUP, CMP_GROUP), 0)
    c = lax.broadcasted_iota(jnp.int32, (CMP_GROUP, CMP_GROUP), 1)
    perm = jnp.where(c == CMP_STRIDE * (r % rows_per_group) + r // rows_per_group, 1.0, 0.0).astype(bf16)

    def regroup(step, carry):
        for pg in range(CMP_UNROLL * pages_per_group):
            page_copy(b, step * CMP_UNROLL * pages_per_group + pg).wait()
        for k in range(CMP_UNROLL):
            gi = step * CMP_UNROLL + k
            tokens = jnp.concatenate([pbuf_ref[gi * pages_per_group + pg] for pg in range(pages_per_group)],
                                     axis=1).astype(bf16)
            rows = lax.dot_general(perm, tokens, _NT, preferred_element_type=f32)
            r0 = pl.multiple_of(gi * rows_per_group, rows_per_group)
            for j in range(CMP_STRIDE):
                for kv in range(2):
                    col = (kv * CMP_STRIDE + j) * NSA_KV_COLS
                    xbuf_ref[pl.ds(r0, rows_per_group), col:col + NSA_KV_COLS] = (
                        rows[j * rows_per_group:(j + 1) * rows_per_group,
                             kv * NSA_KV_COLS:(kv + 1) * NSA_KV_COLS].astype(bf16))
        return carry

    n_groups = n_pages // pages_per_group
    lax.fori_loop(0, n_groups // CMP_UNROLL, regroup, 0)

    @pl.when(b + 1 < pl.num_programs(0))
    def _():
        start_all(b + 1)

    half = CMP_STRIDE * NSA_KV_COLS
    hids = [jnp.dot(xbuf_ref[:, kv * half:(kv + 1) * half], wkv_ref[kv], preferred_element_type=f32)
            for kv in range(2)]
    _compress_rows(hids, pe_ref, w1f_ref, w2_ref, o_ref)


def _compress_paged(cache_t, page_table, wts):
    b, n_pages = page_table.shape
    page = cache_t.shape[2]
    n = n_pages * page // CMP_STRIDE
    assert CMP_GROUP % page == 0 and (n_pages * page) % (CMP_GROUP * CMP_UNROLL) == 0
    full = lambda a: pl.BlockSpec(a.shape, lambda i, pt: (0,) * a.ndim, pipeline_mode=pl.Buffered(1))
    return pl.pallas_call(
        _compress_paged_kernel,
        out_shape=jax.ShapeDtypeStruct((b, n, KV_COLS), f32),
        grid_spec=pltpu.PrefetchScalarGridSpec(
            num_scalar_prefetch=1, grid=(b,),
            in_specs=[pl.BlockSpec(memory_space=pl.ANY), full(wts["phi_kv"]), full(wts["phi_pe"]),
                      full(wts["phi_w1f"]), full(wts["phi_w2"])],
            out_specs=pl.BlockSpec((None, n, KV_COLS), lambda i, pt: (i, 0, 0)),
            scratch_shapes=[pltpu.VMEM((n_pages, KV_COLS, page), f32), pltpu.VMEM((n, CMP_ROW), bf16),
                            pltpu.SemaphoreType.DMA((n_pages * page // CMP_GROUP,))]),
        compiler_params=_cparams(("arbitrary",)),
        name="compress_paged",
    )(page_table, cache_t, wts["phi_kv"], wts["phi_pe"], wts["phi_w1f"], wts["phi_w2"])


def _softmax_rows(s, mask):
    s = jnp.where(mask, s, NEG_INF)
    e = jnp.where(mask, jnp.exp(s - jnp.max(s, axis=-1, keepdims=True)), 0.0)
    return e / jnp.maximum(jnp.sum(e, axis=-1, keepdims=True), 1e-30)


def _flash_step(q, k, v, mask, m, l, acc):
    s = jnp.where(mask, _dot_nt(q, k), NEG_INF)
    m_new = jnp.maximum(m, jnp.max(s, axis=-1, keepdims=True))
    alpha = jnp.exp(m - m_new)
    e = jnp.where(mask, jnp.exp(s - m_new), 0.0)
    return m_new, alpha * l + jnp.sum(e, axis=-1, keepdims=True), alpha * acc + _dot(e, v)


def _flash_init(rows):
    return (jnp.full((rows, 1), NEG_INF, f32), jnp.zeros((rows, 1), f32), jnp.zeros((rows, NSA_HEAD), f32))


def _flash_out(m, l, acc):
    return acc / jnp.maximum(l, 1e-30)


def _importance_matrix(n_cmp_rows, n_sel_cols):
    ratio = SEL_BLOCK // CMP_STRIDE
    i = lax.broadcasted_iota(jnp.int32, (n_cmp_rows, n_sel_cols), 0)
    j = lax.broadcasted_iota(jnp.int32, (n_cmp_rows, n_sel_cols), 1)
    return jnp.where((i >= ratio * j - 1) & (i <= ratio * j + ratio - 1), 1.0, 0.0).astype(bf16)


def _block_scores(imp, j, cur):
    forced = (j == 0) | (j == cur) | (j == cur - 1)
    return jnp.where(j <= cur, jnp.where(forced, BIG, imp), -BIG)


def _stack_heads(x, grp, rows):
    return jnp.concatenate(
        [x[:, (grp * NSA_HPG + h) * NSA_HEAD:(grp * NSA_HPG + h + 1) * NSA_HEAD] for h in range(NSA_HPG)], axis=0)


def _stack_gates(gates, grp, branch):
    return jnp.concatenate(
        [gates[:, 3 * (grp * NSA_HPG + h) + branch:3 * (grp * NSA_HPG + h) + branch + 1] for h in range(NSA_HPG)],
        axis=0)


def _scores_t(qts, ks, biases):
    return tuple(jnp.dot(k.astype(bf16), qt, preferred_element_type=f32) + bias
                 for qt, k, bias in zip(qts, ks, biases))


def _flash_update_t(ss, vts, carry):
    n = len(ss)
    ms, ls, accs = carry[0::3], carry[1::3], carry[2::3]
    m_new = [jnp.maximum(ms[i], jnp.max(ss[i], axis=0, keepdims=True)) for i in range(n)]
    es = [jnp.exp2(ss[i] - m_new[i]).astype(bf16) for i in range(n)]
    ones = jnp.ones((ONES_ROWS, vts[0].shape[1]), bf16)
    pvs = [jnp.dot(jnp.concatenate([vts[i].astype(bf16), ones], axis=0), es[i], preferred_element_type=f32)
           for i in range(n)]
    out = ()
    for i in range(n):
        alpha = jnp.exp2(ms[i] - m_new[i])
        out += (m_new[i], alpha * ls[i] + pvs[i][NSA_HEAD:NSA_HEAD + 1], alpha * accs[i] + pvs[i][:NSA_HEAD])
    return out


M_FLOOR = 0.1 * NEG_INF
ONES_ROWS = 16
LOG2_E = math.log2(math.e)


def _flash_init_t(cols):
    return (jnp.full((1, cols), M_FLOOR, f32), jnp.zeros((1, cols), f32), jnp.zeros((NSA_HEAD, cols), f32))


def _lanes(x, n):
    return jnp.concatenate([x] * n, axis=1)


def _nsa_prompt_kernel(qt_ref, gt_ref, kc_ref, vct_ref, ks_ref, vst_ref, kw_ref, vwt_ref, o_ref):
    tq = qt_ref.shape[1]
    n_cmp = kc_ref.shape[0]
    n_sel = ks_ref.shape[0] // SEL_BLOCK
    i = pl.program_id(1)
    t0 = i * tq
    cols = NSA_HPG * tq
    gates = jax.nn.sigmoid(gt_ref[...])
    q_pos = t0 + lax.broadcasted_iota(jnp.int32, (1, tq), 1)
    q_pos_c = _lanes(q_pos, NSA_HPG)
    cmp_end = lax.broadcasted_iota(jnp.int32, (n_cmp, 1), 0) * CMP_STRIDE + (CMP_LEN - 1)
    ratio = SEL_BLOCK // CMP_STRIDE
    ij = lax.broadcasted_iota(jnp.int32, (n_sel, n_cmp), 0)
    ii = lax.broadcasted_iota(jnp.int32, (n_sel, n_cmp), 1)
    imp_mat = jnp.where((ii >= ratio * ij - 1) & (ii <= ratio * ij + ratio - 1), 1.0, 0.0).astype(bf16)
    jt = lax.broadcasted_iota(jnp.int32, (n_sel, tq), 0)
    cur_t = (t0 + lax.broadcasted_iota(jnp.int32, (n_sel, tq), 1)) // SEL_BLOCK
    krow = lax.broadcasted_iota(jnp.int32, (K_TILE, 1), 0)
    er = lax.broadcasted_iota(jnp.int32, (K_TILE, n_sel), 0) // SEL_BLOCK
    ej = lax.broadcasted_iota(jnp.int32, (K_TILE, n_sel), 1)
    groups = range(NSA_KV)
    heads_of = lambda grp: range(grp * NSA_HPG, (grp + 1) * NSA_HPG)
    qts, sels, o_cs = [], [], []

    for grp in groups:
        d0 = grp * NSA_HEAD
        heads = heads_of(grp)
        q_all = jnp.concatenate([qt_ref[h * NSA_HEAD:(h + 1) * NSA_HEAD, :] for h in heads], axis=1)
        qt = (q_all * NSA_SCALE).astype(bf16)
        mask_c = cmp_end <= q_pos_c
        s = jnp.where(mask_c, jnp.dot(kc_ref[:, d0:d0 + NSA_HEAD].astype(bf16), qt, preferred_element_type=f32),
                      NEG_INF)
        e = jnp.where(mask_c, jnp.exp(s - jnp.max(s, axis=0, keepdims=True)), 0.0)
        p_c = e / jnp.maximum(jnp.sum(e, axis=0, keepdims=True), 1e-30)
        o_c = jnp.dot(vct_ref[d0:d0 + NSA_HEAD, :].astype(bf16), p_c.astype(bf16), preferred_element_type=f32)
        p_sum = p_c[:, 0:tq]
        for h in range(1, NSA_HPG):
            p_sum = p_sum + p_c[:, h * tq:(h + 1) * tq]
        hi = p_sum.astype(bf16)
        r1 = p_sum - hi.astype(f32)
        mid = r1.astype(bf16)
        lo = (r1 - mid.astype(f32)).astype(bf16)
        imp = sum(jnp.dot(imp_mat, part, preferred_element_type=f32) for part in (hi, mid, lo))
        score = _block_scores(imp, jt, cur_t)
        rank = jnp.zeros((n_sel, tq), f32)
        for j2 in range(n_sel):
            row = score[j2:j2 + 1, :]
            ge = jnp.where(row >= score, 1.0, 0.0)
            gt = jnp.where(row > score, 1.0, 0.0)
            rank = rank + jnp.where(jt > j2, ge, gt)
        q_exp2 = (q_all * (NSA_SCALE * LOG2_E)).astype(bf16)
        qts.extend(q_exp2[:, n * tq:(n + 1) * tq] for n in range(NSA_HPG))
        sels.append(jnp.where(rank < SEL_TOPK, 1.0, 0.0).astype(bf16))
        o_cs.extend(o_c[:, n * tq:(n + 1) * tq] for n in range(NSA_HPG))

    last = (t0 + tq - 1) // K_TILE
    key_cols = lambda ref, k0: [ref[pl.ds(k0, K_TILE), g * NSA_HEAD:(g + 1) * NSA_HEAD] for g in groups]
    val_rows = lambda ref, k0: [ref[g * NSA_HEAD:(g + 1) * NSA_HEAD, pl.ds(k0, K_TILE)] for g in groups]

    def flash_heads(keys, vals, biases, carry):
        out = ()
        for h0 in range(0, NSA_HEADS, FLASH_HEADS):
            hs = range(h0, h0 + FLASH_HEADS)
            grp_of = [h // NSA_HPG for h in hs]
            ss = _scores_t([qts[h] for h in hs], [keys[g] for g in grp_of], [biases[g] for g in grp_of])
            out += _flash_update_t(ss, [vals[g] for g in grp_of], carry[3 * h0:3 * (h0 + FLASH_HEADS)])
        return out

    def sel_step(kt, carry):
        k0 = pl.multiple_of(kt * K_TILE, K_TILE)
        expand = jnp.where(ej == kt * (K_TILE // SEL_BLOCK) + er, 1.0, 0.0).astype(bf16)
        causal = (k0 + krow) <= q_pos
        biases = []
        for grp in groups:
            chosen = jnp.dot(expand, sels[grp], preferred_element_type=f32)
            biases.append(jnp.where(causal & (chosen > 0.5), 0.0, NEG_INF))
        return flash_heads(key_cols(ks_ref, k0), val_rows(vst_ref, k0), biases, carry)

    sel_acc = lax.fori_loop(0, last + 1, sel_step, _flash_init_t(tq) * NSA_HEADS)

    def win_step(kt, carry):
        k0 = pl.multiple_of(kt * K_TILE, K_TILE)
        dist = q_pos - (k0 + krow)
        bias = jnp.where((dist >= 0) & (dist < WINDOW), 0.0, NEG_INF)
        return flash_heads(key_cols(kw_ref, k0), val_rows(vwt_ref, k0), [bias] * NSA_KV, carry)

    win_acc = lax.fori_loop(jnp.maximum(t0 - WINDOW, 0) // K_TILE, last + 1, win_step,
                            _flash_init_t(tq) * NSA_HEADS)

    for h in range(NSA_HEADS):
        o_s = _flash_out(*sel_acc[3 * h:3 * h + 3])
        o_w = _flash_out(*win_acc[3 * h:3 * h + 3])
        gate = lambda br: gates[3 * h + br:3 * h + br + 1, :]
        o_ref[h * NSA_HEAD:(h + 1) * NSA_HEAD, :] = gate(0) * o_cs[h] + gate(1) * o_s + gate(2) * o_w


def _nsa_prompt(qt, gt, kvc, kvct, kvs, kvst, kvw, kvwt):
    b, _, t = qt.shape
    n_cmp = kvc.shape[1]
    tq = Q_TILE
    assert t % tq == 0 and t % K_TILE == 0 and K_TILE % SEL_BLOCK == 0
    keys =lambda r: pl.BlockSpec((None, r, NSA_KV_COLS), lambda i, j: (i, 0, 0))
    vals = lambda c: pl.BlockSpec((None, NSA_KV_COLS, c), lambda i, j: (i, 1, 0))
    return pl.pallas_call(
        _nsa_prompt_kernel,
        out_shape=jax.ShapeDtypeStruct((b, NSA_WIDTH, t), f32),
        grid=(b, t // tq),
        in_specs=[pl.BlockSpec((None, NSA_WIDTH, tq), lambda i, j: (i, 0, j)),
                  pl.BlockSpec((None, GATE_PAD, tq), lambda i, j: (i, 0, j)),
                  keys(n_cmp), vals(n_cmp), keys(t), vals(t), keys(t), vals(t)],
        out_specs=pl.BlockSpec((None, NSA_WIDTH, tq), lambda i, j: (i, 0, j)),
        compiler_params=_cparams(("parallel", "arbitrary")),
        name="nsa_prompt",
    )(qt, gt, kvc, kvct, kvs, kvst, kvw, kvwt)


def _nsa_sample_kernel(past_len, pt_ref, q_ref, gate_ref, kvc_ref, cache_ref, kvs_new_ref, win_ref, kvw_new_ref,
                       o_ref, selbufs_ref, winbuf_ref, qbuf_ref, gbuf_ref, obuf_ref, sem):
    b = pl.program_id(0)
    t_new = q_ref.shape[0]
    tp = qbuf_ref.shape[0]
    n_pages = pt_ref.shape[1]
    page = cache_ref.shape[2]
    n_cmp = kvc_ref.shape[0]
    n_win = win_ref.shape[1]
    total = past_len + t_new
    n_sel = -(-total // SEL_BLOCK)
    n_sel_pad = -(-n_sel // LANES) * LANES
    tile_pages = S_KTILE // page
    tile_blocks = S_KTILE // SEL_BLOCK
    rows = NSA_HPG * tp

    slot = b % 2
    selbuf_ref = selbufs_ref.at[slot]

    def page_copy(seq, i, into):
        return pltpu.make_async_copy(cache_ref.at[pt_ref[seq, i]], selbufs_ref.at[into, i],
                                     sem.at[into, i // tile_pages])

    def start_all(seq, into):
        def start(i, carry):
            page_copy(seq, i, into).start()
            return carry
        lax.fori_loop(0, n_pages, start, 0)

    @pl.when(b == 0)
    def _():
        start_all(0, 0)

    @pl.when(b + 1 < pl.num_programs(0))
    def _():
        start_all(b + 1, 1 - slot)

    src = lax.broadcasted_iota(jnp.int32, (kvs_new_ref.shape[1], LANES), 0)
    dst = lax.broadcasted_iota(jnp.int32, (kvs_new_ref.shape[1], LANES), 1)
    pick = jnp.where((src == b * t_new + dst) & (dst < t_new), 1.0, 0.0).astype(bf16)
    selbuf_ref[n_pages] = _dot(kvs_new_ref[...], pick)
    winbuf_ref[:, 0:n_win] = win_ref[...]
    winbuf_ref[:, n_win:n_win + LANES] = _dot(kvw_new_ref[...], pick)
    qbuf_ref[...] = jnp.zeros(qbuf_ref.shape, f32)
    qbuf_ref[0:t_new, :] = q_ref[...]
    gbuf_ref[...] = jnp.zeros(gbuf_ref.shape, f32)
    gbuf_ref[0:t_new, :] = gate_ref[...]

    q = qbuf_ref[...] * NSA_SCALE
    gates = jax.nn.sigmoid(gbuf_ref[...])
    q_pos = past_len + lax.broadcasted_iota(jnp.int32, (tp, 1), 0)
    q_pos_r = jnp.concatenate([q_pos] * NSA_HPG, axis=0)
    cmp_end = lax.broadcasted_iota(jnp.int32, (1, n_cmp), 1) * CMP_STRIDE + (CMP_LEN - 1)
    imp_mat = _importance_matrix(n_cmp, n_sel_pad)
    jj = lax.broadcasted_iota(jnp.int32, (tp, n_sel_pad), 1)
    expand = jnp.where(lax.broadcasted_iota(jnp.int32, (tile_blocks, S_KTILE), 0)
                       == lax.broadcasted_iota(jnp.int32, (tile_blocks, S_KTILE), 1) // SEL_BLOCK,
                       1.0, 0.0).astype(bf16)

    def flash(qs, k_t, v_t, mask, m, l, acc):
        s = jnp.where(mask, jnp.dot(qs, k_t.astype(bf16), preferred_element_type=f32), NEG_INF)
        m_new = jnp.maximum(m, jnp.max(s, axis=-1, keepdims=True))
        alpha = jnp.exp(m - m_new)
        e = jnp.where(mask, jnp.exp(s - m_new), 0.0)
        pv = lax.dot_general(e.astype(bf16), v_t.astype(bf16), _NT, preferred_element_type=f32)
        return m_new, alpha * l + jnp.sum(e, axis=-1, keepdims=True), alpha * acc + pv

    qss, sels, o_cs = [], [], []
    for grp in range(NSA_KV):
        kcol0, vcol0 = grp * NSA_HEAD, NSA_KV_COLS + grp * NSA_HEAD
        qs = _stack_heads(q, grp, tp).astype(bf16)
        p_c = _softmax_rows(_dot_nt(qs, kvc_ref[:, kcol0:kcol0 + NSA_HEAD]), cmp_end <= q_pos_r)
        o_cs.append(_dot(p_c, kvc_ref[:, vcol0:vcol0 + NSA_HEAD]))
        p_sum = p_c[0:tp]
        for h in range(1, NSA_HPG):
            p_sum = p_sum + p_c[h * tp:(h + 1) * tp]
        score = _block_scores(_dot_sel(p_sum, imp_mat), jj, q_pos // SEL_BLOCK)
        rank = jnp.zeros((tp, n_sel_pad), f32)
        for j2 in range(n_sel):
            col = score[:, j2:j2 + 1]
            ge = jnp.where(col >= score, 1.0, 0.0)
            gt = jnp.where(col > score, 1.0, 0.0)
            rank = rank + jnp.where(jj > j2, ge, gt)
        sel = jnp.where((rank < SEL_TOPK) & (jj < n_sel), 1.0, 0.0).astype(bf16)
        sels.append(jnp.concatenate([sel] * NSA_HPG, axis=0))
        qss.append(qs)

    def wait(i, carry):
        page_copy(b, i, slot).wait()
        return carry

    lax.fori_loop(0, n_pages, wait, 0)
    n_tiles = past_len // S_KTILE
    tiles = [(kt * S_KTILE, S_KTILE if kt < n_tiles else LANES) for kt in range(n_tiles + 1)]
    probs = [(grp, kt) for kt in range(n_tiles + 1) for grp in range(NSA_KV)]

    def tile_rows(kt, r0):
        pages = range(kt * tile_pages, (kt + 1) * tile_pages) if kt < n_tiles else [n_pages]
        return jnp.concatenate([selbuf_ref[p, r0:r0 + NSA_HEAD, :] for p in pages], axis=1)

    masks, ss = {}, {}
    for grp, kt in probs:
        k0, width = tiles[kt]
        k_pos = k0 + lax.broadcasted_iota(jnp.int32, (1, width), 1)
        blocks = sels[grp][:, kt * tile_blocks:(kt + 1) * tile_blocks]
        chosen = jnp.dot(blocks, expand[:, 0:width], preferred_element_type=f32)
        masks[grp, kt] = (chosen > 0.5) & (k_pos <= q_pos_r)
        k_t = tile_rows(kt, grp * NSA_HEAD)
        ss[grp, kt] = jnp.where(masks[grp, kt], jnp.dot(qss[grp], k_t.astype(bf16), preferred_element_type=f32),
                                NEG_INF)
    row_max = [functools.reduce(jnp.maximum, [jnp.max(ss[grp, kt], axis=-1, keepdims=True)
                                               for kt in range(n_tiles + 1)]) for grp in range(NSA_KV)]
    es = {p: jnp.where(masks[p], jnp.exp(ss[p] - row_max[p[0]]), 0.0) for p in probs}
    pvs = {}
    for grp, kt in probs:
        k0, width = tiles[kt]
        v_t = tile_rows(kt, NSA_KV_COLS + grp * NSA_HEAD)
        pvs[grp, kt] = lax.dot_general(es[grp, kt].astype(bf16), v_t.astype(bf16), _NT, preferred_element_type=f32)

    w_pos = past_len - n_win + lax.broadcasted_iota(jnp.int32, (1, winbuf_ref.shape[1]), 1)
    dist = jnp.where(w_pos >= 0, q_pos_r - w_pos, -1)
    for grp in range(NSA_KV):
        krow0, vrow0 = grp * NSA_HEAD, NSA_KV_COLS + grp * NSA_HEAD
        denom = sum(jnp.sum(es[grp, kt], axis=-1, keepdims=True) for kt in range(n_tiles + 1))
        o_s = sum(pvs[grp, kt] for kt in range(n_tiles + 1)) / jnp.maximum(denom, 1e-30)
        o_w = _flash_out(*flash(qss[grp], winbuf_ref[krow0:krow0 + NSA_HEAD, :], winbuf_ref[vrow0:vrow0 + NSA_HEAD, :],
                                (dist >= 0) & (dist < WINDOW), *_flash_init(rows)))
        o = (_stack_gates(gates, grp, 0) * o_cs[grp] + _stack_gates(gates, grp, 1) * o_s
             + _stack_gates(gates, grp, 2) * o_w)
        for h in range(NSA_HPG):
            col = (grp * NSA_HPG + h) * NSA_HEAD
            obuf_ref[:, col:col + NSA_HEAD] = o[h * tp:(h + 1) * tp]
    o_ref[...] = obuf_ref[0:t_new, :]


def _nsa_sample(p_q, p_gate, kvc_cmp, cache_sel_t, page_table, kvs_new_t, win_state_t, kvw_new_t):
    b, t_new, _ = p_q.shape
    n_pages = page_table.shape[1]
    page = cache_sel_t.shape[2]
    past_len = n_pages * page
    n_cmp = kvc_cmp.shape[1]
    n_win = win_state_t.shape[2]
    tp = 8
    assert t_new <= tp and past_len % S_KTILE == 0 and S_KTILE % page == 0 and n_win % LANES == 0
    assert b * t_new == kvs_new_t.shape[1] and page == LANES
    n_sel_pad = -(-(-(-(past_len + t_new) // SEL_BLOCK)) // LANES) * LANES
    assert (past_len // S_KTILE + 1) * (S_KTILE // SEL_BLOCK) <= n_sel_pad
    per_seq = lambda n, w: pl.BlockSpec((None, n, w), lambda i, pt: (i, 0, 0))
    whole = lambda a: pl.BlockSpec(a.shape, lambda i, pt: (0,) * a.ndim, pipeline_mode=pl.Buffered(1))
    return pl.pallas_call(
        functools.partial(_nsa_sample_kernel, past_len),
        out_shape=jax.ShapeDtypeStruct((b, t_new, NSA_WIDTH), f32),
        grid_spec=pltpu.PrefetchScalarGridSpec(
            num_scalar_prefetch=1, grid=(b,),
            in_specs=[per_seq(t_new, NSA_WIDTH), per_seq(t_new, GATE_PAD), per_seq(n_cmp, KV_COLS),
                      pl.BlockSpec(memory_space=pl.ANY), whole(kvs_new_t), per_seq(KV_COLS, n_win),
                      whole(kvw_new_t)],
            out_specs=per_seq(t_new, NSA_WIDTH),
            scratch_shapes=[pltpu.VMEM((2, n_pages + 1, KV_COLS, page), f32), pltpu.VMEM((KV_COLS, n_win + LANES), f32),
                            pltpu.VMEM((tp, NSA_WIDTH), f32), pltpu.VMEM((tp, GATE_PAD), f32),
                            pltpu.VMEM((tp, NSA_WIDTH), f32), pltpu.SemaphoreType.DMA((2, past_len // S_KTILE))]),
        compiler_params=_cparams(("arbitrary",)),
        name="nsa_sample",
    )(page_table, p_q, p_gate, kvc_cmp, cache_sel_t, kvs_new_t, win_state_t, kvw_new_t)


def _prep_weights(l, depth, ln_g, ln_b, ffn_w_gate, ffn_w_up, ffn_w_down, w_in, b_in, rw_mu, rw_w0, rw_w2, rw_a0,
                  rw_a2, rw_g2, rw_k_k, rw_k_a, rw_r_k, rw_ln_w, rw_ln_b, nsa_phi_pe, nsa_phi_w1, nsa_phi_w2,
                  w_out_a, w_out_b, w_o):
    d = D_MODEL
    nc = D_FF // FF_CHUNK
    up = lambda w: w.astype(bf16).reshape(d, nc, FF_CHUNK).transpose(1, 0, 2)
    down = lambda w: w.astype(bf16).reshape(nc, FF_CHUNK, d)
    c1 = RW_SHIFT
    c2 = c1 + NSA_WIDTH
    c3 = c2 + 3 * KV_COLS
    c4 = c3 + N_GATE
    cols = lambda a: jnp.concatenate(
        [a[..., :c3], a[..., c4:], a[..., c3:c4], jnp.zeros(a.shape[:-1] + (GATE_PAD - N_GATE,), a.dtype)], axis=-1)
    z = lambda r, c: jnp.zeros((r, c), f32)
    w = RW_WIDTH
    lora = jnp.concatenate([
        jnp.concatenate([rw_w2[l], z(W_LORA, w), z(W_LORA, w)], axis=1),
        jnp.concatenate([z(A_LORA, w), rw_a2[l], z(A_LORA, w)], axis=1),
        jnp.concatenate([z(G_LORA, w), z(G_LORA, w), rw_g2[l]], axis=1)], axis=0)
    hid = lax.broadcasted_iota(jnp.int32, (w, w), 0) // RW_HEAD == lax.broadcasted_iota(jnp.int32, (w, w), 1) // RW_HEAD
    vec = jnp.stack([rw_w0[l], rw_a0[l], rw_k_k[l], rw_k_a[l], rw_r_k[l].reshape(w), rw_ln_w[l], rw_ln_b[l],
                     jnp.zeros((w,), f32)])
    w1 = nsa_phi_w1[l].reshape(2, 2, CMP_STRIDE, NSA_HEAD, CMP_HID)
    eye = jnp.eye(NSA_KV, dtype=f32)
    big = jnp.einsum("kljde,kK,gG->jkgdKGle", w1, jnp.eye(2, dtype=f32), eye)
    per_kv = jnp.einsum("kljde,gG->kjgdGle", w1, eye)
    pe = jnp.broadcast_to(nsa_phi_pe[l].reshape(2, 1, CMP_LEN * NSA_HEAD), (2, 8, CMP_LEN * NSA_HEAD))
    w_cols, b_cols = cols(w_in[l]), cols(b_in[l])
    seg_t = _segments(_PROJ, [name for name, _ in _PROJ_T])
    w_t = jnp.concatenate([w_cols[:, lo:hi] for lo, hi in seg_t], axis=1).T
    b_t = jnp.concatenate([b_cols[lo:hi] for lo, hi in seg_t])
    return dict(
        alpha=(2 * depth) ** 0.25,
        wg1=up(ffn_w_gate[l, 0]), wu1=up(ffn_w_up[l, 0]), wd1=down(ffn_w_down[l, 0]),
        wg2=up(ffn_w_gate[l, 1]), wu2=up(ffn_w_up[l, 1]), wd2=down(ffn_w_down[l, 1]),
        ln_g=ln_g[l], ln_b=ln_b[l],
        w_in=w_cols.astype(bf16), b_in=b_cols.reshape(1, -1),
        w_in_t=w_t.astype(bf16), b_in_t=b_t[:, None],
        phi_w2t=jnp.swapaxes(nsa_phi_w2[l], 1, 2).astype(bf16),
        rw_mu=rw_mu[l].reshape(1, -1), rw_lora=lora.astype(bf16), rw_vec=vec, rw_hsum=hid.astype(bf16),
        phi_big=big.reshape(CMP_ROW, 2 * NSA_KV * 2 * CMP_HID).astype(bf16), phi_pe=pe,
        phi_kv=per_kv.reshape(2, CMP_STRIDE * NSA_KV_COLS, NSA_KV * 2 * CMP_HID).astype(bf16),
        phi_w1f=nsa_phi_w1[l].reshape(2, CMP_LEN * NSA_HEAD, CMP_HID).astype(bf16),
        phi_w2=nsa_phi_w2[l].astype(bf16),
        w_out_a=w_out_a[l].astype(bf16), w_out_b=w_out_b[l].astype(bf16), w_o=w_o[l].astype(bf16))


def _kv_state(p):
    return p.reshape(p.shape[:-1] + (2, NSA_KV, NSA_HEAD))


def _tokens_last(a):
    n, t = a.shape[:2]
    return a.transpose(0, 2, 3, 4, 1).reshape(n, KV_COLS, t)


def _kv_state_t(p_t):
    b, _, t = p_t.shape
    return p_t.reshape(b, 2, NSA_KV, NSA_HEAD, t).transpose(0, 4, 1, 2, 3)


def kernel(x_prompt, x_sample, c_prompt, c_sample, cache_kv_cmp, cache_kv_sel, state_kv_win, state_wkv, state_shift, page_table, w_ada, b_ada, ln_g, ln_b, ffn_w_gate, ffn_w_up, ffn_w_down, w_in, b_in, rw_mu, rw_w0, rw_w2, rw_a0, rw_a2, rw_g2, rw_k_k, rw_k_a, rw_r_k, rw_ln_w, rw_ln_b, nsa_phi_pe, nsa_phi_w1, nsa_phi_w2, w_out_a, w_out_b, w_o):
    bp, seq, d = x_prompt.shape
    bd, t_new, _ = x_sample.shape
    depth = w_ada.shape[0]
    n_phys, page = cache_kv_cmp.shape[1:3]
    n_pages = page_table.shape[1]
    n_win = state_kv_win.shape[2]
    rows_s = bd * t_new
    tm_p = min(TM_PROJ, seq)
    tm_ffn = min(TM_FFN, seq)
    assert seq % tm_p == 0 and seq % tm_ffn == 0 and seq % CMP_STRIDE == 0 and page % CMP_STRIDE == 0

    y_p, y_s = x_prompt, x_sample.reshape(1, rows_s, d)
    st_p, st_s = [], []
    for l in range(depth):
        wts = _prep_weights(l, depth, ln_g, ln_b, ffn_w_gate, ffn_w_up, ffn_w_down, w_in, b_in, rw_mu, rw_w0,
                            rw_w2, rw_a0, rw_a2, rw_g2, rw_k_k, rw_k_a, rw_r_k, rw_ln_w, rw_ln_b, nsa_phi_pe,
                            nsa_phi_w1, nsa_phi_w2, w_out_a, w_out_b, w_o)
        mod = _ada_mod(jnp.concatenate([c_prompt, c_sample], axis=0), w_ada[l], b_ada[l])
        mod_p = mod[:bp].reshape(bp, 9, 1, d)
        mod_s = jnp.repeat(mod[bp:].reshape(bd, 9, d), t_new, axis=0).reshape(rows_s, 9, d)
        mod_s = mod_s.transpose(1, 0, 2).reshape(1, 9, rows_s, d)

        x1 = _ffn_block(y_p, mod_p, wts, tm_ffn, 0)
        p_rw, p_kc, p_ks, p_kw, p_m, q_t, g_t, ks_t, kw_t = _in_proj(
            x1, mod_p, wts, tm_p, ("rw", "kc", "ks", "kw", "merge"), ("q", "gate", "ks", "kw"))
        ya, wkv_p = _rwkv(p_rw, jnp.zeros((bp, RW_SHIFT), f32), jnp.zeros((bp, RW_HEADS, RW_HEAD, RW_HEAD), f32),
                          wts)
        kvc_cmp, kvc_cmp_t = _compress_dense(p_kc.reshape(bp, seq // CMP_STRIDE, CMP_ROW), wts)
        yb_t = _nsa_prompt(q_t, g_t, kvc_cmp, kvc_cmp_t, p_ks, ks_t, p_kw, kw_t)
        y_p = _ffn_block(_mix(x1, ya, yb_t, p_m, mod_p, wts, tm_p, True), mod_p, wts, tm_ffn, 1)
        n_keep = min(WINDOW, seq)
        st_p.append((_kv_state(p_kc), _kv_state_t(ks_t), _kv_state_t(kw_t[:, :, seq - n_keep:]), wkv_p,
                     p_rw[:, -1]))

        x1 = _ffn_block(y_s, mod_s, wts, rows_s, 0)
        p_rw, p_q, p_kc, p_ks, p_m, p_g, ks_t, kw_t = _in_proj(
            x1, mod_s, wts, rows_s, ("rw", "q", "kc", "ks", "merge", "gate"), ("ks", "kw"))
        per_seq = lambda a: a.reshape(bd, t_new, a.shape[-1])
        p_rw, p_q, p_kc, p_ks, p_g = map(per_seq, (p_rw, p_q, p_kc, p_ks, p_g))
        ya, wkv_s = _rwkv(p_rw, state_shift[l], state_wkv[l], wts)
        kvc_cmp = _compress_paged(_tokens_last(cache_kv_cmp[l]), page_table, wts)
        win_state_t = _tokens_last(state_kv_win[l])
        yb = _nsa_sample(p_q, p_g, kvc_cmp, _tokens_last(cache_kv_sel[l]), page_table, ks_t[0], win_state_t,
                         kw_t[0])
        x2 = _mix(x1, ya.reshape(1, rows_s, RW_WIDTH), yb.reshape(1, rows_s, NSA_WIDTH), p_m, mod_s, wts, rows_s,
                  False)
        y_s = _ffn_block(x2, mod_s, wts, rows_s, 1)
        kw_new_t = kw_t[0].reshape(KV_COLS, bd, t_new).transpose(1, 0, 2)
        new_win_t = jnp.concatenate([win_state_t, kw_new_t], axis=2)[:, :, t_new:]
        st_s.append((_kv_state(p_kc), _kv_state(p_ks), _kv_state_t(new_win_t), wkv_s, p_rw[:, -1]))

    stack = lambda sts, i: jnp.stack([s[i] for s in sts])
    return (y_p, y_s.reshape(bd, t_new, d)) + tuple(stack(st_p, i) for i in range(5)) + tuple(
        stack(st_s, i) for i in range(5))
```

```python
import functools
import math

import jax
import jax.numpy as jnp
from jax import lax
from jax.experimental import pallas as pl
from jax.experimental.pallas import tpu as pltpu

f32 = jnp.float32
bf16 = jnp.bfloat16

D_MODEL = 1024
RW_HEAD = 64
RW_HEADS = 8
RW_WIDTH = RW_HEADS * RW_HEAD
W_LORA = 64
A_LORA = 64
G_LORA = 128
LORA_IN = W_LORA + A_LORA + G_LORA
RW_GN_EPS = 64e-5
RW_SHIFT = 3 * RW_WIDTH + LORA_IN
NSA_HEAD = 64
NSA_HEADS = 8
NSA_KV = 2
NSA_HPG = NSA_HEADS // NSA_KV
NSA_WIDTH = NSA_HEADS * NSA_HEAD
NSA_KV_COLS = NSA_KV * NSA_HEAD
KV_COLS = 2 * NSA_KV_COLS
NSA_SCALE = NSA_HEAD ** -0.5
CMP_STRIDE = 16
CMP_LEN = 2 * CMP_STRIDE
CMP_HID = 128
CMP_ROW = CMP_STRIDE * KV_COLS
SEL_BLOCK = 64
SEL_TOPK = 16
WINDOW = 512
D_FF = 2816
N_GATE = 3 * NSA_HEADS
LN_EPS = 1e-5
NEG_INF = -1e30
BIG = 1e9

LANES = 128
GATE_PAD = LANES
FF_CHUNK = 256
TM_FFN = 1024
TM_PROJ = 512
RW_CHUNK = 64
RW_HPB = 4
RW_SEQS = 4
Q_TILE = 256
K_TILE = 256
FLASH_HEADS = 8
S_KTILE = 2048
VMEM_LIMIT = 56 * 1024 * 1024

_NT = (((1,), (1,)), ((), ()))
_TN = (((0,), (0,)), ((), ()))
_HI = lax.Precision.HIGHEST


def _dot(a, b):
    return jnp.dot(a.astype(bf16), b.astype(bf16), preferred_element_type=f32)


def _dot_nt(a, b):
    return lax.dot_general(a.astype(bf16), b.astype(bf16), _NT, preferred_element_type=f32)


def _dot_f32(a, b, dims=None):
    if dims is None:
        return jnp.dot(a, b, preferred_element_type=f32, precision=_HI)
    return lax.dot_general(a, b, dims, preferred_element_type=f32, precision=_HI)


def _dot_sel(x, m01):
    hi = x.astype(bf16)
    r1 = x - hi.astype(f32)
    mid = r1.astype(bf16)
    lo = (r1 - mid.astype(f32)).astype(bf16)
    d = lambda t: jnp.dot(t, m01, preferred_element_type=f32)
    return d(hi) + d(mid) + d(lo)


def _layer_norm(x, g, b):
    mu = jnp.mean(x, axis=-1, keepdims=True)
    xc = x - mu
    var = jnp.mean(xc * xc, axis=-1, keepdims=True)
    return xc * lax.rsqrt(var + LN_EPS) * g + b


def _cparams(sem):
    return pltpu.CompilerParams(dimension_semantics=sem, vmem_limit_bytes=VMEM_LIMIT)


def _resident(shape):
    nd = len(shape)
    return pl.BlockSpec(shape, lambda *_: (0,) * nd, pipeline_mode=pl.Buffered(1))


def _ada_kernel(c_ref, w_ref, b_ref, o_ref):
    c = c_ref[...]
    o_ref[...] = _dot(c * jax.nn.sigmoid(c), w_ref[...]) + b_ref[...]


def _ada_mod(c, w_ada, b_ada):
    rows, d = c.shape
    n = w_ada.shape[1]
    tn = d
    return pl.pallas_call(
        _ada_kernel,
        out_shape=jax.ShapeDtypeStruct((rows, n), f32),
        grid=(n // tn,),
        in_specs=[pl.BlockSpec((rows, d), lambda j: (0, 0)),
                  pl.BlockSpec((d, tn), lambda j: (0, j)),
                  pl.BlockSpec((1, tn), lambda j: (0, j))],
        out_specs=pl.BlockSpec((rows, tn), lambda j: (0, j)),
        compiler_params=_cparams(("parallel",)),
        name="ada_mod",
    )(c, w_ada, b_ada.reshape(1, n))


def _ffn(u, wg_ref, wu_ref, wd_ref, acc_ref):
    ub = u.astype(bf16)
    acc_ref[...] = jnp.zeros(acc_ref.shape, f32)

    for lo in range(0, wg_ref.shape[1], FF_CHUNK):
        hg = jnp.dot(ub, wg_ref[:, lo:lo + FF_CHUNK], preferred_element_type=f32)
        hu = jnp.dot(ub, wu_ref[:, lo:lo + FF_CHUNK], preferred_element_type=f32)
        h = hg * jax.nn.sigmoid(hg) * hu
        acc_ref[...] += jnp.dot(h.astype(bf16), wd_ref[lo:lo + FF_CHUNK, :], preferred_element_type=f32)
    return acc_ref[...]


def _ffn_block_kernel(alpha, half, x_ref, mod_ref, wg_ref, wu_ref, wd_ref, lng_ref, lnb_ref, o_ref, acc_ref):
    m0 = 6 * half
    ln = 2 * half
    x = x_ref[...]
    u = x * (1.0 + mod_ref[m0 + 1]) + mod_ref[m0]
    f = _ffn(u, wg_ref, wu_ref, wd_ref, acc_ref)
    o_ref[...] = _layer_norm(alpha * x + 0.5 * (1.0 + mod_ref[m0 + 2]) * f, lng_ref[ln:ln + 1, :],
                             lnb_ref[ln:ln + 1, :])


def _ffn_block(x, mod, wts, tm, half):
    g, t, d = x.shape
    r = mod.shape[2]
    row = pl.BlockSpec((None, tm, d), lambda i, j: (i, j, 0))
    names = ("wg1", "wu1", "wd1") if half == 0 else ("wg2", "wu2", "wd2")
    weights = [wts[n] for n in names] + [wts["ln_g"], wts["ln_b"]]
    return pl.pallas_call(
        functools.partial(_ffn_block_kernel, wts["alpha"], half),
        out_shape=jax.ShapeDtypeStruct((g, t, d), f32),
        grid=(g, t // tm),
        in_specs=[row, pl.BlockSpec((None, 9, r, d), lambda i, j: (i, 0, 0, 0))] + [_resident(w.shape) for w in weights],
        out_specs=row,
        scratch_shapes=[pltpu.VMEM((tm, d), f32)],
        compiler_params=_cparams(("parallel", "parallel")),
        name="ffn_block",
    )(x, mod, *weights)


def _in_proj_kernel(seg, seg_t, x1_ref, mod_ref, win_ref, bin_ref, wint_ref, bint_ref, *out_refs):
    u2 = (x1_ref[...] * (1.0 + mod_ref[4]) + mod_ref[3]).astype(bf16)
    for o_ref, (lo, hi) in zip(out_refs, seg):
        o_ref[...] = jnp.dot(u2, win_ref[:, lo:hi], preferred_element_type=f32) + bin_ref[:, lo:hi]
    for o_ref, (lo, hi) in zip(out_refs[len(seg):], seg_t):
        o_ref[...] = lax.dot_general(wint_ref[lo:hi, :], u2, _NT, preferred_element_type=f32) + bint_ref[lo:hi, :]


_PROJ = (("rw", RW_SHIFT), ("q", NSA_WIDTH), ("kc", KV_COLS), ("ks", KV_COLS), ("kw", KV_COLS),
         ("merge", 2 * D_MODEL), ("gate", GATE_PAD))
_PROJ_T = (("q", NSA_WIDTH), ("gate", GATE_PAD), ("ks", KV_COLS), ("kw", KV_COLS))


def _segments(table, names):
    seg, lo = {}, 0
    for name, w in table:
        seg[name] = (lo, lo + w)
        lo += w
    return tuple(seg[n] for n in names)


def _in_proj(x1, mod, wts, tm, names, names_t):
    g, t, d = x1.shape
    r = mod.shape[2]
    seg, seg_t = _segments(_PROJ, names), _segments(_PROJ_T, names_t)
    row = lambda w: pl.BlockSpec((None, tm, w), lambda i, j: (i, j, 0))
    col = lambda w: pl.BlockSpec((None, w, tm), lambda i, j: (i, 0, j))
    weights = [wts["w_in"], wts["b_in"], wts["w_in_t"], wts["b_in_t"]]
    return pl.pallas_call(
        functools.partial(_in_proj_kernel, seg, seg_t),
        out_shape=([jax.ShapeDtypeStruct((g, t, hi - lo), f32) for lo, hi in seg]
                   + [jax.ShapeDtypeStruct((g, hi - lo, t), f32) for lo, hi in seg_t]),
        grid=(g, t // tm),
        in_specs=[row(d), pl.BlockSpec((None, 9, r, d), lambda i, j: (i, 0, 0, 0))]
                 + [_resident(w.shape) for w in weights],
        out_specs=[row(hi - lo) for lo, hi in seg] + [col(hi - lo) for lo, hi in seg_t],
        compiler_params=_cparams(("parallel", "parallel")),
        name="in_proj",
    )(x1, mod, *weights)


def _mix_kernel(alpha, yb_on_lanes, x1_ref, ya_ref, yb_ref, pm_ref, mod_ref, woa_ref, wob_ref, wo_ref,
                lng_ref, lnb_ref, o_ref):
    x1 = x1_ref[...]
    pm = pm_ref[...]
    d = x1.shape[-1]
    ga = jax.nn.sigmoid(pm[:, :d])
    gb = jax.nn.sigmoid(pm[:, d:])
    if yb_on_lanes:
        yb_out = lax.dot_general(yb_ref[...].astype(bf16), wob_ref[...], _TN, preferred_element_type=f32)
    else:
        yb_out = _dot(yb_ref[...], wob_ref[...])
    merged = ga * _dot(ya_ref[...], woa_ref[...]) + gb * yb_out
    m = _dot(merged, wo_ref[...])
    o_ref[...] = _layer_norm(alpha * x1 + (1.0 + mod_ref[5]) * m, lng_ref[1:2, :], lnb_ref[1:2, :])


def _mix(x1, ya, yb, pm, mod, wts, tm, yb_on_lanes):
    g, t, d = x1.shape
    r = mod.shape[2]
    row = lambda w: pl.BlockSpec((None, tm, w), lambda i, j: (i, j, 0))
    yb_spec = pl.BlockSpec((None, NSA_WIDTH, tm), lambda i, j: (i, 0, j)) if yb_on_lanes else row(NSA_WIDTH)
    weights = [wts["w_out_a"], wts["w_out_b"], wts["w_o"], wts["ln_g"], wts["ln_b"]]
    return pl.pallas_call(
        functools.partial(_mix_kernel, wts["alpha"], yb_on_lanes),
        out_shape=jax.ShapeDtypeStruct((g, t, d), f32),
        grid=(g, t // tm),
        in_specs=[row(d), row(ya.shape[-1]), yb_spec, row(pm.shape[-1]),
                  pl.BlockSpec((None, 9, r, d), lambda i, j: (i, 0, 0, 0))] + [_resident(w.shape) for w in weights],
        out_specs=row(d),
        compiler_params=_cparams(("parallel", "parallel")),
        name="mix",
    )(x1, ya, yb, pm, mod, *weights)


def _rwkv_kernel(t_blk, p_ref, prev_ref, s0_ref, mu_ref, wl_ref, vec_ref, hsum_ref,
                 y_ref, sout_ref, state_ref, prevrow_ref, stage_ref, yn_ref):
    n_seq = p_ref.shape[0]
    ci = pl.program_id(1)

    @pl.when(ci == 0)
    def _():
        state_ref[...] = jnp.zeros(state_ref.shape, f32)
        for s in range(n_seq):
            for h in range(RW_HEADS):
                d0 = (h % RW_HPB) * RW_HEAD
                state_ref[s, h // RW_HPB, d0:d0 + RW_HEAD, d0:d0 + RW_HEAD] = s0_ref[s, h]
        prevrow_ref[...] = prev_ref[...]

    c = stage_ref.shape[1]
    gw = RW_HPB * RW_HEAD
    rows = RW_HPB * c
    br = lax.broadcasted_iota(jnp.int32, (rows, gw), 0)
    bc = lax.broadcasted_iota(jnp.int32, (rows, gw), 1)
    same_head = (br // c) == (bc // RW_HEAD)
    tr = lax.broadcasted_iota(jnp.int32, (rows, rows), 0) % c
    tc = lax.broadcasted_iota(jnp.int32, (rows, rows), 1) % c
    strict = tr > tc
    incl = tr >= tc
    blocks = lambda x: jnp.where(same_head, jnp.concatenate([x] * RW_HPB, axis=0), 0.0).astype(bf16)
    nt = lambda x, y: lax.dot_general(x, y, _NT, preferred_element_type=f32)
    tn = lambda x, y: lax.dot_general(x, y, _TN, preferred_element_type=f32)
    mm = lambda x, y: jnp.dot(x, y, preferred_element_type=f32)
    n_fac = max(1, (c - 1).bit_length())

    hsum = hsum_ref[...]

    def head_sums(xs):
        out = _dot_sel(jnp.concatenate(xs, axis=0), hsum)
        return [out[i * c:(i + 1) * c] for i in range(len(xs))]

    raw = [_rwkv_inputs(t_blk, p_ref.at[s], mu_ref, wl_ref, vec_ref, prevrow_ref.at[s], stage_ref.at[s])
           for s in range(n_seq)]
    sums = head_sums([d["kk"] * d["kk"] for d in raw] + [d["rk"] for d in raw])
    pre = [_rwkv_features(t_blk, raw[s], sums[s], sums[n_seq + s]) for s in range(n_seq)]
    chains = [(s, grp) for s in range(n_seq) for grp in range(RW_HEADS // RW_HPB)]
    ops, low, u = {}, {}, {}
    for s, grp in chains:
        gl = slice(grp * gw, (grp + 1) * gw)
        ops[s, grp] = [blocks(pre[s][name][:, gl]) for name in ("at", "rt", "bt", "kt", "bh", "kh", "v")]
    for ch in chains:
        at4, rt4, bt4, kt4, bh4, kh4, v4 = ops[ch]
        s_b = state_ref[ch].astype(bf16)
        low[ch] = jnp.where(strict, nt(at4, bt4), 0.0).astype(bf16)
        u[ch] = nt(at4, s_b) + mm(jnp.where(strict, nt(at4, kt4), 0.0).astype(bf16), v4)
    for i in range(n_fac):
        for ch in chains:
            u[ch] = u[ch] + mm(low[ch], u[ch].astype(bf16))
        if i + 1 < n_fac:
            for ch in chains:
                low[ch] = mm(low[ch], low[ch]).astype(bf16)
    for ch in chains:
        s, grp = ch
        gl = slice(grp * gw, (grp + 1) * gw)
        at4, rt4, bt4, kt4, bh4, kh4, v4 = ops[ch]
        s_bd = state_ref[ch]
        u_b = u[ch].astype(bf16)
        y4 = (nt(rt4, s_bd.astype(bf16)) + mm(jnp.where(incl, nt(rt4, bt4), 0.0).astype(bf16), u_b)
              + mm(jnp.where(incl, nt(rt4, kt4), 0.0).astype(bf16), v4))
        state_ref[ch] = s_bd * pre[s]["p_last"][:, gl] + tn(u_b, bh4) + tn(v4, kh4)
        yg = y4[0:c]
        for hh in range(1, RW_HPB):
            yg = yg + y4[hh * c:(hh + 1) * c]
        yn_ref[s, :, gl] = yg

    ln_w, ln_b = vec_ref[5:6, :], vec_ref[6:7, :]
    means = head_sums([yn_ref[s] for s in range(n_seq)])
    ycs = [yn_ref[s] - means[s] * (1.0 / RW_HEAD) for s in range(n_seq)]
    sqs = head_sums([yc * yc for yc in ycs])
    for s in range(n_seq):
        var = sqs[s] * (1.0 / RW_HEAD)
        y = (ycs[s] * lax.rsqrt(var + RW_GN_EPS) * ln_w + ln_b + pre[s]["bonus"]) * pre[s]["g"]
        y_ref[s] = y[0:t_blk, :] if t_blk < c else y

    @pl.when(ci == pl.num_programs(1) - 1)
    def _():
        for s in range(n_seq):
            for h in range(RW_HEADS):
                d0 = (h % RW_HPB) * RW_HEAD
                sout_ref[s, h] = state_ref[s, h // RW_HPB, d0:d0 + RW_HEAD, d0:d0 + RW_HEAD]


def _rwkv_inputs(t_blk, p_ref, mu_ref, wl_ref, vec_ref, prevrow_ref, stage_ref):
    c = stage_ref.shape[0]
    w = RW_WIDTH
    if t_blk < c:
        stage_ref[...] = jnp.zeros(stage_ref.shape, f32)
        stage_ref[0:t_blk, :] = p_ref[...]
        p = stage_ref[...]
    else:
        p = p_ref[...]
    ridx = lax.broadcasted_iota(jnp.int32, (c, 1), 0)
    valid = ridx < t_blk
    p_prev = jnp.where(ridx == 0, prevrow_ref[...], pltpu.roll(p, 1, 0))
    prevrow_ref[...] = p[c - 1:c, :]
    xs = p + (p_prev - p) * mu_ref[...]

    tail = xs[:, 3 * w:]
    li = lax.broadcasted_iota(jnp.int32, tail.shape, 1)
    act = jnp.where(li < W_LORA, jnp.tanh(tail),
                    jnp.where(li < W_LORA + A_LORA, tail, jax.nn.sigmoid(tail)))
    lora = _dot(act, wl_ref[...])
    w0, a0, k_k, k_a, r_k = (vec_ref[i:i + 1, :] for i in range(5))
    z = -(w0 + lora[:, :w])
    softplus = jnp.maximum(z, 0.0) + jnp.log1p(jnp.exp(-jnp.abs(z)))
    lw = -jnp.exp(-softplus - 0.5)
    a = jax.nn.sigmoid(a0 + lora[:, w:2 * w])
    g = lora[:, 2 * w:]
    r, k, v = xs[:, :w], xs[:, w:2 * w], xs[:, 2 * w:3 * w]
    k2 = k * (1.0 + (a - 1.0) * k_a)
    return dict(r=r, v=v, lw=lw, a=a, g=g, kk=k * k_k, k2=k2, rk=r * k2 * r_k, valid=valid)


def _rwkv_features(t_blk, raw, kk_sq_sum, rk_sum):
    r, v, lw, a, g, k2, valid = (raw[n] for n in ("r", "v", "lw", "a", "g", "k2", "valid"))
    c = r.shape[0]
    kk = raw["kk"] * lax.rsqrt(jnp.maximum(kk_sq_sum, 1e-24))
    bonus = rk_sum * v
    if t_blk < c:
        lw = jnp.where(valid, lw, 0.0)
        kk = jnp.where(valid, kk, 0.0)
        k2 = jnp.where(valid, k2, 0.0)
        v = jnp.where(valid, v, 0.0)
    b = kk * a

    rr = lax.broadcasted_iota(jnp.int32, (c, c), 0)
    cc = lax.broadcasted_iota(jnp.int32, (c, c), 1)
    cum = _dot_f32(jnp.where(rr >= cc, 1.0, 0.0).astype(f32), lw)
    cum_last = cum[c - 1:c, :]
    e_neg = jnp.exp(-cum)
    e_rem = jnp.exp(cum_last - cum)
    at = -kk * jnp.exp(cum - lw)
    rt = r * jnp.exp(cum)
    bt, kt = b * e_neg, k2 * e_neg
    bh, kh = b * e_rem, k2 * e_rem
    p_last = jnp.exp(cum_last)
    return dict(at=at, rt=rt, bt=bt, kt=kt, bh=bh, kh=kh, v=v, p_last=p_last, bonus=bonus, g=g)


def _rwkv(p_rw, prev, s0, wts):
    g, t, ws = p_rw.shape
    c = RW_CHUNK
    t_blk = min(c, t)
    sb = RW_SEQS
    assert t % t_blk == 0 and g % sb == 0
    gw = RW_HPB * RW_HEAD
    return pl.pallas_call(
        functools.partial(_rwkv_kernel, t_blk),
        out_shape=[jax.ShapeDtypeStruct((g, t, RW_WIDTH), f32), jax.ShapeDtypeStruct(s0.shape, f32)],
        grid=(g // sb, t // t_blk),
        in_specs=[pl.BlockSpec((sb, t_blk, ws), lambda i, j: (i, j, 0)),
                  pl.BlockSpec((sb, 1, ws), lambda i, j: (i, 0, 0)),
                  pl.BlockSpec((sb,) + s0.shape[1:], lambda i, j: (i, 0, 0, 0)),
                  _resident(wts["rw_mu"].shape), _resident(wts["rw_lora"].shape),
                  _resident(wts["rw_vec"].shape), _resident(wts["rw_hsum"].shape)],
        out_specs=[pl.BlockSpec((sb, t_blk, RW_WIDTH), lambda i, j: (i, j, 0)),
                   pl.BlockSpec((sb,) + s0.shape[1:], lambda i, j: (i, 0, 0, 0))],
        scratch_shapes=[pltpu.VMEM((sb, RW_HEADS // RW_HPB, gw, gw), f32), pltpu.VMEM((sb, 1, ws), f32),
                        pltpu.VMEM((sb, c, ws), f32), pltpu.VMEM((sb, c, RW_WIDTH), f32)],
        compiler_params=_cparams(("parallel", "arbitrary")),
        name="rwkv",
    )(p_rw, prev.reshape(g, 1, ws), s0, wts["rw_mu"], wts["rw_lora"], wts["rw_vec"], wts["rw_hsum"])


def _gelu_tanh(x):
    return x * (0.5 * (1.0 + jnp.tanh(math.sqrt(2.0 / math.pi) * (x + 0.044715 * (x * x * x)))))


def _compress_rows(hids, pe_ref, w1f_ref, w2_ref, o_ref, w2t_ref=None, ot_ref=None):
    n = hids[0].shape[0]
    for kv in range(2):
        pe_term = _dot(pe_ref[kv], w1f_ref[kv])[0:1, :]
        for grp in range(NSA_KV):
            base = grp * 2 * CMP_HID
            lo = hids[kv][:, base:base + CMP_HID]
            hi = pltpu.roll(hids[kv][:, base + CMP_HID:base + 2 * CMP_HID], n - 1, 0)
            act = _gelu_tanh(lo + hi + pe_term).astype(bf16)
            col = (kv * NSA_KV + grp) * NSA_HEAD
            o_ref[:, col:col + NSA_HEAD] = jnp.dot(act, w2_ref[kv], preferred_element_type=f32)
            if ot_ref is not None:
                ot_ref[col:col + NSA_HEAD, :] = lax.dot_general(w2t_ref[kv], act, _NT, preferred_element_type=f32)


def _compress_dense_kernel(x_ref, wbig_ref, pe_ref, w1f_ref, w2_ref, w2t_ref, o_ref, ot_ref):
    hid = _dot(x_ref[...], wbig_ref[...])
    half = hid.shape[1] // 2
    _compress_rows([hid[:, :half], hid[:, half:]], pe_ref, w1f_ref, w2_ref, o_ref, w2t_ref, ot_ref)


def _compress_dense(x, wts):
    b, n, _ = x.shape
    return pl.pallas_call(
        _compress_dense_kernel,
        out_shape=[jax.ShapeDtypeStruct((b, n, KV_COLS), f32), jax.ShapeDtypeStruct((b, KV_COLS, n), f32)],
        grid=(b,),
        in_specs=[pl.BlockSpec((None, n, CMP_ROW), lambda i: (i, 0, 0)),
                  _resident(wts["phi_big"].shape), _resident(wts["phi_pe"].shape),
                  _resident(wts["phi_w1f"].shape), _resident(wts["phi_w2"].shape),
                  _resident(wts["phi_w2t"].shape)],
        out_specs=[pl.BlockSpec((None, n, KV_COLS), lambda i: (i, 0, 0)),
                   pl.BlockSpec((None, KV_COLS, n), lambda i: (i, 0, 0))],
        compiler_params=_cparams(("parallel",)),
        name="compress_dense",
    )(x, wts["phi_big"], wts["phi_pe"], wts["phi_w1f"], wts["phi_w2"], wts["phi_w2t"])


CMP_GROUP = 2 * LANES
CMP_UNROLL = 4


def _compress_paged_kernel(pt_ref, cache_ref, wkv_ref, pe_ref, w1f_ref, w2_ref, o_ref, pbuf_ref, xbuf_ref, sem):
    b = pl.program_id(0)
    n_pages = pt_ref.shape[1]
    page = cache_ref.shape[2]
    pages_per_group = CMP_GROUP // page
    rows_per_group = CMP_GROUP // CMP_STRIDE

    def page_copy(seq, i):
        return pltpu.make_async_copy(cache_ref.at[pt_ref[seq, i]], pbuf_ref.at[i], sem.at[i // pages_per_group])

    def start_all(seq):
        def start(i, carry):
            page_copy(seq, i).start()
            return carry
        lax.fori_loop(0, n_pages, start, 0)

    @pl.when(b == 0)
    def _():
        start_all(0)

    r = lax.broadcasted_iota(jnp.int32, (CMP_GROUP, CMP_GROUP), 0)
    c = lax.broadcasted_iota(jnp.int32, (CMP_GROUP, CMP_GROUP), 1)
    perm = jnp.where(c == CMP_STRIDE * (r % rows_per_group) + r // rows_per_group, 1.0, 0.0).astype(bf16)

    def regroup(step, carry):
        for pg in range(CMP_UNROLL * pages_per_group):
            page_copy(b, step * CMP_UNROLL * pages_per_group + pg).wait()
        for k in range(CMP_UNROLL):
            gi = step * CMP_UNROLL + k
            tokens = jnp.concatenate([pbuf_ref[gi * pages_per_group + pg] for pg in range(pages_per_group)],
                                     axis=1).astype(bf16)
            rows = lax.dot_general(perm, tokens, _NT, preferred_element_type=f32)
            r0 = pl.multiple_of(gi * rows_per_group, rows_per_group)
            for j in range(CMP_STRIDE):
                for kv in range(2):
                    col = (kv * CMP_STRIDE + j) * NSA_KV_COLS
                    xbuf_ref[pl.ds(r0, rows_per_group), col:col + NSA_KV_COLS] = (
                        rows[j * rows_per_group:(j + 1) * rows_per_group,
                             kv * NSA_KV_COLS:(kv + 1) * NSA_KV_COLS].astype(bf16))
        return carry

    n_groups = n_pages // pages_per_group
    lax.fori_loop(0, n_groups // CMP_UNROLL, regroup, 0)

    @pl.when(b + 1 < pl.num_programs(0))
    def _():
        start_all(b + 1)

    half = CMP_STRIDE * NSA_KV_COLS
    hids = [jnp.dot(xbuf_ref[:, kv * half:(kv + 1) * half], wkv_ref[kv], preferred_element_type=f32)
            for kv in range(2)]
    _compress_rows(hids, pe_ref, w1f_ref, w2_ref, o_ref)


def _compress_paged(cache_t, page_table, wts):
    b, n_pages = page_table.shape
    page = cache_t.shape[2]
    n = n_pages * page // CMP_STRIDE
    assert CMP_GROUP % page == 0 and (n_pages * page) % (CMP_GROUP * CMP_UNROLL) == 0
    full = lambda a: pl.BlockSpec(a.shape, lambda i, pt: (0,) * a.ndim, pipeline_mode=pl.Buffered(1))
    return pl.pallas_call(
        _compress_paged_kernel,
        out_shape=jax.ShapeDtypeStruct((b, n, KV_COLS), f32),
        grid_spec=pltpu.PrefetchScalarGridSpec(
            num_scalar_prefetch=1, grid=(b,),
            in_specs=[pl.BlockSpec(memory_space=pl.ANY), full(wts["phi_kv"]), full(wts["phi_pe"]),
                      full(wts["phi_w1f"]), full(wts["phi_w2"])],
            out_specs=pl.BlockSpec((None, n, KV_COLS), lambda i, pt: (i, 0, 0)),
            scratch_shapes=[pltpu.VMEM((n_pages, KV_COLS, page), f32), pltpu.VMEM((n, CMP_ROW), bf16),
                            pltpu.SemaphoreType.DMA((n_pages * page // CMP_GROUP,))]),
        compiler_params=_cparams(("arbitrary",)),
        name="compress_paged",
    )(page_table, cache_t, wts["phi_kv"], wts["phi_pe"], wts["phi_w1f"], wts["phi_w2"])


def _softmax_rows(s, mask):
    s = jnp.where(mask, s, NEG_INF)
    e = jnp.where(mask, jnp.exp(s - jnp.max(s, axis=-1, keepdims=True)), 0.0)
    return e / jnp.maximum(jnp.sum(e, axis=-1, keepdims=True), 1e-30)


def _flash_step(q, k, v, mask, m, l, acc):
    s = jnp.where(mask, _dot_nt(q, k), NEG_INF)
    m_new = jnp.maximum(m, jnp.max(s, axis=-1, keepdims=True))
    alpha = jnp.exp(m - m_new)
    e = jnp.where(mask, jnp.exp(s - m_new), 0.0)
    return m_new, alpha * l + jnp.sum(e, axis=-1, keepdims=True), alpha * acc + _dot(e, v)


def _flash_init(rows):
    return (jnp.full((rows, 1), NEG_INF, f32), jnp.zeros((rows, 1), f32), jnp.zeros((rows, NSA_HEAD), f32))


def _flash_out(m, l, acc):
    return acc / jnp.maximum(l, 1e-30)


def _importance_matrix(n_cmp_rows, n_sel_cols):
    ratio = SEL_BLOCK // CMP_STRIDE
    i = lax.broadcasted_iota(jnp.int32, (n_cmp_rows, n_sel_cols), 0)
    j = lax.broadcasted_iota(jnp.int32, (n_cmp_rows, n_sel_cols), 1)
    return jnp.where((i >= ratio * j - 1) & (i <= ratio * j + ratio - 1), 1.0, 0.0).astype(bf16)


def _block_scores(imp, j, cur):
    forced = (j == 0) | (j == cur) | (j == cur - 1)
    return jnp.where(j <= cur, jnp.where(forced, BIG, imp), -BIG)


def _stack_heads(x, grp, rows):
    return jnp.concatenate(
        [x[:, (grp * NSA_HPG + h) * NSA_HEAD:(grp * NSA_HPG + h + 1) * NSA_HEAD] for h in range(NSA_HPG)], axis=0)


def _stack_gates(gates, grp, branch):
    return jnp.concatenate(
        [gates[:, 3 * (grp * NSA_HPG + h) + branch:3 * (grp * NSA_HPG + h) + branch + 1] for h in range(NSA_HPG)],
        axis=0)


def _scores_t(qts, ks, biases):
    return tuple(jnp.dot(k.astype(bf16), qt, preferred_element_type=f32) + bias
                 for qt, k, bias in zip(qts, ks, biases))


def _flash_update_t(ss, vts, carry):
    n = len(ss)
    ms, ls, accs = carry[0::3], carry[1::3], carry[2::3]
    m_new = [jnp.maximum(ms[i], jnp.max(ss[i], axis=0, keepdims=True)) for i in range(n)]
    es = [jnp.exp2(ss[i] - m_new[i]).astype(bf16) for i in range(n)]
    ones = jnp.ones((ONES_ROWS, vts[0].shape[1]), bf16)
    pvs = [jnp.dot(jnp.concatenate([vts[i].astype(bf16), ones], axis=0), es[i], preferred_element_type=f32)
           for i in range(n)]
    out = ()
    for i in range(n):
        alpha = jnp.exp2(ms[i] - m_new[i])
        out += (m_new[i], alpha * ls[i] + pvs[i][NSA_HEAD:NSA_HEAD + 1], alpha * accs[i] + pvs[i][:NSA_HEAD])
    return out


M_FLOOR = 0.1 * NEG_INF
ONES_ROWS = 16
LOG2_E = math.log2(math.e)


def _flash_init_t(cols):
    return (jnp.full((1, cols), M_FLOOR, f32), jnp.zeros((1, cols), f32), jnp.zeros((NSA_HEAD, cols), f32))


def _lanes(x, n):
    return jnp.concatenate([x] * n, axis=1)


def _nsa_prompt_kernel(qt_ref, gt_ref, kc_ref, vct_ref, ks_ref, vst_ref, kw_ref, vwt_ref, o_ref):
    tq = qt_ref.shape[1]
    n_cmp = kc_ref.shape[0]
    n_sel = ks_ref.shape[0] // SEL_BLOCK
    i = pl.program_id(1)
    t0 = i * tq
    cols = NSA_HPG * tq
    gates = jax.nn.sigmoid(gt_ref[...])
    q_pos = t0 + lax.broadcasted_iota(jnp.int32, (1, tq), 1)
    q_pos_c = _lanes(q_pos, NSA_HPG)
    cmp_end = lax.broadcasted_iota(jnp.int32, (n_cmp, 1), 0) * CMP_STRIDE + (CMP_LEN - 1)
    ratio = SEL_BLOCK // CMP_STRIDE
    ij = lax.broadcasted_iota(jnp.int32, (n_sel, n_cmp), 0)
    ii = lax.broadcasted_iota(jnp.int32, (n_sel, n_cmp), 1)
    imp_mat = jnp.where((ii >= ratio * ij - 1) & (ii <= ratio * ij + ratio - 1), 1.0, 0.0).astype(bf16)
    jt = lax.broadcasted_iota(jnp.int32, (n_sel, tq), 0)
    cur_t = (t0 + lax.broadcasted_iota(jnp.int32, (n_sel, tq), 1)) // SEL_BLOCK
    krow = lax.broadcasted_iota(jnp.int32, (K_TILE, 1), 0)
    er = lax.broadcasted_iota(jnp.int32, (K_TILE, n_sel), 0) // SEL_BLOCK
    ej = lax.broadcasted_iota(jnp.int32, (K_TILE, n_sel), 1)
    groups = range(NSA_KV)
    heads_of = lambda grp: range(grp * NSA_HPG, (grp + 1) * NSA_HPG)
    qts, sels, o_cs = [], [], []

    for grp in groups:
        d0 = grp * NSA_HEAD
        heads = heads_of(grp)
        q_all = jnp.concatenate([qt_ref[h * NSA_HEAD:(h + 1) * NSA_HEAD, :] for h in heads], axis=1)
        qt = (q_all * NSA_SCALE).astype(bf16)
        mask_c = cmp_end <= q_pos_c
        s = jnp.where(mask_c, jnp.dot(kc_ref[:, d0:d0 + NSA_HEAD].astype(bf16), qt, preferred_element_type=f32),
                      NEG_INF)
        e = jnp.where(mask_c, jnp.exp(s - jnp.max(s, axis=0, keepdims=True)), 0.0)
        p_c = e / jnp.maximum(jnp.sum(e, axis=0, keepdims=True), 1e-30)
        o_c = jnp.dot(vct_ref[d0:d0 + NSA_HEAD, :].astype(bf16), p_c.astype(bf16), preferred_element_type=f32)
        p_sum = p_c[:, 0:tq]
        for h in range(1, NSA_HPG):
            p_sum = p_sum + p_c[:, h * tq:(h + 1) * tq]
        hi = p_sum.astype(bf16)
        r1 = p_sum - hi.astype(f32)
        mid = r1.astype(bf16)
        lo = (r1 - mid.astype(f32)).astype(bf16)
        imp = sum(jnp.dot(imp_mat, part, preferred_element_type=f32) for part in (hi, mid, lo))
        score = _block_scores(imp, jt, cur_t)
        rank = jnp.zeros((n_sel, tq), f32)
        for j2 in range(n_sel):
            row = score[j2:j2 + 1, :]
            ge = jnp.where(row >= score, 1.0, 0.0)
            gt = jnp.where(row > score, 1.0, 0.0)
            rank = rank + jnp.where(jt > j2, ge, gt)
        q_exp2 = (q_all * (NSA_SCALE * LOG2_E)).astype(bf16)
        qts.extend(q_exp2[:, n * tq:(n + 1) * tq] for n in range(NSA_HPG))
        sels.append(jnp.where(rank < SEL_TOPK, 1.0, 0.0).astype(bf16))
        o_cs.extend(o_c[:, n * tq:(n + 1) * tq] for n in range(NSA_HPG))

    last = (t0 + tq - 1) // K_TILE
    key_cols = lambda ref, k0: [ref[pl.ds(k0, K_TILE), g * NSA_HEAD:(g + 1) * NSA_HEAD] for g in groups]
    val_rows = lambda ref, k0: [ref[g * NSA_HEAD:(g + 1) * NSA_HEAD, pl.ds(k0, K_TILE)] for g in groups]

    def flash_heads(keys, vals, biases, carry):
        out = ()
        for h0 in range(0, NSA_HEADS, FLASH_HEADS):
            hs = range(h0, h0 + FLASH_HEADS)
            grp_of = [h // NSA_HPG for h in hs]
            ss = _scores_t([qts[h] for h in hs], [keys[g] for g in grp_of], [biases[g] for g in grp_of])
            out += _flash_update_t(ss, [vals[g] for g in grp_of], carry[3 * h0:3 * (h0 + FLASH_HEADS)])
        return out

    def sel_step(kt, carry):
        k0 = pl.multiple_of(kt * K_TILE, K_TILE)
        expand = jnp.where(ej == kt * (K_TILE // SEL_BLOCK) + er, 1.0, 0.0).astype(bf16)
        causal = (k0 + krow) <= q_pos
        biases = []
        for grp in groups:
            chosen = jnp.dot(expand, sels[grp], preferred_element_type=f32)
            biases.append(jnp.where(causal & (chosen > 0.5), 0.0, NEG_INF))
        return flash_heads(key_cols(ks_ref, k0), val_rows(vst_ref, k0), biases, carry)

    sel_acc = lax.fori_loop(0, last + 1, sel_step, _flash_init_t(tq) * NSA_HEADS)

    def win_step(kt, carry):
        k0 = pl.multiple_of(kt * K_TILE, K_TILE)
        dist = q_pos - (k0 + krow)
        bias = jnp.where((dist >= 0) & (dist < WINDOW), 0.0, NEG_INF)
        return flash_heads(key_cols(kw_ref, k0), val_rows(vwt_ref, k0), [bias] * NSA_KV, carry)

    win_acc = lax.fori_loop(jnp.maximum(t0 - WINDOW, 0) // K_TILE, last + 1, win_step,
                            _flash_init_t(tq) * NSA_HEADS)

    for h in range(NSA_HEADS):
        o_s = _flash_out(*sel_acc[3 * h:3 * h + 3])
        o_w = _flash_out(*win_acc[3 * h:3 * h + 3])
        gate = lambda br: gates[3 * h + br:3 * h + br + 1, :]
        o_ref[h * NSA_HEAD:(h + 1) * NSA_HEAD, :] = gate(0) * o_cs[h] + gate(1) * o_s + gate(2) * o_w


def _nsa_prompt(qt, gt, kvc, kvct, kvs, kvst, kvw, kvwt):
    b, _, t = qt.shape
    n_cmp = kvc.shape[1]
    tq = Q_TILE
    assert t % tq == 0 and t % K_TILE == 0 and K_TILE % SEL_BLOCK == 0
    keys =lambda r: pl.BlockSpec((None, r, NSA_KV_COLS), lambda i, j: (i, 0, 0))
    vals = lambda c: pl.BlockSpec((None, NSA_KV_COLS, c), lambda i, j: (i, 1, 0))
    return pl.pallas_call(
        _nsa_prompt_kernel,
        out_shape=jax.ShapeDtypeStruct((b, NSA_WIDTH, t), f32),
        grid=(b, t // tq),
        in_specs=[pl.BlockSpec((None, NSA_WIDTH, tq), lambda i, j: (i, 0, j)),
                  pl.BlockSpec((None, GATE_PAD, tq), lambda i, j: (i, 0, j)),
                  keys(n_cmp), vals(n_cmp), keys(t), vals(t), keys(t), vals(t)],
        out_specs=pl.BlockSpec((None, NSA_WIDTH, tq), lambda i, j: (i, 0, j)),
        compiler_params=_cparams(("parallel", "arbitrary")),
        name="nsa_prompt",
    )(qt, gt, kvc, kvct, kvs, kvst, kvw, kvwt)


def _nsa_sample_kernel(past_len, pt_ref, q_ref, gate_ref, kvc_ref, cache_ref, kvs_new_ref, win_ref, kvw_new_ref,
                       o_ref, selbufs_ref, winbuf_ref, qbuf_ref, gbuf_ref, obuf_ref, sem):
    b = pl.program_id(0)
    t_new = q_ref.shape[0]
    tp = qbuf_ref.shape[0]
    n_pages = pt_ref.shape[1]
    page = cache_ref.shape[2]
    n_cmp = kvc_ref.shape[0]
    n_win = win_ref.shape[1]
    total = past_len + t_new
    n_sel = -(-total // SEL_BLOCK)
    n_sel_pad = -(-n_sel // LANES) * LANES
    tile_pages = S_KTILE // page
    tile_blocks = S_KTILE // SEL_BLOCK
    rows = NSA_HPG * tp

    slot = b % 2
    selbuf_ref = selbufs_ref.at[slot]

    def page_copy(seq, i, into):
        return pltpu.make_async_copy(cache_ref.at[pt_ref[seq, i]], selbufs_ref.at[into, i],
                                     sem.at[into, i // tile_pages])

    def start_all(seq, into):
        def start(i, carry):
            page_copy(seq, i, into).start()
            return carry
        lax.fori_loop(0, n_pages, start, 0)

    @pl.when(b == 0)
    def _():
        start_all(0, 0)

    @pl.when(b + 1 < pl.num_programs(0))
    def _():
        start_all(b + 1, 1 - slot)

    src = lax.broadcasted_iota(jnp.int32, (kvs_new_ref.shape[1], LANES), 0)
    dst = lax.broadcasted_iota(jnp.int32, (kvs_new_ref.shape[1], LANES), 1)
    pick = jnp.where((src == b * t_new + dst) & (dst < t_new), 1.0, 0.0).astype(bf16)
    selbuf_ref[n_pages] = _dot(kvs_new_ref[...], pick)
    winbuf_ref[:, 0:n_win] = win_ref[...]
    winbuf_ref[:, n_win:n_win + LANES] = _dot(kvw_new_ref[...], pick)
    qbuf_ref[...] = jnp.zeros(qbuf_ref.shape, f32)
    qbuf_ref[0:t_new, :] = q_ref[...]
    gbuf_ref[...] = jnp.zeros(gbuf_ref.shape, f32)
    gbuf_ref[0:t_new, :] = gate_ref[...]

    q = qbuf_ref[...] * NSA_SCALE
    gates = jax.nn.sigmoid(gbuf_ref[...])
    q_pos = past_len + lax.broadcasted_iota(jnp.int32, (tp, 1), 0)
    q_pos_r = jnp.concatenate([q_pos] * NSA_HPG, axis=0)
    cmp_end = lax.broadcasted_iota(jnp.int32, (1, n_cmp), 1) * CMP_STRIDE + (CMP_LEN - 1)
    imp_mat = _importance_matrix(n_cmp, n_sel_pad)
    jj = lax.broadcasted_iota(jnp.int32, (tp, n_sel_pad), 1)
    expand = jnp.where(lax.broadcasted_iota(jnp.int32, (tile_blocks, S_KTILE), 0)
                       == lax.broadcasted_iota(jnp.int32, (tile_blocks, S_KTILE), 1) // SEL_BLOCK,
                       1.0, 0.0).astype(bf16)

    def flash(qs, k_t, v_t, mask, m, l, acc):
        s = jnp.where(mask, jnp.dot(qs, k_t.astype(bf16), preferred_element_type=f32), NEG_INF)
        m_new = jnp.maximum(m, jnp.max(s, axis=-1, keepdims=True))
        alpha = jnp.exp(m - m_new)
        e = jnp.where(mask, jnp.exp(s - m_new), 0.0)
        pv = lax.dot_general(e.astype(bf16), v_t.astype(bf16), _NT, preferred_element_type=f32)
        return m_new, alpha * l + jnp.sum(e, axis=-1, keepdims=True), alpha * acc + pv

    qss, sels, o_cs = [], [], []
    for grp in range(NSA_KV):
        kcol0, vcol0 = grp * NSA_HEAD, NSA_KV_COLS + grp * NSA_HEAD
        qs = _stack_heads(q, grp, tp).astype(bf16)
        p_c = _softmax_rows(_dot_nt(qs, kvc_ref[:, kcol0:kcol0 + NSA_HEAD]), cmp_end <= q_pos_r)
        o_cs.append(_dot(p_c, kvc_ref[:, vcol0:vcol0 + NSA_HEAD]))
        p_sum = p_c[0:tp]
        for h in range(1, NSA_HPG):
            p_sum = p_sum + p_c[h * tp:(h + 1) * tp]
        score = _block_scores(_dot_sel(p_sum, imp_mat), jj, q_pos // SEL_BLOCK)
        rank = jnp.zeros((tp, n_sel_pad), f32)
        for j2 in range(n_sel):
            col = score[:, j2:j2 + 1]
            ge = jnp.where(col >= score, 1.0, 0.0)
            gt = jnp.where(col > score, 1.0, 0.0)
            rank = rank + jnp.where(jj > j2, ge, gt)
        sel = jnp.where((rank < SEL_TOPK) & (jj < n_sel), 1.0, 0.0).astype(bf16)
        sels.append(jnp.concatenate([sel] * NSA_HPG, axis=0))
        qss.append(qs)

    def wait(i, carry):
        page_copy(b, i, slot).wait()
        return carry

    lax.fori_loop(0, n_pages, wait, 0)
    n_tiles = past_len // S_KTILE
    tiles = [(kt * S_KTILE, S_KTILE if kt < n_tiles else LANES) for kt in range(n_tiles + 1)]
    probs = [(grp, kt) for kt in range(n_tiles + 1) for grp in range(NSA_KV)]

    def tile_rows(kt, r0):
        pages = range(kt * tile_pages, (kt + 1) * tile_pages) if kt < n_tiles else [n_pages]
        return jnp.concatenate([selbuf_ref[p, r0:r0 + NSA_HEAD, :] for p in pages], axis=1)

    masks, ss = {}, {}
    for grp, kt in probs:
        k0, width = tiles[kt]
        k_pos = k0 + lax.broadcasted_iota(jnp.int32, (1, width), 1)
        blocks = sels[grp][:, kt * tile_blocks:(kt + 1) * tile_blocks]
        chosen = jnp.dot(blocks, expand[:, 0:width], preferred_element_type=f32)
        masks[grp, kt] = (chosen > 0.5) & (k_pos <= q_pos_r)
        k_t = tile_rows(kt, grp * NSA_HEAD)
        ss[grp, kt] = jnp.where(masks[grp, kt], jnp.dot(qss[grp], k_t.astype(bf16), preferred_element_type=f32),
                                NEG_INF)
    row_max = [functools.reduce(jnp.maximum, [jnp.max(ss[grp, kt], axis=-1, keepdims=True)
                                               for kt in range(n_tiles + 1)]) for grp in range(NSA_KV)]
    es = {p: jnp.where(masks[p], jnp.exp(ss[p] - row_max[p[0]]), 0.0) for p in probs}
    pvs = {}
    for grp, kt in probs:
        k0, width = tiles[kt]
        v_t = tile_rows(kt, NSA_KV_COLS + grp * NSA_HEAD)
        pvs[grp, kt] = lax.dot_general(es[grp, kt].astype(bf16), v_t.astype(bf16), _NT, preferred_element_type=f32)

    w_pos = past_len - n_win + lax.broadcasted_iota(jnp.int32, (1, winbuf_ref.shape[1]), 1)
    dist = jnp.where(w_pos >= 0, q_pos_r - w_pos, -1)
    for grp in range(NSA_KV):
        krow0, vrow0 = grp * NSA_HEAD, NSA_KV_COLS + grp * NSA_HEAD
        denom = sum(jnp.sum(es[grp, kt], axis=-1, keepdims=True) for kt in range(n_tiles + 1))
        o_s = sum(pvs[grp, kt] for kt in range(n_tiles + 1)) / jnp.maximum(denom, 1e-30)
        o_w = _flash_out(*flash(qss[grp], winbuf_ref[krow0:krow0 + NSA_HEAD, :], winbuf_ref[vrow0:vrow0 + NSA_HEAD, :],
                                (dist >= 0) & (dist < WINDOW), *_flash_init(rows)))
        o = (_stack_gates(gates, grp, 0) * o_cs[grp] + _stack_gates(gates, grp, 1) * o_s
             + _stack_gates(gates, grp, 2) * o_w)
        for h in range(NSA_HPG):
            col = (grp * NSA_HPG + h) * NSA_HEAD
            obuf_ref[:, col:col + NSA_HEAD] = o[h * tp:(h + 1) * tp]
    o_ref[...] = obuf_ref[0:t_new, :]


def _nsa_sample(p_q, p_gate, kvc_cmp, cache_sel_t, page_table, kvs_new_t, win_state_t, kvw_new_t):
    b, t_new, _ = p_q.shape
    n_pages = page_table.shape[1]
    page = cache_sel_t.shape[2]
    past_len = n_pages * page
    n_cmp = kvc_cmp.shape[1]
    n_win = win_state_t.shape[2]
    tp = 8
    assert t_new <= tp and past_len % S_KTILE == 0 and S_KTILE % page == 0 and n_win % LANES == 0
    assert b * t_new == kvs_new_t.shape[1] and page == LANES
    n_sel_pad = -(-(-(-(past_len + t_new) // SEL_BLOCK)) // LANES) * LANES
    assert (past_len // S_KTILE + 1) * (S_KTILE // SEL_BLOCK) <= n_sel_pad
    per_seq = lambda n, w: pl.BlockSpec((None, n, w), lambda i, pt: (i, 0, 0))
    whole = lambda a: pl.BlockSpec(a.shape, lambda i, pt: (0,) * a.ndim, pipeline_mode=pl.Buffered(1))
    return pl.pallas_call(
        functools.partial(_nsa_sample_kernel, past_len),
        out_shape=jax.ShapeDtypeStruct((b, t_new, NSA_WIDTH), f32),
        grid_spec=pltpu.PrefetchScalarGridSpec(
            num_scalar_prefetch=1, grid=(b,),
            in_specs=[per_seq(t_new, NSA_WIDTH), per_seq(t_new, GATE_PAD), per_seq(n_cmp, KV_COLS),
                      pl.BlockSpec(memory_space=pl.ANY), whole(kvs_new_t), per_seq(KV_COLS, n_win),
                      whole(kvw_new_t)],
            out_specs=per_seq(t_new, NSA_WIDTH),
            scratch_shapes=[pltpu.VMEM((2, n_pages + 1, KV_COLS, page), f32), pltpu.VMEM((KV_COLS, n_win + LANES), f32),
                            pltpu.VMEM((tp, NSA_WIDTH), f32), pltpu.VMEM((tp, GATE_PAD), f32),
                            pltpu.VMEM((tp, NSA_WIDTH), f32), pltpu.SemaphoreType.DMA((2, past_len // S_KTILE))]),
        compiler_params=_cparams(("arbitrary",)),
        name="nsa_sample",
    )(page_table, p_q, p_gate, kvc_cmp, cache_sel_t, kvs_new_t, win_state_t, kvw_new_t)


def _prep_weights(l, depth, ln_g, ln_b, ffn_w_gate, ffn_w_up, ffn_w_down, w_in, b_in, rw_mu, rw_w0, rw_w2, rw_a0,
                  rw_a2, rw_g2, rw_k_k, rw_k_a, rw_r_k, rw_ln_w, rw_ln_b, nsa_phi_pe, nsa_phi_w1, nsa_phi_w2,
                  w_out_a, w_out_b, w_o):
    d = D_MODEL
    nc = D_FF // FF_CHUNK
    up = down = lambda w: w.astype(bf16)
    c1 = RW_SHIFT
    c2 = c1 + NSA_WIDTH
    c3 = c2 + 3 * KV_COLS
    c4 = c3 + N_GATE
    cols = lambda a: jnp.concatenate(
        [a[..., :c3], a[..., c4:], a[..., c3:c4], jnp.zeros(a.shape[:-1] + (GATE_PAD - N_GATE,), a.dtype)], axis=-1)
    z = lambda r, c: jnp.zeros((r, c), f32)
    w = RW_WIDTH
    lora = jnp.concatenate([
        jnp.concatenate([rw_w2[l], z(W_LORA, w), z(W_LORA, w)], axis=1),
        jnp.concatenate([z(A_LORA, w), rw_a2[l], z(A_LORA, w)], axis=1),
        jnp.concatenate([z(G_LORA, w), z(G_LORA, w), rw_g2[l]], axis=1)], axis=0)
    hid = lax.broadcasted_iota(jnp.int32, (w, w), 0) // RW_HEAD == lax.broadcasted_iota(jnp.int32, (w, w), 1) // RW_HEAD
    vec = jnp.stack([rw_w0[l], rw_a0[l], rw_k_k[l], rw_k_a[l], rw_r_k[l].reshape(w), rw_ln_w[l], rw_ln_b[l],
                     jnp.zeros((w,), f32)])
    w1 = nsa_phi_w1[l].reshape(2, 2, CMP_STRIDE, NSA_HEAD, CMP_HID)
    eye = jnp.eye(NSA_KV, dtype=f32)
    big = jnp.einsum("kljde,kK,gG->jkgdKGle", w1, jnp.eye(2, dtype=f32), eye)
    per_kv = jnp.einsum("kljde,gG->kjgdGle", w1, eye)
    pe = jnp.broadcast_to(nsa_phi_pe[l].reshape(2, 1, CMP_LEN * NSA_HEAD), (2, 8, CMP_LEN * NSA_HEAD))
    w_cols, b_cols = cols(w_in[l]), cols(b_in[l])
    seg_t = _segments(_PROJ, [name for name, _ in _PROJ_T])
    w_t = jnp.concatenate([w_cols[:, lo:hi] for lo, hi in seg_t], axis=1).T
    b_t = jnp.concatenate([b_cols[lo:hi] for lo, hi in seg_t])
    return dict(
        alpha=(2 * depth) ** 0.25,
        wg1=up(ffn_w_gate[l, 0]), wu1=up(ffn_w_up[l, 0]), wd1=down(ffn_w_down[l, 0]),
        wg2=up(ffn_w_gate[l, 1]), wu2=up(ffn_w_up[l, 1]), wd2=down(ffn_w_down[l, 1]),
        ln_g=ln_g[l], ln_b=ln_b[l],
        w_in=w_cols.astype(bf16), b_in=b_cols.reshape(1, -1),
        w_in_t=w_t.astype(bf16), b_in_t=b_t[:, None],
        phi_w2t=jnp.swapaxes(nsa_phi_w2[l], 1, 2).astype(bf16),
        rw_mu=rw_mu[l].reshape(1, -1), rw_lora=lora.astype(bf16), rw_vec=vec, rw_hsum=hid.astype(bf16),
        phi_big=big.reshape(CMP_ROW, 2 * NSA_KV * 2 * CMP_HID).astype(bf16), phi_pe=pe,
        phi_kv=per_kv.reshape(2, CMP_STRIDE * NSA_KV_COLS, NSA_KV * 2 * CMP_HID).astype(bf16),
        phi_w1f=nsa_phi_w1[l].reshape(2, CMP_LEN * NSA_HEAD, CMP_HID).astype(bf16),
        phi_w2=nsa_phi_w2[l].astype(bf16),
        w_out_a=w_out_a[l].astype(bf16), w_out_b=w_out_b[l].astype(bf16), w_o=w_o[l].astype(bf16))


def _kv_state(p):
    return p.reshape(p.shape[:-1] + (2, NSA_KV, NSA_HEAD))


def _tokens_last(a):
    n, t = a.shape[:2]
    return a.transpose(0, 2, 3, 4, 1).reshape(n, KV_COLS, t)


def _kv_state_t(p_t):
    b, _, t = p_t.shape
    return p_t.reshape(b, 2, NSA_KV, NSA_HEAD, t).transpose(0, 4, 1, 2, 3)


def kernel(x_prompt, x_sample, c_prompt, c_sample, cache_kv_cmp, cache_kv_sel, state_kv_win, state_wkv, state_shift, page_table, w_ada, b_ada, ln_g, ln_b, ffn_w_gate, ffn_w_up, ffn_w_down, w_in, b_in, rw_mu, rw_w0, rw_w2, rw_a0, rw_a2, rw_g2, rw_k_k, rw_k_a, rw_r_k, rw_ln_w, rw_ln_b, nsa_phi_pe, nsa_phi_w1, nsa_phi_w2, w_out_a, w_out_b, w_o):
    bp, seq, d = x_prompt.shape
    bd, t_new, _ = x_sample.shape
    depth = w_ada.shape[0]
    n_phys, page = cache_kv_cmp.shape[1:3]
    n_pages = page_table.shape[1]
    n_win = state_kv_win.shape[2]
    rows_s = bd * t_new
    tm_p = min(TM_PROJ, seq)
    tm_ffn = min(TM_FFN, seq)
    assert seq % tm_p == 0 and seq % tm_ffn == 0 and seq % CMP_STRIDE == 0 and page % CMP_STRIDE == 0

    y_p, y_s = x_prompt, x_sample.reshape(1, rows_s, d)
    st_p, st_s = [], []
    for l in range(depth):
        wts = _prep_weights(l, depth, ln_g, ln_b, ffn_w_gate, ffn_w_up, ffn_w_down, w_in, b_in, rw_mu, rw_w0,
                            rw_w2, rw_a0, rw_a2, rw_g2, rw_k_k, rw_k_a, rw_r_k, rw_ln_w, rw_ln_b, nsa_phi_pe,
                            nsa_phi_w1, nsa_phi_w2, w_out_a, w_out_b, w_o)
        mod = _ada_mod(jnp.concatenate([c_prompt, c_sample], axis=0), w_ada[l], b_ada[l])
        mod_p = mod[:bp].reshape(bp, 9, 1, d)
        mod_s = jnp.repeat(mod[bp:].reshape(bd, 9, d), t_new, axis=0).reshape(rows_s, 9, d)
        mod_s = mod_s.transpose(1, 0, 2).reshape(1, 9, rows_s, d)

        x1 = _ffn_block(y_p, mod_p, wts, tm_ffn, 0)
        p_rw, p_kc, p_ks, p_kw, p_m, q_t, g_t, ks_t, kw_t = _in_proj(
            x1, mod_p, wts, tm_p, ("rw", "kc", "ks", "kw", "merge"), ("q", "gate", "ks", "kw"))
        ya, wkv_p = _rwkv(p_rw, jnp.zeros((bp, RW_SHIFT), f32), jnp.zeros((bp, RW_HEADS, RW_HEAD, RW_HEAD), f32),
                          wts)
        kvc_cmp, kvc_cmp_t = _compress_dense(p_kc.reshape(bp, seq // CMP_STRIDE, CMP_ROW), wts)
        yb_t = _nsa_prompt(q_t, g_t, kvc_cmp, kvc_cmp_t, p_ks, ks_t, p_kw, kw_t)
        y_p = _ffn_block(_mix(x1, ya, yb_t, p_m, mod_p, wts, tm_p, True), mod_p, wts, tm_ffn, 1)
        n_keep = min(WINDOW, seq)
        st_p.append((_kv_state(p_kc), _kv_state_t(ks_t), _kv_state_t(kw_t[:, :, seq - n_keep:]), wkv_p,
                     p_rw[:, -1]))

        x1 = _ffn_block(y_s, mod_s, wts, rows_s, 0)
        p_rw, p_q, p_kc, p_ks, p_m, p_g, ks_t, kw_t = _in_proj(
            x1, mod_s, wts, rows_s, ("rw", "q", "kc", "ks", "merge", "gate"), ("ks", "kw"))
        per_seq = lambda a: a.reshape(bd, t_new, a.shape[-1])
        p_rw, p_q, p_kc, p_ks, p_g = map(per_seq, (p_rw, p_q, p_kc, p_ks, p_g))
        ya, wkv_s = _rwkv(p_rw, state_shift[l], state_wkv[l], wts)
        kvc_cmp = _compress_paged(_tokens_last(cache_kv_cmp[l]), page_table, wts)
        win_state_t = _tokens_last(state_kv_win[l])
        yb = _nsa_sample(p_q, p_g, kvc_cmp, _tokens_last(cache_kv_sel[l]), page_table, ks_t[0], win_state_t,
                         kw_t[0])
        x2 = _mix(x1, ya.reshape(1, rows_s, RW_WIDTH), yb.reshape(1, rows_s, NSA_WIDTH), p_m, mod_s, wts, rows_s,
                  False)
        y_s = _ffn_block(x2, mod_s, wts, rows_s, 1)
        kw_new_t = kw_t[0].reshape(KV_COLS, bd, t_new).transpose(1, 0, 2)
        new_win_t = jnp.concatenate([win_state_t, kw_new_t], axis=2)[:, :, t_new:]
        st_s.append((_kv_state(p_kc), _kv_state(p_ks), _kv_state_t(new_win_t), wkv_s, p_rw[:, -1]))

    stack = lambda sts, i: jnp.stack([s[i] for s in sts])
    return (y_p, y_s.reshape(bd, t_new, d)) + tuple(stack(st_p, i) for i in range(5)) + tuple(
        stack(st_s, i) for i in range(5))
```

```python
import functools
import math

import jax
import jax.numpy as jnp
from jax import lax
from jax.experimental import pallas as pl
from jax.experimental.pallas import tpu as pltpu

f32 = jnp.float32
bf16 = jnp.bfloat16

D_MODEL = 1024
RW_HEAD = 64
RW_HEADS = 8
RW_WIDTH = RW_HEADS * RW_HEAD
W_LORA = 64
A_LORA = 64
G_LORA = 128
LORA_IN = W_LORA + A_LORA + G_LORA
RW_GN_EPS = 64e-5
RW_SHIFT = 3 * RW_WIDTH + LORA_IN
NSA_HEAD = 64
NSA_HEADS = 8
NSA_KV = 2
NSA_HPG = NSA_HEADS // NSA_KV
NSA_WIDTH = NSA_HEADS * NSA_HEAD
NSA_KV_COLS = NSA_KV * NSA_HEAD
KV_COLS = 2 * NSA_KV_COLS
NSA_SCALE = NSA_HEAD ** -0.5
CMP_STRIDE = 16
CMP_LEN = 2 * CMP_STRIDE
CMP_HID = 128
CMP_ROW = CMP_STRIDE * KV_COLS
SEL_BLOCK = 64
SEL_TOPK = 16
WINDOW = 512
D_FF = 2816
N_GATE = 3 * NSA_HEADS
LN_EPS = 1e-5
NEG_INF = -1e30
BIG = 1e9

LANES = 128
GATE_PAD = LANES
FF_CHUNK = 256
TM_FFN = 1024
TM_PROJ = 512
RW_CHUNK = 64
RW_HPB = 4
RW_SEQS = 4
Q_TILE = 256
K_TILE = 256
FLASH_HEADS = 8
S_KTILE = 2048
VMEM_LIMIT = 56 * 1024 * 1024

_NT = (((1,), (1,)), ((), ()))
_TN = (((0,), (0,)), ((), ()))
_HI = lax.Precision.HIGHEST


def _dot(a, b):
    return jnp.dot(a.astype(bf16), b.astype(bf16), preferred_element_type=f32)


def _dot_nt(a, b):
    return lax.dot_general(a.astype(bf16), b.astype(bf16), _NT, preferred_element_type=f32)


def _dot_f32(a, b, dims=None):
    if dims is None:
        return jnp.dot(a, b, preferred_element_type=f32, precision=_HI)
    return lax.dot_general(a, b, dims, preferred_element_type=f32, precision=_HI)


def _dot_sel(x, m01):
    hi = x.astype(bf16)
    r1 = x - hi.astype(f32)
    mid = r1.astype(bf16)
    lo = (r1 - mid.astype(f32)).astype(bf16)
    d = lambda t: jnp.dot(t, m01, preferred_element_type=f32)
    return d(hi) + d(mid) + d(lo)


def _layer_norm(x, g, b):
    mu = jnp.mean(x, axis=-1, keepdims=True)
    xc = x - mu
    var = jnp.mean(xc * xc, axis=-1, keepdims=True)
    return xc * lax.rsqrt(var + LN_EPS) * g + b


def _cparams(sem):
    return pltpu.CompilerParams(dimension_semantics=sem, vmem_limit_bytes=VMEM_LIMIT)


def _resident(shape):
    nd = len(shape)
    return pl.BlockSpec(shape, lambda *_: (0,) * nd, pipeline_mode=pl.Buffered(1))


def _ada_kernel(c_ref, w_ref, b_ref, o_ref):
    c = c_ref[...]
    o_ref[...] = _dot(c * jax.nn.sigmoid(c), w_ref[...]) + b_ref[...]


def _ada_mod(c, w_ada, b_ada):
    rows, d = c.shape
    n = w_ada.shape[1]
    tn = d
    return pl.pallas_call(
        _ada_kernel,
        out_shape=jax.ShapeDtypeStruct((rows, n), f32),
        grid=(n // tn,),
        in_specs=[pl.BlockSpec((rows, d), lambda j: (0, 0)),
                  pl.BlockSpec((d, tn), lambda j: (0, j)),
                  pl.BlockSpec((1, tn), lambda j: (0, j))],
        out_specs=pl.BlockSpec((rows, tn), lambda j: (0, j)),
        compiler_params=_cparams(("parallel",)),
        name="ada_mod",
    )(c, w_ada, b_ada.reshape(1, n))


def _ffn(u, wg_ref, wu_ref, wd_ref, acc_ref):
    ub = u.astype(bf16)
    acc_ref[...] = jnp.zeros(acc_ref.shape, f32)

    for lo in range(0, wg_ref.shape[1], FF_CHUNK):
        hg = jnp.dot(ub, wg_ref[:, lo:lo + FF_CHUNK], preferred_element_type=f32)
        hu = jnp.dot(ub, wu_ref[:, lo:lo + FF_CHUNK], preferred_element_type=f32)
        h = hg * jax.nn.sigmoid(hg) * hu
        acc_ref[...] += jnp.dot(h.astype(bf16), wd_ref[lo:lo + FF_CHUNK, :], preferred_element_type=f32)
    return acc_ref[...]


def _ffn_block_kernel(alpha, half, x_ref, mod_ref, wg_ref, wu_ref, wd_ref, lng_ref, lnb_ref, o_ref, acc_ref):
    m0 = 6 * half
    ln = 2 * half
    x = x_ref[...]
    u = x * (1.0 + mod_ref[m0 + 1]) + mod_ref[m0]
    f = _ffn(u, wg_ref, wu_ref, wd_ref, acc_ref)
    o_ref[...] = _layer_norm(alpha * x + 0.5 * (1.0 + mod_ref[m0 + 2]) * f, lng_ref[ln:ln + 1, :],
                             lnb_ref[ln:ln + 1, :])


def _ffn_block(x, mod, wts, tm, half):
    g, t, d = x.shape
    r = mod.shape[2]
    row = pl.BlockSpec((None, tm, d), lambda i, j: (i, j, 0))
    names = ("wg1", "wu1", "wd1") if half == 0 else ("wg2", "wu2", "wd2")
    weights = [wts[n] for n in names] + [wts["ln_g"], wts["ln_b"]]
    return pl.pallas_call(
        functools.partial(_ffn_block_kernel, wts["alpha"], half),
        out_shape=jax.ShapeDtypeStruct((g, t, d), f32),
        grid=(g, t // tm),
        in_specs=[row, pl.BlockSpec((None, 9, r, d), lambda i, j: (i, 0, 0, 0))] + [_resident(w.shape) for w in weights],
        out_specs=row,
        scratch_shapes=[pltpu.VMEM((tm, d), f32)],
        compiler_params=_cparams(("parallel", "parallel")),
        name="ffn_block",
    )(x, mod, *weights)


def _in_proj_kernel(seg, seg_t, x1_ref, mod_ref, win_ref, bin_ref, wint_ref, bint_ref, *out_refs):
    u2 = (x1_ref[...] * (1.0 + mod_ref[4]) + mod_ref[3]).astype(bf16)
    for o_ref, (lo, hi) in zip(out_refs, seg):
        o_ref[...] = jnp.dot(u2, win_ref[:, lo:hi], preferred_element_type=f32) + bin_ref[:, lo:hi]
    for o_ref, (lo, hi) in zip(out_refs[len(seg):], seg_t):
        o_ref[...] = lax.dot_general(wint_ref[lo:hi, :], u2, _NT, preferred_element_type=f32) + bint_ref[lo:hi, :]


_PROJ = (("rw", RW_SHIFT), ("q", NSA_WIDTH), ("kc", KV_COLS), ("ks", KV_COLS), ("kw", KV_COLS),
         ("merge", 2 * D_MODEL), ("gate", GATE_PAD))
_PROJ_T = (("q", NSA_WIDTH), ("gate", GATE_PAD), ("ks", KV_COLS), ("kw", KV_COLS))


def _segments(table, names):
    seg, lo = {}, 0
    for name, w in table:
        seg[name] = (lo, lo + w)
        lo += w
    return tuple(seg[n] for n in names)


def _in_proj(x1, mod, wts, tm, names, names_t):
    g, t, d = x1.shape
    r = mod.shape[2]
    seg, seg_t = _segments(_PROJ, names), _segments(_PROJ_T, names_t)
    row = lambda w: pl.BlockSpec((None, tm, w), lambda i, j: (i, j, 0))
    col = lambda w: pl.BlockSpec((None, w, tm), lambda i, j: (i, 0, j))
    weights = [wts["w_in"], wts["b_in"], wts["w_in_t"], wts["b_in_t"]]
    return pl.pallas_call(
        functools.partial(_in_proj_kernel, seg, seg_t),
        out_shape=([jax.ShapeDtypeStruct((g, t, hi - lo), f32) for lo, hi in seg]
                   + [jax.ShapeDtypeStruct((g, hi - lo, t), f32) for lo, hi in seg_t]),
        grid=(g, t // tm),
        in_specs=[row(d), pl.BlockSpec((None, 9, r, d), lambda i, j: (i, 0, 0, 0))]
                 + [_resident(w.shape) for w in weights],
        out_specs=[row(hi - lo) for lo, hi in seg] + [col(hi - lo) for lo, hi in seg_t],
        compiler_params=_cparams(("parallel", "parallel")),
        name="in_proj",
    )(x1, mod, *weights)


def _mix_kernel(alpha, yb_on_lanes, x1_ref, ya_ref, yb_ref, pm_ref, mod_ref, woa_ref, wob_ref, wo_ref,
                lng_ref, lnb_ref, o_ref):
    x1 = x1_ref[...]
    pm = pm_ref[...]
    d = x1.shape[-1]
    ga = jax.nn.sigmoid(pm[:, :d])
    gb = jax.nn.sigmoid(pm[:, d:])
    if yb_on_lanes:
        yb_out = lax.dot_general(yb_ref[...].astype(bf16), wob_ref[...], _TN, preferred_element_type=f32)
    else:
        yb_out = _dot(yb_ref[...], wob_ref[...])
    merged = ga * _dot(ya_ref[...], woa_ref[...]) + gb * yb_out
    m = _dot(merged, wo_ref[...])
    o_ref[...] = _layer_norm(alpha * x1 + (1.0 + mod_ref[5]) * m, lng_ref[1:2, :], lnb_ref[1:2, :])


def _mix(x1, ya, yb, pm, mod, wts, tm, yb_on_lanes):
    g, t, d = x1.shape
    r = mod.shape[2]
    row = lambda w: pl.BlockSpec((None, tm, w), lambda i, j: (i, j, 0))
    yb_spec = pl.BlockSpec((None, NSA_WIDTH, tm), lambda i, j: (i, 0, j)) if yb_on_lanes else row(NSA_WIDTH)
    weights = [wts["w_out_a"], wts["w_out_b"], wts["w_o"], wts["ln_g"], wts["ln_b"]]
    return pl.pallas_call(
        functools.partial(_mix_kernel, wts["alpha"], yb_on_lanes),
        out_shape=jax.ShapeDtypeStruct((g, t, d), f32),
        grid=(g, t // tm),
        in_specs=[row(d), row(ya.shape[-1]), yb_spec, row(pm.shape[-1]),
                  pl.BlockSpec((None, 9, r, d), lambda i, j: (i, 0, 0, 0))] + [_resident(w.shape) for w in weights],
        out_specs=row(d),
        compiler_params=_cparams(("parallel", "parallel")),
        name="mix",
    )(x1, ya, yb, pm, mod, *weights)


def _rwkv_kernel(t_blk, p_ref, prev_ref, s0_ref, mu_ref, wl_ref, vec_ref, hsum_ref,
                 y_ref, sout_ref, state_ref, prevrow_ref, stage_ref, yn_ref):
    n_seq = p_ref.shape[0]
    ci = pl.program_id(1)

    @pl.when(ci == 0)
    def _():
        state_ref[...] = jnp.zeros(state_ref.shape, f32)
        for s in range(n_seq):
            for h in range(RW_HEADS):
                d0 = (h % RW_HPB) * RW_HEAD
                state_ref[s, h // RW_HPB, d0:d0 + RW_HEAD, d0:d0 + RW_HEAD] = s0_ref[s, h]
        prevrow_ref[...] = prev_ref[...]

    c = stage_ref.shape[1]
    gw = RW_HPB * RW_HEAD
    rows = RW_HPB * c
    br = lax.broadcasted_iota(jnp.int32, (rows, gw), 0)
    bc = lax.broadcasted_iota(jnp.int32, (rows, gw), 1)
    same_head = (br // c) == (bc // RW_HEAD)
    tr = lax.broadcasted_iota(jnp.int32, (rows, rows), 0) % c
    tc = lax.broadcasted_iota(jnp.int32, (rows, rows), 1) % c
    strict = tr > tc
    incl = tr >= tc
    blocks = lambda x: jnp.where(same_head, jnp.concatenate([x] * RW_HPB, axis=0), 0.0).astype(bf16)
    nt = lambda x, y: lax.dot_general(x, y, _NT, preferred_element_type=f32)
    tn = lambda x, y: lax.dot_general(x, y, _TN, preferred_element_type=f32)
    mm = lambda x, y: jnp.dot(x, y, preferred_element_type=f32)
    n_fac = max(1, (c - 1).bit_length())

    hsum = hsum_ref[...]

    def head_sums(xs):
        out = _dot_sel(jnp.concatenate(xs, axis=0), hsum)
        return [out[i * c:(i + 1) * c] for i in range(len(xs))]

    raw = [_rwkv_inputs(t_blk, p_ref.at[s], mu_ref, wl_ref, vec_ref, prevrow_ref.at[s], stage_ref.at[s])
           for s in range(n_seq)]
    sums = head_sums([d["kk"] * d["kk"] for d in raw] + [d["rk"] for d in raw])
    pre = [_rwkv_features(t_blk, raw[s], sums[s], sums[n_seq + s]) for s in range(n_seq)]
    chains = [(s, grp) for s in range(n_seq) for grp in range(RW_HEADS // RW_HPB)]
    ops, low, u = {}, {}, {}
    for s, grp in chains:
        gl = slice(grp * gw, (grp + 1) * gw)
        ops[s, grp] = [blocks(pre[s][name][:, gl]) for name in ("at", "rt", "bt", "kt", "bh", "kh", "v")]
    y_parts = {}
    for ch in chains:
        at4, rt4, bt4, kt4, bh4, kh4, v4 = ops[ch]
        s_b = state_ref[ch].astype(bf16)
        ar = jnp.concatenate([at4, rt4], axis=0)
        g_b, g_k, g_s = nt(ar, bt4), nt(ar, kt4), nt(ar, s_b)
        low[ch] = jnp.where(strict, g_b[:rows], 0.0).astype(bf16)
        u[ch] = g_s[:rows] + mm(jnp.where(strict, g_k[:rows], 0.0).astype(bf16), v4)
        y_parts[ch] = (g_s[rows:], jnp.where(incl, g_b[rows:], 0.0).astype(bf16),
                       jnp.where(incl, g_k[rows:], 0.0).astype(bf16))
    for i in range(n_fac):
        for ch in chains:
            u[ch] = u[ch] + mm(low[ch], u[ch].astype(bf16))
        if i + 1 < n_fac:
            for ch in chains:
                low[ch] = mm(low[ch], low[ch]).astype(bf16)
    for ch in chains:
        s, grp = ch
        gl = slice(grp * gw, (grp + 1) * gw)
        at4, rt4, bt4, kt4, bh4, kh4, v4 = ops[ch]
        s_bd = state_ref[ch]
        u_b = u[ch].astype(bf16)
        y_s0, rb_l, rk_l = y_parts[ch]
        y4 = y_s0 + mm(rb_l, u_b) + mm(rk_l, v4)
        state_ref[ch] = s_bd * pre[s]["p_last"][:, gl] + tn(u_b, bh4) + tn(v4, kh4)
        yg = y4[0:c]
        for hh in range(1, RW_HPB):
            yg = yg + y4[hh * c:(hh + 1) * c]
        yn_ref[s, :, gl] = yg

    ln_w, ln_b = vec_ref[5:6, :], vec_ref[6:7, :]
    means = head_sums([yn_ref[s] for s in range(n_seq)])
    ycs = [yn_ref[s] - means[s] * (1.0 / RW_HEAD) for s in range(n_seq)]
    sqs = head_sums([yc * yc for yc in ycs])
    for s in range(n_seq):
        var = sqs[s] * (1.0 / RW_HEAD)
        y = (ycs[s] * lax.rsqrt(var + RW_GN_EPS) * ln_w + ln_b + pre[s]["bonus"]) * pre[s]["g"]
        y_ref[s] = y[0:t_blk, :] if t_blk < c else y

    @pl.when(ci == pl.num_programs(1) - 1)
    def _():
        for s in range(n_seq):
            for h in range(RW_HEADS):
                d0 = (h % RW_HPB) * RW_HEAD
                sout_ref[s, h] = state_ref[s, h // RW_HPB, d0:d0 + RW_HEAD, d0:d0 + RW_HEAD]


def _rwkv_inputs(t_blk, p_ref, mu_ref, wl_ref, vec_ref, prevrow_ref, stage_ref):
    c = stage_ref.shape[0]
    w = RW_WIDTH
    if t_blk < c:
        stage_ref[...] = jnp.zeros(stage_ref.shape, f32)
        stage_ref[0:t_blk, :] = p_ref[...]
        p = stage_ref[...]
    else:
        p = p_ref[...]
    ridx = lax.broadcasted_iota(jnp.int32, (c, 1), 0)
    valid = ridx < t_blk
    p_prev = jnp.where(ridx == 0, prevrow_ref[...], pltpu.roll(p, 1, 0))
    prevrow_ref[...] = p[c - 1:c, :]
    xs = p + (p_prev - p) * mu_ref[...]

    tail = xs[:, 3 * w:]
    li = lax.broadcasted_iota(jnp.int32, tail.shape, 1)
    act = jnp.where(li < W_LORA, jnp.tanh(tail),
                    jnp.where(li < W_LORA + A_LORA, tail, jax.nn.sigmoid(tail)))
    lora = _dot(act, wl_ref[...])
    w0, a0, k_k, k_a, r_k = (vec_ref[i:i + 1, :] for i in range(5))
    z = -(w0 + lora[:, :w])
    softplus = jnp.maximum(z, 0.0) + jnp.log1p(jnp.exp(-jnp.abs(z)))
    lw = -jnp.exp(-softplus - 0.5)
    a = jax.nn.sigmoid(a0 + lora[:, w:2 * w])
    g = lora[:, 2 * w:]
    r, k, v = xs[:, :w], xs[:, w:2 * w], xs[:, 2 * w:3 * w]
    k2 = k * (1.0 + (a - 1.0) * k_a)
    return dict(r=r, v=v, lw=lw, a=a, g=g, kk=k * k_k, k2=k2, rk=r * k2 * r_k, valid=valid)


def _rwkv_features(t_blk, raw, kk_sq_sum, rk_sum):
    r, v, lw, a, g, k2, valid = (raw[n] for n in ("r", "v", "lw", "a", "g", "k2", "valid"))
    c = r.shape[0]
    kk = raw["kk"] * lax.rsqrt(jnp.maximum(kk_sq_sum, 1e-24))
    bonus = rk_sum * v
    if t_blk < c:
        lw = jnp.where(valid, lw, 0.0)
        kk = jnp.where(valid, kk, 0.0)
        k2 = jnp.where(valid, k2, 0.0)
        v = jnp.where(valid, v, 0.0)
    b = kk * a

    rr = lax.broadcasted_iota(jnp.int32, (c, c), 0)
    cc = lax.broadcasted_iota(jnp.int32, (c, c), 1)
    cum = _dot_f32(jnp.where(rr >= cc, 1.0, 0.0).astype(f32), lw)
    cum_last = cum[c - 1:c, :]
    e_neg = jnp.exp(-cum)
    e_rem = jnp.exp(cum_last - cum)
    at = -kk * jnp.exp(cum - lw)
    rt = r * jnp.exp(cum)
    bt, kt = b * e_neg, k2 * e_neg
    bh, kh = b * e_rem, k2 * e_rem
    p_last = jnp.exp(cum_last)
    return dict(at=at, rt=rt, bt=bt, kt=kt, bh=bh, kh=kh, v=v, p_last=p_last, bonus=bonus, g=g)


def _rwkv(p_rw, prev, s0, wts):
    g, t, ws = p_rw.shape
    c = RW_CHUNK
    t_blk = min(c, t)
    sb = RW_SEQS
    assert t % t_blk == 0 and g % sb == 0
    gw = RW_HPB * RW_HEAD
    return pl.pallas_call(
        functools.partial(_rwkv_kernel, t_blk),
        out_shape=[jax.ShapeDtypeStruct((g, t, RW_WIDTH), f32), jax.ShapeDtypeStruct(s0.shape, f32)],
        grid=(g // sb, t // t_blk),
        in_specs=[pl.BlockSpec((sb, t_blk, ws), lambda i, j: (i, j, 0)),
                  pl.BlockSpec((sb, 1, ws), lambda i, j: (i, 0, 0)),
                  pl.BlockSpec((sb,) + s0.shape[1:], lambda i, j: (i, 0, 0, 0)),
                  _resident(wts["rw_mu"].shape), _resident(wts["rw_lora"].shape),
                  _resident(wts["rw_vec"].shape), _resident(wts["rw_hsum"].shape)],
        out_specs=[pl.BlockSpec((sb, t_blk, RW_WIDTH), lambda i, j: (i, j, 0)),
                   pl.BlockSpec((sb,) + s0.shape[1:], lambda i, j: (i, 0, 0, 0))],
        scratch_shapes=[pltpu.VMEM((sb, RW_HEADS // RW_HPB, gw, gw), f32), pltpu.VMEM((sb, 1, ws), f32),
                        pltpu.VMEM((sb, c, ws), f32), pltpu.VMEM((sb, c, RW_WIDTH), f32)],
        compiler_params=_cparams(("parallel", "arbitrary")),
        name="rwkv",
    )(p_rw, prev.reshape(g, 1, ws), s0, wts["rw_mu"], wts["rw_lora"], wts["rw_vec"], wts["rw_hsum"])


def _gelu_tanh(x):
    return x * (0.5 * (1.0 + jnp.tanh(math.sqrt(2.0 / math.pi) * (x + 0.044715 * (x * x * x)))))


def _compress_rows(hids, pe_ref, w1f_ref, w2_ref, o_ref, w2t_ref=None, ot_ref=None):
    n = hids[0].shape[0]
    for kv in range(2):
        pe_term = _dot(pe_ref[kv], w1f_ref[kv])[0:1, :]
        for grp in range(NSA_KV):
            base = grp * 2 * CMP_HID
            lo = hids[kv][:, base:base + CMP_HID]
            hi = pltpu.roll(hids[kv][:, base + CMP_HID:base + 2 * CMP_HID], n - 1, 0)
            act = _gelu_tanh(lo + hi + pe_term).astype(bf16)
            col = (kv * NSA_KV + grp) * NSA_HEAD
            o_ref[:, col:col + NSA_HEAD] = jnp.dot(act, w2_ref[kv], preferred_element_type=f32)
            if ot_ref is not None:
                ot_ref[col:col + NSA_HEAD, :] = lax.dot_general(w2t_ref[kv], act, _NT, preferred_element_type=f32)


def _compress_dense_kernel(x_ref, wbig_ref, pe_ref, w1f_ref, w2_ref, w2t_ref, o_ref, ot_ref):
    hid = _dot(x_ref[...], wbig_ref[...])
    half = hid.shape[1] // 2
    _compress_rows([hid[:, :half], hid[:, half:]], pe_ref, w1f_ref, w2_ref, o_ref, w2t_ref, ot_ref)


def _compress_dense(x, wts):
    b, n, _ = x.shape
    return pl.pallas_call(
        _compress_dense_kernel,
        out_shape=[jax.ShapeDtypeStruct((b, n, KV_COLS), f32), jax.ShapeDtypeStruct((b, KV_COLS, n), f32)],
        grid=(b,),
        in_specs=[pl.BlockSpec((None, n, CMP_ROW), lambda i: (i, 0, 0)),
                  _resident(wts["phi_big"].shape), _resident(wts["phi_pe"].shape),
                  _resident(wts["phi_w1f"].shape), _resident(wts["phi_w2"].shape),
                  _resident(wts["phi_w2t"].shape)],
        out_specs=[pl.BlockSpec((None, n, KV_COLS), lambda i: (i, 0, 0)),
                   pl.BlockSpec((None, KV_COLS, n), lambda i: (i, 0, 0))],
        compiler_params=_cparams(("parallel",)),
        name="compress_dense",
    )(x, wts["phi_big"], wts["phi_pe"], wts["phi_w1f"], wts["phi_w2"], wts["phi_w2t"])


CMP_GROUP = 2 * LANES
CMP_UNROLL = 4


def _compress_paged_kernel(pt_ref, cache_ref, wkv_ref, pe_ref, w1f_ref, w2_ref, o_ref, pbuf_ref, xbuf_ref, sem):
    b = pl.program_id(0)
    n_pages = pt_ref.shape[1]
    page = cache_ref.shape[2]
    pages_per_group = CMP_GROUP // page
    rows_per_group = CMP_GROUP // CMP_STRIDE

    def page_copy(seq, i):
        return pltpu.make_async_copy(cache_ref.at[pt_ref[seq, i]], pbuf_ref.at[i], sem.at[i // pages_per_group])

    def start_all(seq):
        def start(i, carry):
            page_copy(seq, i).start()
            return carry
        lax.fori_loop(0, n_pages, start, 0)

    @pl.when(b == 0)
    def _():
        start_all(0)

    r = lax.broadcasted_iota(jnp.int32, (CMP_GROUP, CMP_GROUP), 0)
    c = lax.broadcasted_iota(jnp.int32, (CMP_GROUP, CMP_GROUP), 1)
    perm = jnp.where(c == CMP_STRIDE * (r % rows_per_group) + r // rows_per_group, 1.0, 0.0).astype(bf16)

    def regroup(step, carry):
        for pg in range(CMP_UNROLL * pages_per_group):
            page_copy(b, step * CMP_UNROLL * pages_per_group + pg).wait()
        for k in range(CMP_UNROLL):
            gi = step * CMP_UNROLL + k
            tokens = jnp.concatenate([pbuf_ref[gi * pages_per_group + pg] for pg in range(pages_per_group)],
                                     axis=1).astype(bf16)
            rows = lax.dot_general(perm, tokens, _NT, preferred_element_type=f32)
            r0 = pl.multiple_of(gi * rows_per_group, rows_per_group)
            for j in range(CMP_STRIDE):
                for kv in range(2):
                    col = (kv * CMP_STRIDE + j) * NSA_KV_COLS
                    xbuf_ref[pl.ds(r0, rows_per_group), col:col + NSA_KV_COLS] = (
                        rows[j * rows_per_group:(j + 1) * rows_per_group,
                             kv * NSA_KV_COLS:(kv + 1) * NSA_KV_COLS].astype(bf16))
        return carry

    n_groups = n_pages // pages_per_group
    lax.fori_loop(0, n_groups // CMP_UNROLL, regroup, 0)

    @pl.when(b + 1 < pl.num_programs(0))
    def _():
        start_all(b + 1)

    half = CMP_STRIDE * NSA_KV_COLS
    hids = [jnp.dot(xbuf_ref[:, kv * half:(kv + 1) * half], wkv_ref[kv], preferred_element_type=f32)
            for kv in range(2)]
    _compress_rows(hids, pe_ref, w1f_ref, w2_ref, o_ref)


def _compress_paged(cache_t, page_table, wts):
    b, n_pages = page_table.shape
    page = cache_t.shape[2]
    n = n_pages * page // CMP_STRIDE
    assert CMP_GROUP % page == 0 and (n_pages * page) % (CMP_GROUP * CMP_UNROLL) == 0
    full = lambda a: pl.BlockSpec(a.shape, lambda i, pt: (0,) * a.ndim, pipeline_mode=pl.Buffered(1))
    return pl.pallas_call(
        _compress_paged_kernel,
        out_shape=jax.ShapeDtypeStruct((b, n, KV_COLS), f32),
        grid_spec=pltpu.PrefetchScalarGridSpec(
            num_scalar_prefetch=1, grid=(b,),
            in_specs=[pl.BlockSpec(memory_space=pl.ANY), full(wts["phi_kv"]), full(wts["phi_pe"]),
                      full(wts["phi_w1f"]), full(wts["phi_w2"])],
            out_specs=pl.BlockSpec((None, n, KV_COLS), lambda i, pt: (i, 0, 0)),
            scratch_shapes=[pltpu.VMEM((n_pages, KV_COLS, page), f32), pltpu.VMEM((n, CMP_ROW), bf16),
                            pltpu.SemaphoreType.DMA((n_pages * page // CMP_GROUP,))]),
        compiler_params=_cparams(("arbitrary",)),
        name="compress_paged",
    )(page_table, cache_t, wts["phi_kv"], wts["phi_pe"], wts["phi_w1f"], wts["phi_w2"])


def _softmax_rows(s, mask):
    s = jnp.where(mask, s, NEG_INF)
    e = jnp.where(mask, jnp.exp(s - jnp.max(s, axis=-1, keepdims=True)), 0.0)
    return e / jnp.maximum(jnp.sum(e, axis=-1, keepdims=True), 1e-30)


def _flash_step(q, k, v, mask, m, l, acc):
    s = jnp.where(mask, _dot_nt(q, k), NEG_INF)
    m_new = jnp.maximum(m, jnp.max(s, axis=-1, keepdims=True))
    alpha = jnp.exp(m - m_new)
    e = jnp.where(mask, jnp.exp(s - m_new), 0.0)
    return m_new, alpha * l + jnp.sum(e, axis=-1, keepdims=True), alpha * acc + _dot(e, v)


def _flash_init(rows):
    return (jnp.full((rows, 1), NEG_INF, f32), jnp.zeros((rows, 1), f32), jnp.zeros((rows, NSA_HEAD), f32))


def _flash_out(m, l, acc):
    return acc / jnp.maximum(l, 1e-30)


def _importance_matrix(n_cmp_rows, n_sel_cols):
    ratio = SEL_BLOCK // CMP_STRIDE
    i = lax.broadcasted_iota(jnp.int32, (n_cmp_rows, n_sel_cols), 0)
    j = lax.broadcasted_iota(jnp.int32, (n_cmp_rows, n_sel_cols), 1)
    return jnp.where((i >= ratio * j - 1) & (i <= ratio * j + ratio - 1), 1.0, 0.0).astype(bf16)


def _block_scores(imp, j, cur):
    forced = (j == 0) | (j == cur) | (j == cur - 1)
    return jnp.where(j <= cur, jnp.where(forced, BIG, imp), -BIG)


def _stack_heads(x, grp, rows):
    return jnp.concatenate(
        [x[:, (grp * NSA_HPG + h) * NSA_HEAD:(grp * NSA_HPG + h + 1) * NSA_HEAD] for h in range(NSA_HPG)], axis=0)


def _stack_gates(gates, grp, branch):
    return jnp.concatenate(
        [gates[:, 3 * (grp * NSA_HPG + h) + branch:3 * (grp * NSA_HPG + h) + branch + 1] for h in range(NSA_HPG)],
        axis=0)


def _scores_t(qts, ks, biases):
    return tuple(jnp.dot(k.astype(bf16), qt, preferred_element_type=f32) + bias
                 for qt, k, bias in zip(qts, ks, biases))


def _flash_update_t(ss, vts, carry):
    n = len(ss)
    ms, ls, accs = carry[0::3], carry[1::3], carry[2::3]
    m_new = [jnp.maximum(ms[i], jnp.max(ss[i], axis=0, keepdims=True)) for i in range(n)]
    es = [jnp.exp2(ss[i] - m_new[i]).astype(bf16) for i in range(n)]
    ones = jnp.ones((ONES_ROWS, vts[0].shape[1]), bf16)
    pvs = [jnp.dot(jnp.concatenate([vts[i].astype(bf16), ones], axis=0), es[i], preferred_element_type=f32)
           for i in range(n)]
    out = ()
    for i in range(n):
        alpha = jnp.exp2(ms[i] - m_new[i])
        out += (m_new[i], alpha * ls[i] + pvs[i][NSA_HEAD:NSA_HEAD + 1], alpha * accs[i] + pvs[i][:NSA_HEAD])
    return out


M_FLOOR = 0.1 * NEG_INF
ONES_ROWS = 16
LOG2_E = math.log2(math.e)


def _flash_init_t(cols):
    return (jnp.full((1, cols), M_FLOOR, f32), jnp.zeros((1, cols), f32), jnp.zeros((NSA_HEAD, cols), f32))


def _lanes(x, n):
    return jnp.concatenate([x] * n, axis=1)


def _nsa_prompt_kernel(qt_ref, gt_ref, kc_ref, vct_ref, ks_ref, vst_ref, kw_ref, vwt_ref, o_ref):
    tq = qt_ref.shape[1]
    n_cmp = kc_ref.shape[0]
    n_sel = ks_ref.shape[0] // SEL_BLOCK
    i = pl.program_id(1)
    t0 = i * tq
    cols = NSA_HPG * tq
    gates = jax.nn.sigmoid(gt_ref[...])
    q_pos = t0 + lax.broadcasted_iota(jnp.int32, (1, tq), 1)
    q_pos_c = _lanes(q_pos, NSA_HPG)
    cmp_end = lax.broadcasted_iota(jnp.int32, (n_cmp, 1), 0) * CMP_STRIDE + (CMP_LEN - 1)
    ratio = SEL_BLOCK // CMP_STRIDE
    ij = lax.broadcasted_iota(jnp.int32, (n_sel, n_cmp), 0)
    ii = lax.broadcasted_iota(jnp.int32, (n_sel, n_cmp), 1)
    imp_mat = jnp.where((ii >= ratio * ij - 1) & (ii <= ratio * ij + ratio - 1), 1.0, 0.0).astype(bf16)
    jt = lax.broadcasted_iota(jnp.int32, (n_sel, tq), 0)
    cur_t = (t0 + lax.broadcasted_iota(jnp.int32, (n_sel, tq), 1)) // SEL_BLOCK
    krow = lax.broadcasted_iota(jnp.int32, (K_TILE, 1), 0)
    er = lax.broadcasted_iota(jnp.int32, (K_TILE, n_sel), 0) // SEL_BLOCK
    ej = lax.broadcasted_iota(jnp.int32, (K_TILE, n_sel), 1)
    groups = range(NSA_KV)
    heads_of = lambda grp: range(grp * NSA_HPG, (grp + 1) * NSA_HPG)
    qts, sels, o_cs = [], [], []

    for grp in groups:
        d0 = grp * NSA_HEAD
        heads = heads_of(grp)
        q_all = jnp.concatenate([qt_ref[h * NSA_HEAD:(h + 1) * NSA_HEAD, :] for h in heads], axis=1)
        qt = (q_all * NSA_SCALE).astype(bf16)
        mask_c = cmp_end <= q_pos_c
        s = jnp.where(mask_c, jnp.dot(kc_ref[:, d0:d0 + NSA_HEAD].astype(bf16), qt, preferred_element_type=f32),
                      NEG_INF)
        e = jnp.where(mask_c, jnp.exp(s - jnp.max(s, axis=0, keepdims=True)), 0.0)
        p_c = e / jnp.maximum(jnp.sum(e, axis=0, keepdims=True), 1e-30)
        o_c = jnp.dot(vct_ref[d0:d0 + NSA_HEAD, :].astype(bf16), p_c.astype(bf16), preferred_element_type=f32)
        p_sum = p_c[:, 0:tq]
        for h in range(1, NSA_HPG):
            p_sum = p_sum + p_c[:, h * tq:(h + 1) * tq]
        hi = p_sum.astype(bf16)
        r1 = p_sum - hi.astype(f32)
        mid = r1.astype(bf16)
        lo = (r1 - mid.astype(f32)).astype(bf16)
        imp = sum(jnp.dot(imp_mat, part, preferred_element_type=f32) for part in (hi, mid, lo))
        score = _block_scores(imp, jt, cur_t)
        rank = jnp.zeros((n_sel, tq), f32)
        for j2 in range(n_sel):
            row = score[j2:j2 + 1, :]
            ge = jnp.where(row >= score, 1.0, 0.0)
            gt = jnp.where(row > score, 1.0, 0.0)
            rank = rank + jnp.where(jt > j2, ge, gt)
        q_exp2 = (q_all * (NSA_SCALE * LOG2_E)).astype(bf16)
        qts.extend(q_exp2[:, n * tq:(n + 1) * tq] for n in range(NSA_HPG))
        sels.append(jnp.where(rank < SEL_TOPK, 1.0, 0.0).astype(bf16))
        o_cs.extend(o_c[:, n * tq:(n + 1) * tq] for n in range(NSA_HPG))

    last = (t0 + tq - 1) // K_TILE
    key_cols = lambda ref, k0: [ref[pl.ds(k0, K_TILE), g * NSA_HEAD:(g + 1) * NSA_HEAD] for g in groups]
    val_rows = lambda ref, k0: [ref[g * NSA_HEAD:(g + 1) * NSA_HEAD, pl.ds(k0, K_TILE)] for g in groups]

    def flash_heads(keys, vals, biases, carry):
        out = ()
        for h0 in range(0, NSA_HEADS, FLASH_HEADS):
            hs = range(h0, h0 + FLASH_HEADS)
            grp_of = [h // NSA_HPG for h in hs]
            ss = _scores_t([qts[h] for h in hs], [keys[g] for g in grp_of], [biases[g] for g in grp_of])
            out += _flash_update_t(ss, [vals[g] for g in grp_of], carry[3 * h0:3 * (h0 + FLASH_HEADS)])
        return out

    def sel_step(kt, carry):
        k0 = pl.multiple_of(kt * K_TILE, K_TILE)
        expand = jnp.where(ej == kt * (K_TILE // SEL_BLOCK) + er, 1.0, 0.0).astype(bf16)
        causal = (k0 + krow) <= q_pos
        biases = []
        for grp in groups:
            chosen = jnp.dot(expand, sels[grp], preferred_element_type=f32)
            biases.append(jnp.where(causal & (chosen > 0.5), 0.0, NEG_INF))
        return flash_heads(key_cols(ks_ref, k0), val_rows(vst_ref, k0), biases, carry)

    sel_acc = lax.fori_loop(0, last + 1, sel_step, _flash_init_t(tq) * NSA_HEADS)

    def win_step(kt, carry):
        k0 = pl.multiple_of(kt * K_TILE, K_TILE)
        dist = q_pos - (k0 + krow)
        bias = jnp.where((dist >= 0) & (dist < WINDOW), 0.0, NEG_INF)
        return flash_heads(key_cols(kw_ref, k0), val_rows(vwt_ref, k0), [bias] * NSA_KV, carry)

    win_acc = lax.fori_loop(jnp.maximum(t0 - WINDOW, 0) // K_TILE, last + 1, win_step,
                            _flash_init_t(tq) * NSA_HEADS)

    for h in range(NSA_HEADS):
        o_s = _flash_out(*sel_acc[3 * h:3 * h + 3])
        o_w = _flash_out(*win_acc[3 * h:3 * h + 3])
        gate = lambda br: gates[3 * h + br:3 * h + br + 1, :]
        o_ref[h * NSA_HEAD:(h + 1) * NSA_HEAD, :] = gate(0) * o_cs[h] + gate(1) * o_s + gate(2) * o_w


def _nsa_prompt(qt, gt, kvc, kvct, kvs, kvst, kvw, kvwt):
    b, _, t = qt.shape
    n_cmp = kvc.shape[1]
    tq = Q_TILE
    assert t % tq == 0 and t % K_TILE == 0 and K_TILE % SEL_BLOCK == 0
    keys =lambda r: pl.BlockSpec((None, r, NSA_KV_COLS), lambda i, j: (i, 0, 0))
    vals = lambda c: pl.BlockSpec((None, NSA_KV_COLS, c), lambda i, j: (i, 1, 0))
    return pl.pallas_call(
        _nsa_prompt_kernel,
        out_shape=jax.ShapeDtypeStruct((b, NSA_WIDTH, t), f32),
        grid=(b, t // tq),
        in_specs=[pl.BlockSpec((None, NSA_WIDTH, tq), lambda i, j: (i, 0, j)),
                  pl.BlockSpec((None, GATE_PAD, tq), lambda i, j: (i, 0, j)),
                  keys(n_cmp), vals(n_cmp), keys(t), vals(t), keys(t), vals(t)],
        out_specs=pl.BlockSpec((None, NSA_WIDTH, tq), lambda i, j: (i, 0, j)),
        compiler_params=_cparams(("parallel", "arbitrary")),
        name="nsa_prompt",
    )(qt, gt, kvc, kvct, kvs, kvst, kvw, kvwt)


def _nsa_sample_kernel(past_len, pt_ref, q_ref, gate_ref, kvc_ref, cache_ref, kvs_new_ref, win_ref, kvw_new_ref,
                       o_ref, selbufs_ref, winbuf_ref, qbuf_ref, gbuf_ref, obuf_ref, sem):
    b = pl.program_id(0)
    t_new = q_ref.shape[0]
    tp = qbuf_ref.shape[0]
    n_pages = pt_ref.shape[1]
    page = cache_ref.shape[2]
    n_cmp = kvc_ref.shape[0]
    n_win = win_ref.shape[1]
    total = past_len + t_new
    n_sel = -(-total // SEL_BLOCK)
    n_sel_pad = -(-n_sel // LANES) * LANES
    tile_pages = S_KTILE // page
    tile_blocks = S_KTILE // SEL_BLOCK
    rows = NSA_HPG * tp

    slot = b % 2
    selbuf_ref = selbufs_ref.at[slot]

    def page_copy(seq, i, into):
        return pltpu.make_async_copy(cache_ref.at[pt_ref[seq, i]], selbufs_ref.at[into, i],
                                     sem.at[into, i // tile_pages])

    def start_all(seq, into):
        def start(i, carry):
            page_copy(seq, i, into).start()
            return carry
        lax.fori_loop(0, n_pages, start, 0)

    @pl.when(b == 0)
    def _():
        start_all(0, 0)

    @pl.when(b + 1 < pl.num_programs(0))
    def _():
        start_all(b + 1, 1 - slot)

    src = lax.broadcasted_iota(jnp.int32, (kvs_new_ref.shape[1], LANES), 0)
    dst = lax.broadcasted_iota(jnp.int32, (kvs_new_ref.shape[1], LANES), 1)
    pick = jnp.where((src == b * t_new + dst) & (dst < t_new), 1.0, 0.0).astype(bf16)
    selbuf_ref[n_pages] = _dot(kvs_new_ref[...], pick)
    winbuf_ref[:, 0:n_win] = win_ref[...]
    winbuf_ref[:, n_win:n_win + LANES] = _dot(kvw_new_ref[...], pick)
    qbuf_ref[...] = jnp.zeros(qbuf_ref.shape, f32)
    qbuf_ref[0:t_new, :] = q_ref[...]
    gbuf_ref[...] = jnp.zeros(gbuf_ref.shape, f32)
    gbuf_ref[0:t_new, :] = gate_ref[...]

    q = qbuf_ref[...] * NSA_SCALE
    gates = jax.nn.sigmoid(gbuf_ref[...])
    q_pos = past_len + lax.broadcasted_iota(jnp.int32, (tp, 1), 0)
    q_pos_r = jnp.concatenate([q_pos] * NSA_HPG, axis=0)
    cmp_end = lax.broadcasted_iota(jnp.int32, (1, n_cmp), 1) * CMP_STRIDE + (CMP_LEN - 1)
    imp_mat = _importance_matrix(n_cmp, n_sel_pad)
    jj = lax.broadcasted_iota(jnp.int32, (tp, n_sel_pad), 1)
    expand = jnp.where(lax.broadcasted_iota(jnp.int32, (tile_blocks, S_KTILE), 0)
                       == lax.broadcasted_iota(jnp.int32, (tile_blocks, S_KTILE), 1) // SEL_BLOCK,
                       1.0, 0.0).astype(bf16)

    def flash(qs, k_t, v_t, mask, m, l, acc):
        s = jnp.where(mask, jnp.dot(qs, k_t.astype(bf16), preferred_element_type=f32), NEG_INF)
        m_new = jnp.maximum(m, jnp.max(s, axis=-1, keepdims=True))
        alpha = jnp.exp(m - m_new)
        e = jnp.where(mask, jnp.exp(s - m_new), 0.0)
        pv = lax.dot_general(e.astype(bf16), v_t.astype(bf16), _NT, preferred_element_type=f32)
        return m_new, alpha * l + jnp.sum(e, axis=-1, keepdims=True), alpha * acc + pv

    qss, sels, o_cs = [], [], []
    for grp in range(NSA_KV):
        kcol0, vcol0 = grp * NSA_HEAD, NSA_KV_COLS + grp * NSA_HEAD
        qs = _stack_heads(q, grp, tp).astype(bf16)
        p_c = _softmax_rows(_dot_nt(qs, kvc_ref[:, kcol0:kcol0 + NSA_HEAD]), cmp_end <= q_pos_r)
        o_cs.append(_dot(p_c, kvc_ref[:, vcol0:vcol0 + NSA_HEAD]))
        p_sum = p_c[0:tp]
        for h in range(1, NSA_HPG):
            p_sum = p_sum + p_c[h * tp:(h + 1) * tp]
        score = _block_scores(_dot_sel(p_sum, imp_mat), jj, q_pos // SEL_BLOCK)
        rank = jnp.zeros((tp, n_sel_pad), f32)
        for j2 in range(n_sel):
            col = score[:, j2:j2 + 1]
            ge = jnp.where(col >= score, 1.0, 0.0)
            gt = jnp.where(col > score, 1.0, 0.0)
            rank = rank + jnp.where(jj > j2, ge, gt)
        sel = jnp.where((rank < SEL_TOPK) & (jj < n_sel), 1.0, 0.0).astype(bf16)
        sels.append(jnp.concatenate([sel] * NSA_HPG, axis=0))
        qss.append(qs)

    def wait(i, carry):
        page_copy(b, i, slot).wait()
        return carry

    lax.fori_loop(0, n_pages, wait, 0)
    n_tiles = past_len // S_KTILE
    tiles = [(kt * S_KTILE, S_KTILE if kt < n_tiles else LANES) for kt in range(n_tiles + 1)]
    probs = [(grp, kt) for kt in range(n_tiles + 1) for grp in range(NSA_KV)]

    def tile_rows(kt, r0):
        pages = range(kt * tile_pages, (kt + 1) * tile_pages) if kt < n_tiles else [n_pages]
        return jnp.concatenate([selbuf_ref[p, r0:r0 + NSA_HEAD, :] for p in pages], axis=1)

    masks, ss = {}, {}
    for grp, kt in probs:
        k0, width = tiles[kt]
        k_pos = k0 + lax.broadcasted_iota(jnp.int32, (1, width), 1)
        blocks = sels[grp][:, kt * tile_blocks:(kt + 1) * tile_blocks]
        chosen = jnp.dot(blocks, expand[:, 0:width], preferred_element_type=f32)
        masks[grp, kt] = (chosen > 0.5) & (k_pos <= q_pos_r)
        k_t = tile_rows(kt, grp * NSA_HEAD)
        ss[grp, kt] = jnp.where(masks[grp, kt], jnp.dot(qss[grp], k_t.astype(bf16), preferred_element_type=f32),
                                NEG_INF)
    row_max = [functools.reduce(jnp.maximum, [jnp.max(ss[grp, kt], axis=-1, keepdims=True)
                                               for kt in range(n_tiles + 1)]) for grp in range(NSA_KV)]
    es = {p: jnp.where(masks[p], jnp.exp(ss[p] - row_max[p[0]]), 0.0) for p in probs}
    pvs = {}
    for grp, kt in probs:
        k0, width = tiles[kt]
        v_t = tile_rows(kt, NSA_KV_COLS + grp * NSA_HEAD)
        pvs[grp, kt] = lax.dot_general(es[grp, kt].astype(bf16), v_t.astype(bf16), _NT, preferred_element_type=f32)

    w_pos = past_len - n_win + lax.broadcasted_iota(jnp.int32, (1, winbuf_ref.shape[1]), 1)
    dist = jnp.where(w_pos >= 0, q_pos_r - w_pos, -1)
    for grp in range(NSA_KV):
        krow0, vrow0 = grp * NSA_HEAD, NSA_KV_COLS + grp * NSA_HEAD
        denom = sum(jnp.sum(es[grp, kt], axis=-1, keepdims=True) for kt in range(n_tiles + 1))
        o_s = sum(pvs[grp, kt] for kt in range(n_tiles + 1)) / jnp.maximum(denom, 1e-30)
        o_w = _flash_out(*flash(qss[grp], winbuf_ref[krow0:krow0 + NSA_HEAD, :], winbuf_ref[vrow0:vrow0 + NSA_HEAD, :],
                                (dist >= 0) & (dist < WINDOW), *_flash_init(rows)))
        o = (_stack_gates(gates, grp, 0) * o_cs[grp] + _stack_gates(gates, grp, 1) * o_s
             + _stack_gates(gates, grp, 2) * o_w)
        for h in range(NSA_HPG):
            col = (grp * NSA_HPG + h) * NSA_HEAD
            obuf_ref[:, col:col + NSA_HEAD] = o[h * tp:(h + 1) * tp]
    o_ref[...] = obuf_ref[0:t_new, :]


def _nsa_sample(p_q, p_gate, kvc_cmp, cache_sel_t, page_table, kvs_new_t, win_state_t, kvw_new_t):
    b, t_new, _ = p_q.shape
    n_pages = page_table.shape[1]
    page = cache_sel_t.shape[2]
    past_len = n_pages * page
    n_cmp = kvc_cmp.shape[1]
    n_win = win_state_t.shape[2]
    tp = 8
    assert t_new <= tp and past_len % S_KTILE == 0 and S_KTILE % page == 0 and n_win % LANES == 0
    assert b * t_new == kvs_new_t.shape[1] and page == LANES
    n_sel_pad = -(-(-(-(past_len + t_new) // SEL_BLOCK)) // LANES) * LANES
    assert (past_len // S_KTILE + 1) * (S_KTILE // SEL_BLOCK) <= n_sel_pad
    per_seq = lambda n, w: pl.BlockSpec((None, n, w), lambda i, pt: (i, 0, 0))
    whole = lambda a: pl.BlockSpec(a.shape, lambda i, pt: (0,) * a.ndim, pipeline_mode=pl.Buffered(1))
    return pl.pallas_call(
        functools.partial(_nsa_sample_kernel, past_len),
        out_shape=jax.ShapeDtypeStruct((b, t_new, NSA_WIDTH), f32),
        grid_spec=pltpu.PrefetchScalarGridSpec(
            num_scalar_prefetch=1, grid=(b,),
            in_specs=[per_seq(t_new, NSA_WIDTH), per_seq(t_new, GATE_PAD), per_seq(n_cmp, KV_COLS),
                      pl.BlockSpec(memory_space=pl.ANY), whole(kvs_new_t), per_seq(KV_COLS, n_win),
                      whole(kvw_new_t)],
            out_specs=per_seq(t_new, NSA_WIDTH),
            scratch_shapes=[pltpu.VMEM((2, n_pages + 1, KV_COLS, page), f32), pltpu.VMEM((KV_COLS, n_win + LANES), f32),
                            pltpu.VMEM((tp, NSA_WIDTH), f32), pltpu.VMEM((tp, GATE_PAD), f32),
                            pltpu.VMEM((tp, NSA_WIDTH), f32), pltpu.SemaphoreType.DMA((2, past_len // S_KTILE))]),
        compiler_params=_cparams(("arbitrary",)),
        name="nsa_sample",
    )(page_table, p_q, p_gate, kvc_cmp, cache_sel_t, kvs_new_t, win_state_t, kvw_new_t)


def _prep_weights(l, depth, ln_g, ln_b, ffn_w_gate, ffn_w_up, ffn_w_down, w_in, b_in, rw_mu, rw_w0, rw_w2, rw_a0,
                  rw_a2, rw_g2, rw_k_k, rw_k_a, rw_r_k, rw_ln_w, rw_ln_b, nsa_phi_pe, nsa_phi_w1, nsa_phi_w2,
                  w_out_a, w_out_b, w_o):
    d = D_MODEL
    nc = D_FF // FF_CHUNK
    up = down = lambda w: w.astype(bf16)
    c1 = RW_SHIFT
    c2 = c1 + NSA_WIDTH
    c3 = c2 + 3 * KV_COLS
    c4 = c3 + N_GATE
    cols = lambda a: jnp.concatenate(
        [a[..., :c3], a[..., c4:], a[..., c3:c4], jnp.zeros(a.shape[:-1] + (GATE_PAD - N_GATE,), a.dtype)], axis=-1)
    z = lambda r, c: jnp.zeros((r, c), f32)
    w = RW_WIDTH
    lora = jnp.concatenate([
        jnp.concatenate([rw_w2[l], z(W_LORA, w), z(W_LORA, w)], axis=1),
        jnp.concatenate([z(A_LORA, w), rw_a2[l], z(A_LORA, w)], axis=1),
        jnp.concatenate([z(G_LORA, w), z(G_LORA, w), rw_g2[l]], axis=1)], axis=0)
    hid = lax.broadcasted_iota(jnp.int32, (w, w), 0) // RW_HEAD == lax.broadcasted_iota(jnp.int32, (w, w), 1) // RW_HEAD
    vec = jnp.stack([rw_w0[l], rw_a0[l], rw_k_k[l], rw_k_a[l], rw_r_k[l].reshape(w), rw_ln_w[l], rw_ln_b[l],
                     jnp.zeros((w,), f32)])
    w1 = nsa_phi_w1[l].reshape(2, 2, CMP_STRIDE, NSA_HEAD, CMP_HID)
    eye = jnp.eye(NSA_KV, dtype=f32)
    big = jnp.einsum("kljde,kK,gG->jkgdKGle", w1, jnp.eye(2, dtype=f32), eye)
    per_kv = jnp.einsum("kljde,gG->kjgdGle", w1, eye)
    pe = jnp.broadcast_to(nsa_phi_pe[l].reshape(2, 1, CMP_LEN * NSA_HEAD), (2, 8, CMP_LEN * NSA_HEAD))
    w_cols, b_cols = cols(w_in[l]), cols(b_in[l])
    seg_t = _segments(_PROJ, [name for name, _ in _PROJ_T])
    w_t = jnp.concatenate([w_cols[:, lo:hi] for lo, hi in seg_t], axis=1).T
    b_t = jnp.concatenate([b_cols[lo:hi] for lo, hi in seg_t])
    return dict(
        alpha=(2 * depth) ** 0.25,
        wg1=up(ffn_w_gate[l, 0]), wu1=up(ffn_w_up[l, 0]), wd1=down(ffn_w_down[l, 0]),
        wg2=up(ffn_w_gate[l, 1]), wu2=up(ffn_w_up[l, 1]), wd2=down(ffn_w_down[l, 1]),
        ln_g=ln_g[l], ln_b=ln_b[l],
        w_in=w_cols.astype(bf16), b_in=b_cols.reshape(1, -1),
        w_in_t=w_t.astype(bf16), b_in_t=b_t[:, None],
        phi_w2t=jnp.swapaxes(nsa_phi_w2[l], 1, 2).astype(bf16),
        rw_mu=rw_mu[l].reshape(1, -1), rw_lora=lora.astype(bf16), rw_vec=vec, rw_hsum=hid.astype(bf16),
        phi_big=big.reshape(CMP_ROW, 2 * NSA_KV * 2 * CMP_HID).astype(bf16), phi_pe=pe,
        phi_kv=per_kv.reshape(2, CMP_STRIDE * NSA_KV_COLS, NSA_KV * 2 * CMP_HID).astype(bf16),
        phi_w1f=nsa_phi_w1[l].reshape(2, CMP_LEN * NSA_HEAD, CMP_HID).astype(bf16),
        phi_w2=nsa_phi_w2[l].astype(bf16),
        w_out_a=w_out_a[l].astype(bf16), w_out_b=w_out_b[l].astype(bf16), w_o=w_o[l].astype(bf16))


def _kv_state(p):
    return p.reshape(p.shape[:-1] + (2, NSA_KV, NSA_HEAD))


def _tokens_last(a):
    n, t = a.shape[:2]
    return a.transpose(0, 2, 3, 4, 1).reshape(n, KV_COLS, t)


def _kv_state_t(p_t):
    b, _, t = p_t.shape
    return p_t.reshape(b, 2, NSA_KV, NSA_HEAD, t).transpose(0, 4, 1, 2, 3)


def kernel(x_prompt, x_sample, c_prompt, c_sample, cache_kv_cmp, cache_kv_sel, state_kv_win, state_wkv, state_shift, page_table, w_ada, b_ada, ln_g, ln_b, ffn_w_gate, ffn_w_up, ffn_w_down, w_in, b_in, rw_mu, rw_w0, rw_w2, rw_a0, rw_a2, rw_g2, rw_k_k, rw_k_a, rw_r_k, rw_ln_w, rw_ln_b, nsa_phi_pe, nsa_phi_w1, nsa_phi_w2, w_out_a, w_out_b, w_o):
    bp, seq, d = x_prompt.shape
    bd, t_new, _ = x_sample.shape
    depth = w_ada.shape[0]
    n_phys, page = cache_kv_cmp.shape[1:3]
    n_pages = page_table.shape[1]
    n_win = state_kv_win.shape[2]
    rows_s = bd * t_new
    tm_p = min(TM_PROJ, seq)
    tm_ffn = min(TM_FFN, seq)
    assert seq % tm_p == 0 and seq % tm_ffn == 0 and seq % CMP_STRIDE == 0 and page % CMP_STRIDE == 0

    y_p, y_s = x_prompt, x_sample.reshape(1, rows_s, d)
    st_p, st_s = [], []
    for l in range(depth):
        wts = _prep_weights(l, depth, ln_g, ln_b, ffn_w_gate, ffn_w_up, ffn_w_down, w_in, b_in, rw_mu, rw_w0,
                            rw_w2, rw_a0, rw_a2, rw_g2, rw_k_k, rw_k_a, rw_r_k, rw_ln_w, rw_ln_b, nsa_phi_pe,
                            nsa_phi_w1, nsa_phi_w2, w_out_a, w_out_b, w_o)
        mod = _ada_mod(jnp.concatenate([c_prompt, c_sample], axis=0), w_ada[l], b_ada[l])
        mod_p = mod[:bp].reshape(bp, 9, 1, d)
        mod_s = jnp.repeat(mod[bp:].reshape(bd, 9, d), t_new, axis=0).reshape(rows_s, 9, d)
        mod_s = mod_s.transpose(1, 0, 2).reshape(1, 9, rows_s, d)

        x1 = _ffn_block(y_p, mod_p, wts, tm_ffn, 0)
        p_rw, p_kc, p_ks, p_kw, p_m, q_t, g_t, ks_t, kw_t = _in_proj(
            x1, mod_p, wts, tm_p, ("rw", "kc", "ks", "kw", "merge"), ("q", "gate", "ks", "kw"))
        ya, wkv_p = _rwkv(p_rw, jnp.zeros((bp, RW_SHIFT), f32), jnp.zeros((bp, RW_HEADS, RW_HEAD, RW_HEAD), f32),
                          wts)
        kvc_cmp, kvc_cmp_t = _compress_dense(p_kc.reshape(bp, seq // CMP_STRIDE, CMP_ROW), wts)
        yb_t = _nsa_prompt(q_t, g_t, kvc_cmp, kvc_cmp_t, p_ks, ks_t, p_kw, kw_t)
        y_p = _ffn_block(_mix(x1, ya, yb_t, p_m, mod_p, wts, tm_p, True), mod_p, wts, tm_ffn, 1)
        n_keep = min(WINDOW, seq)
        st_p.append((_kv_state(p_kc), _kv_state_t(ks_t), _kv_state_t(kw_t[:, :, seq - n_keep:]), wkv_p,
                     p_rw[:, -1]))

        x1 = _ffn_block(y_s, mod_s, wts, rows_s, 0)
        p_rw, p_q, p_kc, p_ks, p_m, p_g, ks_t, kw_t = _in_proj(
            x1, mod_s, wts, rows_s, ("rw", "q", "kc", "ks", "merge", "gate"), ("ks", "kw"))
        per_seq = lambda a: a.reshape(bd, t_new, a.shape[-1])
        p_rw, p_q, p_kc, p_ks, p_g = map(per_seq, (p_rw, p_q, p_kc, p_ks, p_g))
        ya, wkv_s = _rwkv(p_rw, state_shift[l], state_wkv[l], wts)
        kvc_cmp = _compress_paged(_tokens_last(cache_kv_cmp[l]), page_table, wts)
        win_state_t = _tokens_last(state_kv_win[l])
        yb = _nsa_sample(p_q, p_g, kvc_cmp, _tokens_last(cache_kv_sel[l]), page_table, ks_t[0], win_state_t,
                         kw_t[0])
        x2 = _mix(x1, ya.reshape(1, rows_s, RW_WIDTH), yb.reshape(1, rows_s, NSA_WIDTH), p_m, mod_s, wts, rows_s,
                  False)
        y_s = _ffn_block(x2, mod_s, wts, rows_s, 1)
        kw_new_t = kw_t[0].reshape(KV_COLS, bd, t_new).transpose(1, 0, 2)
        new_win_t = jnp.concatenate([win_state_t, kw_new_t], axis=2)[:, :, t_new:]
        st_s.append((_kv_state(p_kc), _kv_state(p_ks), _kv_state_t(new_win_t), wkv_s, p_rw[:, -1]))

    stack = lambda sts, i: jnp.stack([s[i] for s in sts])
    return (y_p, y_s.reshape(bd, t_new, d)) + tuple(stack(st_p, i) for i in range(5)) + tuple(
        stack(st_s, i) for i in range(5))
```
